```python
import math
import jax, jax.numpy as jnp
from jax import lax
import numpy as np

D_MODEL = 1024
BATCH = 2
SEQ = 16384
DEPTH = 2

f32 = jnp.float32

N_A_LAYERS = DEPTH // 2
N_B_LAYERS = DEPTH - N_A_LAYERS
ALPHA = (2.0 * DEPTH) ** 0.25
BETA = (8.0 * DEPTH) ** -0.25
LN_EPS = 1e-5

DN_HEADS = 8
DN_HEAD_DIM = 128
DN_WIDTH = DN_HEADS * DN_HEAD_DIM
CONV_K = 4
DN_CHUNK = 64

ATTN_GROUPS = ((128, 1), (512, 4), (2048, 16))
N_GROUPS = len(ATTN_GROUPS)
B_HEADS = 8
B_HEAD_DIM = 128
B_WIDTH = B_HEADS * B_HEAD_DIM
ATTN_BLOCK = 128
ROT_DIM = B_HEAD_DIM // 4
ROPE_THETA = 500000.0

PEER_HEADS = 8
PEER_NKEYS = 128
PEER_EXPERTS = PEER_NKEYS * PEER_NKEYS
PEER_TOPK = 16
PEER_QDIM = 256
PEER_TOK_BLOCK = 128

kernel_name = 'yoco_deltanet_dilated_peer'


def layer_norm(x, g, b):
    xf = x.astype(f32)
    mu = jnp.mean(xf, -1, keepdims=True)
    var = jnp.mean(jnp.square(xf - mu), -1, keepdims=True)
    return ((xf - mu) * lax.rsqrt(var + LN_EPS) * g.astype(f32) + b.astype(f32)).astype(x.dtype)


def l2norm(t):
    return t * lax.rsqrt(jnp.sum(t * t, -1, keepdims=True) + 1e-6)


def causal_short_conv(x, w):
    s = x.shape[1]
    xp = jnp.pad(x, ((0, 0), (CONV_K - 1, 0), (0, 0)))
    y = xp[:, 0:s] * w[0]
    for j in range(1, CONV_K):
        y = y + xp[:, j:j + s] * w[j]
    return y


def gated_delta_rule(q, k, v, g, beta):
    b, s, h, dk = q.shape
    dv = v.shape[-1]
    n = s // DN_CHUNK

    def chunked(t):
        return jnp.moveaxis(t.reshape(b, n, DN_CHUNK, h, -1), 3, 1)

    q = chunked(q) * (dk ** -0.5)
    k = chunked(k)
    v = chunked(v)
    g = jnp.cumsum(jnp.moveaxis(g.reshape(b, n, DN_CHUNK, h), 3, 1), axis=-1)
    beta = jnp.moveaxis(beta.reshape(b, n, DN_CHUNK, h), 3, 1)
    idx = jnp.arange(DN_CHUNK)
    causal = idx[:, None] >= idx[None, :]
    strict = idx[:, None] > idx[None, :]
    decay = jnp.exp(jnp.where(causal, g[..., :, None] - g[..., None, :], -jnp.inf))
    k_beta = k * beta[..., None]
    m = jnp.where(strict, jnp.einsum('bhnid,bhnjd->bhnij', k_beta, k) * decay, 0.0)
    tri = m + jnp.eye(DN_CHUNK, dtype=m.dtype)
    rhs = jnp.concatenate([v * beta[..., None], k_beta * jnp.exp(g)[..., None]], axis=-1)
    sol = lax.linalg.triangular_solve(tri, rhs, left_side=True, lower=True, unit_diagonal=True)
    u, w = sol[..., :dv], sol[..., dv:]
    a_qk = jnp.einsum('bhnid,bhnjd->bhnij', q, k) * decay

    def step(state, xs):
        q_c, k_c, u_c, w_c, g_c, a_c = xs
        v_new = u_c - jnp.einsum('bhck,bhkv->bhcv', w_c, state)
        o_c = (jnp.einsum('bhck,bhkv->bhcv', q_c * jnp.exp(g_c)[..., None], state)
               + jnp.einsum('bhij,bhjv->bhiv', a_c, v_new))
        g_last = g_c[..., -1]
        state = (state * jnp.exp(g_last)[..., None, None]
                 + jnp.einsum('bhck,bhcv->bhkv', k_c * jnp.exp(g_last[..., None] - g_c)[..., None], v_new))
        return state, o_c

    xs = tuple(jnp.moveaxis(t, 2, 0) for t in (q, k, u, w, g, a_qk))
    s0 = jnp.zeros((b, h, dk, dv), f32)
    _, o = lax.scan(step, s0, xs)
    return jnp.transpose(o, (1, 0, 3, 2, 4)).reshape(b, s, h, dv)


def deltanet_mixer(x, w_in, conv_w, a_log, dt_bias, norm_w, w_out):
    b, s, _ = x.shape
    proj = x @ w_in
    qkv = jax.nn.silu(causal_short_conv(proj[..., :3 * DN_WIDTH], conv_w))
    z = proj[..., 3 * DN_WIDTH:4 * DN_WIDTH].reshape(b, s, DN_HEADS, DN_HEAD_DIM).astype(f32)
    a = proj[..., 4 * DN_WIDTH:4 * DN_WIDTH + DN_HEADS].astype(f32)
    bl = proj[..., 4 * DN_WIDTH + DN_HEADS:].astype(f32)
    q, k, v = (qkv[..., i * DN_WIDTH:(i + 1) * DN_WIDTH].reshape(b, s, DN_HEADS, DN_HEAD_DIM).astype(f32)
               for i in range(3))
    q = l2norm(q)
    k = l2norm(k)
    beta = jax.nn.sigmoid(bl)
    g = -jnp.exp(a_log.astype(f32)) * jax.nn.softplus(a + dt_bias.astype(f32))
    o = gated_delta_rule(q, k, v, g, beta)
    o = o * lax.rsqrt(jnp.mean(o * o, -1, keepdims=True) + 1e-6) * norm_w.astype(f32)
    o = o * jax.nn.silu(z)
    return o.reshape(b, s, DN_WIDTH).astype(x.dtype) @ w_out


def rope_tables(s):
    half = ROT_DIM // 2
    inv_freq = ROPE_THETA ** (-jnp.arange(half, dtype=f32) * 2.0 / ROT_DIM)
    ang = jnp.arange(s, dtype=f32)[:, None] * inv_freq[None, :]
    return jnp.cos(ang), jnp.sin(ang)


def partial_rope(t, cos, sin):
    half = ROT_DIM // 2
    c = cos[:, None, None, :]
    sn = sin[:, None, None, :]
    tf = t.astype(f32)
    x1 = tf[..., :half]
    x2 = tf[..., half:ROT_DIM]
    out = jnp.concatenate([x1 * c - x2 * sn, x2 * c + x1 * sn, tf[..., ROT_DIM:]], axis=-1)
    return out.astype(t.dtype)


def dilated_window_attention(q, k, v, dilation, window):
    b, s, h, dh = q.shape
    steps = window // dilation
    n = s // dilation
    nb = -(-n // ATTN_BLOCK)
    pad = nb * ATTN_BLOCK - n

    def to_sub(t):
        return jnp.transpose(t.reshape(b, n, dilation, h, dh), (0, 2, 1, 3, 4)).reshape(b * dilation, n, h, dh)

    qs = jnp.pad(to_sub(q), ((0, 0), (0, pad), (0, 0), (0, 0))).reshape(b * dilation, nb, ATTN_BLOCK, h, dh)

    def kv_blocks(t):
        tp = jnp.pad(to_sub(t), ((0, 0), (ATTN_BLOCK, pad), (0, 0), (0, 0)))
        tp = tp.reshape(b * dilation, nb + 1, ATTN_BLOCK, h, dh)
        return jnp.concatenate([tp[:, :-1], tp[:, 1:]], axis=2)

    kw = kv_blocks(k)
    vw = kv_blocks(v)
    sc = jnp.einsum('znqhd,znkhd->znhqk', qs, kw, preferred_element_type=f32) * (dh ** -0.5)
    qi = jnp.arange(ATTN_BLOCK)[:, None]
    kj = jnp.arange(2 * ATTN_BLOCK)[None, :]
    dist = qi + ATTN_BLOCK - kj
    key_pos = jnp.arange(nb)[:, None, None] * ATTN_BLOCK + kj[None] - ATTN_BLOCK
    mask = (dist >= 0)[None] & (dist <= steps)[None] & (key_pos >= 0)
    sc = jnp.where(mask[None, :, None], sc, -jnp.inf)
    mx = jnp.max(sc, -1, keepdims=True)
    p = jnp.exp(sc - mx)
    den = jnp.sum(p, -1)
    o = jnp.einsum('znhqk,znkhd->znqhd', p, vw.astype(f32)) / jnp.moveaxis(den, 2, 3)[..., None]
    lse = jnp.moveaxis(mx[..., 0] + jnp.log(den), 2, 3)

    def from_sub(t):
        t = t.reshape(b, dilation, nb * ATTN_BLOCK, *t.shape[3:])[:, :, :n]
        return jnp.swapaxes(t, 1, 2).reshape(b, s, *t.shape[3:])

    return from_sub(o), from_sub(lse)


def dilated_mixer(x, w_q, k_sh, v_sh, w_out, cos, sin):
    b, s, _ = x.shape
    q = partial_rope((x @ w_q).reshape(b, s, N_GROUPS, B_HEADS, B_HEAD_DIM), cos, sin)
    outs, lses = [], []
    for gi, (window, dilation) in enumerate(ATTN_GROUPS):
        o, l = dilated_window_attention(q[:, :, gi], k_sh[:, :, gi], v_sh[:, :, gi], dilation, window)
        outs.append(o)
        lses.append(l)
    wts = jax.nn.softmax(jnp.stack(lses, 0), axis=0)
    o = jnp.sum(wts[..., None] * jnp.stack(outs, 0), axis=0)
    return o.reshape(b, s, B_WIDTH).astype(x.dtype) @ w_out


def peer_ffn(x, w_query, sub_keys, u_tab, v_tab):
    b, s, d = x.shape
    t_blk = PEER_TOK_BLOCK
    xt = x.reshape(b * s // t_blk, t_blk, d)

    def block(xb):
        qh = (xb @ w_query).reshape(t_blk, PEER_HEADS, 2, PEER_QDIM // 2)
        sc = jnp.einsum('thcd,hckd->thck', qh, sub_keys, preferred_element_type=f32)
        top_s, top_i = lax.top_k(sc, PEER_TOPK)
        cand_s = (top_s[:, :, 0, :, None] + top_s[:, :, 1, None, :]).reshape(t_blk, PEER_HEADS, PEER_TOPK * PEER_TOPK)
        cand_i = (top_i[:, :, 0, :, None] * PEER_NKEYS + top_i[:, :, 1, None, :]).reshape(t_blk, PEER_HEADS, PEER_TOPK * PEER_TOPK)
        best_s, pos = lax.top_k(cand_s, PEER_TOPK)
        idx = jnp.take_along_axis(cand_i, pos, axis=-1)
        gate = jax.nn.softmax(best_s, axis=-1)
        u = jnp.take(u_tab, idx, axis=0)
        act = jax.nn.gelu(jnp.einsum('thkd,td->thk', u, xb, preferred_element_type=f32), approximate=False)
        vsel = jnp.take(v_tab, idx, axis=0)
        return jnp.einsum('thk,thkd->td', (gate * act).astype(xb.dtype), vsel)

    return lax.map(block, xt).reshape(b, s, d)


def setup_inputs(seed: int = 0) -> dict:
    key = jax.random.key(seed)
    ks = jax.random.split(key, 20)

    def nrm(k, shape, std):
        return jax.random.normal(k, shape, f32) * std

    dt = jnp.exp(jax.random.uniform(ks[4], (N_A_LAYERS, DN_HEADS), f32, math.log(1e-3), math.log(1e-1)))
    return {
        'x': nrm(ks[0], (BATCH, SEQ, D_MODEL), 1.0),
        'dn_w_in': nrm(ks[1], (N_A_LAYERS, D_MODEL, 4 * DN_WIDTH + 2 * DN_HEADS), D_MODEL ** -0.5),
        'dn_conv': nrm(ks[2], (N_A_LAYERS, CONV_K, 3 * DN_WIDTH), CONV_K ** -0.5),
        'dn_a_log': jnp.log(jax.random.uniform(ks[3], (N_A_LAYERS, DN_HEADS), f32, 1.0, 16.0)),
        'dn_dt_bias': dt + jnp.log(-jnp.expm1(-dt)),
        'dn_norm_w': 1.0 + nrm(ks[5], (N_A_LAYERS, DN_HEAD_DIM), 0.02),
        'dn_w_out': nrm(ks[6], (N_A_LAYERS, DN_WIDTH, D_MODEL), BETA * DN_WIDTH ** -0.5),
        'shared_w_kv': nrm(ks[7], (D_MODEL, 2 * N_GROUPS * B_WIDTH), D_MODEL ** -0.5),
        'attn_w_q': nrm(ks[8], (N_B_LAYERS, D_MODEL, N_GROUPS * B_WIDTH), D_MODEL ** -0.5),
        'attn_w_out': nrm(ks[9], (N_B_LAYERS, B_WIDTH, D_MODEL), BETA * B_WIDTH ** -0.5),
        'peer_w_query': nrm(ks[10], (DEPTH, D_MODEL, PEER_HEADS * PEER_QDIM), D_MODEL ** -0.5),
        'peer_sub_keys': nrm(ks[11], (DEPTH, PEER_HEADS, 2, PEER_NKEYS, PEER_QDIM // 2), (PEER_QDIM // 2) ** -0.5),
        'peer_u': nrm(ks[12], (DEPTH, PEER_EXPERTS, D_MODEL), D_MODEL ** -0.5),
        'peer_v': nrm(ks[13], (DEPTH, PEER_EXPERTS, D_MODEL), BETA),
        'ln_mix_g': 1.0 + nrm(ks[14], (DEPTH, D_MODEL), 0.02),
        'ln_mix_b': nrm(ks[15], (DEPTH, D_MODEL), 0.02),
        'ln_ffn_g': 1.0 + nrm(ks[16], (DEPTH, D_MODEL), 0.02),
        'ln_ffn_b': nrm(ks[17], (DEPTH, D_MODEL), 0.02),
    }


def reference(x, dn_w_in, dn_conv, dn_a_log, dn_dt_bias, dn_norm_w, dn_w_out, shared_w_kv, attn_w_q, attn_w_out,
              peer_w_query, peer_sub_keys, peer_u, peer_v, ln_mix_g, ln_mix_b, ln_ffn_g, ln_ffn_b):
    b, s, _ = x.shape
    cos, sin = rope_tables(s)
    k_sh = None
    v_sh = None
    for layer in range(DEPTH):
        if layer < N_A_LAYERS:
            i = layer
            h = deltanet_mixer(x, dn_w_in[i], dn_conv[i], dn_a_log[i], dn_dt_bias[i], dn_norm_w[i], dn_w_out[i])
        else:
            j = layer - N_A_LAYERS
            if j == 0:
                kv = (x @ shared_w_kv).reshape(b, s, 2, N_GROUPS, B_HEADS, B_HEAD_DIM)
                k_sh = partial_rope(kv[:, :, 0], cos, sin)
                v_sh = kv[:, :, 1]
            h = dilated_mixer(x, attn_w_q[j], k_sh, v_sh, attn_w_out[j], cos, sin)
        x = layer_norm(ALPHA * x + h, ln_mix_g[layer], ln_mix_b[layer])
        f = peer_ffn(x, peer_w_query[layer], peer_sub_keys[layer], peer_u[layer], peer_v[layer])
        x = layer_norm(ALPHA * x + f, ln_ffn_g[layer], ln_ffn_b[layer])
    return x
```

```python
import functools
import math

import jax
import jax.numpy as jnp
from jax import lax
from jax.experimental import pallas as pl
from jax.experimental.pallas import tpu as pltpu

f32 = jnp.float32
bf16 = jnp.bfloat16
i32 = jnp.int32

D_MODEL = 1024
DEPTH = 2
ALPHA = (2.0 * DEPTH) ** 0.25
LN_EPS = 1e-5

HEADS = 8
HEAD_DIM = 128
WIDTH = HEADS * HEAD_DIM
CONV_K = 4
DN_CHUNK = 64

ATTN_GROUPS = ((128, 1), (512, 4), (2048, 16))
N_GROUPS = len(ATTN_GROUPS)
ATTN_BLOCK = 128
ROT_DIM = HEAD_DIM // 4
ROPE_THETA = 500000.0

PEER_HEADS = 8
PEER_NKEYS = 128
PEER_TOPK = 16
PEER_QDIM = 256
PEER_SEL = PEER_HEADS * PEER_TOPK

LANES = 128
SUBLANES = 8
VMEM_LIMIT = 48 * 1024 * 1024


def _params(sem):
    return pltpu.CompilerParams(dimension_semantics=sem, vmem_limit_bytes=VMEM_LIMIT)


def _dot(a, b):
    return jnp.dot(a.astype(bf16), b.astype(bf16), preferred_element_type=f32)


def _dot_nt(a, b):
    return lax.dot_general(a.astype(bf16), b.astype(bf16), (((1,), (1,)), ((), ())), preferred_element_type=f32)


def _dot_tn(a, b):
    return lax.dot_general(a.astype(bf16), b.astype(bf16), (((0,), (0,)), ((), ())), preferred_element_type=f32)


def _sigmoid(x):
    return 1.0 / (1.0 + jnp.exp(-x))


def _layer_norm(y, g, b):
    mu = jnp.mean(y, -1, keepdims=True)
    yc = y - mu
    var = jnp.mean(yc * yc, -1, keepdims=True)
    return yc * lax.rsqrt(var + LN_EPS) * g + b


def _mm_kernel(a_ref, w_ref, o_ref):
    o_ref[...] = _dot(a_ref[...], w_ref[...])


def _matmul(a, w, tm=512, tn=None):
    m, k = a.shape
    n = w.shape[1]
    tn = tn or n
    return pl.pallas_call(
        _mm_kernel,
        grid=(m // tm, n // tn),
        in_specs=[pl.BlockSpec((tm, k), lambda i, j: (i, 0)), pl.BlockSpec((k, tn), lambda i, j: (0, j))],
        out_specs=pl.BlockSpec((tm, tn), lambda i, j: (i, j)),
        out_shape=jax.ShapeDtypeStruct((m, n), f32),
        compiler_params=_params(("parallel", "parallel")),
        name="matmul",
    )(a, w)


def _rope(y, c, s_lo, s_hi):
    return y * c + pltpu.roll(y, ROT_DIM // 2, axis=1) * s_hi + pltpu.roll(y, LANES - ROT_DIM // 2, axis=1) * s_lo


def _mm_rope_kernel(a_ref, w_ref, c_ref, slo_ref, shi_ref, o_ref, *, heads):
    y = _dot(a_ref[...], w_ref[...])
    c, s_lo, s_hi = c_ref[...], slo_ref[...], shi_ref[...]
    for h in range(heads):
        sl = slice(h * HEAD_DIM, (h + 1) * HEAD_DIM)
        o_ref[:, sl] = _rope(y[:, sl], c, s_lo, s_hi)


def _matmul_rope(a, w, tabs, seq, tm=512, tn=1024):
    m, k = a.shape
    n = w.shape[1]
    nsb = seq // tm
    tab_spec = pl.BlockSpec((tm, LANES), lambda i, j: (i % nsb, 0))
    return pl.pallas_call(
        functools.partial(_mm_rope_kernel, heads=tn // HEAD_DIM),
        grid=(m // tm, n // tn),
        in_specs=[pl.BlockSpec((tm, k), lambda i, j: (i, 0)), pl.BlockSpec((k, tn), lambda i, j: (0, j)),
                  tab_spec, tab_spec, tab_spec],
        out_specs=pl.BlockSpec((tm, tn), lambda i, j: (i, j)),
        out_shape=jax.ShapeDtypeStruct((m, n), f32),
        compiler_params=_params(("parallel", "parallel")),
        name="matmul_rope",
    )(a, w, *tabs)


def _mm_res_ln_kernel(a_ref, w_ref, x_ref, g_ref, b_ref, o_ref):
    h = _dot(a_ref[...], w_ref[...])
    o_ref[...] = _layer_norm(ALPHA * x_ref[...] + h, g_ref[...], b_ref[...])


def _matmul_res_ln(a, w, x, g, b, tm=512):
    m, k = a.shape
    n = w.shape[1]
    row = pl.BlockSpec((1, n), lambda i: (0, 0))
    return pl.pallas_call(
        _mm_res_ln_kernel,
        grid=(m // tm,),
        in_specs=[pl.BlockSpec((tm, k), lambda i: (i, 0)), pl.BlockSpec((k, n), lambda i: (0, 0)),
                  pl.BlockSpec((tm, n), lambda i: (i, 0)), row, row],
        out_specs=pl.BlockSpec((tm, n), lambda i: (i, 0)),
        out_shape=jax.ShapeDtypeStruct((m, n), f32),
        compiler_params=_params(("parallel",)),
        name="matmul_res_ln",
    )(a, w, x, g.reshape(1, n), b.reshape(1, n))


DN_PROJ_PAD = 4 * WIDTH + LANES


def _dn_prep_kernel(proj_ref, halo_ref, conv_ref, alog_ref, dtb_ref, q_ref, k_ref, v_ref, gate_ref, *, tm, nsb):
    i = pl.program_id(0)
    x = proj_ref[:, : 3 * WIDTH]
    halo = jnp.where(i % nsb == 0, 0.0, halo_ref[...])
    w = conv_ref[...]
    row8 = lax.broadcasted_iota(i32, (SUBLANES, 1), 0)
    acc = x * w[CONV_K - 1:CONV_K]
    for s in range(1, CONV_K):
        xs = pltpu.roll(x, s, axis=0)
        hs = pltpu.roll(halo, s, axis=0)
        first = jnp.where(row8 < s, hs, xs[:SUBLANES])
        xs = jnp.concatenate([first, xs[SUBLANES:]], axis=0)
        acc = acc + xs * w[CONV_K - 1 - s:CONV_K - s]
    qkv = acc * _sigmoid(acc)
    for h in range(HEADS):
        sl = slice(h * HEAD_DIM, (h + 1) * HEAD_DIM)
        qh = qkv[:, sl]
        q_ref[:, sl] = qh * lax.rsqrt(jnp.sum(qh * qh, -1, keepdims=True) + 1e-6) * (HEAD_DIM ** -0.5)
        kh = qkv[:, WIDTH + h * HEAD_DIM: WIDTH + (h + 1) * HEAD_DIM]
        k_ref[:, sl] = kh * lax.rsqrt(jnp.sum(kh * kh, -1, keepdims=True) + 1e-6)
    v_ref[...] = qkv[:, 2 * WIDTH:]
    ab = proj_ref[:, 4 * WIDTH:]
    z = ab + dtb_ref[...]
    softplus = jnp.maximum(z, 0.0) + jnp.log(1.0 + jnp.exp(-jnp.abs(z)))
    g = -jnp.exp(alog_ref[...]) * softplus
    row = lax.broadcasted_iota(i32, (tm, 1), 0) % DN_CHUNK
    s = 1
    while s < DN_CHUNK:
        g = g + jnp.where(row >= s, pltpu.roll(g, s, axis=0), 0.0)
        s *= 2
    lane = lax.broadcasted_iota(i32, (1, LANES), 1)
    gate_ref[...] = jnp.where(lane < HEADS, g, _sigmoid(ab))


def _dn_prep(proj, conv_w, a_log, dt_bias, seq, tm=256):
    m = proj.shape[0]
    nsb = seq // tm
    pad = LANES - HEADS
    alog = jnp.pad(a_log.astype(f32), (0, pad)).reshape(1, LANES)
    dtb = jnp.pad(dt_bias.astype(f32), (0, pad)).reshape(1, LANES)
    hb = tm // SUBLANES
    out_w = pl.BlockSpec((tm, WIDTH), lambda i: (i, 0))
    return pl.pallas_call(
        functools.partial(_dn_prep_kernel, tm=tm, nsb=nsb),
        grid=(m // tm,),
        in_specs=[pl.BlockSpec((tm, DN_PROJ_PAD), lambda i: (i, 0)),
                  pl.BlockSpec((SUBLANES, 3 * WIDTH), lambda i: (jnp.maximum(i * hb - 1, 0), 0)),
                  pl.BlockSpec((CONV_K, 3 * WIDTH), lambda i: (0, 0)),
                  pl.BlockSpec((1, LANES), lambda i: (0, 0)), pl.BlockSpec((1, LANES), lambda i: (0, 0))],
        out_specs=[out_w, out_w, out_w, pl.BlockSpec((tm, LANES), lambda i: (i, 0))],
        out_shape=[jax.ShapeDtypeStruct((m, WIDTH), f32)] * 3 + [jax.ShapeDtypeStruct((m, LANES), f32)],
        compiler_params=_params(("parallel",)),
        name="dn_prep",
    )(proj, proj, conv_w, alog, dtb)


def _dn_delta_kernel(q_ref, k_ref, v_ref, z_ref, gate_ref, gt_ref, nw_ref, o_ref, state_ref, *, chunks):
    @pl.when(pl.program_id(1) == 0)
    def _():
        state_ref[...] = jnp.zeros_like(state_ref)

    c = DN_CHUNK
    ri = lax.broadcasted_iota(i32, (c, c), 0)
    ci = lax.broadcasted_iota(i32, (c, c), 1)
    causal = ri >= ci
    strict = ri > ci
    eye = (ri == ci).astype(f32)
    nw = nw_ref[...]
    for ch in range(chunks):
        rows = slice(ch * c, (ch + 1) * c)
        for h in range(HEADS):
            sl = slice(h * HEAD_DIM, (h + 1) * HEAD_DIM)
            q, k, v = q_ref[rows, sl], k_ref[rows, sl], v_ref[rows, sl]
            gcol = gate_ref[rows, h:h + 1]
            beta = gate_ref[rows, HEADS + h:HEADS + h + 1]
            grow = gt_ref[h:h + 1, rows]
            diff = gcol - grow
            decay = jnp.where(causal, jnp.exp(jnp.where(causal, diff, 0.0)), 0.0)
            kb = k * beta
            m = jnp.where(strict, _dot_nt(kb, k) * decay, 0.0)
            inv = eye - m
            p = m
            step = 2
            while step < c:
                p = _dot(p, p)
                inv = inv + _dot(inv, p)
                step *= 2
            egc = jnp.exp(gcol)
            u = _dot(inv, v * beta)
            w = _dot(inv, kb * egc)
            a_qk = _dot_nt(q, k) * decay
            st = state_ref[h]
            v_new = u - _dot(w, st)
            o = _dot(q * egc, st) + _dot(a_qk, v_new)
            glast = gcol[c - 1:c, :]
            state_ref[h] = st * jnp.exp(glast) + _dot_tn(k * jnp.exp(glast - gcol), v_new)
            o = o * lax.rsqrt(jnp.mean(o * o, -1, keepdims=True) + 1e-6) * nw
            zz = z_ref[rows, sl]
            o_ref[rows, sl] = o * (zz * _sigmoid(zz))


def _dn_delta(q, k, v, proj, gates, gates_t, norm_w, batch, seq, cb=128):
    nb = seq // cb
    wide = pl.BlockSpec((cb, WIDTH), lambda b, i: (b * nb + i, 0))
    return pl.pallas_call(
        functools.partial(_dn_delta_kernel, chunks=cb // DN_CHUNK),
        grid=(batch, nb),
        in_specs=[wide, wide, wide,
                  pl.BlockSpec((cb, WIDTH), lambda b, i: (b * nb + i, 3)),
                  pl.BlockSpec((cb, LANES), lambda b, i: (b * nb + i, 0)),
                  pl.BlockSpec((SUBLANES, cb), lambda b, i: (0, b * nb + i)),
                  pl.BlockSpec((1, HEAD_DIM), lambda b, i: (0, 0))],
        out_specs=wide,
        out_shape=jax.ShapeDtypeStruct((batch * seq, WIDTH), f32),
        scratch_shapes=[pltpu.VMEM((HEADS, HEAD_DIM, HEAD_DIM), f32)],
        compiler_params=_params(("parallel", "arbitrary")),
        name="dn_delta",
    )(q, k, v, proj, gates, gates_t, norm_w.reshape(1, HEAD_DIM).astype(f32))


def _deltanet_mixer(x2, w_in, conv_w, a_log, dt_bias, norm_w, batch, seq):
    w_pad = jnp.pad(w_in, ((0, 0), (0, DN_PROJ_PAD - w_in.shape[1]))).astype(bf16)
    proj = _matmul(x2, w_pad, tn=DN_PROJ_PAD // 3)
    q, k, v, gates = _dn_prep(proj, conv_w, a_log, dt_bias, seq)
    gates_t = gates[:, :SUBLANES].T
    return _dn_delta(q, k, v, proj, gates, gates_t, norm_w, batch, seq)


def _attn_kernel(q_ref, kp_ref, kc_ref, vp_ref, vc_ref, o_ref, l_ref, *, steps):
    nbi = pl.program_id(2)
    blk = ATTN_BLOCK
    qi = lax.broadcasted_iota(i32, (blk, 2 * blk), 0)
    kj = lax.broadcasted_iota(i32, (blk, 2 * blk), 1)
    dist = qi + blk - kj
    mask = (dist >= 0) & (dist <= steps) & ((kj >= blk) | (nbi > 0))
    scale = HEAD_DIM ** -0.5
    for h in range(HEADS):
        sl = slice(h * HEAD_DIM, (h + 1) * HEAD_DIM)
        q = q_ref[:, sl]
        kk = jnp.concatenate([kp_ref[:, sl], kc_ref[:, sl]], axis=0)
        vv = jnp.concatenate([vp_ref[:, sl], vc_ref[:, sl]], axis=0)
        sc = jnp.where(mask, _dot_nt(q, kk) * scale, -jnp.inf)
        mx = jnp.max(sc, -1, keepdims=True)
        p = jnp.exp(sc - mx)
        den = jnp.sum(p, -1, keepdims=True)
        o_ref[:, sl] = _dot(p, vv) / den
        l_ref[:, sl] = jnp.broadcast_to(mx + jnp.log(den), (blk, HEAD_DIM))


def _dilated_attention(q, k, v, group, window, dilation, batch, seq):
    n = seq // dilation
    nb = n // ATTN_BLOCK
    cols = dilation * N_GROUPS * WIDTH

    def view(t):
        return t.reshape(batch, n, cols)

    def cur(b, r, i):
        return (b, i, r * N_GROUPS + group)

    def prev(b, r, i):
        return (b, jnp.maximum(i - 1, 0), r * N_GROUPS + group)

    blk = (None, ATTN_BLOCK, WIDTH)
    out_cols = dilation * WIDTH
    out_spec = pl.BlockSpec(blk, lambda b, r, i: (b, i, r))
    o, lse = pl.pallas_call(
        functools.partial(_attn_kernel, steps=window // dilation),
        grid=(batch, dilation, nb),
        in_specs=[pl.BlockSpec(blk, cur), pl.BlockSpec(blk, prev), pl.BlockSpec(blk, cur),
                  pl.BlockSpec(blk, prev), pl.BlockSpec(blk, cur)],
        out_specs=[out_spec, out_spec],
        out_shape=[jax.ShapeDtypeStruct((batch, n, out_cols), f32)] * 2,
        compiler_params=_params(("parallel", "parallel", "parallel")),
        name=f"dilated_attn_d{dilation}",
    )(view(q), view(k), view(k), view(v), view(v))
    return o.reshape(batch * seq, WIDTH), lse.reshape(batch * seq, WIDTH)


def _combine_out_ln_kernel(o0, o1, o2, l0, l1, l2, w_ref, x_ref, g_ref, b_ref, out_ref):
    ls = [l0[...], l1[...], l2[...]]
    mx = jnp.maximum(jnp.maximum(ls[0], ls[1]), ls[2])
    es = [jnp.exp(l - mx) for l in ls]
    o = (es[0] * o0[...] + es[1] * o1[...] + es[2] * o2[...]) / (es[0] + es[1] + es[2])
    h = _dot(o, w_ref[...])
    out_ref[...] = _layer_norm(ALPHA * x_ref[...] + h, g_ref[...], b_ref[...])


def _combine_out_ln(outs, lses, w, x, g, b, tm=256):
    m, n = x.shape
    blk = pl.BlockSpec((tm, n), lambda i: (i, 0))
    row = pl.BlockSpec((1, n), lambda i: (0, 0))
    return pl.pallas_call(
        _combine_out_ln_kernel,
        grid=(m // tm,),
        in_specs=[blk] * 6 + [pl.BlockSpec((WIDTH, n), lambda i: (0, 0)), blk, row, row],
        out_specs=blk,
        out_shape=jax.ShapeDtypeStruct((m, n), f32),
        compiler_params=_params(("parallel",)),
        name="attn_combine_out_ln",
    )(*outs, *lses, w, x, g.reshape(1, n), b.reshape(1, n))


def _rope_lane_tables(seq):
    half = ROT_DIM // 2
    inv_freq = ROPE_THETA ** (-jnp.arange(half, dtype=f32) * 2.0 / ROT_DIM)
    ang = jnp.arange(seq, dtype=f32)[:, None] * inv_freq[None, :]
    cos, sin = jnp.cos(ang), jnp.sin(ang)
    ones = jnp.ones((seq, HEAD_DIM - ROT_DIM), f32)
    zeros = jnp.zeros((seq, HEAD_DIM - half), f32)
    c = jnp.concatenate([cos, cos, ones], axis=1)
    s_lo = jnp.concatenate([-sin, zeros], axis=1)
    s_hi = jnp.concatenate([jnp.zeros((seq, half), f32), sin, jnp.zeros((seq, HEAD_DIM - ROT_DIM), f32)], axis=1)
    return c, s_lo, s_hi


def _top_rows(s, payload, k):
    r = s.shape[0]
    rowid = lax.broadcasted_iota(i32, s.shape, 0)
    vals, pays = [], []
    for _ in range(k):
        m = jnp.max(s, axis=0, keepdims=True)
        pos = jnp.min(jnp.where(s == m, rowid, r), axis=0, keepdims=True)
        sel = rowid == pos
        pays.append(jnp.max(jnp.where(sel, payload, -1), axis=0, keepdims=True))
        vals.append(m)
        s = jnp.where(sel, -jnp.inf, s)
    return jnp.concatenate(vals, axis=0), jnp.concatenate(pays, axis=0)


def _peer_route_kernel(qp_ref, keys_ref, idx_ref, gate_ref):
    kt = PEER_TOPK
    half = PEER_QDIM // 2
    keyid = lax.broadcasted_iota(i32, (PEER_NKEYS, LANES), 0)
    idx_rows, gate_rows = [], []
    for h in range(PEER_HEADS):
        tops = []
        for c in range(2):
            col = (h * 2 + c) * half
            s = _dot_nt(keys_ref[h * 2 + c], qp_ref[:, col:col + half])
            tops.append(_top_rows(s, keyid, kt))
        (v1, i1), (v2, i2) = tops
        rep = lambda t: jnp.concatenate([jnp.broadcast_to(t[a:a + 1], (kt, LANES)) for a in range(kt)], axis=0)
        til = lambda t: jnp.concatenate([t] * kt, axis=0)
        cand = rep(v1) + til(v2)
        cand_id = rep(i1) * PEER_NKEYS + til(i2)
        best, ids = _top_rows(cand, cand_id, kt)
        e = jnp.exp(best - best[0:1])
        gate_rows.append(e / jnp.sum(e, axis=0, keepdims=True))
        idx_rows.append(ids)
    idx_ref[...] = jnp.concatenate(idx_rows, axis=0).astype(f32).T.astype(i32)
    gate_ref[...] = jnp.concatenate(gate_rows, axis=0).T


def _peer_route(qp, sub_keys):
    m = qp.shape[0]
    tb = LANES
    keys = sub_keys.reshape(PEER_HEADS * 2, PEER_NKEYS, PEER_QDIM // 2).astype(bf16)
    out = pl.BlockSpec((tb, PEER_SEL), lambda i: (i, 0))
    return pl.pallas_call(
        _peer_route_kernel,
        grid=(m // tb,),
        in_specs=[pl.BlockSpec((tb, qp.shape[1]), lambda i: (i, 0)),
                  pl.BlockSpec(keys.shape, lambda i: (0, 0, 0))],
        out_specs=[out, out],
        out_shape=[jax.ShapeDtypeStruct((m, PEER_SEL), i32), jax.ShapeDtypeStruct((m, PEER_SEL), f32)],
        compiler_params=_params(("parallel",)),
        name="peer_route",
    )(qp, keys)


PEER_TB = 8


def _gather_rows(tab_hbm, idx_ref, buf, sem):
    def one_token(t, carry):
        def one_row(j, carry):
            pltpu.make_async_copy(tab_hbm.at[pl.ds(idx_ref[t, j], 1), :],
                                  buf.at[pl.ds(t * PEER_SEL + j, 1), :], sem).start()
            return carry
        return lax.fori_loop(0, PEER_SEL, one_row, carry, unroll=8)
    lax.fori_loop(0, PEER_TB, one_token, 0)


def _wait_rows(tab_hbm, buf, sem):
    pltpu.make_async_copy(tab_hbm.at[pl.ds(0, buf.shape[0]), :], buf, sem).wait()


def _peer_expert_kernel(idx_ref, x_ref, gate_ref, g_ref, b_ref, u_hbm, v_hbm, o_ref, ubuf, vbuf, sems):
    _gather_rows(u_hbm, idx_ref, ubuf, sems.at[0])
    _gather_rows(v_hbm, idx_ref, vbuf, sems.at[1])
    x = x_ref[...]
    rowid = lax.broadcasted_iota(i32, (PEER_TB, 1), 0)
    _wait_rows(u_hbm, ubuf, sems.at[0])
    pre = jnp.zeros((PEER_TB, PEER_SEL), f32)
    for t in range(PEER_TB):
        pre = jnp.where(rowid == t, _dot_nt(x, ubuf[t * PEER_SEL:(t + 1) * PEER_SEL, :]), pre)
    act = 0.5 * pre * (1.0 + lax.erf(pre * (2.0 ** -0.5)))
    w = gate_ref[...] * act
    _wait_rows(v_hbm, vbuf, sems.at[1])
    f = jnp.zeros_like(x)
    for t in range(PEER_TB):
        f = jnp.where(rowid == t, _dot(w, vbuf[t * PEER_SEL:(t + 1) * PEER_SEL, :]), f)
    o_ref[...] = _layer_norm(ALPHA * x + f, g_ref[...], b_ref[...])


def _peer_expert_ln(x, idx, gate, u_tab, v_tab, g, b):
    m, d = x.shape
    tb = PEER_TB
    row = pl.BlockSpec((1, d), lambda i: (0, 0))
    return pl.pallas_call(
        _peer_expert_kernel,
        grid=(m // tb,),
        in_specs=[pl.BlockSpec((tb, PEER_SEL), lambda i: (i, 0), memory_space=pltpu.SMEM),
                  pl.BlockSpec((tb, d), lambda i: (i, 0)),
                  pl.BlockSpec((tb, PEER_SEL), lambda i: (i, 0)),
                  row, row,
                  pl.BlockSpec(memory_space=pl.ANY), pl.BlockSpec(memory_space=pl.ANY)],
        out_specs=pl.BlockSpec((tb, d), lambda i: (i, 0)),
        out_shape=jax.ShapeDtypeStruct((m, d), f32),
        scratch_shapes=[pltpu.VMEM((tb * PEER_SEL, d), f32), pltpu.VMEM((tb * PEER_SEL, d), f32),
                        pltpu.SemaphoreType.DMA((2,))],
        compiler_params=_params(("arbitrary",)),
        name="peer_expert_ln",
    )(idx, x, gate, g.reshape(1, d), b.reshape(1, d), u_tab, v_tab)


def _peer_ffn_ln(x2, w_query, sub_keys, u_tab, v_tab, g, b):
    qp = _matmul(x2, w_query.astype(bf16))
    idx, gate = _peer_route(qp, sub_keys)
    return _peer_expert_ln(x2, idx, gate, u_tab, v_tab, g, b)


def kernel(x, dn_w_in, dn_conv, dn_a_log, dn_dt_bias, dn_norm_w, dn_w_out, shared_w_kv, attn_w_q, attn_w_out,
           peer_w_query, peer_sub_keys, peer_u, peer_v, ln_mix_g, ln_mix_b, ln_ffn_g, ln_ffn_b):
    batch, seq, d = x.shape
    x2 = x.reshape(batch * seq, d)
    n_a = DEPTH // 2
    tabs = _rope_lane_tables(seq)
    k_sh = v_sh = None
    for layer in range(DEPTH):
        if layer < n_a:
            o = _deltanet_mixer(x2, dn_w_in[layer], dn_conv[layer], dn_a_log[layer], dn_dt_bias[layer],
                                dn_norm_w[layer], batch, seq)
            x2 = _matmul_res_ln(o, dn_w_out[layer].astype(bf16), x2, ln_mix_g[layer], ln_mix_b[layer])
        else:
            j = layer - n_a
            if j == 0:
                gw = N_GROUPS * WIDTH
                k_sh = _matmul_rope(x2, shared_w_kv[:, :gw].astype(bf16), tabs, seq)
                v_sh = _matmul(x2, shared_w_kv[:, gw:].astype(bf16), tn=1024)
            q = _matmul_rope(x2, attn_w_q[j].astype(bf16), tabs, seq)
            outs, lses = [], []
            for gi, (window, dilation) in enumerate(ATTN_GROUPS):
                o, l = _dilated_attention(q, k_sh, v_sh, gi, window, dilation, batch, seq)
                outs.append(o)
                lses.append(l)
            x2 = _combine_out_ln(outs, lses, attn_w_out[j].astype(bf16), x2, ln_mix_g[layer], ln_mix_b[layer])
        x2 = _peer_ffn_ln(x2, peer_w_query[layer], peer_sub_keys[layer], peer_u[layer], peer_v[layer],
                          ln_ffn_g[layer], ln_ffn_b[layer])
    return x2.reshape(batch, seq, d)
```

```python
import functools
import math

import jax
import jax.numpy as jnp
from jax import lax
from jax.experimental import pallas as pl
from jax.experimental.pallas import tpu as pltpu
from jax.experimental.pallas import tpu_sc as plsc

f32 = jnp.float32
bf16 = jnp.bfloat16
i32 = jnp.int32

D_MODEL = 1024
DEPTH = 2
ALPHA = (2.0 * DEPTH) ** 0.25
LN_EPS = 1e-5

HEADS = 8
HEAD_DIM = 128
WIDTH = HEADS * HEAD_DIM
CONV_K = 4
DN_CHUNK = 64

ATTN_GROUPS = ((128, 1), (512, 4), (2048, 16))
N_GROUPS = len(ATTN_GROUPS)
ATTN_BLOCK = 128
ROT_DIM = HEAD_DIM // 4
ROPE_THETA = 500000.0

PEER_HEADS = 8
PEER_NKEYS = 128
PEER_TOPK = 16
PEER_QDIM = 256
PEER_SEL = PEER_HEADS * PEER_TOPK

LANES = 128
SUBLANES = 8
VMEM_LIMIT = 48 * 1024 * 1024


def _params(sem):
    return pltpu.CompilerParams(dimension_semantics=sem, vmem_limit_bytes=VMEM_LIMIT)


def _dot(a, b):
    return jnp.dot(a.astype(bf16), b.astype(bf16), preferred_element_type=f32)


def _dot_nt(a, b):
    return lax.dot_general(a.astype(bf16), b.astype(bf16), (((1,), (1,)), ((), ())), preferred_element_type=f32)


def _dot_tn(a, b):
    return lax.dot_general(a.astype(bf16), b.astype(bf16), (((0,), (0,)), ((), ())), preferred_element_type=f32)


def _sigmoid(x):
    return 1.0 / (1.0 + jnp.exp(-x))


def _layer_norm(y, g, b):
    mu = jnp.mean(y, -1, keepdims=True)
    yc = y - mu
    var = jnp.mean(yc * yc, -1, keepdims=True)
    return yc * lax.rsqrt(var + LN_EPS) * g + b


def _mm_kernel(a_ref, w_ref, o_ref):
    o_ref[...] = _dot(a_ref[...], w_ref[...])


def _matmul(a, w, tm=512, tn=None):
    m, k = a.shape
    n = w.shape[1]
    tn = tn or n
    return pl.pallas_call(
        _mm_kernel,
        grid=(m // tm, n // tn),
        in_specs=[pl.BlockSpec((tm, k), lambda i, j: (i, 0)), pl.BlockSpec((k, tn), lambda i, j: (0, j))],
        out_specs=pl.BlockSpec((tm, tn), lambda i, j: (i, j)),
        out_shape=jax.ShapeDtypeStruct((m, n), f32),
        compiler_params=_params(("parallel", "parallel")),
        name="matmul",
    )(a, w)


def _rope(y, c, s_lo, s_hi):
    return y * c + pltpu.roll(y, ROT_DIM // 2, axis=1) * s_hi + pltpu.roll(y, LANES - ROT_DIM // 2, axis=1) * s_lo


def _mm_rope_kernel(a_ref, w_ref, c_ref, slo_ref, shi_ref, o_ref, *, heads):
    y = _dot(a_ref[...], w_ref[...])
    c, s_lo, s_hi = c_ref[...], slo_ref[...], shi_ref[...]
    for h in range(heads):
        sl = slice(h * HEAD_DIM, (h + 1) * HEAD_DIM)
        o_ref[:, sl] = _rope(y[:, sl], c, s_lo, s_hi)


def _matmul_rope(a, w, tabs, seq, tm=512, tn=1024):
    m, k = a.shape
    n = w.shape[1]
    nsb = seq // tm
    tab_spec = pl.BlockSpec((tm, LANES), lambda i, j: (i % nsb, 0))
    return pl.pallas_call(
        functools.partial(_mm_rope_kernel, heads=tn // HEAD_DIM),
        grid=(m // tm, n // tn),
        in_specs=[pl.BlockSpec((tm, k), lambda i, j: (i, 0)), pl.BlockSpec((k, tn), lambda i, j: (0, j)),
                  tab_spec, tab_spec, tab_spec],
        out_specs=pl.BlockSpec((tm, tn), lambda i, j: (i, j)),
        out_shape=jax.ShapeDtypeStruct((m, n), f32),
        compiler_params=_params(("parallel", "parallel")),
        name="matmul_rope",
    )(a, w, *tabs)


def _mm_res_ln_kernel(a_ref, w_ref, x_ref, g_ref, b_ref, o_ref):
    h = _dot(a_ref[...], w_ref[...])
    o_ref[...] = _layer_norm(ALPHA * x_ref[...] + h, g_ref[...], b_ref[...])


def _matmul_res_ln(a, w, x, g, b, tm=512):
    m, k = a.shape
    n = w.shape[1]
    row = pl.BlockSpec((1, n), lambda i: (0, 0))
    return pl.pallas_call(
        _mm_res_ln_kernel,
        grid=(m // tm,),
        in_specs=[pl.BlockSpec((tm, k), lambda i: (i, 0)), pl.BlockSpec((k, n), lambda i: (0, 0)),
                  pl.BlockSpec((tm, n), lambda i: (i, 0)), row, row],
        out_specs=pl.BlockSpec((tm, n), lambda i: (i, 0)),
        out_shape=jax.ShapeDtypeStruct((m, n), f32),
        compiler_params=_params(("parallel",)),
        name="matmul_res_ln",
    )(a, w, x, g.reshape(1, n), b.reshape(1, n))


DN_PROJ_PAD = 4 * WIDTH + LANES


def _dn_prep_kernel(proj_ref, halo_ref, conv_ref, alog_ref, dtb_ref, q_ref, k_ref, v_ref, gate_ref, *, tm, nsb):
    i = pl.program_id(0)
    x = proj_ref[:, : 3 * WIDTH]
    halo = jnp.where(i % nsb == 0, 0.0, halo_ref[...])
    w = conv_ref[...]
    row8 = lax.broadcasted_iota(i32, (SUBLANES, 1), 0)
    acc = x * w[CONV_K - 1:CONV_K]
    for s in range(1, CONV_K):
        xs = pltpu.roll(x, s, axis=0)
        hs = pltpu.roll(halo, s, axis=0)
        first = jnp.where(row8 < s, hs, xs[:SUBLANES])
        xs = jnp.concatenate([first, xs[SUBLANES:]], axis=0)
        acc = acc + xs * w[CONV_K - 1 - s:CONV_K - s]
    qkv = acc * _sigmoid(acc)
    for h in range(HEADS):
        sl = slice(h * HEAD_DIM, (h + 1) * HEAD_DIM)
        qh = qkv[:, sl]
        q_ref[:, sl] = qh * lax.rsqrt(jnp.sum(qh * qh, -1, keepdims=True) + 1e-6) * (HEAD_DIM ** -0.5)
        kh = qkv[:, WIDTH + h * HEAD_DIM: WIDTH + (h + 1) * HEAD_DIM]
        k_ref[:, sl] = kh * lax.rsqrt(jnp.sum(kh * kh, -1, keepdims=True) + 1e-6)
    v_ref[...] = qkv[:, 2 * WIDTH:]
    ab = proj_ref[:, 4 * WIDTH:]
    z = ab + dtb_ref[...]
    softplus = jnp.maximum(z, 0.0) + jnp.log(1.0 + jnp.exp(-jnp.abs(z)))
    g = -jnp.exp(alog_ref[...]) * softplus
    row = lax.broadcasted_iota(i32, (tm, 1), 0) % DN_CHUNK
    s = 1
    while s < DN_CHUNK:
        g = g + jnp.where(row >= s, pltpu.roll(g, s, axis=0), 0.0)
        s *= 2
    lane = lax.broadcasted_iota(i32, (1, LANES), 1)
    gate_ref[...] = jnp.where(lane < HEADS, g, _sigmoid(ab))


def _dn_prep(proj, conv_w, a_log, dt_bias, seq, tm=256):
    m = proj.shape[0]
    nsb = seq // tm
    pad = LANES - HEADS
    alog = jnp.pad(a_log.astype(f32), (0, pad)).reshape(1, LANES)
    dtb = jnp.pad(dt_bias.astype(f32), (0, pad)).reshape(1, LANES)
    hb = tm // SUBLANES
    out_w = pl.BlockSpec((tm, WIDTH), lambda i: (i, 0))
    return pl.pallas_call(
        functools.partial(_dn_prep_kernel, tm=tm, nsb=nsb),
        grid=(m // tm,),
        in_specs=[pl.BlockSpec((tm, DN_PROJ_PAD), lambda i: (i, 0)),
                  pl.BlockSpec((SUBLANES, 3 * WIDTH), lambda i: (jnp.maximum(i * hb - 1, 0), 0)),
                  pl.BlockSpec((CONV_K, 3 * WIDTH), lambda i: (0, 0)),
                  pl.BlockSpec((1, LANES), lambda i: (0, 0)), pl.BlockSpec((1, LANES), lambda i: (0, 0))],
        out_specs=[out_w, out_w, out_w, pl.BlockSpec((tm, LANES), lambda i: (i, 0))],
        out_shape=[jax.ShapeDtypeStruct((m, WIDTH), f32)] * 3 + [jax.ShapeDtypeStruct((m, LANES), f32)],
        compiler_params=_params(("parallel",)),
        name="dn_prep",
    )(proj, proj, conv_w, alog, dtb)


def _dn_delta_kernel(q_ref, k_ref, v_ref, z_ref, gate_ref, gt_ref, nw_ref, o_ref, state_ref, *, chunks):
    @pl.when(pl.program_id(1) == 0)
    def _():
        state_ref[...] = jnp.zeros_like(state_ref)

    c = DN_CHUNK
    ri = lax.broadcasted_iota(i32, (c, c), 0)
    ci = lax.broadcasted_iota(i32, (c, c), 1)
    causal = ri >= ci
    strict = ri > ci
    eye = (ri == ci).astype(f32)
    nw = nw_ref[...]
    for ch in range(chunks):
        rows = slice(ch * c, (ch + 1) * c)
        for h in range(HEADS):
            sl = slice(h * HEAD_DIM, (h + 1) * HEAD_DIM)
            q, k, v = q_ref[rows, sl], k_ref[rows, sl], v_ref[rows, sl]
            gcol = gate_ref[rows, h:h + 1]
            beta = gate_ref[rows, HEADS + h:HEADS + h + 1]
            grow = gt_ref[h:h + 1, rows]
            diff = gcol - grow
            decay = jnp.where(causal, jnp.exp(jnp.where(causal, diff, 0.0)), 0.0)
            kb = k * beta
            m = jnp.where(strict, _dot_nt(kb, k) * decay, 0.0)
            inv = eye - m
            p = m
            step = 2
            while step < c:
                p = _dot(p, p)
                inv = inv + _dot(inv, p)
                step *= 2
            egc = jnp.exp(gcol)
            u = _dot(inv, v * beta)
            w = _dot(inv, kb * egc)
            a_qk = _dot_nt(q, k) * decay
            st = state_ref[h]
            v_new = u - _dot(w, st)
            o = _dot(q * egc, st) + _dot(a_qk, v_new)
            glast = gcol[c - 1:c, :]
            state_ref[h] = st * jnp.exp(glast) + _dot_tn(k * jnp.exp(glast - gcol), v_new)
            o = o * lax.rsqrt(jnp.mean(o * o, -1, keepdims=True) + 1e-6) * nw
            zz = z_ref[rows, sl]
            o_ref[rows, sl] = o * (zz * _sigmoid(zz))


def _dn_delta(q, k, v, proj, gates, gates_t, norm_w, batch, seq, cb=128):
    nb = seq // cb
    wide = pl.BlockSpec((cb, WIDTH), lambda b, i: (b * nb + i, 0))
    return pl.pallas_call(
        functools.partial(_dn_delta_kernel, chunks=cb // DN_CHUNK),
        grid=(batch, nb),
        in_specs=[wide, wide, wide,
                  pl.BlockSpec((cb, WIDTH), lambda b, i: (b * nb + i, 3)),
                  pl.BlockSpec((cb, LANES), lambda b, i: (b * nb + i, 0)),
                  pl.BlockSpec((SUBLANES, cb), lambda b, i: (0, b * nb + i)),
                  pl.BlockSpec((1, HEAD_DIM), lambda b, i: (0, 0))],
        out_specs=wide,
        out_shape=jax.ShapeDtypeStruct((batch * seq, WIDTH), f32),
        scratch_shapes=[pltpu.VMEM((HEADS, HEAD_DIM, HEAD_DIM), f32)],
        compiler_params=_params(("parallel", "arbitrary")),
        name="dn_delta",
    )(q, k, v, proj, gates, gates_t, norm_w.reshape(1, HEAD_DIM).astype(f32))


def _deltanet_mixer(x2, w_in, conv_w, a_log, dt_bias, norm_w, batch, seq):
    w_pad = jnp.pad(w_in, ((0, 0), (0, DN_PROJ_PAD - w_in.shape[1]))).astype(bf16)
    proj = _matmul(x2, w_pad, tn=DN_PROJ_PAD // 3)
    q, k, v, gates = _dn_prep(proj, conv_w, a_log, dt_bias, seq)
    gates_t = gates[:, :SUBLANES].T
    return _dn_delta(q, k, v, proj, gates, gates_t, norm_w, batch, seq)


def _attn_kernel(q_ref, kp_ref, kc_ref, vp_ref, vc_ref, o_ref, l_ref, *, steps):
    nbi = pl.program_id(2)
    blk = ATTN_BLOCK
    qi = lax.broadcasted_iota(i32, (blk, 2 * blk), 0)
    kj = lax.broadcasted_iota(i32, (blk, 2 * blk), 1)
    dist = qi + blk - kj
    mask = (dist >= 0) & (dist <= steps) & ((kj >= blk) | (nbi > 0))
    scale = HEAD_DIM ** -0.5
    for h in range(HEADS):
        sl = slice(h * HEAD_DIM, (h + 1) * HEAD_DIM)
        q = q_ref[:, sl]
        kk = jnp.concatenate([kp_ref[:, sl], kc_ref[:, sl]], axis=0)
        vv = jnp.concatenate([vp_ref[:, sl], vc_ref[:, sl]], axis=0)
        sc = jnp.where(mask, _dot_nt(q, kk) * scale, -jnp.inf)
        mx = jnp.max(sc, -1, keepdims=True)
        p = jnp.exp(sc - mx)
        den = jnp.sum(p, -1, keepdims=True)
        o_ref[:, sl] = _dot(p, vv) / den
        l_ref[:, sl] = jnp.broadcast_to(mx + jnp.log(den), (blk, HEAD_DIM))


def _dilated_attention(q, k, v, group, window, dilation, batch, seq):
    n = seq // dilation
    nb = n // ATTN_BLOCK
    cols = dilation * N_GROUPS * WIDTH

    def view(t):
        return t.reshape(batch, n, cols)

    def cur(b, r, i):
        return (b, i, r * N_GROUPS + group)

    def prev(b, r, i):
        return (b, jnp.maximum(i - 1, 0), r * N_GROUPS + group)

    blk = (None, ATTN_BLOCK, WIDTH)
    out_cols = dilation * WIDTH
    out_spec = pl.BlockSpec(blk, lambda b, r, i: (b, i, r))
    o, lse = pl.pallas_call(
        functools.partial(_attn_kernel, steps=window // dilation),
        grid=(batch, dilation, nb),
        in_specs=[pl.BlockSpec(blk, cur), pl.BlockSpec(blk, prev), pl.BlockSpec(blk, cur),
                  pl.BlockSpec(blk, prev), pl.BlockSpec(blk, cur)],
        out_specs=[out_spec, out_spec],
        out_shape=[jax.ShapeDtypeStruct((batch, n, out_cols), f32)] * 2,
        compiler_params=_params(("parallel", "parallel", "parallel")),
        name=f"dilated_attn_d{dilation}",
    )(view(q), view(k), view(k), view(v), view(v))
    return o.reshape(batch * seq, WIDTH), lse.reshape(batch * seq, WIDTH)


def _combine_out_ln_kernel(o0, o1, o2, l0, l1, l2, w_ref, x_ref, g_ref, b_ref, out_ref):
    ls = [l0[...], l1[...], l2[...]]
    mx = jnp.maximum(jnp.maximum(ls[0], ls[1]), ls[2])
    es = [jnp.exp(l - mx) for l in ls]
    o = (es[0] * o0[...] + es[1] * o1[...] + es[2] * o2[...]) / (es[0] + es[1] + es[2])
    h = _dot(o, w_ref[...])
    out_ref[...] = _layer_norm(ALPHA * x_ref[...] + h, g_ref[...], b_ref[...])


def _combine_out_ln(outs, lses, w, x, g, b, tm=256):
    m, n = x.shape
    blk = pl.BlockSpec((tm, n), lambda i: (i, 0))
    row = pl.BlockSpec((1, n), lambda i: (0, 0))
    return pl.pallas_call(
        _combine_out_ln_kernel,
        grid=(m // tm,),
        in_specs=[blk] * 6 + [pl.BlockSpec((WIDTH, n), lambda i: (0, 0)), blk, row, row],
        out_specs=blk,
        out_shape=jax.ShapeDtypeStruct((m, n), f32),
        compiler_params=_params(("parallel",)),
        name="attn_combine_out_ln",
    )(*outs, *lses, w, x, g.reshape(1, n), b.reshape(1, n))


def _rope_lane_tables(seq):
    half = ROT_DIM // 2
    inv_freq = ROPE_THETA ** (-jnp.arange(half, dtype=f32) * 2.0 / ROT_DIM)
    ang = jnp.arange(seq, dtype=f32)[:, None] * inv_freq[None, :]
    cos, sin = jnp.cos(ang), jnp.sin(ang)
    ones = jnp.ones((seq, HEAD_DIM - ROT_DIM), f32)
    zeros = jnp.zeros((seq, HEAD_DIM - half), f32)
    c = jnp.concatenate([cos, cos, ones], axis=1)
    s_lo = jnp.concatenate([-sin, zeros], axis=1)
    s_hi = jnp.concatenate([jnp.zeros((seq, half), f32), sin, jnp.zeros((seq, HEAD_DIM - ROT_DIM), f32)], axis=1)
    return c, s_lo, s_hi


def _top_rows(s, payload, k):
    r = s.shape[0]
    rowid = lax.broadcasted_iota(i32, s.shape, 0)
    vals, pays = [], []
    for _ in range(k):
        m = jnp.max(s, axis=0, keepdims=True)
        pos = jnp.min(jnp.where(s == m, rowid, r), axis=0, keepdims=True)
        sel = rowid == pos
        pays.append(jnp.max(jnp.where(sel, payload, -1), axis=0, keepdims=True))
        vals.append(m)
        s = jnp.where(sel, -jnp.inf, s)
    return jnp.concatenate(vals, axis=0), jnp.concatenate(pays, axis=0)


def _peer_route_kernel(qp_ref, keys_ref, idx_ref, gate_ref):
    kt = PEER_TOPK
    half = PEER_QDIM // 2
    keyid = lax.broadcasted_iota(i32, (PEER_NKEYS, LANES), 0)
    idx_rows, gate_rows = [], []
    for h in range(PEER_HEADS):
        tops = []
        for c in range(2):
            col = (h * 2 + c) * half
            s = _dot_nt(keys_ref[h * 2 + c], qp_ref[:, col:col + half])
            tops.append(_top_rows(s, keyid, kt))
        (v1, i1), (v2, i2) = tops
        rep = lambda t: jnp.concatenate([jnp.broadcast_to(t[a:a + 1], (kt, LANES)) for a in range(kt)], axis=0)
        til = lambda t: jnp.concatenate([t] * kt, axis=0)
        cand = rep(v1) + til(v2)
        cand_id = rep(i1) * PEER_NKEYS + til(i2)
        best, ids = _top_rows(cand, cand_id, kt)
        e = jnp.exp(best - best[0:1])
        gate_rows.append(e / jnp.sum(e, axis=0, keepdims=True))
        idx_rows.append(ids)
    idx_ref[...] = jnp.concatenate(idx_rows, axis=0).astype(f32).T.astype(i32)
    gate_ref[...] = jnp.concatenate(gate_rows, axis=0).T


def _peer_route(qp, sub_keys):
    m = qp.shape[0]
    tb = LANES
    keys = sub_keys.reshape(PEER_HEADS * 2, PEER_NKEYS, PEER_QDIM // 2).astype(bf16)
    out = pl.BlockSpec((tb, PEER_SEL), lambda i: (i, 0))
    return pl.pallas_call(
        _peer_route_kernel,
        grid=(m // tb,),
        in_specs=[pl.BlockSpec((tb, qp.shape[1]), lambda i: (i, 0)),
                  pl.BlockSpec(keys.shape, lambda i: (0, 0, 0))],
        out_specs=[out, out],
        out_shape=[jax.ShapeDtypeStruct((m, PEER_SEL), i32), jax.ShapeDtypeStruct((m, PEER_SEL), f32)],
        compiler_params=_params(("parallel",)),
        name="peer_route",
    )(qp, keys)


SC_CORES = 2
SC_SUBCORES = 16
SC_LANES = 16
SC_WORKERS = SC_CORES * SC_SUBCORES
SC_CHUNKS = D_MODEL // SC_LANES
SC_TOK_GROUP = 8
SC_ROWS = 32
SC_GATHERS_PER_TOK = PEER_SEL // SC_ROWS
SC_GATHERS = SC_TOK_GROUP * SC_GATHERS_PER_TOK
SC_ROW_GROUP = 8


def _sc_mesh():
    return plsc.VectorSubcoreMesh(core_axis_name="c", subcore_axis_name="s")


def _sc_worker():
    return lax.axis_index("s") * SC_CORES + lax.axis_index("c")


def _sc_gather_loop(gather, compute):
    gather(0, 0).start()

    @pl.loop(0, SC_GATHERS, step=2)
    def _(q):
        gather(q + 1, 1).start()
        gather(q, 0).wait()
        compute(q, 0)

        @pl.when(q + 2 < SC_GATHERS)
        def _():
            gather(q + 2, 0).start()
        gather(q + 1, 1).wait()
        compute(q + 1, 1)


def _sc_scratch(stage_shape):
    return [pltpu.VMEM((SC_GATHERS, SC_ROWS), i32), pltpu.VMEM(stage_shape[0], f32), pltpu.VMEM(stage_shape[1], f32),
            pltpu.VMEM((SC_ROWS, D_MODEL), f32), pltpu.VMEM((SC_ROWS, D_MODEL), f32),
            pltpu.SemaphoreType.DMA, pltpu.SemaphoreType.DMA]


def _peer_udot(x, idx, u_tab):
    t_all = x.shape[0]
    tpw = t_all // SC_WORKERS
    idx2 = idx.reshape(t_all * SC_GATHERS_PER_TOK, SC_ROWS)

    @functools.partial(
        pl.kernel, mesh=_sc_mesh(), compiler_params=pltpu.CompilerParams(needs_layout_passes=False),
        out_type=jax.ShapeDtypeStruct((t_all, PEER_SEL), f32),
        scratch_types=_sc_scratch(((SC_TOK_GROUP, D_MODEL), (SC_TOK_GROUP, PEER_SEL))), name="peer_udot_sc")
    def k(x_hbm, idx_hbm, u_hbm, pre_hbm, idx_v, x_v, pre_v, rows0, rows1, sem0, sem1):
        bufs = ((rows0, sem0), (rows1, sem1))
        base = _sc_worker() * tpw
        lane = lax.iota(i32, SC_LANES)
        zero = jnp.zeros((SC_LANES,), f32)

        def gather(q, b):
            return pltpu.make_async_copy(u_hbm.at[idx_v.at[q]], bufs[b][0], bufs[b][1])

        def compute(q, b):
            rows = bufs[b][0]
            t = q // SC_GATHERS_PER_TOK
            col0 = (q % SC_GATHERS_PER_TOK) * SC_ROWS
            for hh in range(SC_ROWS // SC_LANES):
                vec = zero
                for rg in range(SC_LANES // SC_ROW_GROUP):
                    r0 = hh * SC_LANES + rg * SC_ROW_GROUP

                    def body(c, acc):
                        sl = pl.ds(c * SC_LANES, SC_LANES)
                        xv = x_v[t, sl]
                        return tuple(acc[r] + rows[r0 + r, sl] * xv for r in range(SC_ROW_GROUP))

                    acc = lax.fori_loop(0, SC_CHUNKS, body, (zero,) * SC_ROW_GROUP, unroll=2)
                    for r in range(SC_ROW_GROUP):
                        vec = jnp.where(lane == rg * SC_ROW_GROUP + r, jnp.sum(acc[r]), vec)
                pre_v[t, pl.ds(col0 + hh * SC_LANES, SC_LANES)] = vec

        @pl.loop(0, tpw // SC_TOK_GROUP)
        def _(g):
            tok0 = base + g * SC_TOK_GROUP
            pltpu.sync_copy(idx_hbm.at[pl.ds(tok0 * SC_GATHERS_PER_TOK, SC_GATHERS)], idx_v)
            pltpu.sync_copy(x_hbm.at[pl.ds(tok0, SC_TOK_GROUP)], x_v)
            _sc_gather_loop(gather, compute)
            pltpu.sync_copy(pre_v, pre_hbm.at[pl.ds(tok0, SC_TOK_GROUP)])

    return k(x, idx2, u_tab)


def _peer_vacc(w, idx, v_tab):
    t_all = w.shape[0]
    tpw = t_all // SC_WORKERS
    idx2 = idx.reshape(t_all * SC_GATHERS_PER_TOK, SC_ROWS)
    half_ch = SC_CHUNKS // 2

    @functools.partial(
        pl.kernel, mesh=_sc_mesh(), compiler_params=pltpu.CompilerParams(needs_layout_passes=False),
        out_type=jax.ShapeDtypeStruct((t_all, D_MODEL), f32),
        scratch_types=_sc_scratch(((SC_TOK_GROUP, PEER_SEL), (SC_TOK_GROUP, D_MODEL))), name="peer_vacc_sc")
    def k(w_hbm, idx_hbm, v_hbm, f_hbm, idx_v, w_v, f_v, rows0, rows1, sem0, sem1):
        bufs = ((rows0, sem0), (rows1, sem1))
        base = _sc_worker() * tpw
        zero = jnp.zeros((SC_LANES,), f32)

        def gather(q, b):
            return pltpu.make_async_copy(v_hbm.at[idx_v.at[q]], bufs[b][0], bufs[b][1])

        def compute(q, b):
            rows = bufs[b][0]
            t = q // SC_GATHERS_PER_TOK
            col0 = (q % SC_GATHERS_PER_TOK) * SC_ROWS
            tvec = jnp.full((SC_LANES,), t, i32)
            for half in range(2):
                def body(r, acc):
                    wr = plsc.load_gather(w_v, [tvec, jnp.full((SC_LANES,), col0 + r, i32)])
                    return tuple(acc[c] + wr * rows[r, pl.ds((half * half_ch + c) * SC_LANES, SC_LANES)]
                                 for c in range(half_ch))

                acc = lax.fori_loop(0, SC_ROWS, body, (zero,) * half_ch)
                for c in range(half_ch):
                    sl = pl.ds((half * half_ch + c) * SC_LANES, SC_LANES)
                    f_v[t, sl] = f_v[t, sl] + acc[c]

        @pl.loop(0, tpw // SC_TOK_GROUP)
        def _(g):
            tok0 = base + g * SC_TOK_GROUP
            pltpu.sync_copy(idx_hbm.at[pl.ds(tok0 * SC_GATHERS_PER_TOK, SC_GATHERS)], idx_v)
            pltpu.sync_copy(w_hbm.at[pl.ds(tok0, SC_TOK_GROUP)], w_v)

            @pl.loop(0, SC_TOK_GROUP)
            def _(t):
                @pl.loop(0, SC_CHUNKS)
                def _(c):
                    f_v[t, pl.ds(c * SC_LANES, SC_LANES)] = zero

            _sc_gather_loop(gather, compute)
            pltpu.sync_copy(f_v, f_hbm.at[pl.ds(tok0, SC_TOK_GROUP)])

    return k(w, idx2, v_tab)


def _peer_act_kernel(pre_ref, gate_ref, w_ref):
    pre = pre_ref[...]
    w_ref[...] = gate_ref[...] * (0.5 * pre * (1.0 + lax.erf(pre * (2.0 ** -0.5))))


def _peer_act(pre, gate, tm=2048):
    m, n = pre.shape
    blk = pl.BlockSpec((tm, n), lambda i: (i, 0))
    return pl.pallas_call(
        _peer_act_kernel, grid=(m // tm,), in_specs=[blk, blk], out_specs=blk,
        out_shape=jax.ShapeDtypeStruct((m, n), f32), compiler_params=_params(("parallel",)), name="peer_act",
    )(pre, gate)


def _res_ln_kernel(x_ref, f_ref, g_ref, b_ref, o_ref):
    o_ref[...] = _layer_norm(ALPHA * x_ref[...] + f_ref[...], g_ref[...], b_ref[...])


def _res_ln(x, f, g, b, tm=512):
    m, n = x.shape
    blk = pl.BlockSpec((tm, n), lambda i: (i, 0))
    row = pl.BlockSpec((1, n), lambda i: (0, 0))
    return pl.pallas_call(
        _res_ln_kernel, grid=(m // tm,), in_specs=[blk, blk, row, row], out_specs=blk,
        out_shape=jax.ShapeDtypeStruct((m, n), f32), compiler_params=_params(("parallel",)), name="res_ln",
    )(x, f, g.reshape(1, n), b.reshape(1, n))


def _peer_expert_ln(x, idx, gate, u_tab, v_tab, g, b):
    pre = _peer_udot(x, idx, u_tab)
    w = _peer_act(pre, gate)
    f = _peer_vacc(w, idx, v_tab)
    return _res_ln(x, f, g, b)


def _peer_ffn_ln(x2, w_query, sub_keys, u_tab, v_tab, g, b):
    qp = _matmul(x2, w_query.astype(bf16))
    idx, gate = _peer_route(qp, sub_keys)
    return _peer_expert_ln(x2, idx, gate, u_tab, v_tab, g, b)


def kernel(x, dn_w_in, dn_conv, dn_a_log, dn_dt_bias, dn_norm_w, dn_w_out, shared_w_kv, attn_w_q, attn_w_out,
           peer_w_query, peer_sub_keys, peer_u, peer_v, ln_mix_g, ln_mix_b, ln_ffn_g, ln_ffn_b):
    batch, seq, d = x.shape
    x2 = x.reshape(batch * seq, d)
    n_a = DEPTH // 2
    tabs = _rope_lane_tables(seq)
    k_sh = v_sh = None
    for layer in range(DEPTH):
        if layer < n_a:
            o = _deltanet_mixer(x2, dn_w_in[layer], dn_conv[layer], dn_a_log[layer], dn_dt_bias[layer],
                                dn_norm_w[layer], batch, seq)
            x2 = _matmul_res_ln(o, dn_w_out[layer].astype(bf16), x2, ln_mix_g[layer], ln_mix_b[layer])
        else:
            j = layer - n_a
            if j == 0:
                gw = N_GROUPS * WIDTH
                k_sh = _matmul_rope(x2, shared_w_kv[:, :gw].astype(bf16), tabs, seq)
                v_sh = _matmul(x2, shared_w_kv[:, gw:].astype(bf16), tn=1024)
            q = _matmul_rope(x2, attn_w_q[j].astype(bf16), tabs, seq)
            outs, lses = [], []
            for gi, (window, dilation) in enumerate(ATTN_GROUPS):
                o, l = _dilated_attention(q, k_sh, v_sh, gi, window, dilation, batch, seq)
                outs.append(o)
                lses.append(l)
            x2 = _combine_out_ln(outs, lses, attn_w_out[j].astype(bf16), x2, ln_mix_g[layer], ln_mix_b[layer])
        x2 = _peer_ffn_ln(x2, peer_w_query[layer], peer_sub_keys[layer], peer_u[layer], peer_v[layer],
                          ln_ffn_g[layer], ln_ffn_b[layer])
    return x2.reshape(batch, seq, d)
```

```python
import functools
import math

import jax
import jax.numpy as jnp
from jax import lax
from jax.experimental import pallas as pl
from jax.experimental.pallas import tpu as pltpu
from jax.experimental.pallas import tpu_sc as plsc

f32 = jnp.float32
bf16 = jnp.bfloat16
i32 = jnp.int32

D_MODEL = 1024
DEPTH = 2
ALPHA = (2.0 * DEPTH) ** 0.25
LN_EPS = 1e-5

HEADS = 8
HEAD_DIM = 128
WIDTH = HEADS * HEAD_DIM
CONV_K = 4
DN_CHUNK = 64

ATTN_GROUPS = ((128, 1), (512, 4), (2048, 16))
N_GROUPS = len(ATTN_GROUPS)
ATTN_BLOCK = 128
ROT_DIM = HEAD_DIM // 4
ROPE_THETA = 500000.0

PEER_HEADS = 8
PEER_NKEYS = 128
PEER_TOPK = 16
PEER_QDIM = 256
PEER_SEL = PEER_HEADS * PEER_TOPK

LANES = 128
SUBLANES = 8
VMEM_LIMIT = 48 * 1024 * 1024


def _params(sem):
    return pltpu.CompilerParams(dimension_semantics=sem, vmem_limit_bytes=VMEM_LIMIT)


def _dot(a, b):
    return jnp.dot(a.astype(bf16), b.astype(bf16), preferred_element_type=f32)


def _dot_nt(a, b):
    return lax.dot_general(a.astype(bf16), b.astype(bf16), (((1,), (1,)), ((), ())), preferred_element_type=f32)


def _dot_tn(a, b):
    return lax.dot_general(a.astype(bf16), b.astype(bf16), (((0,), (0,)), ((), ())), preferred_element_type=f32)


def _sigmoid(x):
    return 1.0 / (1.0 + jnp.exp(-x))


def _layer_norm(y, g, b):
    mu = jnp.mean(y, -1, keepdims=True)
    yc = y - mu
    var = jnp.mean(yc * yc, -1, keepdims=True)
    return yc * lax.rsqrt(var + LN_EPS) * g + b


def _mm_kernel(a_ref, w_ref, o_ref):
    o_ref[...] = _dot(a_ref[...], w_ref[...])


def _matmul(a, w, tm=512, tn=None):
    m, k = a.shape
    n = w.shape[1]
    tn = tn or n
    return pl.pallas_call(
        _mm_kernel,
        grid=(m // tm, n // tn),
        in_specs=[pl.BlockSpec((tm, k), lambda i, j: (i, 0)), pl.BlockSpec((k, tn), lambda i, j: (0, j))],
        out_specs=pl.BlockSpec((tm, tn), lambda i, j: (i, j)),
        out_shape=jax.ShapeDtypeStruct((m, n), f32),
        compiler_params=_params(("parallel", "parallel")),
        name="matmul",
    )(a, w)


def _rope(y, c, s_lo, s_hi):
    return y * c + pltpu.roll(y, ROT_DIM // 2, axis=1) * s_hi + pltpu.roll(y, LANES - ROT_DIM // 2, axis=1) * s_lo


def _mm_rope_kernel(a_ref, w_ref, c_ref, slo_ref, shi_ref, o_ref, *, heads):
    y = _dot(a_ref[...], w_ref[...])
    c, s_lo, s_hi = c_ref[...], slo_ref[...], shi_ref[...]
    for h in range(heads):
        sl = slice(h * HEAD_DIM, (h + 1) * HEAD_DIM)
        o_ref[:, sl] = _rope(y[:, sl], c, s_lo, s_hi)


def _matmul_rope(a, w, tabs, seq, tm=512, tn=1024):
    m, k = a.shape
    n = w.shape[1]
    nsb = seq // tm
    tab_spec = pl.BlockSpec((tm, LANES), lambda i, j: (i % nsb, 0))
    return pl.pallas_call(
        functools.partial(_mm_rope_kernel, heads=tn // HEAD_DIM),
        grid=(m // tm, n // tn),
        in_specs=[pl.BlockSpec((tm, k), lambda i, j: (i, 0)), pl.BlockSpec((k, tn), lambda i, j: (0, j)),
                  tab_spec, tab_spec, tab_spec],
        out_specs=pl.BlockSpec((tm, tn), lambda i, j: (i, j)),
        out_shape=jax.ShapeDtypeStruct((m, n), f32),
        compiler_params=_params(("parallel", "parallel")),
        name="matmul_rope",
    )(a, w, *tabs)


def _mm_res_ln_kernel(a_ref, w_ref, x_ref, g_ref, b_ref, o_ref):
    h = _dot(a_ref[...], w_ref[...])
    o_ref[...] = _layer_norm(ALPHA * x_ref[...] + h, g_ref[...], b_ref[...])


def _matmul_res_ln(a, w, x, g, b, tm=512):
    m, k = a.shape
    n = w.shape[1]
    row = pl.BlockSpec((1, n), lambda i: (0, 0))
    return pl.pallas_call(
        _mm_res_ln_kernel,
        grid=(m // tm,),
        in_specs=[pl.BlockSpec((tm, k), lambda i: (i, 0)), pl.BlockSpec((k, n), lambda i: (0, 0)),
                  pl.BlockSpec((tm, n), lambda i: (i, 0)), row, row],
        out_specs=pl.BlockSpec((tm, n), lambda i: (i, 0)),
        out_shape=jax.ShapeDtypeStruct((m, n), f32),
        compiler_params=_params(("parallel",)),
        name="matmul_res_ln",
    )(a, w, x, g.reshape(1, n), b.reshape(1, n))


DN_PROJ_PAD = 4 * WIDTH + LANES


def _dn_prep_kernel(proj_ref, halo_ref, conv_ref, alog_ref, dtb_ref, q_ref, k_ref, v_ref, gate_ref, *, tm, nsb):
    i = pl.program_id(0)
    x = proj_ref[:, : 3 * WIDTH]
    halo = jnp.where(i % nsb == 0, 0.0, halo_ref[...])
    w = conv_ref[...]
    row8 = lax.broadcasted_iota(i32, (SUBLANES, 1), 0)
    acc = x * w[CONV_K - 1:CONV_K]
    for s in range(1, CONV_K):
        xs = pltpu.roll(x, s, axis=0)
        hs = pltpu.roll(halo, s, axis=0)
        first = jnp.where(row8 < s, hs, xs[:SUBLANES])
        xs = jnp.concatenate([first, xs[SUBLANES:]], axis=0)
        acc = acc + xs * w[CONV_K - 1 - s:CONV_K - s]
    qkv = acc * _sigmoid(acc)
    for h in range(HEADS):
        sl = slice(h * HEAD_DIM, (h + 1) * HEAD_DIM)
        qh = qkv[:, sl]
        q_ref[:, sl] = qh * lax.rsqrt(jnp.sum(qh * qh, -1, keepdims=True) + 1e-6) * (HEAD_DIM ** -0.5)
        kh = qkv[:, WIDTH + h * HEAD_DIM: WIDTH + (h + 1) * HEAD_DIM]
        k_ref[:, sl] = kh * lax.rsqrt(jnp.sum(kh * kh, -1, keepdims=True) + 1e-6)
    v_ref[...] = qkv[:, 2 * WIDTH:]
    ab = proj_ref[:, 4 * WIDTH:]
    z = ab + dtb_ref[...]
    softplus = jnp.maximum(z, 0.0) + jnp.log(1.0 + jnp.exp(-jnp.abs(z)))
    g = -jnp.exp(alog_ref[...]) * softplus
    row = lax.broadcasted_iota(i32, (tm, 1), 0) % DN_CHUNK
    s = 1
    while s < DN_CHUNK:
        g = g + jnp.where(row >= s, pltpu.roll(g, s, axis=0), 0.0)
        s *= 2
    lane = lax.broadcasted_iota(i32, (1, LANES), 1)
    gate_ref[...] = jnp.where(lane < HEADS, g, _sigmoid(ab))


def _dn_prep(proj, conv_w, a_log, dt_bias, seq, tm=256):
    m = proj.shape[0]
    nsb = seq // tm
    pad = LANES - HEADS
    alog = jnp.pad(a_log.astype(f32), (0, pad)).reshape(1, LANES)
    dtb = jnp.pad(dt_bias.astype(f32), (0, pad)).reshape(1, LANES)
    hb = tm // SUBLANES
    out_w = pl.BlockSpec((tm, WIDTH), lambda i: (i, 0))
    return pl.pallas_call(
        functools.partial(_dn_prep_kernel, tm=tm, nsb=nsb),
        grid=(m // tm,),
        in_specs=[pl.BlockSpec((tm, DN_PROJ_PAD), lambda i: (i, 0)),
                  pl.BlockSpec((SUBLANES, 3 * WIDTH), lambda i: (jnp.maximum(i * hb - 1, 0), 0)),
                  pl.BlockSpec((CONV_K, 3 * WIDTH), lambda i: (0, 0)),
                  pl.BlockSpec((1, LANES), lambda i: (0, 0)), pl.BlockSpec((1, LANES), lambda i: (0, 0))],
        out_specs=[out_w, out_w, out_w, pl.BlockSpec((tm, LANES), lambda i: (i, 0))],
        out_shape=[jax.ShapeDtypeStruct((m, WIDTH), f32)] * 3 + [jax.ShapeDtypeStruct((m, LANES), f32)],
        compiler_params=_params(("parallel",)),
        name="dn_prep",
    )(proj, proj, conv_w, alog, dtb)


def _dn_delta_kernel(q_ref, k_ref, v_ref, z_ref, gate_ref, gt_ref, nw_ref, o_ref, state_ref, *, chunks):
    @pl.when(pl.program_id(1) == 0)
    def _():
        state_ref[...] = jnp.zeros_like(state_ref)

    c = DN_CHUNK
    ri = lax.broadcasted_iota(i32, (c, c), 0)
    ci = lax.broadcasted_iota(i32, (c, c), 1)
    causal = ri >= ci
    strict = ri > ci
    eye = (ri == ci).astype(f32)
    nw = nw_ref[...]
    for ch in range(chunks):
        rows = slice(ch * c, (ch + 1) * c)
        for h in range(HEADS):
            sl = slice(h * HEAD_DIM, (h + 1) * HEAD_DIM)
            q, k, v = q_ref[rows, sl], k_ref[rows, sl], v_ref[rows, sl]
            gcol = gate_ref[rows, h:h + 1]
            beta = gate_ref[rows, HEADS + h:HEADS + h + 1]
            grow = gt_ref[h:h + 1, rows]
            diff = gcol - grow
            decay = jnp.where(causal, jnp.exp(jnp.where(causal, diff, 0.0)), 0.0)
            kb = k * beta
            m = jnp.where(strict, _dot_nt(kb, k) * decay, 0.0)
            inv = eye - m
            p = m
            step = 2
            while step < c:
                p = _dot(p, p)
                inv = inv + _dot(inv, p)
                step *= 2
            egc = jnp.exp(gcol)
            u = _dot(inv, v * beta)
            w = _dot(inv, kb * egc)
            a_qk = _dot_nt(q, k) * decay
            st = state_ref[h]
            v_new = u - _dot(w, st)
            o = _dot(q * egc, st) + _dot(a_qk, v_new)
            glast = gcol[c - 1:c, :]
            state_ref[h] = st * jnp.exp(glast) + _dot_tn(k * jnp.exp(glast - gcol), v_new)
            o = o * lax.rsqrt(jnp.mean(o * o, -1, keepdims=True) + 1e-6) * nw
            zz = z_ref[rows, sl]
            o_ref[rows, sl] = o * (zz * _sigmoid(zz))


def _dn_delta(q, k, v, proj, gates, gates_t, norm_w, batch, seq, cb=128):
    nb = seq // cb
    wide = pl.BlockSpec((cb, WIDTH), lambda b, i: (b * nb + i, 0))
    return pl.pallas_call(
        functools.partial(_dn_delta_kernel, chunks=cb // DN_CHUNK),
        grid=(batch, nb),
        in_specs=[wide, wide, wide,
                  pl.BlockSpec((cb, WIDTH), lambda b, i: (b * nb + i, 3)),
                  pl.BlockSpec((cb, LANES), lambda b, i: (b * nb + i, 0)),
                  pl.BlockSpec((SUBLANES, cb), lambda b, i: (0, b * nb + i)),
                  pl.BlockSpec((1, HEAD_DIM), lambda b, i: (0, 0))],
        out_specs=wide,
        out_shape=jax.ShapeDtypeStruct((batch * seq, WIDTH), f32),
        scratch_shapes=[pltpu.VMEM((HEADS, HEAD_DIM, HEAD_DIM), f32)],
        compiler_params=_params(("parallel", "arbitrary")),
        name="dn_delta",
    )(q, k, v, proj, gates, gates_t, norm_w.reshape(1, HEAD_DIM).astype(f32))


def _deltanet_mixer(x2, w_in, conv_w, a_log, dt_bias, norm_w, batch, seq):
    w_pad = jnp.pad(w_in, ((0, 0), (0, DN_PROJ_PAD - w_in.shape[1]))).astype(bf16)
    proj = _matmul(x2, w_pad, tn=DN_PROJ_PAD // 3)
    q, k, v, gates = _dn_prep(proj, conv_w, a_log, dt_bias, seq)
    gates_t = gates[:, :SUBLANES].T
    return _dn_delta(q, k, v, proj, gates, gates_t, norm_w, batch, seq)


def _attn_kernel(q_ref, kp_ref, kc_ref, vp_ref, vc_ref, o_ref, l_ref, *, steps):
    nbi = pl.program_id(2)
    blk = ATTN_BLOCK
    qi = lax.broadcasted_iota(i32, (blk, 2 * blk), 0)
    kj = lax.broadcasted_iota(i32, (blk, 2 * blk), 1)
    dist = qi + blk - kj
    mask = (dist >= 0) & (dist <= steps) & ((kj >= blk) | (nbi > 0))
    scale = HEAD_DIM ** -0.5
    for h in range(HEADS):
        sl = slice(h * HEAD_DIM, (h + 1) * HEAD_DIM)
        q = q_ref[:, sl]
        kk = jnp.concatenate([kp_ref[:, sl], kc_ref[:, sl]], axis=0)
        vv = jnp.concatenate([vp_ref[:, sl], vc_ref[:, sl]], axis=0)
        sc = jnp.where(mask, _dot_nt(q, kk) * scale, -jnp.inf)
        mx = jnp.max(sc, -1, keepdims=True)
        p = jnp.exp(sc - mx)
        den = jnp.sum(p, -1, keepdims=True)
        o_ref[:, sl] = _dot(p, vv) / den
        l_ref[:, sl] = jnp.broadcast_to(mx + jnp.log(den), (blk, HEAD_DIM))


def _dilated_attention(q, k, v, group, window, dilation, batch, seq):
    n = seq // dilation
    nb = n // ATTN_BLOCK
    cols = dilation * N_GROUPS * WIDTH

    def view(t):
        return t.reshape(batch, n, cols)

    def cur(b, r, i):
        return (b, i, r * N_GROUPS + group)

    def prev(b, r, i):
        return (b, jnp.maximum(i - 1, 0), r * N_GROUPS + group)

    blk = (None, ATTN_BLOCK, WIDTH)
    out_cols = dilation * WIDTH
    out_spec = pl.BlockSpec(blk, lambda b, r, i: (b, i, r))
    o, lse = pl.pallas_call(
        functools.partial(_attn_kernel, steps=window // dilation),
        grid=(batch, dilation, nb),
        in_specs=[pl.BlockSpec(blk, cur), pl.BlockSpec(blk, prev), pl.BlockSpec(blk, cur),
                  pl.BlockSpec(blk, prev), pl.BlockSpec(blk, cur)],
        out_specs=[out_spec, out_spec],
        out_shape=[jax.ShapeDtypeStruct((batch, n, out_cols), f32)] * 2,
        compiler_params=_params(("parallel", "parallel", "parallel")),
        name=f"dilated_attn_d{dilation}",
    )(view(q), view(k), view(k), view(v), view(v))
    return o.reshape(batch * seq, WIDTH), lse.reshape(batch * seq, WIDTH)


def _combine_out_ln_kernel(o0, o1, o2, l0, l1, l2, w_ref, x_ref, g_ref, b_ref, out_ref):
    ls = [l0[...], l1[...], l2[...]]
    mx = jnp.maximum(jnp.maximum(ls[0], ls[1]), ls[2])
    es = [jnp.exp(l - mx) for l in ls]
    o = (es[0] * o0[...] + es[1] * o1[...] + es[2] * o2[...]) / (es[0] + es[1] + es[2])
    h = _dot(o, w_ref[...])
    out_ref[...] = _layer_norm(ALPHA * x_ref[...] + h, g_ref[...], b_ref[...])


def _combine_out_ln(outs, lses, w, x, g, b, tm=256):
    m, n = x.shape
    blk = pl.BlockSpec((tm, n), lambda i: (i, 0))
    row = pl.BlockSpec((1, n), lambda i: (0, 0))
    return pl.pallas_call(
        _combine_out_ln_kernel,
        grid=(m // tm,),
        in_specs=[blk] * 6 + [pl.BlockSpec((WIDTH, n), lambda i: (0, 0)), blk, row, row],
        out_specs=blk,
        out_shape=jax.ShapeDtypeStruct((m, n), f32),
        compiler_params=_params(("parallel",)),
        name="attn_combine_out_ln",
    )(*outs, *lses, w, x, g.reshape(1, n), b.reshape(1, n))


def _rope_lane_tables(seq):
    half = ROT_DIM // 2
    inv_freq = ROPE_THETA ** (-jnp.arange(half, dtype=f32) * 2.0 / ROT_DIM)
    ang = jnp.arange(seq, dtype=f32)[:, None] * inv_freq[None, :]
    cos, sin = jnp.cos(ang), jnp.sin(ang)
    ones = jnp.ones((seq, HEAD_DIM - ROT_DIM), f32)
    zeros = jnp.zeros((seq, HEAD_DIM - half), f32)
    c = jnp.concatenate([cos, cos, ones], axis=1)
    s_lo = jnp.concatenate([-sin, zeros], axis=1)
    s_hi = jnp.concatenate([jnp.zeros((seq, half), f32), sin, jnp.zeros((seq, HEAD_DIM - ROT_DIM), f32)], axis=1)
    return c, s_lo, s_hi


def _top_rows(s, order, payload, k):
    big = jnp.int32(2 ** 30)
    vals, pays = [], []
    for _ in range(k):
        m = jnp.max(s, axis=0, keepdims=True)
        pos = jnp.min(jnp.where(s == m, order, big), axis=0, keepdims=True)
        sel = order == pos
        pays.append(pos if payload is None else jnp.max(jnp.where(sel, payload, -1), axis=0, keepdims=True))
        vals.append(m)
        s = jnp.where(sel, -jnp.inf, s)
    return jnp.concatenate(vals, axis=0), jnp.concatenate(pays, axis=0)


def _pair_candidates(t1, t2, combine, fill):
    kt = PEER_TOPK
    row = lax.broadcasted_iota(i32, (SUBLANES, LANES), 0)
    bc = lambda t, r: jnp.broadcast_to(t[r:r + 1], (SUBLANES, LANES))
    lo1, hi1, lo2, hi2 = t1[:SUBLANES], t1[SUBLANES:], t2[:SUBLANES], t2[SUBLANES:]
    tiles = [combine(bc(t1, 0), lo2), combine(bc(t1, 0), hi2), combine(bc(t1, 1), lo2),
             jnp.where(row < kt // 3, combine(bc(t1, 2), lo2), fill),
             jnp.where(row < kt // 4, combine(bc(t1, 3), lo2), fill),
             combine(hi1, bc(t2, 0)),
             jnp.where(row >= 4, combine(lo1, bc(t2, 0)), fill),
             jnp.where(row >= 4, combine(lo1, bc(t2, 1)), jnp.where(row == 0, combine(bc(t1, 4), bc(t2, 2)), fill))]
    return jnp.concatenate(tiles, axis=0)


def _pair_rank():
    kt = PEER_TOPK
    row = lax.broadcasted_iota(i32, (SUBLANES, LANES), 0)
    unused = kt * kt + row
    tiles = [row, row + SUBLANES, kt + row,
             jnp.where(row < kt // 3, 2 * kt + row, unused),
             jnp.where(row < kt // 4, 3 * kt + row, unused + SUBLANES),
             (row + SUBLANES) * kt,
             jnp.where(row >= 4, row * kt, unused + 2 * SUBLANES),
             jnp.where(row >= 4, row * kt + 1, jnp.where(row == 0, 4 * kt + 2, unused + 3 * SUBLANES))]
    return jnp.concatenate(tiles, axis=0)


def _peer_route_kernel(qp_ref, keys_ref, idx_ref, gate_ref):
    kt = PEER_TOPK
    half = PEER_QDIM // 2
    keyid = lax.broadcasted_iota(i32, (PEER_NKEYS, LANES), 0)
    rank = _pair_rank()
    idx_rows, gate_rows = [], []
    for h in range(PEER_HEADS):
        tops = []
        for c in range(2):
            col = (h * 2 + c) * half
            s = _dot_nt(keys_ref[h * 2 + c], qp_ref[:, col:col + half])
            tops.append(_top_rows(s, keyid, None, kt))
        (v1, i1), (v2, i2) = tops
        cand = _pair_candidates(v1, v2, lambda a, b: a + b, -jnp.inf)
        cand_id = _pair_candidates(i1, i2, lambda a, b: a * PEER_NKEYS + b, -1)
        best, ids = _top_rows(cand, rank, cand_id, kt)
        e = jnp.exp(best - best[0:1])
        gate_rows.append(e / jnp.sum(e, axis=0, keepdims=True))
        idx_rows.append(ids)
    idx_ref[...] = jnp.concatenate(idx_rows, axis=0).astype(f32).T.astype(i32)
    gate_ref[...] = jnp.concatenate(gate_rows, axis=0).T


def _peer_route(qp, sub_keys):
    m = qp.shape[0]
    tb = LANES
    keys = sub_keys.reshape(PEER_HEADS * 2, PEER_NKEYS, PEER_QDIM // 2).astype(bf16)
    out = pl.BlockSpec((tb, PEER_SEL), lambda i: (i, 0))
    return pl.pallas_call(
        _peer_route_kernel,
        grid=(m // tb,),
        in_specs=[pl.BlockSpec((tb, qp.shape[1]), lambda i: (i, 0)),
                  pl.BlockSpec(keys.shape, lambda i: (0, 0, 0))],
        out_specs=[out, out],
        out_shape=[jax.ShapeDtypeStruct((m, PEER_SEL), i32), jax.ShapeDtypeStruct((m, PEER_SEL), f32)],
        compiler_params=_params(("parallel",)),
        name="peer_route",
    )(qp, keys)


SC_CORES = 2
SC_SUBCORES = 16
SC_LANES = 16
SC_WORKERS = SC_CORES * SC_SUBCORES
SC_CHUNKS = D_MODEL // SC_LANES
SC_WORDS = D_MODEL // 2
SC_WORD_CHUNKS = SC_WORDS // SC_LANES
SC_TOK_GROUP = 8
SC_ROWS = 64
SC_GATHERS_PER_TOK = PEER_SEL // SC_ROWS
SC_GATHERS = SC_TOK_GROUP * SC_GATHERS_PER_TOK
SC_ROW_GROUP = 8


def _pack_table(tab):
    e = tab.shape[0]
    bits = lax.bitcast_convert_type(tab.astype(bf16), jnp.uint16).astype(jnp.uint32)
    bits = bits.reshape(e, SC_WORD_CHUNKS, 2, SC_LANES)
    word = bits[:, :, 0, :] | (bits[:, :, 1, :] << 16)
    return lax.bitcast_convert_type(word.reshape(e, SC_WORDS), i32)


def _unpack_words(w):
    return lax.bitcast_convert_type(w << 16, f32), lax.bitcast_convert_type(w & (-65536), f32)


def _sc_mesh():
    return plsc.VectorSubcoreMesh(core_axis_name="c", subcore_axis_name="s")


def _sc_worker():
    return lax.axis_index("s") * SC_CORES + lax.axis_index("c")


def _sc_gather_loop(gather, compute):
    gather(0, 0).start()

    @pl.loop(0, SC_GATHERS, step=2)
    def _(q):
        gather(q + 1, 1).start()
        gather(q, 0).wait()
        compute(q, 0)

        @pl.when(q + 2 < SC_GATHERS)
        def _():
            gather(q + 2, 0).start()
        gather(q + 1, 1).wait()
        compute(q + 1, 1)


def _sc_scratch(stage_shape):
    return [pltpu.VMEM((SC_GATHERS, SC_ROWS), i32), pltpu.VMEM(stage_shape[0], f32), pltpu.VMEM(stage_shape[1], f32),
            pltpu.VMEM((SC_ROWS, SC_WORDS), i32), pltpu.VMEM((SC_ROWS, SC_WORDS), i32),
            pltpu.SemaphoreType.DMA, pltpu.SemaphoreType.DMA]


def _peer_udot(x, idx, u_tab):
    t_all = x.shape[0]
    tpw = t_all // SC_WORKERS
    idx2 = idx.reshape(t_all * SC_GATHERS_PER_TOK, SC_ROWS)

    @functools.partial(
        pl.kernel, mesh=_sc_mesh(), compiler_params=pltpu.CompilerParams(needs_layout_passes=False),
        out_type=jax.ShapeDtypeStruct((t_all, PEER_SEL), f32),
        scratch_types=_sc_scratch(((SC_TOK_GROUP, D_MODEL), (SC_TOK_GROUP, PEER_SEL))), name="peer_udot_sc")
    def k(x_hbm, idx_hbm, u_hbm, pre_hbm, idx_v, x_v, pre_v, rows0, rows1, sem0, sem1):
        bufs = ((rows0, sem0), (rows1, sem1))
        base = _sc_worker() * tpw
        lane = lax.iota(i32, SC_LANES)
        zero = jnp.zeros((SC_LANES,), f32)

        def gather(q, b):
            return pltpu.make_async_copy(u_hbm.at[idx_v.at[q]], bufs[b][0], bufs[b][1])

        def compute(q, b):
            rows = bufs[b][0]
            t = q // SC_GATHERS_PER_TOK
            col0 = (q % SC_GATHERS_PER_TOK) * SC_ROWS
            for hh in range(SC_ROWS // SC_LANES):
                vec = zero
                for rg in range(SC_LANES // SC_ROW_GROUP):
                    r0 = hh * SC_LANES + rg * SC_ROW_GROUP

                    def body(c, acc):
                        x_lo = x_v[t, pl.ds(c * 2 * SC_LANES, SC_LANES)]
                        x_hi = x_v[t, pl.ds(c * 2 * SC_LANES + SC_LANES, SC_LANES)]
                        out = []
                        for r in range(SC_ROW_GROUP):
                            lo, hi = _unpack_words(rows[r0 + r, pl.ds(c * SC_LANES, SC_LANES)])
                            out.append(acc[r] + lo * x_lo + hi * x_hi)
                        return tuple(out)

                    acc = lax.fori_loop(0, SC_WORD_CHUNKS, body, (zero,) * SC_ROW_GROUP, unroll=2)
                    for r in range(SC_ROW_GROUP):
                        vec = jnp.where(lane == rg * SC_ROW_GROUP + r, jnp.sum(acc[r]), vec)
                pre_v[t, pl.ds(col0 + hh * SC_LANES, SC_LANES)] = vec

        @pl.loop(0, tpw // SC_TOK_GROUP)
        def _(g):
            tok0 = base + g * SC_TOK_GROUP
            pltpu.sync_copy(idx_hbm.at[pl.ds(tok0 * SC_GATHERS_PER_TOK, SC_GATHERS)], idx_v)
            pltpu.sync_copy(x_hbm.at[pl.ds(tok0, SC_TOK_GROUP)], x_v)
            _sc_gather_loop(gather, compute)
            pltpu.sync_copy(pre_v, pre_hbm.at[pl.ds(tok0, SC_TOK_GROUP)])

    return k(x, idx2, u_tab)


def _peer_vacc(w, idx, v_tab):
    t_all = w.shape[0]
    tpw = t_all // SC_WORKERS
    idx2 = idx.reshape(t_all * SC_GATHERS_PER_TOK, SC_ROWS)
    half_w = SC_WORD_CHUNKS // 2

    @functools.partial(
        pl.kernel, mesh=_sc_mesh(), compiler_params=pltpu.CompilerParams(needs_layout_passes=False),
        out_type=jax.ShapeDtypeStruct((t_all, D_MODEL), f32),
        scratch_types=_sc_scratch(((SC_TOK_GROUP, PEER_SEL), (SC_TOK_GROUP, D_MODEL))), name="peer_vacc_sc")
    def k(w_hbm, idx_hbm, v_hbm, f_hbm, idx_v, w_v, f_v, rows0, rows1, sem0, sem1):
        bufs = ((rows0, sem0), (rows1, sem1))
        base = _sc_worker() * tpw
        zero = jnp.zeros((SC_LANES,), f32)

        def gather(q, b):
            return pltpu.make_async_copy(v_hbm.at[idx_v.at[q]], bufs[b][0], bufs[b][1])

        def compute(q, b):
            rows = bufs[b][0]
            t = q // SC_GATHERS_PER_TOK
            col0 = (q % SC_GATHERS_PER_TOK) * SC_ROWS
            tvec = jnp.full((SC_LANES,), t, i32)
            for half in range(2):
                def body(r, acc):
                    wr = plsc.load_gather(w_v, [tvec, jnp.full((SC_LANES,), col0 + r, i32)])
                    out = []
                    for c in range(half_w):
                        lo, hi = _unpack_words(rows[r, pl.ds((half * half_w + c) * SC_LANES, SC_LANES)])
                        out.append(acc[2 * c] + wr * lo)
                        out.append(acc[2 * c + 1] + wr * hi)
                    return tuple(out)

                acc = lax.fori_loop(0, SC_ROWS, body, (zero,) * (2 * half_w))
                for c in range(2 * half_w):
                    sl = pl.ds((half * 2 * half_w + c) * SC_LANES, SC_LANES)
                    f_v[t, sl] = f_v[t, sl] + acc[c]

        @pl.loop(0, tpw // SC_TOK_GROUP)
        def _(g):
            tok0 = base + g * SC_TOK_GROUP
            pltpu.sync_copy(idx_hbm.at[pl.ds(tok0 * SC_GATHERS_PER_TOK, SC_GATHERS)], idx_v)
            pltpu.sync_copy(w_hbm.at[pl.ds(tok0, SC_TOK_GROUP)], w_v)

            @pl.loop(0, SC_TOK_GROUP)
            def _(t):
                @pl.loop(0, SC_CHUNKS)
                def _(c):
                    f_v[t, pl.ds(c * SC_LANES, SC_LANES)] = zero

            _sc_gather_loop(gather, compute)
            pltpu.sync_copy(f_v, f_hbm.at[pl.ds(tok0, SC_TOK_GROUP)])

    return k(w, idx2, v_tab)


def _peer_act_kernel(pre_ref, gate_ref, w_ref):
    pre = pre_ref[...]
    w_ref[...] = gate_ref[...] * (0.5 * pre * (1.0 + lax.erf(pre * (2.0 ** -0.5))))


def _peer_act(pre, gate, tm=2048):
    m, n = pre.shape
    blk = pl.BlockSpec((tm, n), lambda i: (i, 0))
    return pl.pallas_call(
        _peer_act_kernel, grid=(m // tm,), in_specs=[blk, blk], out_specs=blk,
        out_shape=jax.ShapeDtypeStruct((m, n), f32), compiler_params=_params(("parallel",)), name="peer_act",
    )(pre, gate)


def _res_ln_kernel(x_ref, f_ref, g_ref, b_ref, o_ref):
    o_ref[...] = _layer_norm(ALPHA * x_ref[...] + f_ref[...], g_ref[...], b_ref[...])


def _res_ln(x, f, g, b, tm=512):
    m, n = x.shape
    blk = pl.BlockSpec((tm, n), lambda i: (i, 0))
    row = pl.BlockSpec((1, n), lambda i: (0, 0))
    return pl.pallas_call(
        _res_ln_kernel, grid=(m // tm,), in_specs=[blk, blk, row, row], out_specs=blk,
        out_shape=jax.ShapeDtypeStruct((m, n), f32), compiler_params=_params(("parallel",)), name="res_ln",
    )(x, f, g.reshape(1, n), b.reshape(1, n))


def _peer_expert_ln(x, idx, gate, u_tab, v_tab, g, b):
    pre = _peer_udot(x, idx, u_tab)
    w = _peer_act(pre, gate)
    f = _peer_vacc(w, idx, v_tab)
    return _res_ln(x, f, g, b)


def _peer_ffn_ln(x2, w_query, sub_keys, u_tab, v_tab, g, b):
    qp = _matmul(x2, w_query.astype(bf16))
    idx, gate = _peer_route(qp, sub_keys)
    return _peer_expert_ln(x2, idx, gate, _pack_table(u_tab), _pack_table(v_tab), g, b)


def kernel(x, dn_w_in, dn_conv, dn_a_log, dn_dt_bias, dn_norm_w, dn_w_out, shared_w_kv, attn_w_q, attn_w_out,
           peer_w_query, peer_sub_keys, peer_u, peer_v, ln_mix_g, ln_mix_b, ln_ffn_g, ln_ffn_b):
    batch, seq, d = x.shape
    x2 = x.reshape(batch * seq, d)
    n_a = DEPTH // 2
    tabs = _rope_lane_tables(seq)
    k_sh = v_sh = None
    for layer in range(DEPTH):
        if layer < n_a:
            o = _deltanet_mixer(x2, dn_w_in[layer], dn_conv[layer], dn_a_log[layer], dn_dt_bias[layer],
                                dn_norm_w[layer], batch, seq)
            x2 = _matmul_res_ln(o, dn_w_out[layer].astype(bf16), x2, ln_mix_g[layer], ln_mix_b[layer])
        else:
            j = layer - n_a
            if j == 0:
                gw = N_GROUPS * WIDTH
                k_sh = _matmul_rope(x2, shared_w_kv[:, :gw].astype(bf16), tabs, seq)
                v_sh = _matmul(x2, shared_w_kv[:, gw:].astype(bf16), tn=1024)
            q = _matmul_rope(x2, attn_w_q[j].astype(bf16), tabs, seq)
            outs, lses = [], []
            for gi, (window, dilation) in enumerate(ATTN_GROUPS):
                o, l = _dilated_attention(q, k_sh, v_sh, gi, window, dilation, batch, seq)
                outs.append(o)
                lses.append(l)
            x2 = _combine_out_ln(outs, lses, attn_w_out[j].astype(bf16), x2, ln_mix_g[layer], ln_mix_b[layer])
        x2 = _peer_ffn_ln(x2, peer_w_query[layer], peer_sub_keys[layer], peer_u[layer], peer_v[layer],
                          ln_ffn_g[layer], ln_ffn_b[layer])
    return x2.reshape(batch, seq, d)
```

```python
import functools
import math

import jax
import jax.numpy as jnp
from jax import lax
from jax.experimental import pallas as pl
from jax.experimental.pallas import tpu as pltpu
from jax.experimental.pallas import tpu_sc as plsc

f32 = jnp.float32
bf16 = jnp.bfloat16
i32 = jnp.int32

D_MODEL = 1024
DEPTH = 2
ALPHA = (2.0 * DEPTH) ** 0.25
LN_EPS = 1e-5

HEADS = 8
HEAD_DIM = 128
WIDTH = HEADS * HEAD_DIM
CONV_K = 4
DN_CHUNK = 64

ATTN_GROUPS = ((128, 1), (512, 4), (2048, 16))
N_GROUPS = len(ATTN_GROUPS)
ATTN_BLOCK = 128
ROT_DIM = HEAD_DIM // 4
ROPE_THETA = 500000.0

PEER_HEADS = 8
PEER_NKEYS = 128
PEER_TOPK = 16
PEER_QDIM = 256
PEER_SEL = PEER_HEADS * PEER_TOPK

LANES = 128
SUBLANES = 8
VMEM_LIMIT = 48 * 1024 * 1024


def _params(sem):
    return pltpu.CompilerParams(dimension_semantics=sem, vmem_limit_bytes=VMEM_LIMIT)


def _dot(a, b):
    return jnp.dot(a.astype(bf16), b.astype(bf16), preferred_element_type=f32)


def _dot_nt(a, b):
    return lax.dot_general(a.astype(bf16), b.astype(bf16), (((1,), (1,)), ((), ())), preferred_element_type=f32)


def _dot_tn(a, b):
    return lax.dot_general(a.astype(bf16), b.astype(bf16), (((0,), (0,)), ((), ())), preferred_element_type=f32)


def _sigmoid(x):
    return 1.0 / (1.0 + jnp.exp(-x))


def _layer_norm(y, g, b):
    mu = jnp.mean(y, -1, keepdims=True)
    yc = y - mu
    var = jnp.mean(yc * yc, -1, keepdims=True)
    return yc * lax.rsqrt(var + LN_EPS) * g + b


def _mm_kernel(a_ref, w_ref, o_ref):
    o_ref[...] = _dot(a_ref[...], w_ref[...])


def _matmul(a, w, tm=512, tn=None):
    m, k = a.shape
    n = w.shape[1]
    tn = tn or n
    return pl.pallas_call(
        _mm_kernel,
        grid=(m // tm, n // tn),
        in_specs=[pl.BlockSpec((tm, k), lambda i, j: (i, 0)), pl.BlockSpec((k, tn), lambda i, j: (0, j))],
        out_specs=pl.BlockSpec((tm, tn), lambda i, j: (i, j)),
        out_shape=jax.ShapeDtypeStruct((m, n), f32),
        compiler_params=_params(("parallel", "parallel")),
        name="matmul",
    )(a, w)


def _rope(y, c, s_lo, s_hi):
    return y * c + pltpu.roll(y, ROT_DIM // 2, axis=1) * s_hi + pltpu.roll(y, LANES - ROT_DIM // 2, axis=1) * s_lo


def _mm_rope_kernel(a_ref, w_ref, c_ref, slo_ref, shi_ref, o_ref, *, heads):
    y = _dot(a_ref[...], w_ref[...])
    c, s_lo, s_hi = c_ref[...], slo_ref[...], shi_ref[...]
    for h in range(heads):
        sl = slice(h * HEAD_DIM, (h + 1) * HEAD_DIM)
        o_ref[:, sl] = _rope(y[:, sl], c, s_lo, s_hi)


def _matmul_rope(a, w, tabs, seq, tm=512, tn=1024):
    m, k = a.shape
    n = w.shape[1]
    nsb = seq // tm
    tab_spec = pl.BlockSpec((tm, LANES), lambda i, j: (i % nsb, 0))
    return pl.pallas_call(
        functools.partial(_mm_rope_kernel, heads=tn // HEAD_DIM),
        grid=(m // tm, n // tn),
        in_specs=[pl.BlockSpec((tm, k), lambda i, j: (i, 0)), pl.BlockSpec((k, tn), lambda i, j: (0, j)),
                  tab_spec, tab_spec, tab_spec],
        out_specs=pl.BlockSpec((tm, tn), lambda i, j: (i, j)),
        out_shape=jax.ShapeDtypeStruct((m, n), f32),
        compiler_params=_params(("parallel", "parallel")),
        name="matmul_rope",
    )(a, w, *tabs)


def _mm_res_ln_kernel(a_ref, w_ref, x_ref, g_ref, b_ref, o_ref):
    h = _dot(a_ref[...], w_ref[...])
    o_ref[...] = _layer_norm(ALPHA * x_ref[...] + h, g_ref[...], b_ref[...])


def _matmul_res_ln(a, w, x, g, b, tm=512):
    m, k = a.shape
    n = w.shape[1]
    row = pl.BlockSpec((1, n), lambda i: (0, 0))
    return pl.pallas_call(
        _mm_res_ln_kernel,
        grid=(m // tm,),
        in_specs=[pl.BlockSpec((tm, k), lambda i: (i, 0)), pl.BlockSpec((k, n), lambda i: (0, 0)),
                  pl.BlockSpec((tm, n), lambda i: (i, 0)), row, row],
        out_specs=pl.BlockSpec((tm, n), lambda i: (i, 0)),
        out_shape=jax.ShapeDtypeStruct((m, n), f32),
        compiler_params=_params(("parallel",)),
        name="matmul_res_ln",
    )(a, w, x, g.reshape(1, n), b.reshape(1, n))


DN_PROJ_PAD = 4 * WIDTH + LANES


def _dn_prep_kernel(proj_ref, halo_ref, conv_ref, alog_ref, dtb_ref, q_ref, k_ref, v_ref, gate_ref, *, tm, nsb):
    i = pl.program_id(0)
    x = proj_ref[:, : 3 * WIDTH]
    halo = jnp.where(i % nsb == 0, 0.0, halo_ref[...])
    w = conv_ref[...]
    row8 = lax.broadcasted_iota(i32, (SUBLANES, 1), 0)
    acc = x * w[CONV_K - 1:CONV_K]
    for s in range(1, CONV_K):
        xs = pltpu.roll(x, s, axis=0)
        hs = pltpu.roll(halo, s, axis=0)
        first = jnp.where(row8 < s, hs, xs[:SUBLANES])
        xs = jnp.concatenate([first, xs[SUBLANES:]], axis=0)
        acc = acc + xs * w[CONV_K - 1 - s:CONV_K - s]
    qkv = acc * _sigmoid(acc)
    for h in range(HEADS):
        sl = slice(h * HEAD_DIM, (h + 1) * HEAD_DIM)
        qh = qkv[:, sl]
        q_ref[:, sl] = qh * lax.rsqrt(jnp.sum(qh * qh, -1, keepdims=True) + 1e-6) * (HEAD_DIM ** -0.5)
        kh = qkv[:, WIDTH + h * HEAD_DIM: WIDTH + (h + 1) * HEAD_DIM]
        k_ref[:, sl] = kh * lax.rsqrt(jnp.sum(kh * kh, -1, keepdims=True) + 1e-6)
    v_ref[...] = qkv[:, 2 * WIDTH:]
    ab = proj_ref[:, 4 * WIDTH:]
    z = ab + dtb_ref[...]
    softplus = jnp.maximum(z, 0.0) + jnp.log(1.0 + jnp.exp(-jnp.abs(z)))
    g = -jnp.exp(alog_ref[...]) * softplus
    row = lax.broadcasted_iota(i32, (tm, 1), 0) % DN_CHUNK
    s = 1
    while s < DN_CHUNK:
        g = g + jnp.where(row >= s, pltpu.roll(g, s, axis=0), 0.0)
        s *= 2
    lane = lax.broadcasted_iota(i32, (1, LANES), 1)
    gate_ref[...] = jnp.where(lane < HEADS, g, _sigmoid(ab))


def _dn_prep(proj, conv_w, a_log, dt_bias, seq, tm=256):
    m = proj.shape[0]
    nsb = seq // tm
    pad = LANES - HEADS
    alog = jnp.pad(a_log.astype(f32), (0, pad)).reshape(1, LANES)
    dtb = jnp.pad(dt_bias.astype(f32), (0, pad)).reshape(1, LANES)
    hb = tm // SUBLANES
    out_w = pl.BlockSpec((tm, WIDTH), lambda i: (i, 0))
    return pl.pallas_call(
        functools.partial(_dn_prep_kernel, tm=tm, nsb=nsb),
        grid=(m // tm,),
        in_specs=[pl.BlockSpec((tm, DN_PROJ_PAD), lambda i: (i, 0)),
                  pl.BlockSpec((SUBLANES, 3 * WIDTH), lambda i: (jnp.maximum(i * hb - 1, 0), 0)),
                  pl.BlockSpec((CONV_K, 3 * WIDTH), lambda i: (0, 0)),
                  pl.BlockSpec((1, LANES), lambda i: (0, 0)), pl.BlockSpec((1, LANES), lambda i: (0, 0))],
        out_specs=[out_w, out_w, out_w, pl.BlockSpec((tm, LANES), lambda i: (i, 0))],
        out_shape=[jax.ShapeDtypeStruct((m, WIDTH), f32)] * 3 + [jax.ShapeDtypeStruct((m, LANES), f32)],
        compiler_params=_params(("parallel",)),
        name="dn_prep",
    )(proj, proj, conv_w, alog, dtb)


def _dn_delta_kernel(q_ref, k_ref, v_ref, z_ref, gate_ref, gt_ref, nw_ref, o_ref, state_ref, *, chunks):
    @pl.when(pl.program_id(1) == 0)
    def _():
        state_ref[...] = jnp.zeros_like(state_ref)

    c = DN_CHUNK
    ri = lax.broadcasted_iota(i32, (c, c), 0)
    ci = lax.broadcasted_iota(i32, (c, c), 1)
    causal = ri >= ci
    strict = ri > ci
    eye = (ri == ci).astype(f32)
    nw = nw_ref[...]
    for ch in range(chunks):
        rows = slice(ch * c, (ch + 1) * c)
        for h in range(HEADS):
            sl = slice(h * HEAD_DIM, (h + 1) * HEAD_DIM)
            q, k, v = q_ref[rows, sl], k_ref[rows, sl], v_ref[rows, sl]
            gcol = gate_ref[rows, h:h + 1]
            beta = gate_ref[rows, HEADS + h:HEADS + h + 1]
            grow = gt_ref[h:h + 1, rows]
            diff = gcol - grow
            decay = jnp.where(causal, jnp.exp(jnp.where(causal, diff, 0.0)), 0.0)
            kb = k * beta
            m = jnp.where(strict, _dot_nt(kb, k) * decay, 0.0)
            inv = eye - m
            p = m
            step = 2
            while step < c:
                p = _dot(p, p)
                inv = inv + _dot(inv, p)
                step *= 2
            egc = jnp.exp(gcol)
            u = _dot(inv, v * beta)
            w = _dot(inv, kb * egc)
            a_qk = _dot_nt(q, k) * decay
            st = state_ref[h]
            v_new = u - _dot(w, st)
            o = _dot(q * egc, st) + _dot(a_qk, v_new)
            glast = gcol[c - 1:c, :]
            state_ref[h] = st * jnp.exp(glast) + _dot_tn(k * jnp.exp(glast - gcol), v_new)
            o = o * lax.rsqrt(jnp.mean(o * o, -1, keepdims=True) + 1e-6) * nw
            zz = z_ref[rows, sl]
            o_ref[rows, sl] = o * (zz * _sigmoid(zz))


def _dn_delta(q, k, v, proj, gates, gates_t, norm_w, batch, seq, cb=128):
    nb = seq // cb
    wide = pl.BlockSpec((cb, WIDTH), lambda b, i: (b * nb + i, 0))
    return pl.pallas_call(
        functools.partial(_dn_delta_kernel, chunks=cb // DN_CHUNK),
        grid=(batch, nb),
        in_specs=[wide, wide, wide,
                  pl.BlockSpec((cb, WIDTH), lambda b, i: (b * nb + i, 3)),
                  pl.BlockSpec((cb, LANES), lambda b, i: (b * nb + i, 0)),
                  pl.BlockSpec((SUBLANES, cb), lambda b, i: (0, b * nb + i)),
                  pl.BlockSpec((1, HEAD_DIM), lambda b, i: (0, 0))],
        out_specs=wide,
        out_shape=jax.ShapeDtypeStruct((batch * seq, WIDTH), f32),
        scratch_shapes=[pltpu.VMEM((HEADS, HEAD_DIM, HEAD_DIM), f32)],
        compiler_params=_params(("parallel", "arbitrary")),
        name="dn_delta",
    )(q, k, v, proj, gates, gates_t, norm_w.reshape(1, HEAD_DIM).astype(f32))


def _deltanet_mixer(x2, w_pad, conv_w, a_log, dt_bias, norm_w, batch, seq):
    proj = _matmul(x2, w_pad, tn=DN_PROJ_PAD // 3)
    q, k, v, gates = _dn_prep(proj, conv_w, a_log, dt_bias, seq)
    gates_t = gates[:, :SUBLANES].T
    return _dn_delta(q, k, v, proj, gates, gates_t, norm_w, batch, seq)


def _attn_kernel(q_ref, kp_ref, kc_ref, vp_ref, vc_ref, o_ref, l_ref, *, steps):
    nbi = pl.program_id(2)
    blk = ATTN_BLOCK
    qi = lax.broadcasted_iota(i32, (blk, 2 * blk), 0)
    kj = lax.broadcasted_iota(i32, (blk, 2 * blk), 1)
    dist = qi + blk - kj
    mask = (dist >= 0) & (dist <= steps) & ((kj >= blk) | (nbi > 0))
    scale = HEAD_DIM ** -0.5
    for h in range(HEADS):
        sl = slice(h * HEAD_DIM, (h + 1) * HEAD_DIM)
        q = q_ref[:, sl]
        kk = jnp.concatenate([kp_ref[:, sl], kc_ref[:, sl]], axis=0)
        vv = jnp.concatenate([vp_ref[:, sl], vc_ref[:, sl]], axis=0)
        sc = jnp.where(mask, _dot_nt(q, kk) * scale, -jnp.inf)
        mx = jnp.max(sc, -1, keepdims=True)
        p = jnp.exp(sc - mx)
        den = jnp.sum(p, -1, keepdims=True)
        o_ref[:, sl] = _dot(p, vv) / den
        l_ref[:, sl] = jnp.broadcast_to(mx + jnp.log(den), (blk, HEAD_DIM))


def _dilated_attention(q, k, v, group, window, dilation, batch, seq):
    n = seq // dilation
    nb = n // ATTN_BLOCK
    cols = dilation * N_GROUPS * WIDTH

    def view(t):
        return t.reshape(batch, n, cols)

    def cur(b, r, i):
        return (b, i, r * N_GROUPS + group)

    def prev(b, r, i):
        return (b, jnp.maximum(i - 1, 0), r * N_GROUPS + group)

    blk = (None, ATTN_BLOCK, WIDTH)
    out_cols = dilation * WIDTH
    out_spec = pl.BlockSpec(blk, lambda b, r, i: (b, i, r))
    o, lse = pl.pallas_call(
        functools.partial(_attn_kernel, steps=window // dilation),
        grid=(batch, dilation, nb),
        in_specs=[pl.BlockSpec(blk, cur), pl.BlockSpec(blk, prev), pl.BlockSpec(blk, cur),
                  pl.BlockSpec(blk, prev), pl.BlockSpec(blk, cur)],
        out_specs=[out_spec, out_spec],
        out_shape=[jax.ShapeDtypeStruct((batch, n, out_cols), f32)] * 2,
        compiler_params=_params(("parallel", "parallel", "parallel")),
        name=f"dilated_attn_d{dilation}",
    )(view(q), view(k), view(k), view(v), view(v))
    return o.reshape(batch * seq, WIDTH), lse.reshape(batch * seq, WIDTH)


def _combine_out_ln_kernel(o0, o1, o2, l0, l1, l2, w_ref, x_ref, g_ref, b_ref, out_ref):
    ls = [l0[...], l1[...], l2[...]]
    mx = jnp.maximum(jnp.maximum(ls[0], ls[1]), ls[2])
    es = [jnp.exp(l - mx) for l in ls]
    o = (es[0] * o0[...] + es[1] * o1[...] + es[2] * o2[...]) / (es[0] + es[1] + es[2])
    h = _dot(o, w_ref[...])
    out_ref[...] = _layer_norm(ALPHA * x_ref[...] + h, g_ref[...], b_ref[...])


def _combine_out_ln(outs, lses, w, x, g, b, tm=256):
    m, n = x.shape
    blk = pl.BlockSpec((tm, n), lambda i: (i, 0))
    row = pl.BlockSpec((1, n), lambda i: (0, 0))
    return pl.pallas_call(
        _combine_out_ln_kernel,
        grid=(m // tm,),
        in_specs=[blk] * 6 + [pl.BlockSpec((WIDTH, n), lambda i: (0, 0)), blk, row, row],
        out_specs=blk,
        out_shape=jax.ShapeDtypeStruct((m, n), f32),
        compiler_params=_params(("parallel",)),
        name="attn_combine_out_ln",
    )(*outs, *lses, w, x, g.reshape(1, n), b.reshape(1, n))


def _rope_lane_tables(seq):
    half = ROT_DIM // 2
    inv_freq = ROPE_THETA ** (-jnp.arange(half, dtype=f32) * 2.0 / ROT_DIM)
    ang = jnp.arange(seq, dtype=f32)[:, None] * inv_freq[None, :]
    cos, sin = jnp.cos(ang), jnp.sin(ang)
    ones = jnp.ones((seq, HEAD_DIM - ROT_DIM), f32)
    zeros = jnp.zeros((seq, HEAD_DIM - half), f32)
    c = jnp.concatenate([cos, cos, ones], axis=1)
    s_lo = jnp.concatenate([-sin, zeros], axis=1)
    s_hi = jnp.concatenate([jnp.zeros((seq, half), f32), sin, jnp.zeros((seq, HEAD_DIM - ROT_DIM), f32)], axis=1)
    return c, s_lo, s_hi


def _top_rows(s, order, payload, k):
    big = jnp.int32(2 ** 30)
    vals, pays = [], []
    for _ in range(k):
        m = jnp.max(s, axis=0, keepdims=True)
        pos = jnp.min(jnp.where(s == m, order, big), axis=0, keepdims=True)
        sel = order == pos
        pays.append(pos if payload is None else jnp.max(jnp.where(sel, payload, -1), axis=0, keepdims=True))
        vals.append(m)
        s = jnp.where(sel, -jnp.inf, s)
    return jnp.concatenate(vals, axis=0), jnp.concatenate(pays, axis=0)


def _pair_candidates(t1, t2, combine, fill):
    kt = PEER_TOPK
    row = lax.broadcasted_iota(i32, (SUBLANES, LANES), 0)
    bc = lambda t, r: jnp.broadcast_to(t[r:r + 1], (SUBLANES, LANES))
    lo1, hi1, lo2, hi2 = t1[:SUBLANES], t1[SUBLANES:], t2[:SUBLANES], t2[SUBLANES:]
    tiles = [combine(bc(t1, 0), lo2), combine(bc(t1, 0), hi2), combine(bc(t1, 1), lo2),
             jnp.where(row < kt // 3, combine(bc(t1, 2), lo2), fill),
             jnp.where(row < kt // 4, combine(bc(t1, 3), lo2), fill),
             combine(hi1, bc(t2, 0)),
             jnp.where(row >= 4, combine(lo1, bc(t2, 0)), fill),
             jnp.where(row >= 4, combine(lo1, bc(t2, 1)), jnp.where(row == 0, combine(bc(t1, 4), bc(t2, 2)), fill))]
    return jnp.concatenate(tiles, axis=0)


def _pair_rank():
    kt = PEER_TOPK
    row = lax.broadcasted_iota(i32, (SUBLANES, LANES), 0)
    unused = kt * kt + row
    tiles = [row, row + SUBLANES, kt + row,
             jnp.where(row < kt // 3, 2 * kt + row, unused),
             jnp.where(row < kt // 4, 3 * kt + row, unused + SUBLANES),
             (row + SUBLANES) * kt,
             jnp.where(row >= 4, row * kt, unused + 2 * SUBLANES),
             jnp.where(row >= 4, row * kt + 1, jnp.where(row == 0, 4 * kt + 2, unused + 3 * SUBLANES))]
    return jnp.concatenate(tiles, axis=0)


def _peer_route_kernel(qp_ref, keys_ref, idx_ref, gate_ref):
    kt = PEER_TOPK
    half = PEER_QDIM // 2
    keyid = lax.broadcasted_iota(i32, (PEER_NKEYS, LANES), 0)
    rank = _pair_rank()
    idx_rows, gate_rows = [], []
    for h in range(PEER_HEADS):
        tops = []
        for c in range(2):
            col = (h * 2 + c) * half
            s = _dot_nt(keys_ref[h * 2 + c], qp_ref[:, col:col + half])
            tops.append(_top_rows(s, keyid, None, kt))
        (v1, i1), (v2, i2) = tops
        cand = _pair_candidates(v1, v2, lambda a, b: a + b, -jnp.inf)
        cand_id = _pair_candidates(i1, i2, lambda a, b: a * PEER_NKEYS + b, -1)
        best, ids = _top_rows(cand, rank, cand_id, kt)
        e = jnp.exp(best - best[0:1])
        gate_rows.append(e / jnp.sum(e, axis=0, keepdims=True))
        idx_rows.append(ids)
    idx_ref[...] = jnp.concatenate(idx_rows, axis=0).astype(f32).T.astype(i32)
    gate_ref[...] = jnp.concatenate(gate_rows, axis=0).T


def _peer_route(qp, sub_keys):
    m = qp.shape[0]
    tb = LANES
    keys = sub_keys.reshape(PEER_HEADS * 2, PEER_NKEYS, PEER_QDIM // 2).astype(bf16)
    out = pl.BlockSpec((tb, PEER_SEL), lambda i: (i, 0))
    return pl.pallas_call(
        _peer_route_kernel,
        grid=(m // tb,),
        in_specs=[pl.BlockSpec((tb, qp.shape[1]), lambda i: (i, 0)),
                  pl.BlockSpec(keys.shape, lambda i: (0, 0, 0))],
        out_specs=[out, out],
        out_shape=[jax.ShapeDtypeStruct((m, PEER_SEL), i32), jax.ShapeDtypeStruct((m, PEER_SEL), f32)],
        compiler_params=_params(("parallel",)),
        name="peer_route",
    )(qp, keys)


SC_CORES = 2
SC_SUBCORES = 16
SC_LANES = 16
SC_WORKERS = SC_CORES * SC_SUBCORES
SC_CHUNKS = D_MODEL // SC_LANES
SC_WORDS = D_MODEL // 2
SC_WORD_CHUNKS = SC_WORDS // SC_LANES
SC_TOK_GROUP = 8
SC_ROWS = 64
SC_GATHERS_PER_TOK = PEER_SEL // SC_ROWS
SC_GATHERS = SC_TOK_GROUP * SC_GATHERS_PER_TOK
SC_ROW_GROUP = 8


def _pack_table(tab):
    e = tab.shape[0]
    bits = lax.bitcast_convert_type(tab.astype(bf16), jnp.uint16).astype(jnp.uint32)
    bits = bits.reshape(e, SC_WORD_CHUNKS, 2, SC_LANES)
    word = bits[:, :, 0, :] | (bits[:, :, 1, :] << 16)
    return lax.bitcast_convert_type(word.reshape(e, SC_WORDS), i32)


def _unpack_words(w):
    return lax.bitcast_convert_type(w << 16, f32), lax.bitcast_convert_type(w & (-65536), f32)


def _sc_mesh():
    return plsc.VectorSubcoreMesh(core_axis_name="c", subcore_axis_name="s")


def _sc_worker():
    return lax.axis_index("s") * SC_CORES + lax.axis_index("c")


def _sc_gather_loop(gather, compute):
    gather(0, 0).start()

    @pl.loop(0, SC_GATHERS, step=2)
    def _(q):
        gather(q + 1, 1).start()
        gather(q, 0).wait()
        compute(q, 0)

        @pl.when(q + 2 < SC_GATHERS)
        def _():
            gather(q + 2, 0).start()
        gather(q + 1, 1).wait()
        compute(q + 1, 1)


def _sc_scratch(stage_shape):
    return [pltpu.VMEM((SC_GATHERS, SC_ROWS), i32), pltpu.VMEM(stage_shape[0], f32), pltpu.VMEM(stage_shape[1], f32),
            pltpu.VMEM((SC_ROWS, SC_WORDS), i32), pltpu.VMEM((SC_ROWS, SC_WORDS), i32),
            pltpu.SemaphoreType.DMA, pltpu.SemaphoreType.DMA]


def _peer_udot(x, idx, u_tab):
    t_all = x.shape[0]
    tpw = t_all // SC_WORKERS
    idx2 = idx.reshape(t_all * SC_GATHERS_PER_TOK, SC_ROWS)

    @functools.partial(
        pl.kernel, mesh=_sc_mesh(), compiler_params=pltpu.CompilerParams(needs_layout_passes=False),
        out_type=jax.ShapeDtypeStruct((t_all, PEER_SEL), f32),
        scratch_types=_sc_scratch(((SC_TOK_GROUP, D_MODEL), (SC_TOK_GROUP, PEER_SEL))), name="peer_udot_sc")
    def k(x_hbm, idx_hbm, u_hbm, pre_hbm, idx_v, x_v, pre_v, rows0, rows1, sem0, sem1):
        bufs = ((rows0, sem0), (rows1, sem1))
        base = _sc_worker() * tpw
        lane = lax.iota(i32, SC_LANES)
        zero = jnp.zeros((SC_LANES,), f32)

        def gather(q, b):
            return pltpu.make_async_copy(u_hbm.at[idx_v.at[q]], bufs[b][0], bufs[b][1])

        def compute(q, b):
            rows = bufs[b][0]
            t = q // SC_GATHERS_PER_TOK
            col0 = (q % SC_GATHERS_PER_TOK) * SC_ROWS
            for hh in range(SC_ROWS // SC_LANES):
                vec = zero
                for rg in range(SC_LANES // SC_ROW_GROUP):
                    r0 = hh * SC_LANES + rg * SC_ROW_GROUP

                    def body(c, acc):
                        x_lo = x_v[t, pl.ds(c * 2 * SC_LANES, SC_LANES)]
                        x_hi = x_v[t, pl.ds(c * 2 * SC_LANES + SC_LANES, SC_LANES)]
                        out = []
                        for r in range(SC_ROW_GROUP):
                            lo, hi = _unpack_words(rows[r0 + r, pl.ds(c * SC_LANES, SC_LANES)])
                            out.append(acc[r] + lo * x_lo + hi * x_hi)
                        return tuple(out)

                    acc = lax.fori_loop(0, SC_WORD_CHUNKS, body, (zero,) * SC_ROW_GROUP, unroll=2)
                    for r in range(SC_ROW_GROUP):
                        vec = jnp.where(lane == rg * SC_ROW_GROUP + r, jnp.sum(acc[r]), vec)
                pre_v[t, pl.ds(col0 + hh * SC_LANES, SC_LANES)] = vec

        @pl.loop(0, tpw // SC_TOK_GROUP)
        def _(g):
            tok0 = base + g * SC_TOK_GROUP
            pltpu.sync_copy(idx_hbm.at[pl.ds(tok0 * SC_GATHERS_PER_TOK, SC_GATHERS)], idx_v)
            pltpu.sync_copy(x_hbm.at[pl.ds(tok0, SC_TOK_GROUP)], x_v)
            _sc_gather_loop(gather, compute)
            pltpu.sync_copy(pre_v, pre_hbm.at[pl.ds(tok0, SC_TOK_GROUP)])

    return k(x, idx2, u_tab)


def _peer_vacc(w, idx, v_tab):
    t_all = w.shape[0]
    tpw = t_all // SC_WORKERS
    idx2 = idx.reshape(t_all * SC_GATHERS_PER_TOK, SC_ROWS)
    half_w = SC_WORD_CHUNKS // 2

    @functools.partial(
        pl.kernel, mesh=_sc_mesh(), compiler_params=pltpu.CompilerParams(needs_layout_passes=False),
        out_type=jax.ShapeDtypeStruct((t_all, D_MODEL), f32),
        scratch_types=_sc_scratch(((SC_TOK_GROUP, PEER_SEL), (SC_TOK_GROUP, D_MODEL))), name="peer_vacc_sc")
    def k(w_hbm, idx_hbm, v_hbm, f_hbm, idx_v, w_v, f_v, rows0, rows1, sem0, sem1):
        bufs = ((rows0, sem0), (rows1, sem1))
        base = _sc_worker() * tpw
        zero = jnp.zeros((SC_LANES,), f32)

        def gather(q, b):
            return pltpu.make_async_copy(v_hbm.at[idx_v.at[q]], bufs[b][0], bufs[b][1])

        def compute(q, b):
            rows = bufs[b][0]
            t = q // SC_GATHERS_PER_TOK
            col0 = (q % SC_GATHERS_PER_TOK) * SC_ROWS
            tvec = jnp.full((SC_LANES,), t, i32)
            for half in range(2):
                def body(r, acc):
                    wr = plsc.load_gather(w_v, [tvec, jnp.full((SC_LANES,), col0 + r, i32)])
                    out = []
                    for c in range(half_w):
                        lo, hi = _unpack_words(rows[r, pl.ds((half * half_w + c) * SC_LANES, SC_LANES)])
                        out.append(acc[2 * c] + wr * lo)
                        out.append(acc[2 * c + 1] + wr * hi)
                    return tuple(out)

                acc = lax.fori_loop(0, SC_ROWS, body, (zero,) * (2 * half_w))
                for c in range(2 * half_w):
                    sl = pl.ds((half * 2 * half_w + c) * SC_LANES, SC_LANES)
                    f_v[t, sl] = f_v[t, sl] + acc[c]

        @pl.loop(0, tpw // SC_TOK_GROUP)
        def _(g):
            tok0 = base + g * SC_TOK_GROUP
            pltpu.sync_copy(idx_hbm.at[pl.ds(tok0 * SC_GATHERS_PER_TOK, SC_GATHERS)], idx_v)
            pltpu.sync_copy(w_hbm.at[pl.ds(tok0, SC_TOK_GROUP)], w_v)

            @pl.loop(0, SC_TOK_GROUP)
            def _(t):
                @pl.loop(0, SC_CHUNKS)
                def _(c):
                    f_v[t, pl.ds(c * SC_LANES, SC_LANES)] = zero

            _sc_gather_loop(gather, compute)
            pltpu.sync_copy(f_v, f_hbm.at[pl.ds(tok0, SC_TOK_GROUP)])

    return k(w, idx2, v_tab)


def _peer_act_kernel(pre_ref, gate_ref, w_ref):
    pre = pre_ref[...]
    w_ref[...] = gate_ref[...] * (0.5 * pre * (1.0 + lax.erf(pre * (2.0 ** -0.5))))


def _peer_act(pre, gate, tm=2048):
    m, n = pre.shape
    blk = pl.BlockSpec((tm, n), lambda i: (i, 0))
    return pl.pallas_call(
        _peer_act_kernel, grid=(m // tm,), in_specs=[blk, blk], out_specs=blk,
        out_shape=jax.ShapeDtypeStruct((m, n), f32), compiler_params=_params(("parallel",)), name="peer_act",
    )(pre, gate)


def _res_ln_kernel(x_ref, f_ref, g_ref, b_ref, o_ref):
    o_ref[...] = _layer_norm(ALPHA * x_ref[...] + f_ref[...], g_ref[...], b_ref[...])


def _res_ln(x, f, g, b, tm=512):
    m, n = x.shape
    blk = pl.BlockSpec((tm, n), lambda i: (i, 0))
    row = pl.BlockSpec((1, n), lambda i: (0, 0))
    return pl.pallas_call(
        _res_ln_kernel, grid=(m // tm,), in_specs=[blk, blk, row, row], out_specs=blk,
        out_shape=jax.ShapeDtypeStruct((m, n), f32), compiler_params=_params(("parallel",)), name="res_ln",
    )(x, f, g.reshape(1, n), b.reshape(1, n))


def _peer_expert_ln(x, idx, gate, u_tab, v_tab, g, b):
    pre = _peer_udot(x, idx, u_tab)
    w = _peer_act(pre, gate)
    f = _peer_vacc(w, idx, v_tab)
    return _res_ln(x, f, g, b)


def _peer_ffn_ln(x2, w_query, sub_keys, u_pack, v_pack, g, b):
    qp = _matmul(x2, w_query)
    idx, gate = _peer_route(qp, sub_keys)
    return _peer_expert_ln(x2, idx, gate, u_pack, v_pack, g, b)


def kernel(x, dn_w_in, dn_conv, dn_a_log, dn_dt_bias, dn_norm_w, dn_w_out, shared_w_kv, attn_w_q, attn_w_out,
           peer_w_query, peer_sub_keys, peer_u, peer_v, ln_mix_g, ln_mix_b, ln_ffn_g, ln_ffn_b):
    batch, seq, d = x.shape
    n_a = DEPTH // 2
    tabs = _rope_lane_tables(seq)
    gw = N_GROUPS * WIDTH
    dn_w_pad = [jnp.pad(w, ((0, 0), (0, DN_PROJ_PAD - w.shape[1]))).astype(bf16) for w in dn_w_in]
    dn_w_out_b = dn_w_out.astype(bf16)
    w_k, w_v = shared_w_kv[:, :gw].astype(bf16), shared_w_kv[:, gw:].astype(bf16)
    w_q, w_o = attn_w_q.astype(bf16), attn_w_out.astype(bf16)
    peer_wq = peer_w_query.astype(bf16)
    u_pack = [_pack_table(t) for t in peer_u]
    v_pack = [_pack_table(t) for t in peer_v]

    def trunk(x2):
        k_sh = v_sh = None
        for layer in range(DEPTH):
            if layer < n_a:
                o = _deltanet_mixer(x2, dn_w_pad[layer], dn_conv[layer], dn_a_log[layer], dn_dt_bias[layer],
                                    dn_norm_w[layer], 1, seq)
                x2 = _matmul_res_ln(o, dn_w_out_b[layer], x2, ln_mix_g[layer], ln_mix_b[layer])
            else:
                j = layer - n_a
                if j == 0:
                    k_sh = _matmul_rope(x2, w_k, tabs, seq)
                    v_sh = _matmul(x2, w_v, tn=1024)
                q = _matmul_rope(x2, w_q[j], tabs, seq)
                outs, lses = [], []
                for gi, (window, dilation) in enumerate(ATTN_GROUPS):
                    o, l = _dilated_attention(q, k_sh, v_sh, gi, window, dilation, 1, seq)
                    outs.append(o)
                    lses.append(l)
                x2 = _combine_out_ln(outs, lses, w_o[j], x2, ln_mix_g[layer], ln_mix_b[layer])
            x2 = _peer_ffn_ln(x2, peer_wq[layer], peer_sub_keys[layer], u_pack[layer], v_pack[layer],
                              ln_ffn_g[layer], ln_ffn_b[layer])
        return x2

    return jnp.stack([trunk(x[b]) for b in range(batch)], axis=0)
```

```python
import functools
import math

import jax
import jax.numpy as jnp
from jax import lax
from jax.experimental import pallas as pl
from jax.experimental.pallas import tpu as pltpu
from jax.experimental.pallas import tpu_sc as plsc

f32 = jnp.float32
bf16 = jnp.bfloat16
i32 = jnp.int32

D_MODEL = 1024
DEPTH = 2
ALPHA = (2.0 * DEPTH) ** 0.25
LN_EPS = 1e-5

HEADS = 8
HEAD_DIM = 128
WIDTH = HEADS * HEAD_DIM
CONV_K = 4
DN_CHUNK = 64

ATTN_GROUPS = ((128, 1), (512, 4), (2048, 16))
N_GROUPS = len(ATTN_GROUPS)
ATTN_BLOCK = 128
ROT_DIM = HEAD_DIM // 4
ROPE_THETA = 500000.0

PEER_HEADS = 8
PEER_NKEYS = 128
PEER_TOPK = 16
PEER_QDIM = 256
PEER_SEL = PEER_HEADS * PEER_TOPK

LANES = 128
SUBLANES = 8
VMEM_LIMIT = 48 * 1024 * 1024


def _params(sem):
    return pltpu.CompilerParams(dimension_semantics=sem, vmem_limit_bytes=VMEM_LIMIT)


def _dot(a, b):
    return jnp.dot(a.astype(bf16), b.astype(bf16), preferred_element_type=f32)


def _dot_nt(a, b):
    return lax.dot_general(a.astype(bf16), b.astype(bf16), (((1,), (1,)), ((), ())), preferred_element_type=f32)


def _dot_tn(a, b):
    return lax.dot_general(a.astype(bf16), b.astype(bf16), (((0,), (0,)), ((), ())), preferred_element_type=f32)


def _sigmoid(x):
    return 1.0 / (1.0 + jnp.exp(-x))


def _layer_norm(y, g, b):
    mu = jnp.mean(y, -1, keepdims=True)
    yc = y - mu
    var = jnp.mean(yc * yc, -1, keepdims=True)
    return yc * lax.rsqrt(var + LN_EPS) * g + b


def _mm_kernel(a_ref, w_ref, o_ref):
    o_ref[...] = _dot(a_ref[...], w_ref[...])


def _matmul(a, w, tm=512, tn=None):
    m, k = a.shape
    n = w.shape[1]
    tn = tn or n
    return pl.pallas_call(
        _mm_kernel,
        grid=(m // tm, n // tn),
        in_specs=[pl.BlockSpec((tm, k), lambda i, j: (i, 0)), pl.BlockSpec((k, tn), lambda i, j: (0, j))],
        out_specs=pl.BlockSpec((tm, tn), lambda i, j: (i, j)),
        out_shape=jax.ShapeDtypeStruct((m, n), f32),
        compiler_params=_params(("parallel", "parallel")),
        name="matmul",
    )(a, w)


def _rope(y, c, s_lo, s_hi):
    return y * c + pltpu.roll(y, ROT_DIM // 2, axis=1) * s_hi + pltpu.roll(y, LANES - ROT_DIM // 2, axis=1) * s_lo


def _mm_rope_kernel(a_ref, w_ref, c_ref, slo_ref, shi_ref, o_ref, *, heads):
    y = _dot(a_ref[...], w_ref[...])
    c, s_lo, s_hi = c_ref[...], slo_ref[...], shi_ref[...]
    for h in range(heads):
        sl = slice(h * HEAD_DIM, (h + 1) * HEAD_DIM)
        o_ref[:, sl] = _rope(y[:, sl], c, s_lo, s_hi)


def _matmul_rope(a, w, tabs, seq, tm=512, tn=1024):
    m, k = a.shape
    n = w.shape[1]
    nsb = seq // tm
    tab_spec = pl.BlockSpec((tm, LANES), lambda i, j: (i % nsb, 0))
    return pl.pallas_call(
        functools.partial(_mm_rope_kernel, heads=tn // HEAD_DIM),
        grid=(m // tm, n // tn),
        in_specs=[pl.BlockSpec((tm, k), lambda i, j: (i, 0)), pl.BlockSpec((k, tn), lambda i, j: (0, j)),
                  tab_spec, tab_spec, tab_spec],
        out_specs=pl.BlockSpec((tm, tn), lambda i, j: (i, j)),
        out_shape=jax.ShapeDtypeStruct((m, n), f32),
        compiler_params=_params(("parallel", "parallel")),
        name="matmul_rope",
    )(a, w, *tabs)


def _mm_res_ln_kernel(a_ref, w_ref, x_ref, g_ref, b_ref, o_ref):
    h = _dot(a_ref[...], w_ref[...])
    o_ref[...] = _layer_norm(ALPHA * x_ref[...] + h, g_ref[...], b_ref[...])


def _matmul_res_ln(a, w, x, g, b, tm=512):
    m, k = a.shape
    n = w.shape[1]
    row = pl.BlockSpec((1, n), lambda i: (0, 0))
    return pl.pallas_call(
        _mm_res_ln_kernel,
        grid=(m // tm,),
        in_specs=[pl.BlockSpec((tm, k), lambda i: (i, 0)), pl.BlockSpec((k, n), lambda i: (0, 0)),
                  pl.BlockSpec((tm, n), lambda i: (i, 0)), row, row],
        out_specs=pl.BlockSpec((tm, n), lambda i: (i, 0)),
        out_shape=jax.ShapeDtypeStruct((m, n), f32),
        compiler_params=_params(("parallel",)),
        name="matmul_res_ln",
    )(a, w, x, g.reshape(1, n), b.reshape(1, n))


DN_PROJ_PAD = 4 * WIDTH + LANES


def _dn_prep_kernel(proj_ref, halo_ref, conv_ref, alog_ref, dtb_ref, q_ref, k_ref, v_ref, gate_ref, *, tm, nsb):
    i = pl.program_id(0)
    x = proj_ref[:, : 3 * WIDTH]
    halo = jnp.where(i % nsb == 0, 0.0, halo_ref[...])
    w = conv_ref[...]
    row8 = lax.broadcasted_iota(i32, (SUBLANES, 1), 0)
    acc = x * w[CONV_K - 1:CONV_K]
    for s in range(1, CONV_K):
        xs = pltpu.roll(x, s, axis=0)
        hs = pltpu.roll(halo, s, axis=0)
        first = jnp.where(row8 < s, hs, xs[:SUBLANES])
        xs = jnp.concatenate([first, xs[SUBLANES:]], axis=0)
        acc = acc + xs * w[CONV_K - 1 - s:CONV_K - s]
    qkv = acc * _sigmoid(acc)
    for h in range(HEADS):
        sl = slice(h * HEAD_DIM, (h + 1) * HEAD_DIM)
        qh = qkv[:, sl]
        q_ref[:, sl] = qh * lax.rsqrt(jnp.sum(qh * qh, -1, keepdims=True) + 1e-6) * (HEAD_DIM ** -0.5)
        kh = qkv[:, WIDTH + h * HEAD_DIM: WIDTH + (h + 1) * HEAD_DIM]
        k_ref[:, sl] = kh * lax.rsqrt(jnp.sum(kh * kh, -1, keepdims=True) + 1e-6)
    v_ref[...] = qkv[:, 2 * WIDTH:]
    ab = proj_ref[:, 4 * WIDTH:]
    z = ab + dtb_ref[...]
    softplus = jnp.maximum(z, 0.0) + jnp.log(1.0 + jnp.exp(-jnp.abs(z)))
    g = -jnp.exp(alog_ref[...]) * softplus
    row = lax.broadcasted_iota(i32, (tm, 1), 0) % DN_CHUNK
    s = 1
    while s < DN_CHUNK:
        g = g + jnp.where(row >= s, pltpu.roll(g, s, axis=0), 0.0)
        s *= 2
    lane = lax.broadcasted_iota(i32, (1, LANES), 1)
    gate_ref[...] = jnp.where(lane < HEADS, g, _sigmoid(ab))


def _dn_prep(proj, conv_w, a_log, dt_bias, seq, tm=256):
    m = proj.shape[0]
    nsb = seq // tm
    pad = LANES - HEADS
    alog = jnp.pad(a_log.astype(f32), (0, pad)).reshape(1, LANES)
    dtb = jnp.pad(dt_bias.astype(f32), (0, pad)).reshape(1, LANES)
    hb = tm // SUBLANES
    out_w = pl.BlockSpec((tm, WIDTH), lambda i: (i, 0))
    return pl.pallas_call(
        functools.partial(_dn_prep_kernel, tm=tm, nsb=nsb),
        grid=(m // tm,),
        in_specs=[pl.BlockSpec((tm, DN_PROJ_PAD), lambda i: (i, 0)),
                  pl.BlockSpec((SUBLANES, 3 * WIDTH), lambda i: (jnp.maximum(i * hb - 1, 0), 0)),
                  pl.BlockSpec((CONV_K, 3 * WIDTH), lambda i: (0, 0)),
                  pl.BlockSpec((1, LANES), lambda i: (0, 0)), pl.BlockSpec((1, LANES), lambda i: (0, 0))],
        out_specs=[out_w, out_w, out_w, pl.BlockSpec((tm, LANES), lambda i: (i, 0))],
        out_shape=[jax.ShapeDtypeStruct((m, WIDTH), f32)] * 3 + [jax.ShapeDtypeStruct((m, LANES), f32)],
        compiler_params=_params(("parallel",)),
        name="dn_prep",
    )(proj, proj, conv_w, alog, dtb)


def _dn_delta_kernel(q_ref, k_ref, v_ref, z_ref, gate_ref, gt_ref, nw_ref, o_ref, state_ref, *, chunks):
    @pl.when(pl.program_id(1) == 0)
    def _():
        state_ref[...] = jnp.zeros_like(state_ref)

    c = DN_CHUNK
    ri = lax.broadcasted_iota(i32, (c, c), 0)
    ci = lax.broadcasted_iota(i32, (c, c), 1)
    causal = ri >= ci
    strict = ri > ci
    eye = (ri == ci).astype(f32)
    nw = nw_ref[...]
    inst = [(ch, h) for ch in range(chunks) for h in range(HEADS)]
    pre = {}
    for ch, h in inst:
        rows = slice(ch * c, (ch + 1) * c)
        sl = slice(h * HEAD_DIM, (h + 1) * HEAD_DIM)
        q, k, v = q_ref[rows, sl], k_ref[rows, sl], v_ref[rows, sl]
        gcol = gate_ref[rows, h:h + 1]
        beta = gate_ref[rows, HEADS + h:HEADS + h + 1]
        grow = gt_ref[h:h + 1, rows]
        decay = jnp.where(causal, jnp.exp(jnp.where(causal, gcol - grow, 0.0)), 0.0)
        kb = k * beta
        egc = jnp.exp(gcol)
        glast = gcol[c - 1:c, :]
        qk = _dot_nt(jnp.concatenate([q, kb], axis=0), k)
        m = jnp.where(strict, qk[c:] * decay, 0.0)
        pre[ch, h] = dict(a_qk=qk[:c] * decay, inv=eye - m, p=m,
                          rhs=jnp.concatenate([v * beta, kb * egc], axis=1), qe=q * egc,
                          kd=k * jnp.exp(glast - gcol), eg=jnp.exp(glast))
    step = 2
    while step < c:
        for key in inst:
            pre[key]["p"] = _dot(pre[key]["p"], pre[key]["p"])
        for key in inst:
            pre[key]["inv"] = pre[key]["inv"] + _dot(pre[key]["inv"], pre[key]["p"])
        step *= 2
    for key in inst:
        pre[key]["uw"] = _dot(pre[key]["inv"], pre[key]["rhs"])
    states = [state_ref[h] for h in range(HEADS)]
    for ch in range(chunks):
        rows = slice(ch * c, (ch + 1) * c)
        for h in range(HEADS):
            sl = slice(h * HEAD_DIM, (h + 1) * HEAD_DIM)
            d = pre[ch, h]
            u, w = d["uw"][:, :HEAD_DIM], d["uw"][:, HEAD_DIM:]
            ws = _dot(jnp.concatenate([w, d["qe"]], axis=0), states[h])
            v_new = u - ws[:c]
            o = ws[c:] + _dot(d["a_qk"], v_new)
            states[h] = states[h] * d["eg"] + _dot_tn(d["kd"], v_new)
            o = o * lax.rsqrt(jnp.mean(o * o, -1, keepdims=True) + 1e-6) * nw
            zz = z_ref[rows, sl]
            o_ref[rows, sl] = o * (zz * _sigmoid(zz))
    for h in range(HEADS):
        state_ref[h] = states[h]


def _dn_delta(q, k, v, proj, gates, gates_t, norm_w, batch, seq, cb=128):
    nb = seq // cb
    wide = pl.BlockSpec((cb, WIDTH), lambda b, i: (b * nb + i, 0))
    return pl.pallas_call(
        functools.partial(_dn_delta_kernel, chunks=cb // DN_CHUNK),
        grid=(batch, nb),
        in_specs=[wide, wide, wide,
                  pl.BlockSpec((cb, WIDTH), lambda b, i: (b * nb + i, 3)),
                  pl.BlockSpec((cb, LANES), lambda b, i: (b * nb + i, 0)),
                  pl.BlockSpec((SUBLANES, cb), lambda b, i: (0, b * nb + i)),
                  pl.BlockSpec((1, HEAD_DIM), lambda b, i: (0, 0))],
        out_specs=wide,
        out_shape=jax.ShapeDtypeStruct((batch * seq, WIDTH), f32),
        scratch_shapes=[pltpu.VMEM((HEADS, HEAD_DIM, HEAD_DIM), f32)],
        compiler_params=_params(("parallel", "arbitrary")),
        name="dn_delta",
    )(q, k, v, proj, gates, gates_t, norm_w.reshape(1, HEAD_DIM).astype(f32))


def _deltanet_mixer(x2, w_pad, conv_w, a_log, dt_bias, norm_w, batch, seq):
    proj = _matmul(x2, w_pad, tn=DN_PROJ_PAD // 3)
    q, k, v, gates = _dn_prep(proj, conv_w, a_log, dt_bias, seq)
    gates_t = gates[:, :SUBLANES].T
    return _dn_delta(q, k, v, proj, gates, gates_t, norm_w, batch, seq)


def _attn_kernel(q_ref, kp_ref, kc_ref, vp_ref, vc_ref, o_ref, l_ref, *, steps, dilation, heads):
    nbi = pl.program_id(1)
    blk = ATTN_BLOCK
    qi = lax.broadcasted_iota(i32, (blk, 2 * blk), 0)
    kj = lax.broadcasted_iota(i32, (blk, 2 * blk), 1)
    dist = qi + blk - kj
    mask = (dist >= 0) & (dist <= steps) & ((kj >= blk) | (nbi > 0))
    scale = HEAD_DIM ** -0.5
    for r in range(dilation):
        rows = pl.ds(r, blk, stride=dilation) if dilation > 1 else slice(None)
        for h in range(heads):
            sl = slice(h * HEAD_DIM, (h + 1) * HEAD_DIM)
            q = q_ref[rows, sl]
            kk = jnp.concatenate([kp_ref[rows, sl], kc_ref[rows, sl]], axis=0)
            vv = jnp.concatenate([vp_ref[rows, sl], vc_ref[rows, sl]], axis=0)
            sc = jnp.where(mask, _dot_nt(q, kk) * scale, -jnp.inf)
            mx = jnp.max(sc, -1, keepdims=True)
            p = jnp.exp(sc - mx)
            den = jnp.sum(p, -1, keepdims=True)
            o_ref[rows, sl] = _dot(p, vv) / den
            l_ref[rows, sl] = jnp.broadcast_to(mx + jnp.log(den), (blk, HEAD_DIM))


def _dilated_attention(q, k, v, group, window, dilation, batch, seq):
    rows = ATTN_BLOCK * dilation
    nb = seq // rows
    heads = HEADS if dilation == 1 else 1
    hblocks = HEADS // heads
    blk = (rows, heads * HEAD_DIM)

    def cur(b, i, j):
        return (b * nb + i, group * hblocks + j)

    def prev(b, i, j):
        return (b * nb + jnp.maximum(i - 1, 0), group * hblocks + j)

    out_spec = pl.BlockSpec(blk, lambda b, i, j: (b * nb + i, j))
    return pl.pallas_call(
        functools.partial(_attn_kernel, steps=window // dilation, dilation=dilation, heads=heads),
        grid=(batch, nb, hblocks),
        in_specs=[pl.BlockSpec(blk, cur), pl.BlockSpec(blk, prev), pl.BlockSpec(blk, cur),
                  pl.BlockSpec(blk, prev), pl.BlockSpec(blk, cur)],
        out_specs=[out_spec, out_spec],
        out_shape=[jax.ShapeDtypeStruct((batch * seq, WIDTH), f32)] * 2,
        compiler_params=_params(("parallel", "parallel", "parallel")),
        name=f"dilated_attn_d{dilation}",
    )(q, k, k, v, v)


def _combine_out_ln_kernel(o0, o1, o2, l0, l1, l2, w_ref, x_ref, g_ref, b_ref, out_ref):
    ls = [l0[...], l1[...], l2[...]]
    mx = jnp.maximum(jnp.maximum(ls[0], ls[1]), ls[2])
    es = [jnp.exp(l - mx) for l in ls]
    o = (es[0] * o0[...] + es[1] * o1[...] + es[2] * o2[...]) / (es[0] + es[1] + es[2])
    h = _dot(o, w_ref[...])
    out_ref[...] = _layer_norm(ALPHA * x_ref[...] + h, g_ref[...], b_ref[...])


def _combine_out_ln(outs, lses, w, x, g, b, tm=256):
    m, n = x.shape
    blk = pl.BlockSpec((tm, n), lambda i: (i, 0))
    row = pl.BlockSpec((1, n), lambda i: (0, 0))
    return pl.pallas_call(
        _combine_out_ln_kernel,
        grid=(m // tm,),
        in_specs=[blk] * 6 + [pl.BlockSpec((WIDTH, n), lambda i: (0, 0)), blk, row, row],
        out_specs=blk,
        out_shape=jax.ShapeDtypeStruct((m, n), f32),
        compiler_params=_params(("parallel",)),
        name="attn_combine_out_ln",
    )(*outs, *lses, w, x, g.reshape(1, n), b.reshape(1, n))


def _rope_lane_tables(seq):
    half = ROT_DIM // 2
    inv_freq = ROPE_THETA ** (-jnp.arange(half, dtype=f32) * 2.0 / ROT_DIM)
    ang = jnp.arange(seq, dtype=f32)[:, None] * inv_freq[None, :]
    cos, sin = jnp.cos(ang), jnp.sin(ang)
    ones = jnp.ones((seq, HEAD_DIM - ROT_DIM), f32)
    zeros = jnp.zeros((seq, HEAD_DIM - half), f32)
    c = jnp.concatenate([cos, cos, ones], axis=1)
    s_lo = jnp.concatenate([-sin, zeros], axis=1)
    s_hi = jnp.concatenate([jnp.zeros((seq, half), f32), sin, jnp.zeros((seq, HEAD_DIM - ROT_DIM), f32)], axis=1)
    return c, s_lo, s_hi


def _top_rows(s, order, payload, k):
    big = jnp.int32(2 ** 30)
    vals, pays = [], []
    for _ in range(k):
        m = jnp.max(s, axis=0, keepdims=True)
        pos = jnp.min(jnp.where(s == m, order, big), axis=0, keepdims=True)
        sel = order == pos
        pays.append(pos if payload is None else jnp.max(jnp.where(sel, payload, -1), axis=0, keepdims=True))
        vals.append(m)
        s = jnp.where(sel, -jnp.inf, s)
    return jnp.concatenate(vals, axis=0), jnp.concatenate(pays, axis=0)


def _pair_candidates(t1, t2, combine, fill):
    kt = PEER_TOPK
    row = lax.broadcasted_iota(i32, (SUBLANES, LANES), 0)
    bc = lambda t, r: jnp.broadcast_to(t[r:r + 1], (SUBLANES, LANES))
    lo1, hi1, lo2, hi2 = t1[:SUBLANES], t1[SUBLANES:], t2[:SUBLANES], t2[SUBLANES:]
    tiles = [combine(bc(t1, 0), lo2), combine(bc(t1, 0), hi2), combine(bc(t1, 1), lo2),
             jnp.where(row < kt // 3, combine(bc(t1, 2), lo2), fill),
             jnp.where(row < kt // 4, combine(bc(t1, 3), lo2), fill),
             combine(hi1, bc(t2, 0)),
             jnp.where(row >= 4, combine(lo1, bc(t2, 0)), fill),
             jnp.where(row >= 4, combine(lo1, bc(t2, 1)), jnp.where(row == 0, combine(bc(t1, 4), bc(t2, 2)), fill))]
    return jnp.concatenate(tiles, axis=0)


def _pair_rank():
    kt = PEER_TOPK
    row = lax.broadcasted_iota(i32, (SUBLANES, LANES), 0)
    unused = kt * kt + row
    tiles = [row, row + SUBLANES, kt + row,
             jnp.where(row < kt // 3, 2 * kt + row, unused),
             jnp.where(row < kt // 4, 3 * kt + row, unused + SUBLANES),
             (row + SUBLANES) * kt,
             jnp.where(row >= 4, row * kt, unused + 2 * SUBLANES),
             jnp.where(row >= 4, row * kt + 1, jnp.where(row == 0, 4 * kt + 2, unused + 3 * SUBLANES))]
    return jnp.concatenate(tiles, axis=0)


def _peer_route_kernel(qp_ref, keys_ref, idx_ref, gate_ref):
    kt = PEER_TOPK
    half = PEER_QDIM // 2
    keyid = lax.broadcasted_iota(i32, (PEER_NKEYS, LANES), 0)
    rank = _pair_rank()
    idx_rows, gate_rows = [], []
    for h in range(PEER_HEADS):
        tops = []
        for c in range(2):
            col = (h * 2 + c) * half
            s = _dot_nt(keys_ref[h * 2 + c], qp_ref[:, col:col + half])
            tops.append(_top_rows(s, keyid, None, kt))
        (v1, i1), (v2, i2) = tops
        cand = _pair_candidates(v1, v2, lambda a, b: a + b, -jnp.inf)
        cand_id = _pair_candidates(i1, i2, lambda a, b: a * PEER_NKEYS + b, -1)
        best, ids = _top_rows(cand, rank, cand_id, kt)
        e = jnp.exp(best - best[0:1])
        gate_rows.append(e / jnp.sum(e, axis=0, keepdims=True))
        idx_rows.append(ids)
    idx_ref[...] = jnp.concatenate(idx_rows, axis=0).astype(f32).T.astype(i32)
    gate_ref[...] = jnp.concatenate(gate_rows, axis=0).T


def _peer_route(qp, sub_keys):
    m = qp.shape[0]
    tb = LANES
    keys = sub_keys.reshape(PEER_HEADS * 2, PEER_NKEYS, PEER_QDIM // 2).astype(bf16)
    out = pl.BlockSpec((tb, PEER_SEL), lambda i: (i, 0))
    return pl.pallas_call(
        _peer_route_kernel,
        grid=(m // tb,),
        in_specs=[pl.BlockSpec((tb, qp.shape[1]), lambda i: (i, 0)),
                  pl.BlockSpec(keys.shape, lambda i: (0, 0, 0))],
        out_specs=[out, out],
        out_shape=[jax.ShapeDtypeStruct((m, PEER_SEL), i32), jax.ShapeDtypeStruct((m, PEER_SEL), f32)],
        compiler_params=_params(("parallel",)),
        name="peer_route",
    )(qp, keys)


SC_CORES = 2
SC_SUBCORES = 16
SC_LANES = 16
SC_WORKERS = SC_CORES * SC_SUBCORES
SC_CHUNKS = D_MODEL // SC_LANES
SC_WORDS = D_MODEL // 2
SC_WORD_CHUNKS = SC_WORDS // SC_LANES
SC_TOK_GROUP = 8
SC_ROWS = 64
SC_GATHERS_PER_TOK = PEER_SEL // SC_ROWS
SC_GATHERS = SC_TOK_GROUP * SC_GATHERS_PER_TOK
SC_ROW_GROUP = 8


def _pack_table(tab):
    e = tab.shape[0]
    bits = lax.bitcast_convert_type(tab.astype(bf16), jnp.uint16).astype(jnp.uint32)
    bits = bits.reshape(e, SC_WORD_CHUNKS, 2, SC_LANES)
    word = bits[:, :, 0, :] | (bits[:, :, 1, :] << 16)
    return lax.bitcast_convert_type(word.reshape(e, SC_WORDS), i32)


def _unpack_words(w):
    return lax.bitcast_convert_type(w << 16, f32), lax.bitcast_convert_type(w & (-65536), f32)


def _sc_mesh():
    return plsc.VectorSubcoreMesh(core_axis_name="c", subcore_axis_name="s")


def _sc_worker():
    return lax.axis_index("s") * SC_CORES + lax.axis_index("c")


def _sc_gather_loop(gather, compute):
    gather(0, 0).start()

    @pl.loop(0, SC_GATHERS, step=2)
    def _(q):
        gather(q + 1, 1).start()
        gather(q, 0).wait()
        compute(q, 0)

        @pl.when(q + 2 < SC_GATHERS)
        def _():
            gather(q + 2, 0).start()
        gather(q + 1, 1).wait()
        compute(q + 1, 1)


def _sc_scratch(stage_shape):
    return [pltpu.VMEM((SC_GATHERS, SC_ROWS), i32), pltpu.VMEM(stage_shape[0], f32), pltpu.VMEM(stage_shape[1], f32),
            pltpu.VMEM((SC_ROWS, SC_WORDS), i32), pltpu.VMEM((SC_ROWS, SC_WORDS), i32),
            pltpu.SemaphoreType.DMA, pltpu.SemaphoreType.DMA]


def _peer_udot(x, idx, u_tab):
    t_all = x.shape[0]
    tpw = t_all // SC_WORKERS
    idx2 = idx.reshape(t_all * SC_GATHERS_PER_TOK, SC_ROWS)

    @functools.partial(
        pl.kernel, mesh=_sc_mesh(), compiler_params=pltpu.CompilerParams(needs_layout_passes=False),
        out_type=jax.ShapeDtypeStruct((t_all, PEER_SEL), f32),
        scratch_types=_sc_scratch(((SC_TOK_GROUP, D_MODEL), (SC_TOK_GROUP, PEER_SEL))), name="peer_udot_sc")
    def k(x_hbm, idx_hbm, u_hbm, pre_hbm, idx_v, x_v, pre_v, rows0, rows1, sem0, sem1):
        bufs = ((rows0, sem0), (rows1, sem1))
        base = _sc_worker() * tpw
        lane = lax.iota(i32, SC_LANES)
        zero = jnp.zeros((SC_LANES,), f32)

        def gather(q, b):
            return pltpu.make_async_copy(u_hbm.at[idx_v.at[q]], bufs[b][0], bufs[b][1])

        def compute(q, b):
            rows = bufs[b][0]
            t = q // SC_GATHERS_PER_TOK
            col0 = (q % SC_GATHERS_PER_TOK) * SC_ROWS
            for hh in range(SC_ROWS // SC_LANES):
                vec = zero
                for rg in range(SC_LANES // SC_ROW_GROUP):
                    r0 = hh * SC_LANES + rg * SC_ROW_GROUP

                    def body(c, acc):
                        x_lo = x_v[t, pl.ds(c * 2 * SC_LANES, SC_LANES)]
                        x_hi = x_v[t, pl.ds(c * 2 * SC_LANES + SC_LANES, SC_LANES)]
                        out = []
                        for r in range(SC_ROW_GROUP):
                            lo, hi = _unpack_words(rows[r0 + r, pl.ds(c * SC_LANES, SC_LANES)])
                            out.append(acc[r] + lo * x_lo + hi * x_hi)
                        return tuple(out)

                    acc = lax.fori_loop(0, SC_WORD_CHUNKS, body, (zero,) * SC_ROW_GROUP, unroll=2)
                    for r in range(SC_ROW_GROUP):
                        vec = jnp.where(lane == rg * SC_ROW_GROUP + r, jnp.sum(acc[r]), vec)
                pre_v[t, pl.ds(col0 + hh * SC_LANES, SC_LANES)] = vec

        @pl.loop(0, tpw // SC_TOK_GROUP)
        def _(g):
            tok0 = base + g * SC_TOK_GROUP
            pltpu.sync_copy(idx_hbm.at[pl.ds(tok0 * SC_GATHERS_PER_TOK, SC_GATHERS)], idx_v)
            pltpu.sync_copy(x_hbm.at[pl.ds(tok0, SC_TOK_GROUP)], x_v)
            _sc_gather_loop(gather, compute)
            pltpu.sync_copy(pre_v, pre_hbm.at[pl.ds(tok0, SC_TOK_GROUP)])

    return k(x, idx2, u_tab)


def _peer_vacc(w, idx, v_tab):
    t_all = w.shape[0]
    tpw = t_all // SC_WORKERS
    idx2 = idx.reshape(t_all * SC_GATHERS_PER_TOK, SC_ROWS)
    half_w = SC_WORD_CHUNKS // 2

    @functools.partial(
        pl.kernel, mesh=_sc_mesh(), compiler_params=pltpu.CompilerParams(needs_layout_passes=False),
        out_type=jax.ShapeDtypeStruct((t_all, D_MODEL), f32),
        scratch_types=_sc_scratch(((SC_TOK_GROUP, PEER_SEL), (SC_TOK_GROUP, D_MODEL))), name="peer_vacc_sc")
    def k(w_hbm, idx_hbm, v_hbm, f_hbm, idx_v, w_v, f_v, rows0, rows1, sem0, sem1):
        bufs = ((rows0, sem0), (rows1, sem1))
        base = _sc_worker() * tpw
        zero = jnp.zeros((SC_LANES,), f32)

        def gather(q, b):
            return pltpu.make_async_copy(v_hbm.at[idx_v.at[q]], bufs[b][0], bufs[b][1])

        def compute(q, b):
            rows = bufs[b][0]
            t = q // SC_GATHERS_PER_TOK
            col0 = (q % SC_GATHERS_PER_TOK) * SC_ROWS
            tvec = jnp.full((SC_LANES,), t, i32)
            for half in range(2):
                def body(r, acc):
                    wr = plsc.load_gather(w_v, [tvec, jnp.full((SC_LANES,), col0 + r, i32)])
                    out = []
                    for c in range(half_w):
                        lo, hi = _unpack_words(rows[r, pl.ds((half * half_w + c) * SC_LANES, SC_LANES)])
                        out.append(acc[2 * c] + wr * lo)
                        out.append(acc[2 * c + 1] + wr * hi)
                    return tuple(out)

                acc = lax.fori_loop(0, SC_ROWS, body, (zero,) * (2 * half_w))
                for c in range(2 * half_w):
                    sl = pl.ds((half * 2 * half_w + c) * SC_LANES, SC_LANES)
                    f_v[t, sl] = f_v[t, sl] + acc[c]

        @pl.loop(0, tpw // SC_TOK_GROUP)
        def _(g):
            tok0 = base + g * SC_TOK_GROUP
            pltpu.sync_copy(idx_hbm.at[pl.ds(tok0 * SC_GATHERS_PER_TOK, SC_GATHERS)], idx_v)
            pltpu.sync_copy(w_hbm.at[pl.ds(tok0, SC_TOK_GROUP)], w_v)

            @pl.loop(0, SC_TOK_GROUP)
            def _(t):
                @pl.loop(0, SC_CHUNKS)
                def _(c):
                    f_v[t, pl.ds(c * SC_LANES, SC_LANES)] = zero

            _sc_gather_loop(gather, compute)
            pltpu.sync_copy(f_v, f_hbm.at[pl.ds(tok0, SC_TOK_GROUP)])

    return k(w, idx2, v_tab)


def _peer_act_kernel(pre_ref, gate_ref, w_ref):
    pre = pre_ref[...]
    w_ref[...] = gate_ref[...] * (0.5 * pre * (1.0 + lax.erf(pre * (2.0 ** -0.5))))


def _peer_act(pre, gate, tm=2048):
    m, n = pre.shape
    blk = pl.BlockSpec((tm, n), lambda i: (i, 0))
    return pl.pallas_call(
        _peer_act_kernel, grid=(m // tm,), in_specs=[blk, blk], out_specs=blk,
        out_shape=jax.ShapeDtypeStruct((m, n), f32), compiler_params=_params(("parallel",)), name="peer_act",
    )(pre, gate)


def _res_ln_kernel(x_ref, f_ref, g_ref, b_ref, o_ref):
    o_ref[...] = _layer_norm(ALPHA * x_ref[...] + f_ref[...], g_ref[...], b_ref[...])


def _res_ln(x, f, g, b, tm=512):
    m, n = x.shape
    blk = pl.BlockSpec((tm, n), lambda i: (i, 0))
    row = pl.BlockSpec((1, n), lambda i: (0, 0))
    return pl.pallas_call(
        _res_ln_kernel, grid=(m // tm,), in_specs=[blk, blk, row, row], out_specs=blk,
        out_shape=jax.ShapeDtypeStruct((m, n), f32), compiler_params=_params(("parallel",)), name="res_ln",
    )(x, f, g.reshape(1, n), b.reshape(1, n))


def _peer_expert_ln(x, idx, gate, u_tab, v_tab, g, b):
    pre = _peer_udot(x, idx, u_tab)
    w = _peer_act(pre, gate)
    f = _peer_vacc(w, idx, v_tab)
    return _res_ln(x, f, g, b)


def _peer_ffn_ln(x2, w_query, sub_keys, u_pack, v_pack, g, b):
    qp = _matmul(x2, w_query)
    idx, gate = _peer_route(qp, sub_keys)
    return _peer_expert_ln(x2, idx, gate, u_pack, v_pack, g, b)


def kernel(x, dn_w_in, dn_conv, dn_a_log, dn_dt_bias, dn_norm_w, dn_w_out, shared_w_kv, attn_w_q, attn_w_out,
           peer_w_query, peer_sub_keys, peer_u, peer_v, ln_mix_g, ln_mix_b, ln_ffn_g, ln_ffn_b):
    batch, seq, d = x.shape
    n_a = DEPTH // 2
    tabs = _rope_lane_tables(seq)
    gw = N_GROUPS * WIDTH
    dn_w_pad = [jnp.pad(w, ((0, 0), (0, DN_PROJ_PAD - w.shape[1]))).astype(bf16) for w in dn_w_in]
    dn_w_out_b = dn_w_out.astype(bf16)
    w_k, w_v = shared_w_kv[:, :gw].astype(bf16), shared_w_kv[:, gw:].astype(bf16)
    w_q, w_o = attn_w_q.astype(bf16), attn_w_out.astype(bf16)
    peer_wq = peer_w_query.astype(bf16)
    u_pack = [_pack_table(t) for t in peer_u]
    v_pack = [_pack_table(t) for t in peer_v]

    def trunk(x2):
        k_sh = v_sh = None
        for layer in range(DEPTH):
            if layer < n_a:
                o = _deltanet_mixer(x2, dn_w_pad[layer], dn_conv[layer], dn_a_log[layer], dn_dt_bias[layer],
                                    dn_norm_w[layer], 1, seq)
                x2 = _matmul_res_ln(o, dn_w_out_b[layer], x2, ln_mix_g[layer], ln_mix_b[layer])
            else:
                j = layer - n_a
                if j == 0:
                    k_sh = _matmul_rope(x2, w_k, tabs, seq)
                    v_sh = _matmul(x2, w_v, tn=1024)
                q = _matmul_rope(x2, w_q[j], tabs, seq)
                outs, lses = [], []
                for gi, (window, dilation) in enumerate(ATTN_GROUPS):
                    o, l = _dilated_attention(q, k_sh, v_sh, gi, window, dilation, 1, seq)
                    outs.append(o)
                    lses.append(l)
                x2 = _combine_out_ln(outs, lses, w_o[j], x2, ln_mix_g[layer], ln_mix_b[layer])
            x2 = _peer_ffn_ln(x2, peer_wq[layer], peer_sub_keys[layer], u_pack[layer], v_pack[layer],
                              ln_ffn_g[layer], ln_ffn_b[layer])
        return x2

    return jnp.stack([trunk(x[b]) for b in range(batch)], axis=0)
```

```python
import functools
import math

import jax
import jax.numpy as jnp
from jax import lax
from jax.experimental import pallas as pl
from jax.experimental.pallas import tpu as pltpu
from jax.experimental.pallas import tpu_sc as plsc

f32 = jnp.float32
bf16 = jnp.bfloat16
i32 = jnp.int32

D_MODEL = 1024
DEPTH = 2
ALPHA = (2.0 * DEPTH) ** 0.25
LN_EPS = 1e-5

HEADS = 8
HEAD_DIM = 128
WIDTH = HEADS * HEAD_DIM
CONV_K = 4
DN_CHUNK = 64

ATTN_GROUPS = ((128, 1), (512, 4), (2048, 16))
N_GROUPS = len(ATTN_GROUPS)
ATTN_BLOCK = 128
ROT_DIM = HEAD_DIM // 4
ROPE_THETA = 500000.0

PEER_HEADS = 8
PEER_NKEYS = 128
PEER_TOPK = 16
PEER_QDIM = 256
PEER_SEL = PEER_HEADS * PEER_TOPK

LANES = 128
SUBLANES = 8
VMEM_LIMIT = 48 * 1024 * 1024


def _params(sem):
    return pltpu.CompilerParams(dimension_semantics=sem, vmem_limit_bytes=VMEM_LIMIT)


def _dot(a, b):
    return jnp.dot(a.astype(bf16), b.astype(bf16), preferred_element_type=f32)


def _dot_nt(a, b):
    return lax.dot_general(a.astype(bf16), b.astype(bf16), (((1,), (1,)), ((), ())), preferred_element_type=f32)


def _dot_tn(a, b):
    return lax.dot_general(a.astype(bf16), b.astype(bf16), (((0,), (0,)), ((), ())), preferred_element_type=f32)


def _sigmoid(x):
    return 1.0 / (1.0 + jnp.exp(-x))


def _layer_norm(y, g, b):
    mu = jnp.mean(y, -1, keepdims=True)
    yc = y - mu
    var = jnp.mean(yc * yc, -1, keepdims=True)
    return yc * lax.rsqrt(var + LN_EPS) * g + b


def _mm_kernel(a_ref, w_ref, o_ref):
    o_ref[...] = _dot(a_ref[...], w_ref[...])


def _matmul(a, w, tm=512, tn=None):
    m, k = a.shape
    n = w.shape[1]
    tn = tn or n
    return pl.pallas_call(
        _mm_kernel,
        grid=(m // tm, n // tn),
        in_specs=[pl.BlockSpec((tm, k), lambda i, j: (i, 0)), pl.BlockSpec((k, tn), lambda i, j: (0, j))],
        out_specs=pl.BlockSpec((tm, tn), lambda i, j: (i, j)),
        out_shape=jax.ShapeDtypeStruct((m, n), f32),
        compiler_params=_params(("parallel", "parallel")),
        name="matmul",
    )(a, w)


def _rope(y, c, s_lo, s_hi):
    return y * c + pltpu.roll(y, ROT_DIM // 2, axis=1) * s_hi + pltpu.roll(y, LANES - ROT_DIM // 2, axis=1) * s_lo


def _mm_rope_kernel(a_ref, w_ref, c_ref, slo_ref, shi_ref, o_ref, *, heads):
    y = _dot(a_ref[...], w_ref[...])
    c, s_lo, s_hi = c_ref[...], slo_ref[...], shi_ref[...]
    for h in range(heads):
        sl = slice(h * HEAD_DIM, (h + 1) * HEAD_DIM)
        o_ref[:, sl] = _rope(y[:, sl], c, s_lo, s_hi)


def _matmul_rope(a, w, tabs, seq, tm=512, tn=1024):
    m, k = a.shape
    n = w.shape[1]
    nsb = seq // tm
    tab_spec = pl.BlockSpec((tm, LANES), lambda i, j: (i % nsb, 0))
    return pl.pallas_call(
        functools.partial(_mm_rope_kernel, heads=tn // HEAD_DIM),
        grid=(m // tm, n // tn),
        in_specs=[pl.BlockSpec((tm, k), lambda i, j: (i, 0)), pl.BlockSpec((k, tn), lambda i, j: (0, j)),
                  tab_spec, tab_spec, tab_spec],
        out_specs=pl.BlockSpec((tm, tn), lambda i, j: (i, j)),
        out_shape=jax.ShapeDtypeStruct((m, n), f32),
        compiler_params=_params(("parallel", "parallel")),
        name="matmul_rope",
    )(a, w, *tabs)


def _mm_res_ln_kernel(a_ref, w_ref, x_ref, g_ref, b_ref, o_ref):
    h = _dot(a_ref[...], w_ref[...])
    o_ref[...] = _layer_norm(ALPHA * x_ref[...] + h, g_ref[...], b_ref[...])


def _matmul_res_ln(a, w, x, g, b, tm=512):
    m, k = a.shape
    n = w.shape[1]
    row = pl.BlockSpec((1, n), lambda i: (0, 0))
    return pl.pallas_call(
        _mm_res_ln_kernel,
        grid=(m // tm,),
        in_specs=[pl.BlockSpec((tm, k), lambda i: (i, 0)), pl.BlockSpec((k, n), lambda i: (0, 0)),
                  pl.BlockSpec((tm, n), lambda i: (i, 0)), row, row],
        out_specs=pl.BlockSpec((tm, n), lambda i: (i, 0)),
        out_shape=jax.ShapeDtypeStruct((m, n), f32),
        compiler_params=_params(("parallel",)),
        name="matmul_res_ln",
    )(a, w, x, g.reshape(1, n), b.reshape(1, n))


DN_PROJ_PAD = 4 * WIDTH + LANES


def _dn_prep_kernel(proj_ref, halo_ref, conv_ref, alog_ref, dtb_ref, q_ref, k_ref, v_ref, gate_ref, *, tm, nsb):
    i = pl.program_id(0)
    x = proj_ref[:, : 3 * WIDTH]
    halo = jnp.where(i % nsb == 0, 0.0, halo_ref[...])
    w = conv_ref[...]
    row8 = lax.broadcasted_iota(i32, (SUBLANES, 1), 0)
    acc = x * w[CONV_K - 1:CONV_K]
    for s in range(1, CONV_K):
        xs = pltpu.roll(x, s, axis=0)
        hs = pltpu.roll(halo, s, axis=0)
        first = jnp.where(row8 < s, hs, xs[:SUBLANES])
        xs = jnp.concatenate([first, xs[SUBLANES:]], axis=0)
        acc = acc + xs * w[CONV_K - 1 - s:CONV_K - s]
    qkv = acc * _sigmoid(acc)
    for h in range(HEADS):
        sl = slice(h * HEAD_DIM, (h + 1) * HEAD_DIM)
        qh = qkv[:, sl]
        q_ref[:, sl] = qh * lax.rsqrt(jnp.sum(qh * qh, -1, keepdims=True) + 1e-6) * (HEAD_DIM ** -0.5)
        kh = qkv[:, WIDTH + h * HEAD_DIM: WIDTH + (h + 1) * HEAD_DIM]
        k_ref[:, sl] = kh * lax.rsqrt(jnp.sum(kh * kh, -1, keepdims=True) + 1e-6)
    v_ref[...] = qkv[:, 2 * WIDTH:]
    ab = proj_ref[:, 4 * WIDTH:]
    z = ab + dtb_ref[...]
    softplus = jnp.maximum(z, 0.0) + jnp.log(1.0 + jnp.exp(-jnp.abs(z)))
    g = -jnp.exp(alog_ref[...]) * softplus
    row = lax.broadcasted_iota(i32, (tm, 1), 0) % DN_CHUNK
    s = 1
    while s < DN_CHUNK:
        g = g + jnp.where(row >= s, pltpu.roll(g, s, axis=0), 0.0)
        s *= 2
    lane = lax.broadcasted_iota(i32, (1, LANES), 1)
    gate_ref[...] = jnp.where(lane < HEADS, g, _sigmoid(ab))


def _dn_prep(proj, conv_w, a_log, dt_bias, seq, tm=256):
    m = proj.shape[0]
    nsb = seq // tm
    pad = LANES - HEADS
    alog = jnp.pad(a_log.astype(f32), (0, pad)).reshape(1, LANES)
    dtb = jnp.pad(dt_bias.astype(f32), (0, pad)).reshape(1, LANES)
    hb = tm // SUBLANES
    out_w = pl.BlockSpec((tm, WIDTH), lambda i: (i, 0))
    return pl.pallas_call(
        functools.partial(_dn_prep_kernel, tm=tm, nsb=nsb),
        grid=(m // tm,),
        in_specs=[pl.BlockSpec((tm, DN_PROJ_PAD), lambda i: (i, 0)),
                  pl.BlockSpec((SUBLANES, 3 * WIDTH), lambda i: (jnp.maximum(i * hb - 1, 0), 0)),
                  pl.BlockSpec((CONV_K, 3 * WIDTH), lambda i: (0, 0)),
                  pl.BlockSpec((1, LANES), lambda i: (0, 0)), pl.BlockSpec((1, LANES), lambda i: (0, 0))],
        out_specs=[out_w, out_w, out_w, pl.BlockSpec((tm, LANES), lambda i: (i, 0))],
        out_shape=[jax.ShapeDtypeStruct((m, WIDTH), f32)] * 3 + [jax.ShapeDtypeStruct((m, LANES), f32)],
        compiler_params=_params(("parallel",)),
        name="dn_prep",
    )(proj, proj, conv_w, alog, dtb)


def _dn_delta_kernel(q_ref, k_ref, v_ref, z_ref, gate_ref, gt_ref, nw_ref, o_ref, state_ref, *, chunks):
    @pl.when(pl.program_id(1) == 0)
    def _():
        state_ref[...] = jnp.zeros_like(state_ref)

    c = DN_CHUNK
    ri = lax.broadcasted_iota(i32, (c, c), 0)
    ci = lax.broadcasted_iota(i32, (c, c), 1)
    causal = ri >= ci
    strict = ri > ci
    eye = (ri == ci).astype(f32)
    nw = nw_ref[...]
    inst = [(ch, h) for ch in range(chunks) for h in range(HEADS)]
    pre = {}
    for ch, h in inst:
        rows = slice(ch * c, (ch + 1) * c)
        sl = slice(h * HEAD_DIM, (h + 1) * HEAD_DIM)
        q, k, v = q_ref[rows, sl], k_ref[rows, sl], v_ref[rows, sl]
        gcol = gate_ref[rows, h:h + 1]
        beta = gate_ref[rows, HEADS + h:HEADS + h + 1]
        grow = gt_ref[h:h + 1, rows]
        decay = jnp.where(causal, jnp.exp(jnp.where(causal, gcol - grow, 0.0)), 0.0)
        kb = k * beta
        egc = jnp.exp(gcol)
        glast = gcol[c - 1:c, :]
        qk = _dot_nt(jnp.concatenate([q, kb], axis=0), k)
        m = jnp.where(strict, qk[c:] * decay, 0.0)
        pre[ch, h] = dict(a_qk=qk[:c] * decay, inv=eye - m, p=m,
                          rhs=jnp.concatenate([v * beta, kb * egc], axis=1), qe=q * egc,
                          kd=k * jnp.exp(glast - gcol), eg=jnp.exp(glast))
    step = 2
    while step < c:
        for key in inst:
            pre[key]["p"] = _dot(pre[key]["p"], pre[key]["p"])
        for key in inst:
            pre[key]["inv"] = pre[key]["inv"] + _dot(pre[key]["inv"], pre[key]["p"])
        step *= 2
    for key in inst:
        pre[key]["uw"] = _dot(pre[key]["inv"], pre[key]["rhs"])
    states = [state_ref[h] for h in range(HEADS)]
    for ch in range(chunks):
        rows = slice(ch * c, (ch + 1) * c)
        for h in range(HEADS):
            sl = slice(h * HEAD_DIM, (h + 1) * HEAD_DIM)
            d = pre[ch, h]
            u, w = d["uw"][:, :HEAD_DIM], d["uw"][:, HEAD_DIM:]
            ws = _dot(jnp.concatenate([w, d["qe"]], axis=0), states[h])
            v_new = u - ws[:c]
            o = ws[c:] + _dot(d["a_qk"], v_new)
            states[h] = states[h] * d["eg"] + _dot_tn(d["kd"], v_new)
            o = o * lax.rsqrt(jnp.mean(o * o, -1, keepdims=True) + 1e-6) * nw
            zz = z_ref[rows, sl]
            o_ref[rows, sl] = o * (zz * _sigmoid(zz))
    for h in range(HEADS):
        state_ref[h] = states[h]


def _dn_delta(q, k, v, proj, gates, gates_t, norm_w, batch, seq, cb=128):
    nb = seq // cb
    wide = pl.BlockSpec((cb, WIDTH), lambda b, i: (b * nb + i, 0))
    return pl.pallas_call(
        functools.partial(_dn_delta_kernel, chunks=cb // DN_CHUNK),
        grid=(batch, nb),
        in_specs=[wide, wide, wide,
                  pl.BlockSpec((cb, WIDTH), lambda b, i: (b * nb + i, 3)),
                  pl.BlockSpec((cb, LANES), lambda b, i: (b * nb + i, 0)),
                  pl.BlockSpec((SUBLANES, cb), lambda b, i: (0, b * nb + i)),
                  pl.BlockSpec((1, HEAD_DIM), lambda b, i: (0, 0))],
        out_specs=wide,
        out_shape=jax.ShapeDtypeStruct((batch * seq, WIDTH), f32),
        scratch_shapes=[pltpu.VMEM((HEADS, HEAD_DIM, HEAD_DIM), f32)],
        compiler_params=_params(("parallel", "arbitrary")),
        name="dn_delta",
    )(q, k, v, proj, gates, gates_t, norm_w.reshape(1, HEAD_DIM).astype(f32))


def _deltanet_mixer(x2, w_pad, conv_w, a_log, dt_bias, norm_w, batch, seq):
    proj = _matmul(x2, w_pad, tn=DN_PROJ_PAD // 3)
    q, k, v, gates = _dn_prep(proj, conv_w, a_log, dt_bias, seq)
    gates_t = gates[:, :SUBLANES].T
    return _dn_delta(q, k, v, proj, gates, gates_t, norm_w, batch, seq)


def _attn_kernel(q_ref, kp_ref, kc_ref, vp_ref, vc_ref, o_ref, l_ref, *, steps, dilation, heads):
    nbi = pl.program_id(1)
    blk = ATTN_BLOCK
    qi = lax.broadcasted_iota(i32, (blk, 2 * blk), 0)
    kj = lax.broadcasted_iota(i32, (blk, 2 * blk), 1)
    dist = qi + blk - kj
    mask = (dist >= 0) & (dist <= steps) & ((kj >= blk) | (nbi > 0))
    scale = HEAD_DIM ** -0.5
    for r in range(dilation):
        rows = pl.ds(r, blk, stride=dilation) if dilation > 1 else slice(None)
        for h in range(heads):
            sl = slice(h * HEAD_DIM, (h + 1) * HEAD_DIM)
            q = q_ref[rows, sl]
            kk = jnp.concatenate([kp_ref[rows, sl], kc_ref[rows, sl]], axis=0)
            vv = jnp.concatenate([vp_ref[rows, sl], vc_ref[rows, sl]], axis=0)
            sc = jnp.where(mask, _dot_nt(q, kk) * scale, -jnp.inf)
            mx = jnp.max(sc, -1, keepdims=True)
            p = jnp.exp(sc - mx)
            den = jnp.sum(p, -1, keepdims=True)
            o_ref[rows, sl] = _dot(p, vv) / den
            l_ref[rows, sl] = jnp.broadcast_to(mx + jnp.log(den), (blk, HEAD_DIM))


def _dilated_attention(q, k, v, group, window, dilation, batch, seq):
    rows = ATTN_BLOCK * dilation
    nb = seq // rows
    heads = HEADS if dilation == 1 else 1
    hblocks = HEADS // heads
    blk = (rows, heads * HEAD_DIM)

    def cur(b, i, j):
        return (b * nb + i, group * hblocks + j)

    def prev(b, i, j):
        return (b * nb + jnp.maximum(i - 1, 0), group * hblocks + j)

    out_spec = pl.BlockSpec(blk, lambda b, i, j: (b * nb + i, j))
    return pl.pallas_call(
        functools.partial(_attn_kernel, steps=window // dilation, dilation=dilation, heads=heads),
        grid=(batch, nb, hblocks),
        in_specs=[pl.BlockSpec(blk, cur), pl.BlockSpec(blk, prev), pl.BlockSpec(blk, cur),
                  pl.BlockSpec(blk, prev), pl.BlockSpec(blk, cur)],
        out_specs=[out_spec, out_spec],
        out_shape=[jax.ShapeDtypeStruct((batch * seq, WIDTH), f32)] * 2,
        compiler_params=_params(("parallel", "parallel", "parallel")),
        name=f"dilated_attn_d{dilation}",
    )(q, k, k, v, v)


def _combine_out_ln_kernel(o0, o1, o2, l0, l1, l2, w_ref, x_ref, g_ref, b_ref, out_ref):
    ls = [l0[...], l1[...], l2[...]]
    mx = jnp.maximum(jnp.maximum(ls[0], ls[1]), ls[2])
    es = [jnp.exp(l - mx) for l in ls]
    o = (es[0] * o0[...] + es[1] * o1[...] + es[2] * o2[...]) / (es[0] + es[1] + es[2])
    h = _dot(o, w_ref[...])
    out_ref[...] = _layer_norm(ALPHA * x_ref[...] + h, g_ref[...], b_ref[...])


def _combine_out_ln(outs, lses, w, x, g, b, tm=256):
    m, n = x.shape
    blk = pl.BlockSpec((tm, n), lambda i: (i, 0))
    row = pl.BlockSpec((1, n), lambda i: (0, 0))
    return pl.pallas_call(
        _combine_out_ln_kernel,
        grid=(m // tm,),
        in_specs=[blk] * 6 + [pl.BlockSpec((WIDTH, n), lambda i: (0, 0)), blk, row, row],
        out_specs=blk,
        out_shape=jax.ShapeDtypeStruct((m, n), f32),
        compiler_params=_params(("parallel",)),
        name="attn_combine_out_ln",
    )(*outs, *lses, w, x, g.reshape(1, n), b.reshape(1, n))


def _rope_lane_tables(seq):
    half = ROT_DIM // 2
    inv_freq = ROPE_THETA ** (-jnp.arange(half, dtype=f32) * 2.0 / ROT_DIM)
    ang = jnp.arange(seq, dtype=f32)[:, None] * inv_freq[None, :]
    cos, sin = jnp.cos(ang), jnp.sin(ang)
    ones = jnp.ones((seq, HEAD_DIM - ROT_DIM), f32)
    zeros = jnp.zeros((seq, HEAD_DIM - half), f32)
    c = jnp.concatenate([cos, cos, ones], axis=1)
    s_lo = jnp.concatenate([-sin, zeros], axis=1)
    s_hi = jnp.concatenate([jnp.zeros((seq, half), f32), sin, jnp.zeros((seq, HEAD_DIM - ROT_DIM), f32)], axis=1)
    return c, s_lo, s_hi


def _top_rows(s, order, payload, k):
    big = jnp.int32(2 ** 30)
    vals, pays = [], []
    for _ in range(k):
        m = jnp.max(s, axis=0, keepdims=True)
        pos = jnp.min(jnp.where(s == m, order, big), axis=0, keepdims=True)
        sel = order == pos
        pays.append(pos if payload is None else jnp.max(jnp.where(sel, payload, -1), axis=0, keepdims=True))
        vals.append(m)
        s = jnp.where(sel, -jnp.inf, s)
    return jnp.concatenate(vals, axis=0), jnp.concatenate(pays, axis=0)


def _pair_candidates(t1, t2, combine, fill):
    kt = PEER_TOPK
    row = lax.broadcasted_iota(i32, (SUBLANES, LANES), 0)
    bc = lambda t, r: jnp.broadcast_to(t[r:r + 1], (SUBLANES, LANES))
    lo1, hi1, lo2, hi2 = t1[:SUBLANES], t1[SUBLANES:], t2[:SUBLANES], t2[SUBLANES:]
    tiles = [combine(bc(t1, 0), lo2), combine(bc(t1, 0), hi2), combine(bc(t1, 1), lo2),
             jnp.where(row < kt // 3, combine(bc(t1, 2), lo2), fill),
             jnp.where(row < kt // 4, combine(bc(t1, 3), lo2), fill),
             combine(hi1, bc(t2, 0)),
             jnp.where(row >= 4, combine(lo1, bc(t2, 0)), fill),
             jnp.where(row >= 4, combine(lo1, bc(t2, 1)), jnp.where(row == 0, combine(bc(t1, 4), bc(t2, 2)), fill))]
    return jnp.concatenate(tiles, axis=0)


def _pair_rank():
    kt = PEER_TOPK
    row = lax.broadcasted_iota(i32, (SUBLANES, LANES), 0)
    unused = kt * kt + row
    tiles = [row, row + SUBLANES, kt + row,
             jnp.where(row < kt // 3, 2 * kt + row, unused),
             jnp.where(row < kt // 4, 3 * kt + row, unused + SUBLANES),
             (row + SUBLANES) * kt,
             jnp.where(row >= 4, row * kt, unused + 2 * SUBLANES),
             jnp.where(row >= 4, row * kt + 1, jnp.where(row == 0, 4 * kt + 2, unused + 3 * SUBLANES))]
    return jnp.concatenate(tiles, axis=0)


def _peer_route_kernel(qp_ref, keys_ref, idx_ref, gate_ref):
    kt = PEER_TOPK
    half = PEER_QDIM // 2
    keyid = lax.broadcasted_iota(i32, (PEER_NKEYS, LANES), 0)
    rank = _pair_rank()
    idx_rows, gate_rows = [], []
    for h in range(PEER_HEADS):
        tops = []
        for c in range(2):
            col = (h * 2 + c) * half
            s = _dot_nt(keys_ref[h * 2 + c], qp_ref[:, col:col + half])
            tops.append(_top_rows(s, keyid, None, kt))
        (v1, i1), (v2, i2) = tops
        cand = _pair_candidates(v1, v2, lambda a, b: a + b, -jnp.inf)
        cand_id = _pair_candidates(i1, i2, lambda a, b: a * PEER_NKEYS + b, -1)
        best, ids = _top_rows(cand, rank, cand_id, kt)
        e = jnp.exp(best - best[0:1])
        gate_rows.append(e / jnp.sum(e, axis=0, keepdims=True))
        idx_rows.append(ids)
    idx_ref[...] = jnp.concatenate(idx_rows, axis=0).astype(f32).T.astype(i32)
    gate_ref[...] = jnp.concatenate(gate_rows, axis=0).T


def _peer_route(qp, sub_keys):
    m = qp.shape[0]
    tb = LANES
    keys = sub_keys.reshape(PEER_HEADS * 2, PEER_NKEYS, PEER_QDIM // 2).astype(bf16)
    out = pl.BlockSpec((tb, PEER_SEL), lambda i: (i, 0))
    return pl.pallas_call(
        _peer_route_kernel,
        grid=(m // tb,),
        in_specs=[pl.BlockSpec((tb, qp.shape[1]), lambda i: (i, 0)),
                  pl.BlockSpec(keys.shape, lambda i: (0, 0, 0))],
        out_specs=[out, out],
        out_shape=[jax.ShapeDtypeStruct((m, PEER_SEL), i32), jax.ShapeDtypeStruct((m, PEER_SEL), f32)],
        compiler_params=_params(("parallel",)),
        name="peer_route",
    )(qp, keys)


SC_CORES = 2
SC_SUBCORES = 16
SC_LANES = 16
SC_WORKERS = SC_CORES * SC_SUBCORES
SC_CHUNKS = D_MODEL // SC_LANES
SC_WORDS = D_MODEL // 2
SC_WORD_CHUNKS = SC_WORDS // SC_LANES
SC_TOK_GROUP = 32
SC_ROWS = 64
SC_GATHERS_PER_TOK = PEER_SEL // SC_ROWS
SC_GATHERS = SC_TOK_GROUP * SC_GATHERS_PER_TOK
SC_ROW_GROUP = 8


def _pack_table(tab):
    e = tab.shape[0]
    bits = lax.bitcast_convert_type(tab.astype(bf16), jnp.uint16).astype(jnp.uint32)
    bits = bits.reshape(e, SC_WORD_CHUNKS, 2, SC_LANES)
    word = bits[:, :, 0, :] | (bits[:, :, 1, :] << 16)
    return lax.bitcast_convert_type(word.reshape(e, SC_WORDS), i32)


def _unpack_words(w):
    return lax.bitcast_convert_type(w << 16, f32), lax.bitcast_convert_type(w & (-65536), f32)


def _sc_mesh():
    return plsc.VectorSubcoreMesh(core_axis_name="c", subcore_axis_name="s")


def _sc_worker():
    return lax.axis_index("s") * SC_CORES + lax.axis_index("c")


def _sc_gather_loop(gather, compute):
    gather(0, 0).start()

    @pl.loop(0, SC_GATHERS, step=2)
    def _(q):
        gather(q + 1, 1).start()
        gather(q, 0).wait()
        compute(q, 0)

        @pl.when(q + 2 < SC_GATHERS)
        def _():
            gather(q + 2, 0).start()
        gather(q + 1, 1).wait()
        compute(q + 1, 1)


def _sc_scratch(stage_shape):
    return [pltpu.VMEM((SC_GATHERS, SC_ROWS), i32), pltpu.VMEM(stage_shape[0], f32), pltpu.VMEM(stage_shape[1], f32),
            pltpu.VMEM((SC_ROWS, SC_WORDS), i32), pltpu.VMEM((SC_ROWS, SC_WORDS), i32),
            pltpu.SemaphoreType.DMA, pltpu.SemaphoreType.DMA]


def _peer_udot(x, idx, u_tab):
    t_all = x.shape[0]
    tpw = t_all // SC_WORKERS
    idx2 = idx.reshape(t_all * SC_GATHERS_PER_TOK, SC_ROWS)

    @functools.partial(
        pl.kernel, mesh=_sc_mesh(), compiler_params=pltpu.CompilerParams(needs_layout_passes=False),
        out_type=jax.ShapeDtypeStruct((t_all, PEER_SEL), f32),
        scratch_types=_sc_scratch(((SC_TOK_GROUP, D_MODEL), (SC_TOK_GROUP, PEER_SEL))), name="peer_udot_sc")
    def k(x_hbm, idx_hbm, u_hbm, pre_hbm, idx_v, x_v, pre_v, rows0, rows1, sem0, sem1):
        bufs = ((rows0, sem0), (rows1, sem1))
        base = _sc_worker() * tpw
        lane = lax.iota(i32, SC_LANES)
        zero = jnp.zeros((SC_LANES,), f32)

        def gather(q, b):
            return pltpu.make_async_copy(u_hbm.at[idx_v.at[q]], bufs[b][0], bufs[b][1])

        def compute(q, b):
            rows = bufs[b][0]
            t = q // SC_GATHERS_PER_TOK
            col0 = (q % SC_GATHERS_PER_TOK) * SC_ROWS
            for hh in range(SC_ROWS // SC_LANES):
                vec = zero
                for rg in range(SC_LANES // SC_ROW_GROUP):
                    r0 = hh * SC_LANES + rg * SC_ROW_GROUP

                    def body(c, acc):
                        x_lo = x_v[t, pl.ds(c * 2 * SC_LANES, SC_LANES)]
                        x_hi = x_v[t, pl.ds(c * 2 * SC_LANES + SC_LANES, SC_LANES)]
                        out = []
                        for r in range(SC_ROW_GROUP):
                            lo, hi = _unpack_words(rows[r0 + r, pl.ds(c * SC_LANES, SC_LANES)])
                            out.append(acc[r] + lo * x_lo + hi * x_hi)
                        return tuple(out)

                    acc = lax.fori_loop(0, SC_WORD_CHUNKS, body, (zero,) * SC_ROW_GROUP, unroll=2)
                    for r in range(SC_ROW_GROUP):
                        vec = jnp.where(lane == rg * SC_ROW_GROUP + r, jnp.sum(acc[r]), vec)
                pre_v[t, pl.ds(col0 + hh * SC_LANES, SC_LANES)] = vec

        @pl.loop(0, tpw // SC_TOK_GROUP)
        def _(g):
            tok0 = base + g * SC_TOK_GROUP
            pltpu.sync_copy(idx_hbm.at[pl.ds(tok0 * SC_GATHERS_PER_TOK, SC_GATHERS)], idx_v)
            pltpu.sync_copy(x_hbm.at[pl.ds(tok0, SC_TOK_GROUP)], x_v)
            _sc_gather_loop(gather, compute)
            pltpu.sync_copy(pre_v, pre_hbm.at[pl.ds(tok0, SC_TOK_GROUP)])

    return k(x, idx2, u_tab)


def _peer_vacc(w, idx, v_tab):
    t_all = w.shape[0]
    tpw = t_all // SC_WORKERS
    idx2 = idx.reshape(t_all * SC_GATHERS_PER_TOK, SC_ROWS)
    half_w = SC_WORD_CHUNKS // 2

    @functools.partial(
        pl.kernel, mesh=_sc_mesh(), compiler_params=pltpu.CompilerParams(needs_layout_passes=False),
        out_type=jax.ShapeDtypeStruct((t_all, D_MODEL), f32),
        scratch_types=_sc_scratch(((SC_TOK_GROUP, PEER_SEL), (SC_TOK_GROUP, D_MODEL))), name="peer_vacc_sc")
    def k(w_hbm, idx_hbm, v_hbm, f_hbm, idx_v, w_v, f_v, rows0, rows1, sem0, sem1):
        bufs = ((rows0, sem0), (rows1, sem1))
        base = _sc_worker() * tpw
        zero = jnp.zeros((SC_LANES,), f32)

        def gather(q, b):
            return pltpu.make_async_copy(v_hbm.at[idx_v.at[q]], bufs[b][0], bufs[b][1])

        def compute(q, b):
            rows = bufs[b][0]
            t = q // SC_GATHERS_PER_TOK
            col0 = (q % SC_GATHERS_PER_TOK) * SC_ROWS
            tvec = jnp.full((SC_LANES,), t, i32)
            for half in range(2):
                def body(r, acc):
                    wr = plsc.load_gather(w_v, [tvec, jnp.full((SC_LANES,), col0 + r, i32)])
                    out = []
                    for c in range(half_w):
                        lo, hi = _unpack_words(rows[r, pl.ds((half * half_w + c) * SC_LANES, SC_LANES)])
                        out.append(acc[2 * c] + wr * lo)
                        out.append(acc[2 * c + 1] + wr * hi)
                    return tuple(out)

                acc = lax.fori_loop(0, SC_ROWS, body, (zero,) * (2 * half_w))
                for c in range(2 * half_w):
                    sl = pl.ds((half * 2 * half_w + c) * SC_LANES, SC_LANES)
                    f_v[t, sl] = f_v[t, sl] + acc[c]

        @pl.loop(0, tpw // SC_TOK_GROUP)
        def _(g):
            tok0 = base + g * SC_TOK_GROUP
            pltpu.sync_copy(idx_hbm.at[pl.ds(tok0 * SC_GATHERS_PER_TOK, SC_GATHERS)], idx_v)
            pltpu.sync_copy(w_hbm.at[pl.ds(tok0, SC_TOK_GROUP)], w_v)

            @pl.loop(0, SC_TOK_GROUP)
            def _(t):
                @pl.loop(0, SC_CHUNKS)
                def _(c):
                    f_v[t, pl.ds(c * SC_LANES, SC_LANES)] = zero

            _sc_gather_loop(gather, compute)
            pltpu.sync_copy(f_v, f_hbm.at[pl.ds(tok0, SC_TOK_GROUP)])

    return k(w, idx2, v_tab)


def _peer_act_kernel(pre_ref, gate_ref, w_ref):
    pre = pre_ref[...]
    w_ref[...] = gate_ref[...] * (0.5 * pre * (1.0 + lax.erf(pre * (2.0 ** -0.5))))


def _peer_act(pre, gate, tm=2048):
    m, n = pre.shape
    blk = pl.BlockSpec((tm, n), lambda i: (i, 0))
    return pl.pallas_call(
        _peer_act_kernel, grid=(m // tm,), in_specs=[blk, blk], out_specs=blk,
        out_shape=jax.ShapeDtypeStruct((m, n), f32), compiler_params=_params(("parallel",)), name="peer_act",
    )(pre, gate)


def _res_ln_kernel(x_ref, f_ref, g_ref, b_ref, o_ref):
    o_ref[...] = _layer_norm(ALPHA * x_ref[...] + f_ref[...], g_ref[...], b_ref[...])


def _res_ln(x, f, g, b, tm=512):
    m, n = x.shape
    blk = pl.BlockSpec((tm, n), lambda i: (i, 0))
    row = pl.BlockSpec((1, n), lambda i: (0, 0))
    return pl.pallas_call(
        _res_ln_kernel, grid=(m // tm,), in_specs=[blk, blk, row, row], out_specs=blk,
        out_shape=jax.ShapeDtypeStruct((m, n), f32), compiler_params=_params(("parallel",)), name="res_ln",
    )(x, f, g.reshape(1, n), b.reshape(1, n))


def _peer_expert_ln(x, idx, gate, u_tab, v_tab, g, b):
    pre = _peer_udot(x, idx, u_tab)
    w = _peer_act(pre, gate)
    f = _peer_vacc(w, idx, v_tab)
    return _res_ln(x, f, g, b)


PEER_SPLITS = 2


def _peer_ffn_ln(x2, w_query, sub_keys, u_pack, v_pack, g, b):
    outs = []
    for xs in jnp.split(x2, PEER_SPLITS, axis=0):
        qp = _matmul(xs, w_query)
        idx, gate = _peer_route(qp, sub_keys)
        outs.append(_peer_expert_ln(xs, idx, gate, u_pack, v_pack, g, b))
    return jnp.concatenate(outs, axis=0)


def kernel(x, dn_w_in, dn_conv, dn_a_log, dn_dt_bias, dn_norm_w, dn_w_out, shared_w_kv, attn_w_q, attn_w_out,
           peer_w_query, peer_sub_keys, peer_u, peer_v, ln_mix_g, ln_mix_b, ln_ffn_g, ln_ffn_b):
    batch, seq, d = x.shape
    n_a = DEPTH // 2
    tabs = _rope_lane_tables(seq)
    gw = N_GROUPS * WIDTH
    dn_w_pad = [jnp.pad(w, ((0, 0), (0, DN_PROJ_PAD - w.shape[1]))).astype(bf16) for w in dn_w_in]
    dn_w_out_b = dn_w_out.astype(bf16)
    w_k, w_v = shared_w_kv[:, :gw].astype(bf16), shared_w_kv[:, gw:].astype(bf16)
    w_q, w_o = attn_w_q.astype(bf16), attn_w_out.astype(bf16)
    peer_wq = peer_w_query.astype(bf16)
    u_pack = [_pack_table(t) for t in peer_u]
    v_pack = [_pack_table(t) for t in peer_v]

    def trunk(x2):
        k_sh = v_sh = None
        for layer in range(DEPTH):
            if layer < n_a:
                o = _deltanet_mixer(x2, dn_w_pad[layer], dn_conv[layer], dn_a_log[layer], dn_dt_bias[layer],
                                    dn_norm_w[layer], 1, seq)
                x2 = _matmul_res_ln(o, dn_w_out_b[layer], x2, ln_mix_g[layer], ln_mix_b[layer])
            else:
                j = layer - n_a
                if j == 0:
                    k_sh = _matmul_rope(x2, w_k, tabs, seq)
                    v_sh = _matmul(x2, w_v, tn=1024)
                q = _matmul_rope(x2, w_q[j], tabs, seq)
                outs, lses = [], []
                for gi, (window, dilation) in enumerate(ATTN_GROUPS):
                    o, l = _dilated_attention(q, k_sh, v_sh, gi, window, dilation, 1, seq)
                    outs.append(o)
                    lses.append(l)
                x2 = _combine_out_ln(outs, lses, w_o[j], x2, ln_mix_g[layer], ln_mix_b[layer])
            x2 = _peer_ffn_ln(x2, peer_wq[layer], peer_sub_keys[layer], u_pack[layer], v_pack[layer],
                              ln_ffn_g[layer], ln_ffn_b[layer])
        return x2

    return jnp.stack([trunk(x[b]) for b in range(batch)], axis=0)
```

```python
import functools
import math

import jax
import jax.numpy as jnp
from jax import lax
from jax.experimental import pallas as pl
from jax.experimental.pallas import tpu as pltpu
from jax.experimental.pallas import tpu_sc as plsc

f32 = jnp.float32
bf16 = jnp.bfloat16
i32 = jnp.int32

D_MODEL = 1024
DEPTH = 2
ALPHA = (2.0 * DEPTH) ** 0.25
LN_EPS = 1e-5

HEADS = 8
HEAD_DIM = 128
WIDTH = HEADS * HEAD_DIM
CONV_K = 4
DN_CHUNK = 64

ATTN_GROUPS = ((128, 1), (512, 4), (2048, 16))
N_GROUPS = len(ATTN_GROUPS)
ATTN_BLOCK = 128
ROT_DIM = HEAD_DIM // 4
ROPE_THETA = 500000.0

PEER_HEADS = 8
PEER_NKEYS = 128
PEER_TOPK = 16
PEER_QDIM = 256
PEER_SEL = PEER_HEADS * PEER_TOPK

LANES = 128
SUBLANES = 8
VMEM_LIMIT = 48 * 1024 * 1024


def _params(sem):
    return pltpu.CompilerParams(dimension_semantics=sem, vmem_limit_bytes=VMEM_LIMIT)


def _dot(a, b):
    return jnp.dot(a.astype(bf16), b.astype(bf16), preferred_element_type=f32)


def _dot_nt(a, b):
    return lax.dot_general(a.astype(bf16), b.astype(bf16), (((1,), (1,)), ((), ())), preferred_element_type=f32)


def _dot_tn(a, b):
    return lax.dot_general(a.astype(bf16), b.astype(bf16), (((0,), (0,)), ((), ())), preferred_element_type=f32)


def _sigmoid(x):
    return 1.0 / (1.0 + jnp.exp(-x))


def _layer_norm(y, g, b):
    mu = jnp.mean(y, -1, keepdims=True)
    yc = y - mu
    var = jnp.mean(yc * yc, -1, keepdims=True)
    return yc * lax.rsqrt(var + LN_EPS) * g + b


def _mm_kernel(a_ref, w_ref, o_ref):
    o_ref[...] = _dot(a_ref[...], w_ref[...])


def _matmul(a, w, tm=512, tn=None):
    m, k = a.shape
    n = w.shape[1]
    tn = tn or n
    return pl.pallas_call(
        _mm_kernel,
        grid=(m // tm, n // tn),
        in_specs=[pl.BlockSpec((tm, k), lambda i, j: (i, 0)), pl.BlockSpec((k, tn), lambda i, j: (0, j))],
        out_specs=pl.BlockSpec((tm, tn), lambda i, j: (i, j)),
        out_shape=jax.ShapeDtypeStruct((m, n), f32),
        compiler_params=_params(("parallel", "parallel")),
        name="matmul",
    )(a, w)


def _rope(y, c, s_lo, s_hi):
    return y * c + pltpu.roll(y, ROT_DIM // 2, axis=1) * s_hi + pltpu.roll(y, LANES - ROT_DIM // 2, axis=1) * s_lo


def _mm_rope_kernel(a_ref, w_ref, c_ref, slo_ref, shi_ref, o_ref, *, heads):
    y = _dot(a_ref[...], w_ref[...])
    c, s_lo, s_hi = c_ref[...], slo_ref[...], shi_ref[...]
    for h in range(heads):
        sl = slice(h * HEAD_DIM, (h + 1) * HEAD_DIM)
        o_ref[:, sl] = _rope(y[:, sl], c, s_lo, s_hi)


def _matmul_rope(a, w, tabs, seq, tm=512, tn=1024):
    m, k = a.shape
    n = w.shape[1]
    nsb = seq // tm
    tab_spec = pl.BlockSpec((tm, LANES), lambda i, j: (i % nsb, 0))
    return pl.pallas_call(
        functools.partial(_mm_rope_kernel, heads=tn // HEAD_DIM),
        grid=(m // tm, n // tn),
        in_specs=[pl.BlockSpec((tm, k), lambda i, j: (i, 0)), pl.BlockSpec((k, tn), lambda i, j: (0, j)),
                  tab_spec, tab_spec, tab_spec],
        out_specs=pl.BlockSpec((tm, tn), lambda i, j: (i, j)),
        out_shape=jax.ShapeDtypeStruct((m, n), f32),
        compiler_params=_params(("parallel", "parallel")),
        name="matmul_rope",
    )(a, w, *tabs)


def _mm_res_ln_kernel(a_ref, w_ref, x_ref, g_ref, b_ref, o_ref):
    h = _dot(a_ref[...], w_ref[...])
    o_ref[...] = _layer_norm(ALPHA * x_ref[...] + h, g_ref[...], b_ref[...])


def _matmul_res_ln(a, w, x, g, b, tm=512):
    m, k = a.shape
    n = w.shape[1]
    row = pl.BlockSpec((1, n), lambda i: (0, 0))
    return pl.pallas_call(
        _mm_res_ln_kernel,
        grid=(m // tm,),
        in_specs=[pl.BlockSpec((tm, k), lambda i: (i, 0)), pl.BlockSpec((k, n), lambda i: (0, 0)),
                  pl.BlockSpec((tm, n), lambda i: (i, 0)), row, row],
        out_specs=pl.BlockSpec((tm, n), lambda i: (i, 0)),
        out_shape=jax.ShapeDtypeStruct((m, n), f32),
        compiler_params=_params(("parallel",)),
        name="matmul_res_ln",
    )(a, w, x, g.reshape(1, n), b.reshape(1, n))


DN_PROJ_PAD = 4 * WIDTH + LANES


def _dn_prep_kernel(proj_ref, halo_ref, conv_ref, alog_ref, dtb_ref, q_ref, k_ref, v_ref, gate_ref, *, tm, nsb):
    i = pl.program_id(0)
    x = proj_ref[:, : 3 * WIDTH]
    halo = jnp.where(i % nsb == 0, 0.0, halo_ref[...])
    w = conv_ref[...]
    row8 = lax.broadcasted_iota(i32, (SUBLANES, 1), 0)
    acc = x * w[CONV_K - 1:CONV_K]
    for s in range(1, CONV_K):
        xs = pltpu.roll(x, s, axis=0)
        hs = pltpu.roll(halo, s, axis=0)
        first = jnp.where(row8 < s, hs, xs[:SUBLANES])
        xs = jnp.concatenate([first, xs[SUBLANES:]], axis=0)
        acc = acc + xs * w[CONV_K - 1 - s:CONV_K - s]
    qkv = acc * _sigmoid(acc)
    for h in range(HEADS):
        sl = slice(h * HEAD_DIM, (h + 1) * HEAD_DIM)
        qh = qkv[:, sl]
        q_ref[:, sl] = qh * lax.rsqrt(jnp.sum(qh * qh, -1, keepdims=True) + 1e-6) * (HEAD_DIM ** -0.5)
        kh = qkv[:, WIDTH + h * HEAD_DIM: WIDTH + (h + 1) * HEAD_DIM]
        k_ref[:, sl] = kh * lax.rsqrt(jnp.sum(kh * kh, -1, keepdims=True) + 1e-6)
    v_ref[...] = qkv[:, 2 * WIDTH:]
    ab = proj_ref[:, 4 * WIDTH:]
    z = ab + dtb_ref[...]
    softplus = jnp.maximum(z, 0.0) + jnp.log(1.0 + jnp.exp(-jnp.abs(z)))
    g = -jnp.exp(alog_ref[...]) * softplus
    row = lax.broadcasted_iota(i32, (tm, 1), 0) % DN_CHUNK
    s = 1
    while s < DN_CHUNK:
        g = g + jnp.where(row >= s, pltpu.roll(g, s, axis=0), 0.0)
        s *= 2
    lane = lax.broadcasted_iota(i32, (1, LANES), 1)
    gate_ref[...] = jnp.where(lane < HEADS, g, _sigmoid(ab))


def _dn_prep(proj, conv_w, a_log, dt_bias, seq, tm=256):
    m = proj.shape[0]
    nsb = seq // tm
    pad = LANES - HEADS
    alog = jnp.pad(a_log.astype(f32), (0, pad)).reshape(1, LANES)
    dtb = jnp.pad(dt_bias.astype(f32), (0, pad)).reshape(1, LANES)
    hb = tm // SUBLANES
    out_w = pl.BlockSpec((tm, WIDTH), lambda i: (i, 0))
    return pl.pallas_call(
        functools.partial(_dn_prep_kernel, tm=tm, nsb=nsb),
        grid=(m // tm,),
        in_specs=[pl.BlockSpec((tm, DN_PROJ_PAD), lambda i: (i, 0)),
                  pl.BlockSpec((SUBLANES, 3 * WIDTH), lambda i: (jnp.maximum(i * hb - 1, 0), 0)),
                  pl.BlockSpec((CONV_K, 3 * WIDTH), lambda i: (0, 0)),
                  pl.BlockSpec((1, LANES), lambda i: (0, 0)), pl.BlockSpec((1, LANES), lambda i: (0, 0))],
        out_specs=[out_w, out_w, out_w, pl.BlockSpec((tm, LANES), lambda i: (i, 0))],
        out_shape=[jax.ShapeDtypeStruct((m, WIDTH), f32)] * 3 + [jax.ShapeDtypeStruct((m, LANES), f32)],
        compiler_params=_params(("parallel",)),
        name="dn_prep",
    )(proj, proj, conv_w, alog, dtb)


def _dn_delta_kernel(q_ref, k_ref, v_ref, z_ref, gate_ref, gt_ref, nw_ref, o_ref, state_ref, *, chunks):
    @pl.when(pl.program_id(1) == 0)
    def _():
        state_ref[...] = jnp.zeros_like(state_ref)

    c = DN_CHUNK
    ri = lax.broadcasted_iota(i32, (c, c), 0)
    ci = lax.broadcasted_iota(i32, (c, c), 1)
    causal = ri >= ci
    strict = ri > ci
    eye = (ri == ci).astype(f32)
    nw = nw_ref[...]
    inst = [(ch, h) for ch in range(chunks) for h in range(HEADS)]
    pre = {}
    for ch, h in inst:
        rows = slice(ch * c, (ch + 1) * c)
        sl = slice(h * HEAD_DIM, (h + 1) * HEAD_DIM)
        q, k, v = q_ref[rows, sl], k_ref[rows, sl], v_ref[rows, sl]
        gcol = gate_ref[rows, h:h + 1]
        beta = gate_ref[rows, HEADS + h:HEADS + h + 1]
        grow = gt_ref[h:h + 1, rows]
        decay = jnp.where(causal, jnp.exp(jnp.where(causal, gcol - grow, 0.0)), 0.0)
        kb = k * beta
        egc = jnp.exp(gcol)
        glast = gcol[c - 1:c, :]
        qk = _dot_nt(jnp.concatenate([q, kb], axis=0), k)
        m = jnp.where(strict, qk[c:] * decay, 0.0)
        pre[ch, h] = dict(a_qk=qk[:c] * decay, inv=eye - m, p=m,
                          rhs=jnp.concatenate([v * beta, kb * egc], axis=1), qe=q * egc,
                          kd=k * jnp.exp(glast - gcol), eg=jnp.exp(glast))
    step = 2
    while step < c:
        for key in inst:
            pre[key]["p"] = _dot(pre[key]["p"], pre[key]["p"])
        for key in inst:
            pre[key]["inv"] = pre[key]["inv"] + _dot(pre[key]["inv"], pre[key]["p"])
        step *= 2
    for key in inst:
        pre[key]["uw"] = _dot(pre[key]["inv"], pre[key]["rhs"])
    states = [state_ref[h] for h in range(HEADS)]
    for ch in range(chunks):
        rows = slice(ch * c, (ch + 1) * c)
        for h in range(HEADS):
            sl = slice(h * HEAD_DIM, (h + 1) * HEAD_DIM)
            d = pre[ch, h]
            u, w = d["uw"][:, :HEAD_DIM], d["uw"][:, HEAD_DIM:]
            ws = _dot(jnp.concatenate([w, d["qe"]], axis=0), states[h])
            v_new = u - ws[:c]
            o = ws[c:] + _dot(d["a_qk"], v_new)
            states[h] = states[h] * d["eg"] + _dot_tn(d["kd"], v_new)
            o = o * lax.rsqrt(jnp.mean(o * o, -1, keepdims=True) + 1e-6) * nw
            zz = z_ref[rows, sl]
            o_ref[rows, sl] = o * (zz * _sigmoid(zz))
    for h in range(HEADS):
        state_ref[h] = states[h]


def _dn_delta(q, k, v, proj, gates, gates_t, norm_w, batch, seq, cb=128):
    nb = seq // cb
    wide = pl.BlockSpec((cb, WIDTH), lambda b, i: (b * nb + i, 0))
    return pl.pallas_call(
        functools.partial(_dn_delta_kernel, chunks=cb // DN_CHUNK),
        grid=(batch, nb),
        in_specs=[wide, wide, wide,
                  pl.BlockSpec((cb, WIDTH), lambda b, i: (b * nb + i, 3)),
                  pl.BlockSpec((cb, LANES), lambda b, i: (b * nb + i, 0)),
                  pl.BlockSpec((SUBLANES, cb), lambda b, i: (0, b * nb + i)),
                  pl.BlockSpec((1, HEAD_DIM), lambda b, i: (0, 0))],
        out_specs=wide,
        out_shape=jax.ShapeDtypeStruct((batch * seq, WIDTH), f32),
        scratch_shapes=[pltpu.VMEM((HEADS, HEAD_DIM, HEAD_DIM), f32)],
        compiler_params=_params(("parallel", "arbitrary")),
        name="dn_delta",
    )(q, k, v, proj, gates, gates_t, norm_w.reshape(1, HEAD_DIM).astype(f32))


def _deltanet_mixer(x2, w_pad, conv_w, a_log, dt_bias, norm_w, batch, seq):
    proj = _matmul(x2, w_pad, tn=DN_PROJ_PAD // 3)
    q, k, v, gates = _dn_prep(proj, conv_w, a_log, dt_bias, seq)
    gates_t = gates[:, :SUBLANES].T
    return _dn_delta(q, k, v, proj, gates, gates_t, norm_w, batch, seq)


def _attn_kernel(q_ref, kp_ref, kc_ref, vp_ref, vc_ref, o_ref, l_ref, *, steps, dilation, heads):
    nbi = pl.program_id(1)
    blk = ATTN_BLOCK
    qi = lax.broadcasted_iota(i32, (blk, 2 * blk), 0)
    kj = lax.broadcasted_iota(i32, (blk, 2 * blk), 1)
    dist = qi + blk - kj
    mask = (dist >= 0) & (dist <= steps) & ((kj >= blk) | (nbi > 0))
    scale = HEAD_DIM ** -0.5
    for r in range(dilation):
        rows = pl.ds(r, blk, stride=dilation) if dilation > 1 else slice(None)
        for h in range(heads):
            sl = slice(h * HEAD_DIM, (h + 1) * HEAD_DIM)
            q = q_ref[rows, sl]
            kk = jnp.concatenate([kp_ref[rows, sl], kc_ref[rows, sl]], axis=0)
            vv = jnp.concatenate([vp_ref[rows, sl], vc_ref[rows, sl]], axis=0)
            sc = jnp.where(mask, _dot_nt(q, kk) * scale, -jnp.inf)
            mx = jnp.max(sc, -1, keepdims=True)
            p = jnp.exp(sc - mx)
            den = jnp.sum(p, -1, keepdims=True)
            o_ref[rows, sl] = _dot(p, vv) / den
            l_ref[rows, sl] = jnp.broadcast_to(mx + jnp.log(den), (blk, HEAD_DIM))


def _dilated_attention(q, k, v, group, window, dilation, batch, seq):
    rows = ATTN_BLOCK * dilation
    nb = seq // rows
    heads = HEADS if dilation == 1 else 1
    hblocks = HEADS // heads
    blk = (rows, heads * HEAD_DIM)

    def cur(b, i, j):
        return (b * nb + i, group * hblocks + j)

    def prev(b, i, j):
        return (b * nb + jnp.maximum(i - 1, 0), group * hblocks + j)

    out_spec = pl.BlockSpec(blk, lambda b, i, j: (b * nb + i, j))
    return pl.pallas_call(
        functools.partial(_attn_kernel, steps=window // dilation, dilation=dilation, heads=heads),
        grid=(batch, nb, hblocks),
        in_specs=[pl.BlockSpec(blk, cur), pl.BlockSpec(blk, prev), pl.BlockSpec(blk, cur),
                  pl.BlockSpec(blk, prev), pl.BlockSpec(blk, cur)],
        out_specs=[out_spec, out_spec],
        out_shape=[jax.ShapeDtypeStruct((batch * seq, WIDTH), f32)] * 2,
        compiler_params=_params(("parallel", "parallel", "parallel")),
        name=f"dilated_attn_d{dilation}",
    )(q, k, k, v, v)


def _combine_out_ln_kernel(o0, o1, o2, l0, l1, l2, w_ref, x_ref, g_ref, b_ref, out_ref):
    ls = [l0[...], l1[...], l2[...]]
    mx = jnp.maximum(jnp.maximum(ls[0], ls[1]), ls[2])
    es = [jnp.exp(l - mx) for l in ls]
    o = (es[0] * o0[...] + es[1] * o1[...] + es[2] * o2[...]) / (es[0] + es[1] + es[2])
    h = _dot(o, w_ref[...])
    out_ref[...] = _layer_norm(ALPHA * x_ref[...] + h, g_ref[...], b_ref[...])


def _combine_out_ln(outs, lses, w, x, g, b, tm=256):
    m, n = x.shape
    blk = pl.BlockSpec((tm, n), lambda i: (i, 0))
    row = pl.BlockSpec((1, n), lambda i: (0, 0))
    return pl.pallas_call(
        _combine_out_ln_kernel,
        grid=(m // tm,),
        in_specs=[blk] * 6 + [pl.BlockSpec((WIDTH, n), lambda i: (0, 0)), blk, row, row],
        out_specs=blk,
        out_shape=jax.ShapeDtypeStruct((m, n), f32),
        compiler_params=_params(("parallel",)),
        name="attn_combine_out_ln",
    )(*outs, *lses, w, x, g.reshape(1, n), b.reshape(1, n))


def _rope_lane_tables(seq):
    half = ROT_DIM // 2
    inv_freq = ROPE_THETA ** (-jnp.arange(half, dtype=f32) * 2.0 / ROT_DIM)
    ang = jnp.arange(seq, dtype=f32)[:, None] * inv_freq[None, :]
    cos, sin = jnp.cos(ang), jnp.sin(ang)
    ones = jnp.ones((seq, HEAD_DIM - ROT_DIM), f32)
    zeros = jnp.zeros((seq, HEAD_DIM - half), f32)
    c = jnp.concatenate([cos, cos, ones], axis=1)
    s_lo = jnp.concatenate([-sin, zeros], axis=1)
    s_hi = jnp.concatenate([jnp.zeros((seq, half), f32), sin, jnp.zeros((seq, HEAD_DIM - ROT_DIM), f32)], axis=1)
    return c, s_lo, s_hi


def _top_rows(s, order, payload, k):
    big = jnp.int32(2 ** 30)
    vals, pays = [], []
    for _ in range(k):
        m = jnp.max(s, axis=0, keepdims=True)
        pos = jnp.min(jnp.where(s == m, order, big), axis=0, keepdims=True)
        sel = order == pos
        pays.append(pos if payload is None else jnp.max(jnp.where(sel, payload, -1), axis=0, keepdims=True))
        vals.append(m)
        s = jnp.where(sel, -jnp.inf, s)
    return jnp.concatenate(vals, axis=0), jnp.concatenate(pays, axis=0)


def _pair_candidates(t1, t2, combine, fill):
    kt = PEER_TOPK
    row = lax.broadcasted_iota(i32, (SUBLANES, LANES), 0)
    bc = lambda t, r: jnp.broadcast_to(t[r:r + 1], (SUBLANES, LANES))
    lo1, hi1, lo2, hi2 = t1[:SUBLANES], t1[SUBLANES:], t2[:SUBLANES], t2[SUBLANES:]
    tiles = [combine(bc(t1, 0), lo2), combine(bc(t1, 0), hi2), combine(bc(t1, 1), lo2),
             jnp.where(row < kt // 3, combine(bc(t1, 2), lo2), fill),
             jnp.where(row < kt // 4, combine(bc(t1, 3), lo2), fill),
             combine(hi1, bc(t2, 0)),
             jnp.where(row >= 4, combine(lo1, bc(t2, 0)), fill),
             jnp.where(row >= 4, combine(lo1, bc(t2, 1)), jnp.where(row == 0, combine(bc(t1, 4), bc(t2, 2)), fill))]
    return jnp.concatenate(tiles, axis=0)


def _pair_rank():
    kt = PEER_TOPK
    row = lax.broadcasted_iota(i32, (SUBLANES, LANES), 0)
    unused = kt * kt + row
    tiles = [row, row + SUBLANES, kt + row,
             jnp.where(row < kt // 3, 2 * kt + row, unused),
             jnp.where(row < kt // 4, 3 * kt + row, unused + SUBLANES),
             (row + SUBLANES) * kt,
             jnp.where(row >= 4, row * kt, unused + 2 * SUBLANES),
             jnp.where(row >= 4, row * kt + 1, jnp.where(row == 0, 4 * kt + 2, unused + 3 * SUBLANES))]
    return jnp.concatenate(tiles, axis=0)


def _peer_route_kernel(qp_ref, keys_ref, idx_ref, gate_ref):
    kt = PEER_TOPK
    half = PEER_QDIM // 2
    keyid = lax.broadcasted_iota(i32, (PEER_NKEYS, LANES), 0)
    rank = _pair_rank()
    idx_rows, gate_rows = [], []
    for h in range(PEER_HEADS):
        tops = []
        for c in range(2):
            col = (h * 2 + c) * half
            s = _dot_nt(keys_ref[h * 2 + c], qp_ref[:, col:col + half])
            tops.append(_top_rows(s, keyid, None, kt))
        (v1, i1), (v2, i2) = tops
        cand = _pair_candidates(v1, v2, lambda a, b: a + b, -jnp.inf)
        cand_id = _pair_candidates(i1, i2, lambda a, b: a * PEER_NKEYS + b, -1)
        best, ids = _top_rows(cand, rank, cand_id, kt)
        e = jnp.exp(best - best[0:1])
        gate_rows.append(e / jnp.sum(e, axis=0, keepdims=True))
        idx_rows.append(ids)
    idx_ref[...] = jnp.concatenate(idx_rows, axis=0).astype(f32).T.astype(i32)
    gate_ref[...] = jnp.concatenate(gate_rows, axis=0).T


def _peer_route(qp, sub_keys):
    m = qp.shape[0]
    tb = LANES
    keys = sub_keys.reshape(PEER_HEADS * 2, PEER_NKEYS, PEER_QDIM // 2).astype(bf16)
    out = pl.BlockSpec((tb, PEER_SEL), lambda i: (i, 0))
    return pl.pallas_call(
        _peer_route_kernel,
        grid=(m // tb,),
        in_specs=[pl.BlockSpec((tb, qp.shape[1]), lambda i: (i, 0)),
                  pl.BlockSpec(keys.shape, lambda i: (0, 0, 0))],
        out_specs=[out, out],
        out_shape=[jax.ShapeDtypeStruct((m, PEER_SEL), i32), jax.ShapeDtypeStruct((m, PEER_SEL), f32)],
        compiler_params=_params(("parallel",)),
        name="peer_route",
    )(qp, keys)


SC_CORES = 2
SC_SUBCORES = 16
SC_LANES = 16
SC_WORKERS = SC_CORES * SC_SUBCORES
SC_CHUNKS = D_MODEL // SC_LANES
SC_WORDS = D_MODEL // 2
SC_WORD_CHUNKS = SC_WORDS // SC_LANES
SC_TOK_GROUP = 32
SC_ROWS = 64
SC_GATHERS_PER_TOK = PEER_SEL // SC_ROWS
SC_GATHERS = SC_TOK_GROUP * SC_GATHERS_PER_TOK
SC_ROW_GROUP = 8


def _pack_table(tab):
    e = tab.shape[0]
    bits = lax.bitcast_convert_type(tab.astype(bf16), jnp.uint16).astype(jnp.uint32)
    bits = bits.reshape(e, SC_WORD_CHUNKS, 2, SC_LANES)
    word = bits[:, :, 0, :] | (bits[:, :, 1, :] << 16)
    return lax.bitcast_convert_type(word.reshape(e, SC_WORDS), i32)


def _unpack_words(w):
    return lax.bitcast_convert_type(w << 16, f32), lax.bitcast_convert_type(w & (-65536), f32)


def _sc_mesh():
    return plsc.VectorSubcoreMesh(core_axis_name="c", subcore_axis_name="s")


def _sc_worker():
    return lax.axis_index("s") * SC_CORES + lax.axis_index("c")


def _sc_gather_loop(gather, compute):
    gather(0, 0).start()

    @pl.loop(0, SC_GATHERS, step=2)
    def _(q):
        gather(q + 1, 1).start()
        gather(q, 0).wait()
        compute(q, 0)

        @pl.when(q + 2 < SC_GATHERS)
        def _():
            gather(q + 2, 0).start()
        gather(q + 1, 1).wait()
        compute(q + 1, 1)


def _sc_scratch(stage_shape):
    return [pltpu.VMEM((SC_GATHERS, SC_ROWS), i32), pltpu.VMEM(stage_shape[0], f32), pltpu.VMEM(stage_shape[1], f32),
            pltpu.VMEM((SC_ROWS, SC_WORDS), i32), pltpu.VMEM((SC_ROWS, SC_WORDS), i32),
            pltpu.SemaphoreType.DMA, pltpu.SemaphoreType.DMA]


def _peer_udot(x, idx, u_tab):
    t_all = x.shape[0]
    tpw = t_all // SC_WORKERS
    idx2 = idx.reshape(t_all * SC_GATHERS_PER_TOK, SC_ROWS)

    @functools.partial(
        pl.kernel, mesh=_sc_mesh(), compiler_params=pltpu.CompilerParams(needs_layout_passes=False),
        out_type=jax.ShapeDtypeStruct((t_all, PEER_SEL), f32),
        scratch_types=_sc_scratch(((SC_TOK_GROUP, D_MODEL), (SC_TOK_GROUP, PEER_SEL))), name="peer_udot_sc")
    def k(x_hbm, idx_hbm, u_hbm, pre_hbm, idx_v, x_v, pre_v, rows0, rows1, sem0, sem1):
        bufs = ((rows0, sem0), (rows1, sem1))
        base = _sc_worker() * tpw
        lane = lax.iota(i32, SC_LANES)
        zero = jnp.zeros((SC_LANES,), f32)

        def gather(q, b):
            return pltpu.make_async_copy(u_hbm.at[idx_v.at[q]], bufs[b][0], bufs[b][1])

        def compute(q, b):
            rows = bufs[b][0]
            t = q // SC_GATHERS_PER_TOK
            col0 = (q % SC_GATHERS_PER_TOK) * SC_ROWS
            for hh in range(SC_ROWS // SC_LANES):
                vec = zero
                for rg in range(SC_LANES // SC_ROW_GROUP):
                    r0 = hh * SC_LANES + rg * SC_ROW_GROUP

                    def body(c, acc):
                        x_lo = x_v[t, pl.ds(c * 2 * SC_LANES, SC_LANES)]
                        x_hi = x_v[t, pl.ds(c * 2 * SC_LANES + SC_LANES, SC_LANES)]
                        out = []
                        for r in range(SC_ROW_GROUP):
                            lo, hi = _unpack_words(rows[r0 + r, pl.ds(c * SC_LANES, SC_LANES)])
                            out.append(acc[r] + lo * x_lo + hi * x_hi)
                        return tuple(out)

                    acc = lax.fori_loop(0, SC_WORD_CHUNKS, body, (zero,) * SC_ROW_GROUP, unroll=2)
                    for r in range(SC_ROW_GROUP):
                        vec = jnp.where(lane == rg * SC_ROW_GROUP + r, jnp.sum(acc[r]), vec)
                pre_v[t, pl.ds(col0 + hh * SC_LANES, SC_LANES)] = vec

        @pl.loop(0, tpw // SC_TOK_GROUP)
        def _(g):
            tok0 = base + g * SC_TOK_GROUP
            pltpu.sync_copy(idx_hbm.at[pl.ds(tok0 * SC_GATHERS_PER_TOK, SC_GATHERS)], idx_v)
            pltpu.sync_copy(x_hbm.at[pl.ds(tok0, SC_TOK_GROUP)], x_v)
            _sc_gather_loop(gather, compute)
            pltpu.sync_copy(pre_v, pre_hbm.at[pl.ds(tok0, SC_TOK_GROUP)])

    return k(x, idx2, u_tab)


def _peer_vacc(w, idx, v_tab):
    t_all = w.shape[0]
    tpw = t_all // SC_WORKERS
    idx2 = idx.reshape(t_all * SC_GATHERS_PER_TOK, SC_ROWS)
    half_w = SC_WORD_CHUNKS // 2

    @functools.partial(
        pl.kernel, mesh=_sc_mesh(), compiler_params=pltpu.CompilerParams(needs_layout_passes=False),
        out_type=jax.ShapeDtypeStruct((t_all, D_MODEL), f32),
        scratch_types=_sc_scratch(((SC_TOK_GROUP, PEER_SEL), (SC_TOK_GROUP, D_MODEL))), name="peer_vacc_sc")
    def k(w_hbm, idx_hbm, v_hbm, f_hbm, idx_v, w_v, f_v, rows0, rows1, sem0, sem1):
        bufs = ((rows0, sem0), (rows1, sem1))
        base = _sc_worker() * tpw
        zero = jnp.zeros((SC_LANES,), f32)

        def gather(q, b):
            return pltpu.make_async_copy(v_hbm.at[idx_v.at[q]], bufs[b][0], bufs[b][1])

        def compute(q, b):
            rows = bufs[b][0]
            t = q // SC_GATHERS_PER_TOK
            col0 = (q % SC_GATHERS_PER_TOK) * SC_ROWS
            tvec = jnp.full((SC_LANES,), t, i32)
            for half in range(2):
                def body(r, acc):
                    wr = plsc.load_gather(w_v, [tvec, jnp.full((SC_LANES,), col0 + r, i32)])
                    out = []
                    for c in range(half_w):
                        lo, hi = _unpack_words(rows[r, pl.ds((half * half_w + c) * SC_LANES, SC_LANES)])
                        out.append(acc[2 * c] + wr * lo)
                        out.append(acc[2 * c + 1] + wr * hi)
                    return tuple(out)

                acc = lax.fori_loop(0, SC_ROWS, body, (zero,) * (2 * half_w))
                for c in range(2 * half_w):
                    sl = pl.ds((half * 2 * half_w + c) * SC_LANES, SC_LANES)
                    f_v[t, sl] = f_v[t, sl] + acc[c]

        @pl.loop(0, tpw // SC_TOK_GROUP)
        def _(g):
            tok0 = base + g * SC_TOK_GROUP
            pltpu.sync_copy(idx_hbm.at[pl.ds(tok0 * SC_GATHERS_PER_TOK, SC_GATHERS)], idx_v)
            pltpu.sync_copy(w_hbm.at[pl.ds(tok0, SC_TOK_GROUP)], w_v)

            @pl.loop(0, SC_TOK_GROUP)
            def _(t):
                @pl.loop(0, SC_CHUNKS)
                def _(c):
                    f_v[t, pl.ds(c * SC_LANES, SC_LANES)] = zero

            _sc_gather_loop(gather, compute)
            pltpu.sync_copy(f_v, f_hbm.at[pl.ds(tok0, SC_TOK_GROUP)])

    return k(w, idx2, v_tab)


def _peer_act_kernel(pre_ref, gate_ref, w_ref):
    pre = pre_ref[...]
    w_ref[...] = gate_ref[...] * (0.5 * pre * (1.0 + lax.erf(pre * (2.0 ** -0.5))))


def _peer_act(pre, gate, tm=2048):
    m, n = pre.shape
    blk = pl.BlockSpec((tm, n), lambda i: (i, 0))
    return pl.pallas_call(
        _peer_act_kernel, grid=(m // tm,), in_specs=[blk, blk], out_specs=blk,
        out_shape=jax.ShapeDtypeStruct((m, n), f32), compiler_params=_params(("parallel",)), name="peer_act",
    )(pre, gate)


def _res_ln_kernel(x_ref, f_ref, g_ref, b_ref, o_ref):
    o_ref[...] = _layer_norm(ALPHA * x_ref[...] + f_ref[...], g_ref[...], b_ref[...])


def _res_ln(x, f, g, b, tm=512):
    m, n = x.shape
    blk = pl.BlockSpec((tm, n), lambda i: (i, 0))
    row = pl.BlockSpec((1, n), lambda i: (0, 0))
    return pl.pallas_call(
        _res_ln_kernel, grid=(m // tm,), in_specs=[blk, blk, row, row], out_specs=blk,
        out_shape=jax.ShapeDtypeStruct((m, n), f32), compiler_params=_params(("parallel",)), name="res_ln",
    )(x, f, g.reshape(1, n), b.reshape(1, n))


def _peer_expert_ln(x, idx, gate, u_tab, v_tab, g, b):
    pre = _peer_udot(x, idx, u_tab)
    w = _peer_act(pre, gate)
    f = _peer_vacc(w, idx, v_tab)
    return _res_ln(x, f, g, b)


PEER_SPLITS = 4


def _peer_ffn_ln(x2, w_query, sub_keys, u_pack, v_pack, g, b):
    outs, first_idx = [], None
    for xs in jnp.split(x2, PEER_SPLITS, axis=0):
        qp = _matmul(xs, w_query)
        idx, gate = _peer_route(qp, sub_keys)
        first_idx = idx if first_idx is None else first_idx
        outs.append(_peer_expert_ln(xs, idx, gate, u_pack, v_pack, g, b))
    return jnp.concatenate(outs, axis=0), first_idx


def kernel(x, dn_w_in, dn_conv, dn_a_log, dn_dt_bias, dn_norm_w, dn_w_out, shared_w_kv, attn_w_q, attn_w_out,
           peer_w_query, peer_sub_keys, peer_u, peer_v, ln_mix_g, ln_mix_b, ln_ffn_g, ln_ffn_b):
    batch, seq, d = x.shape
    n_a = DEPTH // 2
    tabs = _rope_lane_tables(seq)
    gw = N_GROUPS * WIDTH
    dn_w_pad = [jnp.pad(w, ((0, 0), (0, DN_PROJ_PAD - w.shape[1]))).astype(bf16) for w in dn_w_in]
    dn_w_out_b = dn_w_out.astype(bf16)
    w_k, w_v = shared_w_kv[:, :gw].astype(bf16), shared_w_kv[:, gw:].astype(bf16)
    w_q, w_o = attn_w_q.astype(bf16), attn_w_out.astype(bf16)
    peer_wq = peer_w_query.astype(bf16)
    u_pack = [_pack_table(t) for t in peer_u]
    v_pack = [_pack_table(t) for t in peer_v]

    def trunk(x2):
        k_sh = v_sh = routed = None
        for layer in range(DEPTH):
            if layer < n_a:
                o = _deltanet_mixer(x2, dn_w_pad[layer], dn_conv[layer], dn_a_log[layer], dn_dt_bias[layer],
                                    dn_norm_w[layer], 1, seq)
                x2 = _matmul_res_ln(o, dn_w_out_b[layer], x2, ln_mix_g[layer], ln_mix_b[layer])
            else:
                j = layer - n_a
                if j == 0:
                    k_sh = _matmul_rope(x2, w_k, tabs, seq)
                    v_sh = _matmul(x2, w_v, tn=1024)
                q = _matmul_rope(x2, w_q[j], tabs, seq)
                outs, lses = [], []
                for gi, (window, dilation) in enumerate(ATTN_GROUPS):
                    o, l = _dilated_attention(q, k_sh, v_sh, gi, window, dilation, 1, seq)
                    outs.append(o)
                    lses.append(l)
                x2 = _combine_out_ln(outs, lses, w_o[j], x2, ln_mix_g[layer], ln_mix_b[layer])
            x2, first_idx = _peer_ffn_ln(x2, peer_wq[layer], peer_sub_keys[layer], u_pack[layer], v_pack[layer],
                                         ln_ffn_g[layer], ln_ffn_b[layer])
            routed = first_idx if routed is None else routed
        return x2, routed

    outs, routed = [], None
    for b in range(batch):
        xb = x[b]
        if routed is not None:
            xb, _ = lax.optimization_barrier((xb, routed))
        yb, routed = trunk(xb)
        outs.append(yb)
    return jnp.stack(outs, axis=0)
```

```python
import functools
import math

import jax
import jax.numpy as jnp
from jax import lax
from jax.experimental import pallas as pl
from jax.experimental.pallas import tpu as pltpu
from jax.experimental.pallas import tpu_sc as plsc

f32 = jnp.float32
bf16 = jnp.bfloat16
i32 = jnp.int32

D_MODEL = 1024
DEPTH = 2
ALPHA = (2.0 * DEPTH) ** 0.25
LN_EPS = 1e-5

HEADS = 8
HEAD_DIM = 128
WIDTH = HEADS * HEAD_DIM
CONV_K = 4
DN_CHUNK = 64

ATTN_GROUPS = ((128, 1), (512, 4), (2048, 16))
N_GROUPS = len(ATTN_GROUPS)
ATTN_BLOCK = 128
ROT_DIM = HEAD_DIM // 4
ROPE_THETA = 500000.0

PEER_HEADS = 8
PEER_NKEYS = 128
PEER_TOPK = 16
PEER_QDIM = 256
PEER_SEL = PEER_HEADS * PEER_TOPK

LANES = 128
SUBLANES = 8
VMEM_LIMIT = 48 * 1024 * 1024


def _params(sem):
    return pltpu.CompilerParams(dimension_semantics=sem, vmem_limit_bytes=VMEM_LIMIT)


def _dot(a, b):
    return jnp.dot(a.astype(bf16), b.astype(bf16), preferred_element_type=f32)


def _dot_nt(a, b):
    return lax.dot_general(a.astype(bf16), b.astype(bf16), (((1,), (1,)), ((), ())), preferred_element_type=f32)


def _dot_tn(a, b):
    return lax.dot_general(a.astype(bf16), b.astype(bf16), (((0,), (0,)), ((), ())), preferred_element_type=f32)


def _sigmoid(x):
    return 1.0 / (1.0 + jnp.exp(-x))


def _layer_norm(y, g, b):
    mu = jnp.mean(y, -1, keepdims=True)
    yc = y - mu
    var = jnp.mean(yc * yc, -1, keepdims=True)
    return yc * lax.rsqrt(var + LN_EPS) * g + b


def _mm_kernel(a_ref, w_ref, o_ref):
    o_ref[...] = _dot(a_ref[...], w_ref[...])


def _matmul(a, w, tm=512, tn=None):
    m, k = a.shape
    n = w.shape[1]
    tn = tn or n
    return pl.pallas_call(
        _mm_kernel,
        grid=(m // tm, n // tn),
        in_specs=[pl.BlockSpec((tm, k), lambda i, j: (i, 0)), pl.BlockSpec((k, tn), lambda i, j: (0, j))],
        out_specs=pl.BlockSpec((tm, tn), lambda i, j: (i, j)),
        out_shape=jax.ShapeDtypeStruct((m, n), f32),
        compiler_params=_params(("parallel", "parallel")),
        name="matmul",
    )(a, w)


def _rope(y, c, s_lo, s_hi):
    return y * c + pltpu.roll(y, ROT_DIM // 2, axis=1) * s_hi + pltpu.roll(y, LANES - ROT_DIM // 2, axis=1) * s_lo


def _mm_rope_kernel(a_ref, w_ref, c_ref, slo_ref, shi_ref, o_ref, *, heads):
    y = _dot(a_ref[...], w_ref[...])
    c, s_lo, s_hi = c_ref[...], slo_ref[...], shi_ref[...]
    for h in range(heads):
        sl = slice(h * HEAD_DIM, (h + 1) * HEAD_DIM)
        o_ref[:, sl] = _rope(y[:, sl], c, s_lo, s_hi)


def _matmul_rope(a, w, tabs, seq, tm=512, tn=1024):
    m, k = a.shape
    n = w.shape[1]
    nsb = seq // tm
    tab_spec = pl.BlockSpec((tm, LANES), lambda i, j: (i % nsb, 0))
    return pl.pallas_call(
        functools.partial(_mm_rope_kernel, heads=tn // HEAD_DIM),
        grid=(m // tm, n // tn),
        in_specs=[pl.BlockSpec((tm, k), lambda i, j: (i, 0)), pl.BlockSpec((k, tn), lambda i, j: (0, j)),
                  tab_spec, tab_spec, tab_spec],
        out_specs=pl.BlockSpec((tm, tn), lambda i, j: (i, j)),
        out_shape=jax.ShapeDtypeStruct((m, n), f32),
        compiler_params=_params(("parallel", "parallel")),
        name="matmul_rope",
    )(a, w, *tabs)


def _mm_res_ln_kernel(a_ref, w_ref, x_ref, g_ref, b_ref, o_ref):
    h = _dot(a_ref[...], w_ref[...])
    o_ref[...] = _layer_norm(ALPHA * x_ref[...] + h, g_ref[...], b_ref[...])


def _matmul_res_ln(a, w, x, g, b, tm=512):
    m, k = a.shape
    n = w.shape[1]
    row = pl.BlockSpec((1, n), lambda i: (0, 0))
    return pl.pallas_call(
        _mm_res_ln_kernel,
        grid=(m // tm,),
        in_specs=[pl.BlockSpec((tm, k), lambda i: (i, 0)), pl.BlockSpec((k, n), lambda i: (0, 0)),
                  pl.BlockSpec((tm, n), lambda i: (i, 0)), row, row],
        out_specs=pl.BlockSpec((tm, n), lambda i: (i, 0)),
        out_shape=jax.ShapeDtypeStruct((m, n), f32),
        compiler_params=_params(("parallel",)),
        name="matmul_res_ln",
    )(a, w, x, g.reshape(1, n), b.reshape(1, n))


DN_PROJ_PAD = 4 * WIDTH + LANES


def _dn_prep_kernel(proj_ref, halo_ref, conv_ref, alog_ref, dtb_ref, q_ref, k_ref, v_ref, gate_ref, *, tm, nsb):
    i = pl.program_id(0)
    x = proj_ref[:, : 3 * WIDTH]
    halo = jnp.where(i % nsb == 0, 0.0, halo_ref[...])
    w = conv_ref[...]
    row8 = lax.broadcasted_iota(i32, (SUBLANES, 1), 0)
    acc = x * w[CONV_K - 1:CONV_K]
    for s in range(1, CONV_K):
        xs = pltpu.roll(x, s, axis=0)
        hs = pltpu.roll(halo, s, axis=0)
        first = jnp.where(row8 < s, hs, xs[:SUBLANES])
        xs = jnp.concatenate([first, xs[SUBLANES:]], axis=0)
        acc = acc + xs * w[CONV_K - 1 - s:CONV_K - s]
    qkv = acc * _sigmoid(acc)
    for h in range(HEADS):
        sl = slice(h * HEAD_DIM, (h + 1) * HEAD_DIM)
        qh = qkv[:, sl]
        q_ref[:, sl] = qh * lax.rsqrt(jnp.sum(qh * qh, -1, keepdims=True) + 1e-6) * (HEAD_DIM ** -0.5)
        kh = qkv[:, WIDTH + h * HEAD_DIM: WIDTH + (h + 1) * HEAD_DIM]
        k_ref[:, sl] = kh * lax.rsqrt(jnp.sum(kh * kh, -1, keepdims=True) + 1e-6)
    v_ref[...] = qkv[:, 2 * WIDTH:]
    ab = proj_ref[:, 4 * WIDTH:]
    z = ab + dtb_ref[...]
    softplus = jnp.maximum(z, 0.0) + jnp.log(1.0 + jnp.exp(-jnp.abs(z)))
    g = -jnp.exp(alog_ref[...]) * softplus
    row = lax.broadcasted_iota(i32, (tm, 1), 0) % DN_CHUNK
    s = 1
    while s < DN_CHUNK:
        g = g + jnp.where(row >= s, pltpu.roll(g, s, axis=0), 0.0)
        s *= 2
    lane = lax.broadcasted_iota(i32, (1, LANES), 1)
    gate_ref[...] = jnp.where(lane < HEADS, g, _sigmoid(ab))


def _dn_prep(proj, conv_w, a_log, dt_bias, seq, tm=256):
    m = proj.shape[0]
    nsb = seq // tm
    pad = LANES - HEADS
    alog = jnp.pad(a_log.astype(f32), (0, pad)).reshape(1, LANES)
    dtb = jnp.pad(dt_bias.astype(f32), (0, pad)).reshape(1, LANES)
    hb = tm // SUBLANES
    out_w = pl.BlockSpec((tm, WIDTH), lambda i: (i, 0))
    return pl.pallas_call(
        functools.partial(_dn_prep_kernel, tm=tm, nsb=nsb),
        grid=(m // tm,),
        in_specs=[pl.BlockSpec((tm, DN_PROJ_PAD), lambda i: (i, 0)),
                  pl.BlockSpec((SUBLANES, 3 * WIDTH), lambda i: (jnp.maximum(i * hb - 1, 0), 0)),
                  pl.BlockSpec((CONV_K, 3 * WIDTH), lambda i: (0, 0)),
                  pl.BlockSpec((1, LANES), lambda i: (0, 0)), pl.BlockSpec((1, LANES), lambda i: (0, 0))],
        out_specs=[out_w, out_w, out_w, pl.BlockSpec((tm, LANES), lambda i: (i, 0))],
        out_shape=[jax.ShapeDtypeStruct((m, WIDTH), f32)] * 3 + [jax.ShapeDtypeStruct((m, LANES), f32)],
        compiler_params=_params(("parallel",)),
        name="dn_prep",
    )(proj, proj, conv_w, alog, dtb)


def _dn_delta_kernel(q_ref, k_ref, v_ref, z_ref, gate_ref, gt_ref, nw_ref, o_ref, state_ref, *, chunks):
    @pl.when(pl.program_id(1) == 0)
    def _():
        state_ref[...] = jnp.zeros_like(state_ref)

    c = DN_CHUNK
    ri = lax.broadcasted_iota(i32, (c, c), 0)
    ci = lax.broadcasted_iota(i32, (c, c), 1)
    causal = ri >= ci
    strict = ri > ci
    eye = (ri == ci).astype(f32)
    nw = nw_ref[...]
    inst = [(ch, h) for ch in range(chunks) for h in range(HEADS)]
    pre = {}
    for ch, h in inst:
        rows = slice(ch * c, (ch + 1) * c)
        sl = slice(h * HEAD_DIM, (h + 1) * HEAD_DIM)
        q, k, v = q_ref[rows, sl], k_ref[rows, sl], v_ref[rows, sl]
        gcol = gate_ref[rows, h:h + 1]
        beta = gate_ref[rows, HEADS + h:HEADS + h + 1]
        grow = gt_ref[h:h + 1, rows]
        decay = jnp.where(causal, jnp.exp(jnp.where(causal, gcol - grow, 0.0)), 0.0)
        kb = k * beta
        egc = jnp.exp(gcol)
        glast = gcol[c - 1:c, :]
        qk = _dot_nt(jnp.concatenate([q, kb], axis=0), k)
        m = jnp.where(strict, qk[c:] * decay, 0.0)
        pre[ch, h] = dict(a_qk=qk[:c] * decay, inv=eye - m, p=m,
                          rhs=jnp.concatenate([v * beta, kb * egc], axis=1), qe=q * egc,
                          kd=k * jnp.exp(glast - gcol), eg=jnp.exp(glast))
    step = 2
    while step < c:
        for key in inst:
            pre[key]["p"] = _dot(pre[key]["p"], pre[key]["p"])
        for key in inst:
            pre[key]["inv"] = pre[key]["inv"] + _dot(pre[key]["inv"], pre[key]["p"])
        step *= 2
    for key in inst:
        pre[key]["uw"] = _dot(pre[key]["inv"], pre[key]["rhs"])
    states = [state_ref[h] for h in range(HEADS)]
    for ch in range(chunks):
        rows = slice(ch * c, (ch + 1) * c)
        for h in range(HEADS):
            sl = slice(h * HEAD_DIM, (h + 1) * HEAD_DIM)
            d = pre[ch, h]
            u, w = d["uw"][:, :HEAD_DIM], d["uw"][:, HEAD_DIM:]
            ws = _dot(jnp.concatenate([w, d["qe"]], axis=0), states[h])
            v_new = u - ws[:c]
            o = ws[c:] + _dot(d["a_qk"], v_new)
            states[h] = states[h] * d["eg"] + _dot_tn(d["kd"], v_new)
            o = o * lax.rsqrt(jnp.mean(o * o, -1, keepdims=True) + 1e-6) * nw
            zz = z_ref[rows, sl]
            o_ref[rows, sl] = o * (zz * _sigmoid(zz))
    for h in range(HEADS):
        state_ref[h] = states[h]


def _dn_delta(q, k, v, proj, gates, gates_t, norm_w, batch, seq, cb=128):
    nb = seq // cb
    wide = pl.BlockSpec((cb, WIDTH), lambda b, i: (b * nb + i, 0))
    return pl.pallas_call(
        functools.partial(_dn_delta_kernel, chunks=cb // DN_CHUNK),
        grid=(batch, nb),
        in_specs=[wide, wide, wide,
                  pl.BlockSpec((cb, WIDTH), lambda b, i: (b * nb + i, 3)),
                  pl.BlockSpec((cb, LANES), lambda b, i: (b * nb + i, 0)),
                  pl.BlockSpec((SUBLANES, cb), lambda b, i: (0, b * nb + i)),
                  pl.BlockSpec((1, HEAD_DIM), lambda b, i: (0, 0))],
        out_specs=wide,
        out_shape=jax.ShapeDtypeStruct((batch * seq, WIDTH), f32),
        scratch_shapes=[pltpu.VMEM((HEADS, HEAD_DIM, HEAD_DIM), f32)],
        compiler_params=_params(("parallel", "arbitrary")),
        name="dn_delta",
    )(q, k, v, proj, gates, gates_t, norm_w.reshape(1, HEAD_DIM).astype(f32))


def _deltanet_mixer(x2, w_pad, conv_w, a_log, dt_bias, norm_w, batch, seq):
    proj = _matmul(x2, w_pad, tn=DN_PROJ_PAD // 3)
    q, k, v, gates = _dn_prep(proj, conv_w, a_log, dt_bias, seq)
    gates_t = gates[:, :SUBLANES].T
    return _dn_delta(q, k, v, proj, gates, gates_t, norm_w, batch, seq)


def _attn_kernel(q_ref, kp_ref, kc_ref, vp_ref, vc_ref, o_ref, l_ref, *, steps, dilation, heads):
    nbi = pl.program_id(1)
    blk = ATTN_BLOCK
    qi = lax.broadcasted_iota(i32, (blk, 2 * blk), 0)
    kj = lax.broadcasted_iota(i32, (blk, 2 * blk), 1)
    dist = qi + blk - kj
    mask = (dist >= 0) & (dist <= steps) & ((kj >= blk) | (nbi > 0))
    scale = HEAD_DIM ** -0.5
    for r in range(dilation):
        rows = pl.ds(r, blk, stride=dilation) if dilation > 1 else slice(None)
        for h in range(heads):
            sl = slice(h * HEAD_DIM, (h + 1) * HEAD_DIM)
            q = q_ref[rows, sl]
            kk = jnp.concatenate([kp_ref[rows, sl], kc_ref[rows, sl]], axis=0)
            vv = jnp.concatenate([vp_ref[rows, sl], vc_ref[rows, sl]], axis=0)
            sc = jnp.where(mask, _dot_nt(q, kk) * scale, -jnp.inf)
            mx = jnp.max(sc, -1, keepdims=True)
            p = jnp.exp(sc - mx)
            den = jnp.sum(p, -1, keepdims=True)
            o_ref[rows, sl] = _dot(p, vv) / den
            l_ref[rows, sl] = jnp.broadcast_to(mx + jnp.log(den), (blk, HEAD_DIM))


def _dilated_attention(q, k, v, group, window, dilation, batch, seq):
    rows = ATTN_BLOCK * dilation
    nb = seq // rows
    heads = HEADS if dilation == 1 else 1
    hblocks = HEADS // heads
    blk = (rows, heads * HEAD_DIM)

    def cur(b, i, j):
        return (b * nb + i, group * hblocks + j)

    def prev(b, i, j):
        return (b * nb + jnp.maximum(i - 1, 0), group * hblocks + j)

    out_spec = pl.BlockSpec(blk, lambda b, i, j: (b * nb + i, j))
    return pl.pallas_call(
        functools.partial(_attn_kernel, steps=window // dilation, dilation=dilation, heads=heads),
        grid=(batch, nb, hblocks),
        in_specs=[pl.BlockSpec(blk, cur), pl.BlockSpec(blk, prev), pl.BlockSpec(blk, cur),
                  pl.BlockSpec(blk, prev), pl.BlockSpec(blk, cur)],
        out_specs=[out_spec, out_spec],
        out_shape=[jax.ShapeDtypeStruct((batch * seq, WIDTH), f32)] * 2,
        compiler_params=_params(("parallel", "parallel", "parallel")),
        name=f"dilated_attn_d{dilation}",
    )(q, k, k, v, v)


def _combine_out_ln_kernel(o0, o1, o2, l0, l1, l2, w_ref, x_ref, g_ref, b_ref, out_ref):
    ls = [l0[...], l1[...], l2[...]]
    mx = jnp.maximum(jnp.maximum(ls[0], ls[1]), ls[2])
    es = [jnp.exp(l - mx) for l in ls]
    o = (es[0] * o0[...] + es[1] * o1[...] + es[2] * o2[...]) / (es[0] + es[1] + es[2])
    h = _dot(o, w_ref[...])
    out_ref[...] = _layer_norm(ALPHA * x_ref[...] + h, g_ref[...], b_ref[...])


def _combine_out_ln(outs, lses, w, x, g, b, tm=256):
    m, n = x.shape
    blk = pl.BlockSpec((tm, n), lambda i: (i, 0))
    row = pl.BlockSpec((1, n), lambda i: (0, 0))
    return pl.pallas_call(
        _combine_out_ln_kernel,
        grid=(m // tm,),
        in_specs=[blk] * 6 + [pl.BlockSpec((WIDTH, n), lambda i: (0, 0)), blk, row, row],
        out_specs=blk,
        out_shape=jax.ShapeDtypeStruct((m, n), f32),
        compiler_params=_params(("parallel",)),
        name="attn_combine_out_ln",
    )(*outs, *lses, w, x, g.reshape(1, n), b.reshape(1, n))


def _rope_lane_tables(seq):
    half = ROT_DIM // 2
    inv_freq = ROPE_THETA ** (-jnp.arange(half, dtype=f32) * 2.0 / ROT_DIM)
    ang = jnp.arange(seq, dtype=f32)[:, None] * inv_freq[None, :]
    cos, sin = jnp.cos(ang), jnp.sin(ang)
    ones = jnp.ones((seq, HEAD_DIM - ROT_DIM), f32)
    zeros = jnp.zeros((seq, HEAD_DIM - half), f32)
    c = jnp.concatenate([cos, cos, ones], axis=1)
    s_lo = jnp.concatenate([-sin, zeros], axis=1)
    s_hi = jnp.concatenate([jnp.zeros((seq, half), f32), sin, jnp.zeros((seq, HEAD_DIM - ROT_DIM), f32)], axis=1)
    return c, s_lo, s_hi


def _top_rows(s, order, payload, k):
    big = jnp.int32(2 ** 30)
    vals, pays = [], []
    for _ in range(k):
        m = jnp.max(s, axis=0, keepdims=True)
        pos = jnp.min(jnp.where(s == m, order, big), axis=0, keepdims=True)
        sel = order == pos
        pays.append(pos if payload is None else jnp.max(jnp.where(sel, payload, -1), axis=0, keepdims=True))
        vals.append(m)
        s = jnp.where(sel, -jnp.inf, s)
    return jnp.concatenate(vals, axis=0), jnp.concatenate(pays, axis=0)


def _pair_candidates(t1, t2, combine, fill):
    kt = PEER_TOPK
    row = lax.broadcasted_iota(i32, (SUBLANES, LANES), 0)
    bc = lambda t, r: jnp.broadcast_to(t[r:r + 1], (SUBLANES, LANES))
    lo1, hi1, lo2, hi2 = t1[:SUBLANES], t1[SUBLANES:], t2[:SUBLANES], t2[SUBLANES:]
    tiles = [combine(bc(t1, 0), lo2), combine(bc(t1, 0), hi2), combine(bc(t1, 1), lo2),
             jnp.where(row < kt // 3, combine(bc(t1, 2), lo2), fill),
             jnp.where(row < kt // 4, combine(bc(t1, 3), lo2), fill),
             combine(hi1, bc(t2, 0)),
             jnp.where(row >= 4, combine(lo1, bc(t2, 0)), fill),
             jnp.where(row >= 4, combine(lo1, bc(t2, 1)), jnp.where(row == 0, combine(bc(t1, 4), bc(t2, 2)), fill))]
    return jnp.concatenate(tiles, axis=0)


def _pair_rank():
    kt = PEER_TOPK
    row = lax.broadcasted_iota(i32, (SUBLANES, LANES), 0)
    unused = kt * kt + row
    tiles = [row, row + SUBLANES, kt + row,
             jnp.where(row < kt // 3, 2 * kt + row, unused),
             jnp.where(row < kt // 4, 3 * kt + row, unused + SUBLANES),
             (row + SUBLANES) * kt,
             jnp.where(row >= 4, row * kt, unused + 2 * SUBLANES),
             jnp.where(row >= 4, row * kt + 1, jnp.where(row == 0, 4 * kt + 2, unused + 3 * SUBLANES))]
    return jnp.concatenate(tiles, axis=0)


def _peer_route_kernel(qp_ref, keys_ref, idx_ref, gate_ref):
    kt = PEER_TOPK
    half = PEER_QDIM // 2
    keyid = lax.broadcasted_iota(i32, (PEER_NKEYS, LANES), 0)
    rank = _pair_rank()
    idx_rows, gate_rows = [], []
    for h in range(PEER_HEADS):
        tops = []
        for c in range(2):
            col = (h * 2 + c) * half
            s = _dot_nt(keys_ref[h * 2 + c], qp_ref[:, col:col + half])
            tops.append(_top_rows(s, keyid, None, kt))
        (v1, i1), (v2, i2) = tops
        cand = _pair_candidates(v1, v2, lambda a, b: a + b, -jnp.inf)
        cand_id = _pair_candidates(i1, i2, lambda a, b: a * PEER_NKEYS + b, -1)
        best, ids = _top_rows(cand, rank, cand_id, kt)
        e = jnp.exp(best - best[0:1])
        gate_rows.append(e / jnp.sum(e, axis=0, keepdims=True))
        idx_rows.append(ids)
    idx_ref[...] = jnp.concatenate(idx_rows, axis=0).astype(f32).T.astype(i32)
    gate_ref[...] = jnp.concatenate(gate_rows, axis=0).T


def _peer_route(qp, sub_keys):
    m = qp.shape[0]
    tb = LANES
    keys = sub_keys.reshape(PEER_HEADS * 2, PEER_NKEYS, PEER_QDIM // 2).astype(bf16)
    out = pl.BlockSpec((tb, PEER_SEL), lambda i: (i, 0))
    return pl.pallas_call(
        _peer_route_kernel,
        grid=(m // tb,),
        in_specs=[pl.BlockSpec((tb, qp.shape[1]), lambda i: (i, 0)),
                  pl.BlockSpec(keys.shape, lambda i: (0, 0, 0))],
        out_specs=[out, out],
        out_shape=[jax.ShapeDtypeStruct((m, PEER_SEL), i32), jax.ShapeDtypeStruct((m, PEER_SEL), f32)],
        compiler_params=_params(("parallel",)),
        name="peer_route",
    )(qp, keys)


SC_CORES = 2
SC_SUBCORES = 16
SC_LANES = 16
SC_WORKERS = SC_CORES * SC_SUBCORES
SC_CHUNKS = D_MODEL // SC_LANES
SC_WORDS = D_MODEL // 2
SC_WORD_CHUNKS = SC_WORDS // SC_LANES
SC_TOK_GROUP = 32
SC_ROWS = 64
SC_GATHERS_PER_TOK = PEER_SEL // SC_ROWS
SC_GATHERS = SC_TOK_GROUP * SC_GATHERS_PER_TOK
SC_ROW_GROUP = 8


def _pack_table(tab):
    e = tab.shape[0]
    bits = lax.bitcast_convert_type(tab.astype(bf16), jnp.uint16).astype(jnp.uint32)
    bits = bits.reshape(e, SC_WORD_CHUNKS, 2, SC_LANES)
    word = bits[:, :, 0, :] | (bits[:, :, 1, :] << 16)
    return lax.bitcast_convert_type(word.reshape(e, SC_WORDS), i32)


def _unpack_words(w):
    return lax.bitcast_convert_type(w << 16, f32), lax.bitcast_convert_type(w & (-65536), f32)


def _sc_mesh():
    return plsc.VectorSubcoreMesh(core_axis_name="c", subcore_axis_name="s")


def _sc_worker():
    return lax.axis_index("s") * SC_CORES + lax.axis_index("c")


def _sc_gather_loop(gather, compute):
    gather(0, 0).start()

    @pl.loop(0, SC_GATHERS, step=2)
    def _(q):
        gather(q + 1, 1).start()
        gather(q, 0).wait()
        compute(q, 0)

        @pl.when(q + 2 < SC_GATHERS)
        def _():
            gather(q + 2, 0).start()
        gather(q + 1, 1).wait()
        compute(q + 1, 1)


def _sc_scratch(stage_shape):
    return [pltpu.VMEM((SC_GATHERS, SC_ROWS), i32), pltpu.VMEM(stage_shape[0], f32), pltpu.VMEM(stage_shape[1], f32),
            pltpu.VMEM((SC_ROWS, SC_WORDS), i32), pltpu.VMEM((SC_ROWS, SC_WORDS), i32),
            pltpu.SemaphoreType.DMA, pltpu.SemaphoreType.DMA]


def _peer_udot(x, idx, u_tab):
    t_all = x.shape[0]
    tpw = t_all // SC_WORKERS
    idx2 = idx.reshape(t_all * SC_GATHERS_PER_TOK, SC_ROWS)

    @functools.partial(
        pl.kernel, mesh=_sc_mesh(), compiler_params=pltpu.CompilerParams(needs_layout_passes=False),
        out_type=jax.ShapeDtypeStruct((t_all, PEER_SEL), f32),
        scratch_types=_sc_scratch(((SC_TOK_GROUP, D_MODEL), (SC_TOK_GROUP, PEER_SEL))), name="peer_udot_sc")
    def k(x_hbm, idx_hbm, u_hbm, pre_hbm, idx_v, x_v, pre_v, rows0, rows1, sem0, sem1):
        bufs = ((rows0, sem0), (rows1, sem1))
        base = _sc_worker() * tpw
        lane = lax.iota(i32, SC_LANES)
        zero = jnp.zeros((SC_LANES,), f32)

        def gather(q, b):
            return pltpu.make_async_copy(u_hbm.at[idx_v.at[q]], bufs[b][0], bufs[b][1])

        def compute(q, b):
            rows = bufs[b][0]
            t = q // SC_GATHERS_PER_TOK
            col0 = (q % SC_GATHERS_PER_TOK) * SC_ROWS
            for hh in range(SC_ROWS // SC_LANES):
                vec = zero
                for rg in range(SC_LANES // SC_ROW_GROUP):
                    r0 = hh * SC_LANES + rg * SC_ROW_GROUP

                    def body(c, acc):
                        x_lo = x_v[t, pl.ds(c * 2 * SC_LANES, SC_LANES)]
                        x_hi = x_v[t, pl.ds(c * 2 * SC_LANES + SC_LANES, SC_LANES)]
                        out = []
                        for r in range(SC_ROW_GROUP):
                            lo, hi = _unpack_words(rows[r0 + r, pl.ds(c * SC_LANES, SC_LANES)])
                            out.append(acc[r] + lo * x_lo + hi * x_hi)
                        return tuple(out)

                    acc = lax.fori_loop(0, SC_WORD_CHUNKS, body, (zero,) * SC_ROW_GROUP, unroll=2)
                    for r in range(SC_ROW_GROUP):
                        vec = jnp.where(lane == rg * SC_ROW_GROUP + r, jnp.sum(acc[r]), vec)
                pre_v[t, pl.ds(col0 + hh * SC_LANES, SC_LANES)] = vec

        @pl.loop(0, tpw // SC_TOK_GROUP)
        def _(g):
            tok0 = base + g * SC_TOK_GROUP
            pltpu.sync_copy(idx_hbm.at[pl.ds(tok0 * SC_GATHERS_PER_TOK, SC_GATHERS)], idx_v)
            pltpu.sync_copy(x_hbm.at[pl.ds(tok0, SC_TOK_GROUP)], x_v)
            _sc_gather_loop(gather, compute)
            pltpu.sync_copy(pre_v, pre_hbm.at[pl.ds(tok0, SC_TOK_GROUP)])

    return k(x, idx2, u_tab)


def _peer_vacc(w, idx, v_tab):
    t_all = w.shape[0]
    tpw = t_all // SC_WORKERS
    idx2 = idx.reshape(t_all * SC_GATHERS_PER_TOK, SC_ROWS)
    half_w = SC_WORD_CHUNKS // 2

    @functools.partial(
        pl.kernel, mesh=_sc_mesh(), compiler_params=pltpu.CompilerParams(needs_layout_passes=False),
        out_type=jax.ShapeDtypeStruct((t_all, D_MODEL), f32),
        scratch_types=_sc_scratch(((SC_TOK_GROUP, PEER_SEL), (SC_TOK_GROUP, D_MODEL))), name="peer_vacc_sc")
    def k(w_hbm, idx_hbm, v_hbm, f_hbm, idx_v, w_v, f_v, rows0, rows1, sem0, sem1):
        bufs = ((rows0, sem0), (rows1, sem1))
        base = _sc_worker() * tpw
        zero = jnp.zeros((SC_LANES,), f32)

        def gather(q, b):
            return pltpu.make_async_copy(v_hbm.at[idx_v.at[q]], bufs[b][0], bufs[b][1])

        def compute(q, b):
            rows = bufs[b][0]
            t = q // SC_GATHERS_PER_TOK
            col0 = (q % SC_GATHERS_PER_TOK) * SC_ROWS
            tvec = jnp.full((SC_LANES,), t, i32)
            for half in range(2):
                def body(r, acc):
                    wr = plsc.load_gather(w_v, [tvec, jnp.full((SC_LANES,), col0 + r, i32)])
                    out = []
                    for c in range(half_w):
                        lo, hi = _unpack_words(rows[r, pl.ds((half * half_w + c) * SC_LANES, SC_LANES)])
                        out.append(acc[2 * c] + wr * lo)
                        out.append(acc[2 * c + 1] + wr * hi)
                    return tuple(out)

                acc = lax.fori_loop(0, SC_ROWS, body, (zero,) * (2 * half_w))
                for c in range(2 * half_w):
                    sl = pl.ds((half * 2 * half_w + c) * SC_LANES, SC_LANES)
                    f_v[t, sl] = f_v[t, sl] + acc[c]

        @pl.loop(0, tpw // SC_TOK_GROUP)
        def _(g):
            tok0 = base + g * SC_TOK_GROUP
            pltpu.sync_copy(idx_hbm.at[pl.ds(tok0 * SC_GATHERS_PER_TOK, SC_GATHERS)], idx_v)
            pltpu.sync_copy(w_hbm.at[pl.ds(tok0, SC_TOK_GROUP)], w_v)

            @pl.loop(0, SC_TOK_GROUP)
            def _(t):
                @pl.loop(0, SC_CHUNKS)
                def _(c):
                    f_v[t, pl.ds(c * SC_LANES, SC_LANES)] = zero

            _sc_gather_loop(gather, compute)
            pltpu.sync_copy(f_v, f_hbm.at[pl.ds(tok0, SC_TOK_GROUP)])

    return k(w, idx2, v_tab)


def _peer_act_kernel(pre_ref, gate_ref, w_ref):
    pre = pre_ref[...]
    w_ref[...] = gate_ref[...] * (0.5 * pre * (1.0 + lax.erf(pre * (2.0 ** -0.5))))


def _peer_act(pre, gate, tm=2048):
    m, n = pre.shape
    blk = pl.BlockSpec((tm, n), lambda i: (i, 0))
    return pl.pallas_call(
        _peer_act_kernel, grid=(m // tm,), in_specs=[blk, blk], out_specs=blk,
        out_shape=jax.ShapeDtypeStruct((m, n), f32), compiler_params=_params(("parallel",)), name="peer_act",
    )(pre, gate)


def _res_ln_kernel(x_ref, f_ref, g_ref, b_ref, o_ref):
    o_ref[...] = _layer_norm(ALPHA * x_ref[...] + f_ref[...], g_ref[...], b_ref[...])


def _res_ln(x, f, g, b, tm=512):
    m, n = x.shape
    blk = pl.BlockSpec((tm, n), lambda i: (i, 0))
    row = pl.BlockSpec((1, n), lambda i: (0, 0))
    return pl.pallas_call(
        _res_ln_kernel, grid=(m // tm,), in_specs=[blk, blk, row, row], out_specs=blk,
        out_shape=jax.ShapeDtypeStruct((m, n), f32), compiler_params=_params(("parallel",)), name="res_ln",
    )(x, f, g.reshape(1, n), b.reshape(1, n))


def _peer_expert_ln(x, idx, gate, u_tab, v_tab, g, b):
    pre = _peer_udot(x, idx, u_tab)
    w = _peer_act(pre, gate)
    f = _peer_vacc(w, idx, v_tab)
    return _res_ln(x, f, g, b)


PEER_SPLITS = 4


def _peer_ffn_ln(x2, w_query, sub_keys, u_pack, v_pack, g, b):
    outs, idx = [], None
    for xs in jnp.split(x2, PEER_SPLITS, axis=0):
        qp = _matmul(xs, w_query)
        idx, gate = _peer_route(qp, sub_keys)
        outs.append(_peer_expert_ln(xs, idx, gate, u_pack, v_pack, g, b))
    return jnp.concatenate(outs, axis=0), idx


def kernel(x, dn_w_in, dn_conv, dn_a_log, dn_dt_bias, dn_norm_w, dn_w_out, shared_w_kv, attn_w_q, attn_w_out,
           peer_w_query, peer_sub_keys, peer_u, peer_v, ln_mix_g, ln_mix_b, ln_ffn_g, ln_ffn_b):
    batch, seq, d = x.shape
    n_a = DEPTH // 2
    tabs = _rope_lane_tables(seq)
    gw = N_GROUPS * WIDTH
    dn_w_pad = [jnp.pad(w, ((0, 0), (0, DN_PROJ_PAD - w.shape[1]))).astype(bf16) for w in dn_w_in]
    dn_w_out_b = dn_w_out.astype(bf16)
    w_k, w_v = shared_w_kv[:, :gw].astype(bf16), shared_w_kv[:, gw:].astype(bf16)
    w_q, w_o = attn_w_q.astype(bf16), attn_w_out.astype(bf16)
    peer_wq = peer_w_query.astype(bf16)
    u_pack = [_pack_table(t) for t in peer_u]
    v_pack = [_pack_table(t) for t in peer_v]

    def trunk(x2):
        k_sh = v_sh = routed = None
        for layer in range(DEPTH):
            if layer < n_a:
                o = _deltanet_mixer(x2, dn_w_pad[layer], dn_conv[layer], dn_a_log[layer], dn_dt_bias[layer],
                                    dn_norm_w[layer], 1, seq)
                x2 = _matmul_res_ln(o, dn_w_out_b[layer], x2, ln_mix_g[layer], ln_mix_b[layer])
            else:
                j = layer - n_a
                if j == 0:
                    k_sh = _matmul_rope(x2, w_k, tabs, seq)
                    v_sh = _matmul(x2, w_v, tn=1024)
                q = _matmul_rope(x2, w_q[j], tabs, seq)
                outs, lses = [], []
                for gi, (window, dilation) in enumerate(ATTN_GROUPS):
                    o, l = _dilated_attention(q, k_sh, v_sh, gi, window, dilation, 1, seq)
                    outs.append(o)
                    lses.append(l)
                x2 = _combine_out_ln(outs, lses, w_o[j], x2, ln_mix_g[layer], ln_mix_b[layer])
            x2, first_idx = _peer_ffn_ln(x2, peer_wq[layer], peer_sub_keys[layer], u_pack[layer], v_pack[layer],
                                         ln_ffn_g[layer], ln_ffn_b[layer])
            routed = first_idx if routed is None else routed
        return x2, routed

    outs, routed = [], None
    for b in range(batch):
        xb = x[b]
        if routed is not None:
            xb, _ = lax.optimization_barrier((xb, routed))
        yb, routed = trunk(xb)
        outs.append(yb)
    return jnp.stack(outs, axis=0)
```

```python
import functools
import math

import jax
import jax.numpy as jnp
from jax import lax
from jax.experimental import pallas as pl
from jax.experimental.pallas import tpu as pltpu
from jax.experimental.pallas import tpu_sc as plsc

f32 = jnp.float32
bf16 = jnp.bfloat16
i32 = jnp.int32

D_MODEL = 1024
DEPTH = 2
ALPHA = (2.0 * DEPTH) ** 0.25
LN_EPS = 1e-5

HEADS = 8
HEAD_DIM = 128
WIDTH = HEADS * HEAD_DIM
CONV_K = 4
DN_CHUNK = 64

ATTN_GROUPS = ((128, 1), (512, 4), (2048, 16))
N_GROUPS = len(ATTN_GROUPS)
ATTN_BLOCK = 128
ROT_DIM = HEAD_DIM // 4
ROPE_THETA = 500000.0

PEER_HEADS = 8
PEER_NKEYS = 128
PEER_TOPK = 16
PEER_QDIM = 256
PEER_SEL = PEER_HEADS * PEER_TOPK

LANES = 128
SUBLANES = 8
VMEM_LIMIT = 48 * 1024 * 1024


def _params(sem):
    return pltpu.CompilerParams(dimension_semantics=sem, vmem_limit_bytes=VMEM_LIMIT)


def _dot(a, b):
    return jnp.dot(a.astype(bf16), b.astype(bf16), preferred_element_type=f32)


def _dot_nt(a, b):
    return lax.dot_general(a.astype(bf16), b.astype(bf16), (((1,), (1,)), ((), ())), preferred_element_type=f32)


def _dot_tn(a, b):
    return lax.dot_general(a.astype(bf16), b.astype(bf16), (((0,), (0,)), ((), ())), preferred_element_type=f32)


def _sigmoid(x):
    return 1.0 / (1.0 + jnp.exp(-x))


def _layer_norm(y, g, b):
    mu = jnp.mean(y, -1, keepdims=True)
    yc = y - mu
    var = jnp.mean(yc * yc, -1, keepdims=True)
    return yc * lax.rsqrt(var + LN_EPS) * g + b


def _mm_kernel(a_ref, w_ref, o_ref):
    o_ref[...] = _dot(a_ref[...], w_ref[...])


def _matmul(a, w, tm=512, tn=None):
    m, k = a.shape
    n = w.shape[1]
    tn = tn or n
    return pl.pallas_call(
        _mm_kernel,
        grid=(m // tm, n // tn),
        in_specs=[pl.BlockSpec((tm, k), lambda i, j: (i, 0)), pl.BlockSpec((k, tn), lambda i, j: (0, j))],
        out_specs=pl.BlockSpec((tm, tn), lambda i, j: (i, j)),
        out_shape=jax.ShapeDtypeStruct((m, n), f32),
        compiler_params=_params(("parallel", "parallel")),
        name="matmul",
    )(a, w)


def _rope(y, c, s_lo, s_hi):
    return y * c + pltpu.roll(y, ROT_DIM // 2, axis=1) * s_hi + pltpu.roll(y, LANES - ROT_DIM // 2, axis=1) * s_lo


def _mm_rope_kernel(a_ref, w_ref, c_ref, slo_ref, shi_ref, o_ref, *, heads):
    y = _dot(a_ref[...], w_ref[...])
    c, s_lo, s_hi = c_ref[...], slo_ref[...], shi_ref[...]
    for h in range(heads):
        sl = slice(h * HEAD_DIM, (h + 1) * HEAD_DIM)
        o_ref[:, sl] = _rope(y[:, sl], c, s_lo, s_hi)


def _matmul_rope(a, w, tabs, seq, tm=512, tn=1024):
    m, k = a.shape
    n = w.shape[1]
    nsb = seq // tm
    tab_spec = pl.BlockSpec((tm, LANES), lambda i, j: (i % nsb, 0))
    return pl.pallas_call(
        functools.partial(_mm_rope_kernel, heads=tn // HEAD_DIM),
        grid=(m // tm, n // tn),
        in_specs=[pl.BlockSpec((tm, k), lambda i, j: (i, 0)), pl.BlockSpec((k, tn), lambda i, j: (0, j)),
                  tab_spec, tab_spec, tab_spec],
        out_specs=pl.BlockSpec((tm, tn), lambda i, j: (i, j)),
        out_shape=jax.ShapeDtypeStruct((m, n), f32),
        compiler_params=_params(("parallel", "parallel")),
        name="matmul_rope",
    )(a, w, *tabs)


def _mm_res_ln_kernel(a_ref, w_ref, x_ref, g_ref, b_ref, o_ref):
    h = _dot(a_ref[...], w_ref[...])
    o_ref[...] = _layer_norm(ALPHA * x_ref[...] + h, g_ref[...], b_ref[...])


def _matmul_res_ln(a, w, x, g, b, tm=512):
    m, k = a.shape
    n = w.shape[1]
    row = pl.BlockSpec((1, n), lambda i: (0, 0))
    return pl.pallas_call(
        _mm_res_ln_kernel,
        grid=(m // tm,),
        in_specs=[pl.BlockSpec((tm, k), lambda i: (i, 0)), pl.BlockSpec((k, n), lambda i: (0, 0)),
                  pl.BlockSpec((tm, n), lambda i: (i, 0)), row, row],
        out_specs=pl.BlockSpec((tm, n), lambda i: (i, 0)),
        out_shape=jax.ShapeDtypeStruct((m, n), f32),
        compiler_params=_params(("parallel",)),
        name="matmul_res_ln",
    )(a, w, x, g.reshape(1, n), b.reshape(1, n))


DN_PROJ_PAD = 4 * WIDTH + LANES


def _dn_prep_kernel(proj_ref, halo_ref, conv_ref, alog_ref, dtb_ref, q_ref, k_ref, v_ref, gate_ref, *, tm, nsb):
    i = pl.program_id(0)
    x = proj_ref[:, : 3 * WIDTH]
    halo = jnp.where(i % nsb == 0, 0.0, halo_ref[...])
    w = conv_ref[...]
    row8 = lax.broadcasted_iota(i32, (SUBLANES, 1), 0)
    acc = x * w[CONV_K - 1:CONV_K]
    for s in range(1, CONV_K):
        xs = pltpu.roll(x, s, axis=0)
        hs = pltpu.roll(halo, s, axis=0)
        first = jnp.where(row8 < s, hs, xs[:SUBLANES])
        xs = jnp.concatenate([first, xs[SUBLANES:]], axis=0)
        acc = acc + xs * w[CONV_K - 1 - s:CONV_K - s]
    qkv = acc * _sigmoid(acc)
    for h in range(HEADS):
        sl = slice(h * HEAD_DIM, (h + 1) * HEAD_DIM)
        qh = qkv[:, sl]
        q_ref[:, sl] = qh * lax.rsqrt(jnp.sum(qh * qh, -1, keepdims=True) + 1e-6) * (HEAD_DIM ** -0.5)
        kh = qkv[:, WIDTH + h * HEAD_DIM: WIDTH + (h + 1) * HEAD_DIM]
        k_ref[:, sl] = kh * lax.rsqrt(jnp.sum(kh * kh, -1, keepdims=True) + 1e-6)
    v_ref[...] = qkv[:, 2 * WIDTH:]
    ab = proj_ref[:, 4 * WIDTH:]
    z = ab + dtb_ref[...]
    softplus = jnp.maximum(z, 0.0) + jnp.log(1.0 + jnp.exp(-jnp.abs(z)))
    g = -jnp.exp(alog_ref[...]) * softplus
    row = lax.broadcasted_iota(i32, (tm, 1), 0) % DN_CHUNK
    s = 1
    while s < DN_CHUNK:
        g = g + jnp.where(row >= s, pltpu.roll(g, s, axis=0), 0.0)
        s *= 2
    lane = lax.broadcasted_iota(i32, (1, LANES), 1)
    gate_ref[...] = jnp.where(lane < HEADS, g, _sigmoid(ab))


def _dn_prep(proj, conv_w, a_log, dt_bias, seq, tm=256):
    m = proj.shape[0]
    nsb = seq // tm
    pad = LANES - HEADS
    alog = jnp.pad(a_log.astype(f32), (0, pad)).reshape(1, LANES)
    dtb = jnp.pad(dt_bias.astype(f32), (0, pad)).reshape(1, LANES)
    hb = tm // SUBLANES
    out_w = pl.BlockSpec((tm, WIDTH), lambda i: (i, 0))
    return pl.pallas_call(
        functools.partial(_dn_prep_kernel, tm=tm, nsb=nsb),
        grid=(m // tm,),
        in_specs=[pl.BlockSpec((tm, DN_PROJ_PAD), lambda i: (i, 0)),
                  pl.BlockSpec((SUBLANES, 3 * WIDTH), lambda i: (jnp.maximum(i * hb - 1, 0), 0)),
                  pl.BlockSpec((CONV_K, 3 * WIDTH), lambda i: (0, 0)),
                  pl.BlockSpec((1, LANES), lambda i: (0, 0)), pl.BlockSpec((1, LANES), lambda i: (0, 0))],
        out_specs=[out_w, out_w, out_w, pl.BlockSpec((tm, LANES), lambda i: (i, 0))],
        out_shape=[jax.ShapeDtypeStruct((m, WIDTH), f32)] * 3 + [jax.ShapeDtypeStruct((m, LANES), f32)],
        compiler_params=_params(("parallel",)),
        name="dn_prep",
    )(proj, proj, conv_w, alog, dtb)


def _dn_delta_kernel(q_ref, k_ref, v_ref, z_ref, gate_ref, gt_ref, nw_ref, o_ref, state_ref, *, chunks):
    @pl.when(pl.program_id(1) == 0)
    def _():
        state_ref[...] = jnp.zeros_like(state_ref)

    c = DN_CHUNK
    ri = lax.broadcasted_iota(i32, (c, c), 0)
    ci = lax.broadcasted_iota(i32, (c, c), 1)
    causal = ri >= ci
    strict = ri > ci
    eye = (ri == ci).astype(f32)
    nw = nw_ref[...]
    inst = [(ch, h) for ch in range(chunks) for h in range(HEADS)]
    pre = {}
    for ch, h in inst:
        rows = slice(ch * c, (ch + 1) * c)
        sl = slice(h * HEAD_DIM, (h + 1) * HEAD_DIM)
        q, k, v = q_ref[rows, sl], k_ref[rows, sl], v_ref[rows, sl]
        gcol = gate_ref[rows, h:h + 1]
        beta = gate_ref[rows, HEADS + h:HEADS + h + 1]
        grow = gt_ref[h:h + 1, rows]
        decay = jnp.where(causal, jnp.exp(jnp.where(causal, gcol - grow, 0.0)), 0.0)
        kb = k * beta
        egc = jnp.exp(gcol)
        glast = gcol[c - 1:c, :]
        qk = _dot_nt(jnp.concatenate([q, kb], axis=0), k)
        m = jnp.where(strict, qk[c:] * decay, 0.0)
        pre[ch, h] = dict(a_qk=qk[:c] * decay, inv=eye - m, p=m,
                          rhs=jnp.concatenate([v * beta, kb * egc], axis=1), qe=q * egc,
                          kd=k * jnp.exp(glast - gcol), eg=jnp.exp(glast))
    step = 2
    while step < c:
        for key in inst:
            pre[key]["p"] = _dot(pre[key]["p"], pre[key]["p"])
        for key in inst:
            pre[key]["inv"] = pre[key]["inv"] + _dot(pre[key]["inv"], pre[key]["p"])
        step *= 2
    for key in inst:
        pre[key]["uw"] = _dot(pre[key]["inv"], pre[key]["rhs"])
    states = [state_ref[h] for h in range(HEADS)]
    for ch in range(chunks):
        rows = slice(ch * c, (ch + 1) * c)
        for h in range(HEADS):
            sl = slice(h * HEAD_DIM, (h + 1) * HEAD_DIM)
            d = pre[ch, h]
            u, w = d["uw"][:, :HEAD_DIM], d["uw"][:, HEAD_DIM:]
            ws = _dot(jnp.concatenate([w, d["qe"]], axis=0), states[h])
            v_new = u - ws[:c]
            o = ws[c:] + _dot(d["a_qk"], v_new)
            states[h] = states[h] * d["eg"] + _dot_tn(d["kd"], v_new)
            o = o * lax.rsqrt(jnp.mean(o * o, -1, keepdims=True) + 1e-6) * nw
            zz = z_ref[rows, sl]
            o_ref[rows, sl] = o * (zz * _sigmoid(zz))
    for h in range(HEADS):
        state_ref[h] = states[h]


def _dn_delta(q, k, v, proj, gates, gates_t, norm_w, batch, seq, cb=128):
    nb = seq // cb
    wide = pl.BlockSpec((cb, WIDTH), lambda b, i: (b * nb + i, 0))
    return pl.pallas_call(
        functools.partial(_dn_delta_kernel, chunks=cb // DN_CHUNK),
        grid=(batch, nb),
        in_specs=[wide, wide, wide,
                  pl.BlockSpec((cb, WIDTH), lambda b, i: (b * nb + i, 3)),
                  pl.BlockSpec((cb, LANES), lambda b, i: (b * nb + i, 0)),
                  pl.BlockSpec((SUBLANES, cb), lambda b, i: (0, b * nb + i)),
                  pl.BlockSpec((1, HEAD_DIM), lambda b, i: (0, 0))],
        out_specs=wide,
        out_shape=jax.ShapeDtypeStruct((batch * seq, WIDTH), f32),
        scratch_shapes=[pltpu.VMEM((HEADS, HEAD_DIM, HEAD_DIM), f32)],
        compiler_params=_params(("parallel", "arbitrary")),
        name="dn_delta",
    )(q, k, v, proj, gates, gates_t, norm_w.reshape(1, HEAD_DIM).astype(f32))


def _deltanet_mixer(x2, w_pad, conv_w, a_log, dt_bias, norm_w, batch, seq):
    proj = _matmul(x2, w_pad, tn=DN_PROJ_PAD // 3)
    q, k, v, gates = _dn_prep(proj, conv_w, a_log, dt_bias, seq)
    gates_t = gates[:, :SUBLANES].T
    return _dn_delta(q, k, v, proj, gates, gates_t, norm_w, batch, seq)


def _attn_kernel(q_ref, kp_ref, kc_ref, vp_ref, vc_ref, o_ref, l_ref, *, steps, dilation, heads):
    nbi = pl.program_id(1)
    blk = ATTN_BLOCK
    qi = lax.broadcasted_iota(i32, (blk, 2 * blk), 0)
    kj = lax.broadcasted_iota(i32, (blk, 2 * blk), 1)
    dist = qi + blk - kj
    mask = (dist >= 0) & (dist <= steps) & ((kj >= blk) | (nbi > 0))
    scale = HEAD_DIM ** -0.5
    for r in range(dilation):
        rows = pl.ds(r, blk, stride=dilation) if dilation > 1 else slice(None)
        for h in range(heads):
            sl = slice(h * HEAD_DIM, (h + 1) * HEAD_DIM)
            q = q_ref[rows, sl]
            kk = jnp.concatenate([kp_ref[rows, sl], kc_ref[rows, sl]], axis=0)
            vv = jnp.concatenate([vp_ref[rows, sl], vc_ref[rows, sl]], axis=0)
            sc = jnp.where(mask, _dot_nt(q, kk) * scale, -jnp.inf)
            mx = jnp.max(sc, -1, keepdims=True)
            p = jnp.exp(sc - mx)
            den = jnp.sum(p, -1, keepdims=True)
            o_ref[rows, sl] = _dot(p, vv) / den
            l_ref[rows, sl] = jnp.broadcast_to(mx + jnp.log(den), (blk, HEAD_DIM))


def _dilated_attention(q, k, v, group, window, dilation, batch, seq):
    rows = ATTN_BLOCK * dilation
    nb = seq // rows
    heads = HEADS if dilation == 1 else 1
    hblocks = HEADS // heads
    blk = (rows, heads * HEAD_DIM)

    def cur(b, i, j):
        return (b * nb + i, group * hblocks + j)

    def prev(b, i, j):
        return (b * nb + jnp.maximum(i - 1, 0), group * hblocks + j)

    out_spec = pl.BlockSpec(blk, lambda b, i, j: (b * nb + i, j))
    return pl.pallas_call(
        functools.partial(_attn_kernel, steps=window // dilation, dilation=dilation, heads=heads),
        grid=(batch, nb, hblocks),
        in_specs=[pl.BlockSpec(blk, cur), pl.BlockSpec(blk, prev), pl.BlockSpec(blk, cur),
                  pl.BlockSpec(blk, prev), pl.BlockSpec(blk, cur)],
        out_specs=[out_spec, out_spec],
        out_shape=[jax.ShapeDtypeStruct((batch * seq, WIDTH), f32)] * 2,
        compiler_params=_params(("parallel", "parallel", "parallel")),
        name=f"dilated_attn_d{dilation}",
    )(q, k, k, v, v)


def _combine_out_ln_kernel(o0, o1, o2, l0, l1, l2, w_ref, x_ref, g_ref, b_ref, out_ref):
    ls = [l0[...], l1[...], l2[...]]
    mx = jnp.maximum(jnp.maximum(ls[0], ls[1]), ls[2])
    es = [jnp.exp(l - mx) for l in ls]
    o = (es[0] * o0[...] + es[1] * o1[...] + es[2] * o2[...]) / (es[0] + es[1] + es[2])
    h = _dot(o, w_ref[...])
    out_ref[...] = _layer_norm(ALPHA * x_ref[...] + h, g_ref[...], b_ref[...])


def _combine_out_ln(outs, lses, w, x, g, b, tm=256):
    m, n = x.shape
    blk = pl.BlockSpec((tm, n), lambda i: (i, 0))
    row = pl.BlockSpec((1, n), lambda i: (0, 0))
    return pl.pallas_call(
        _combine_out_ln_kernel,
        grid=(m // tm,),
        in_specs=[blk] * 6 + [pl.BlockSpec((WIDTH, n), lambda i: (0, 0)), blk, row, row],
        out_specs=blk,
        out_shape=jax.ShapeDtypeStruct((m, n), f32),
        compiler_params=_params(("parallel",)),
        name="attn_combine_out_ln",
    )(*outs, *lses, w, x, g.reshape(1, n), b.reshape(1, n))


def _rope_lane_tables(seq):
    half = ROT_DIM // 2
    inv_freq = ROPE_THETA ** (-jnp.arange(half, dtype=f32) * 2.0 / ROT_DIM)
    ang = jnp.arange(seq, dtype=f32)[:, None] * inv_freq[None, :]
    cos, sin = jnp.cos(ang), jnp.sin(ang)
    ones = jnp.ones((seq, HEAD_DIM - ROT_DIM), f32)
    zeros = jnp.zeros((seq, HEAD_DIM - half), f32)
    c = jnp.concatenate([cos, cos, ones], axis=1)
    s_lo = jnp.concatenate([-sin, zeros], axis=1)
    s_hi = jnp.concatenate([jnp.zeros((seq, half), f32), sin, jnp.zeros((seq, HEAD_DIM - ROT_DIM), f32)], axis=1)
    return c, s_lo, s_hi


def _top_rows(s, order, payload, k):
    big = jnp.int32(2 ** 30)
    vals, pays = [], []
    for _ in range(k):
        m = jnp.max(s, axis=0, keepdims=True)
        pos = jnp.min(jnp.where(s == m, order, big), axis=0, keepdims=True)
        sel = order == pos
        pays.append(pos if payload is None else jnp.max(jnp.where(sel, payload, -1), axis=0, keepdims=True))
        vals.append(m)
        s = jnp.where(sel, -jnp.inf, s)
    return jnp.concatenate(vals, axis=0), jnp.concatenate(pays, axis=0)


def _pair_candidates(t1, t2, combine, fill):
    kt = PEER_TOPK
    row = lax.broadcasted_iota(i32, (SUBLANES, LANES), 0)
    bc = lambda t, r: jnp.broadcast_to(t[r:r + 1], (SUBLANES, LANES))
    lo1, hi1, lo2, hi2 = t1[:SUBLANES], t1[SUBLANES:], t2[:SUBLANES], t2[SUBLANES:]
    tiles = [combine(bc(t1, 0), lo2), combine(bc(t1, 0), hi2), combine(bc(t1, 1), lo2),
             jnp.where(row < kt // 3, combine(bc(t1, 2), lo2), fill),
             jnp.where(row < kt // 4, combine(bc(t1, 3), lo2), fill),
             combine(hi1, bc(t2, 0)),
             jnp.where(row >= 4, combine(lo1, bc(t2, 0)), fill),
             jnp.where(row >= 4, combine(lo1, bc(t2, 1)), jnp.where(row == 0, combine(bc(t1, 4), bc(t2, 2)), fill))]
    return jnp.concatenate(tiles, axis=0)


def _pair_rank():
    kt = PEER_TOPK
    row = lax.broadcasted_iota(i32, (SUBLANES, LANES), 0)
    unused = kt * kt + row
    tiles = [row, row + SUBLANES, kt + row,
             jnp.where(row < kt // 3, 2 * kt + row, unused),
             jnp.where(row < kt // 4, 3 * kt + row, unused + SUBLANES),
             (row + SUBLANES) * kt,
             jnp.where(row >= 4, row * kt, unused + 2 * SUBLANES),
             jnp.where(row >= 4, row * kt + 1, jnp.where(row == 0, 4 * kt + 2, unused + 3 * SUBLANES))]
    return jnp.concatenate(tiles, axis=0)


def _peer_route_kernel(qp_ref, keys_ref, idx_ref, gate_ref):
    kt = PEER_TOPK
    half = PEER_QDIM // 2
    keyid = lax.broadcasted_iota(i32, (PEER_NKEYS, LANES), 0)
    rank = _pair_rank()
    idx_rows, gate_rows = [], []
    for h in range(PEER_HEADS):
        tops = []
        for c in range(2):
            col = (h * 2 + c) * half
            s = _dot_nt(keys_ref[h * 2 + c], qp_ref[:, col:col + half])
            tops.append(_top_rows(s, keyid, None, kt))
        (v1, i1), (v2, i2) = tops
        cand = _pair_candidates(v1, v2, lambda a, b: a + b, -jnp.inf)
        cand_id = _pair_candidates(i1, i2, lambda a, b: a * PEER_NKEYS + b, -1)
        best, ids = _top_rows(cand, rank, cand_id, kt)
        e = jnp.exp(best - best[0:1])
        gate_rows.append(e / jnp.sum(e, axis=0, keepdims=True))
        idx_rows.append(ids)
    idx_ref[...] = jnp.concatenate(idx_rows, axis=0).astype(f32).T.astype(i32)
    gate_ref[...] = jnp.concatenate(gate_rows, axis=0).T


def _peer_route(qp, sub_keys):
    m = qp.shape[0]
    tb = LANES
    keys = sub_keys.reshape(PEER_HEADS * 2, PEER_NKEYS, PEER_QDIM // 2).astype(bf16)
    out = pl.BlockSpec((tb, PEER_SEL), lambda i: (i, 0))
    return pl.pallas_call(
        _peer_route_kernel,
        grid=(m // tb,),
        in_specs=[pl.BlockSpec((tb, qp.shape[1]), lambda i: (i, 0)),
                  pl.BlockSpec(keys.shape, lambda i: (0, 0, 0))],
        out_specs=[out, out],
        out_shape=[jax.ShapeDtypeStruct((m, PEER_SEL), i32), jax.ShapeDtypeStruct((m, PEER_SEL), f32)],
        compiler_params=_params(("parallel",)),
        name="peer_route",
    )(qp, keys)


SC_CORES = 2
SC_SUBCORES = 16
SC_LANES = 16
SC_WORKERS = SC_CORES * SC_SUBCORES
SC_CHUNKS = D_MODEL // SC_LANES
SC_WORDS = D_MODEL // 2
SC_WORD_CHUNKS = SC_WORDS // SC_LANES
SC_TOK_GROUP = 32
SC_ROWS = 64
SC_GATHERS_PER_TOK = PEER_SEL // SC_ROWS
SC_GATHERS = SC_TOK_GROUP * SC_GATHERS_PER_TOK
SC_ROW_GROUP = 8


def _pack_table(tab):
    e = tab.shape[0]
    bits = lax.bitcast_convert_type(tab.astype(bf16), jnp.uint16).astype(jnp.uint32)
    bits = bits.reshape(e, SC_WORD_CHUNKS, 2, SC_LANES)
    word = bits[:, :, 0, :] | (bits[:, :, 1, :] << 16)
    return lax.bitcast_convert_type(word.reshape(e, SC_WORDS), i32)


def _unpack_words(w):
    return lax.bitcast_convert_type(w << 16, f32), lax.bitcast_convert_type(w & (-65536), f32)


def _sc_mesh():
    return plsc.VectorSubcoreMesh(core_axis_name="c", subcore_axis_name="s")


def _sc_worker():
    return lax.axis_index("s") * SC_CORES + lax.axis_index("c")


def _sc_gather_loop(gather, compute):
    gather(0, 0).start()

    @pl.loop(0, SC_GATHERS, step=2)
    def _(q):
        gather(q + 1, 1).start()
        gather(q, 0).wait()
        compute(q, 0)

        @pl.when(q + 2 < SC_GATHERS)
        def _():
            gather(q + 2, 0).start()
        gather(q + 1, 1).wait()
        compute(q + 1, 1)


def _sc_scratch(stage_shape):
    return [pltpu.VMEM((SC_GATHERS, SC_ROWS), i32), pltpu.VMEM(stage_shape[0], f32), pltpu.VMEM(stage_shape[1], f32),
            pltpu.VMEM((SC_ROWS, SC_WORDS), i32), pltpu.VMEM((SC_ROWS, SC_WORDS), i32),
            pltpu.SemaphoreType.DMA, pltpu.SemaphoreType.DMA]


def _peer_udot(x, idx, u_tab):
    t_all = x.shape[0]
    tpw = t_all // SC_WORKERS
    idx2 = idx.reshape(t_all * SC_GATHERS_PER_TOK, SC_ROWS)

    @functools.partial(
        pl.kernel, mesh=_sc_mesh(), compiler_params=pltpu.CompilerParams(needs_layout_passes=False),
        out_type=jax.ShapeDtypeStruct((t_all, PEER_SEL), f32),
        scratch_types=_sc_scratch(((SC_TOK_GROUP, D_MODEL), (SC_TOK_GROUP, PEER_SEL))), name="peer_udot_sc")
    def k(x_hbm, idx_hbm, u_hbm, pre_hbm, idx_v, x_v, pre_v, rows0, rows1, sem0, sem1):
        bufs = ((rows0, sem0), (rows1, sem1))
        base = _sc_worker() * tpw
        lane = lax.iota(i32, SC_LANES)
        zero = jnp.zeros((SC_LANES,), f32)

        def gather(q, b):
            return pltpu.make_async_copy(u_hbm.at[idx_v.at[q]], bufs[b][0], bufs[b][1])

        def compute(q, b):
            rows = bufs[b][0]
            t = q // SC_GATHERS_PER_TOK
            col0 = (q % SC_GATHERS_PER_TOK) * SC_ROWS
            for hh in range(SC_ROWS // SC_LANES):
                vec = zero
                for rg in range(SC_LANES // SC_ROW_GROUP):
                    r0 = hh * SC_LANES + rg * SC_ROW_GROUP

                    def body(c, acc):
                        x_lo = x_v[t, pl.ds(c * 2 * SC_LANES, SC_LANES)]
                        x_hi = x_v[t, pl.ds(c * 2 * SC_LANES + SC_LANES, SC_LANES)]
                        out = []
                        for r in range(SC_ROW_GROUP):
                            lo, hi = _unpack_words(rows[r0 + r, pl.ds(c * SC_LANES, SC_LANES)])
                            out.append(acc[r] + lo * x_lo + hi * x_hi)
                        return tuple(out)

                    acc = lax.fori_loop(0, SC_WORD_CHUNKS, body, (zero,) * SC_ROW_GROUP, unroll=2)
                    for r in range(SC_ROW_GROUP):
                        vec = jnp.where(lane == rg * SC_ROW_GROUP + r, jnp.sum(acc[r]), vec)
                pre_v[t, pl.ds(col0 + hh * SC_LANES, SC_LANES)] = vec

        @pl.loop(0, tpw // SC_TOK_GROUP)
        def _(g):
            tok0 = base + g * SC_TOK_GROUP
            pltpu.sync_copy(idx_hbm.at[pl.ds(tok0 * SC_GATHERS_PER_TOK, SC_GATHERS)], idx_v)
            pltpu.sync_copy(x_hbm.at[pl.ds(tok0, SC_TOK_GROUP)], x_v)
            _sc_gather_loop(gather, compute)
            pltpu.sync_copy(pre_v, pre_hbm.at[pl.ds(tok0, SC_TOK_GROUP)])

    return k(x, idx2, u_tab)


def _peer_vacc(w, idx, v_tab):
    t_all = w.shape[0]
    tpw = t_all // SC_WORKERS
    idx2 = idx.reshape(t_all * SC_GATHERS_PER_TOK, SC_ROWS)
    half_w = SC_WORD_CHUNKS // 2

    @functools.partial(
        pl.kernel, mesh=_sc_mesh(), compiler_params=pltpu.CompilerParams(needs_layout_passes=False),
        out_type=jax.ShapeDtypeStruct((t_all, D_MODEL), f32),
        scratch_types=_sc_scratch(((SC_TOK_GROUP, PEER_SEL), (SC_TOK_GROUP, D_MODEL))), name="peer_vacc_sc")
    def k(w_hbm, idx_hbm, v_hbm, f_hbm, idx_v, w_v, f_v, rows0, rows1, sem0, sem1):
        bufs = ((rows0, sem0), (rows1, sem1))
        base = _sc_worker() * tpw
        zero = jnp.zeros((SC_LANES,), f32)

        def gather(q, b):
            return pltpu.make_async_copy(v_hbm.at[idx_v.at[q]], bufs[b][0], bufs[b][1])

        def compute(q, b):
            rows = bufs[b][0]
            t = q // SC_GATHERS_PER_TOK
            col0 = (q % SC_GATHERS_PER_TOK) * SC_ROWS
            tvec = jnp.full((SC_LANES,), t, i32)
            for half in range(2):
                def body(r, acc):
                    wr = plsc.load_gather(w_v, [tvec, jnp.full((SC_LANES,), col0 + r, i32)])
                    out = []
                    for c in range(half_w):
                        lo, hi = _unpack_words(rows[r, pl.ds((half * half_w + c) * SC_LANES, SC_LANES)])
                        out.append(acc[2 * c] + wr * lo)
                        out.append(acc[2 * c + 1] + wr * hi)
                    return tuple(out)

                acc = lax.fori_loop(0, SC_ROWS, body, (zero,) * (2 * half_w))
                for c in range(2 * half_w):
                    sl = pl.ds((half * 2 * half_w + c) * SC_LANES, SC_LANES)
                    f_v[t, sl] = f_v[t, sl] + acc[c]

        @pl.loop(0, tpw // SC_TOK_GROUP)
        def _(g):
            tok0 = base + g * SC_TOK_GROUP
            pltpu.sync_copy(idx_hbm.at[pl.ds(tok0 * SC_GATHERS_PER_TOK, SC_GATHERS)], idx_v)
            pltpu.sync_copy(w_hbm.at[pl.ds(tok0, SC_TOK_GROUP)], w_v)

            @pl.loop(0, SC_TOK_GROUP)
            def _(t):
                @pl.loop(0, SC_CHUNKS)
                def _(c):
                    f_v[t, pl.ds(c * SC_LANES, SC_LANES)] = zero

            _sc_gather_loop(gather, compute)
            pltpu.sync_copy(f_v, f_hbm.at[pl.ds(tok0, SC_TOK_GROUP)])

    return k(w, idx2, v_tab)


def _peer_act_kernel(pre_ref, gate_ref, w_ref):
    pre = pre_ref[...]
    w_ref[...] = gate_ref[...] * (0.5 * pre * (1.0 + lax.erf(pre * (2.0 ** -0.5))))


def _peer_act(pre, gate, tm=2048):
    m, n = pre.shape
    blk = pl.BlockSpec((tm, n), lambda i: (i, 0))
    return pl.pallas_call(
        _peer_act_kernel, grid=(m // tm,), in_specs=[blk, blk], out_specs=blk,
        out_shape=jax.ShapeDtypeStruct((m, n), f32), compiler_params=_params(("parallel",)), name="peer_act",
    )(pre, gate)


def _res_ln_kernel(x_ref, f_ref, g_ref, b_ref, o_ref):
    o_ref[...] = _layer_norm(ALPHA * x_ref[...] + f_ref[...], g_ref[...], b_ref[...])


def _res_ln(x, f, g, b, tm=512):
    m, n = x.shape
    blk = pl.BlockSpec((tm, n), lambda i: (i, 0))
    row = pl.BlockSpec((1, n), lambda i: (0, 0))
    return pl.pallas_call(
        _res_ln_kernel, grid=(m // tm,), in_specs=[blk, blk, row, row], out_specs=blk,
        out_shape=jax.ShapeDtypeStruct((m, n), f32), compiler_params=_params(("parallel",)), name="res_ln",
    )(x, f, g.reshape(1, n), b.reshape(1, n))


def _peer_expert_ln(x, idx, gate, u_tab, v_tab, g, b):
    pre = _peer_udot(x, idx, u_tab)
    w = _peer_act(pre, gate)
    f = _peer_vacc(w, idx, v_tab)
    return _res_ln(x, f, g, b)


PEER_SPLITS = 4


def _peer_ffn_ln(x2, w_query, sub_keys, u_pack, v_pack, g, b):
    outs, ids = [], []
    for xs in jnp.split(x2, PEER_SPLITS, axis=0):
        qp = _matmul(xs, w_query)
        idx, gate = _peer_route(qp, sub_keys)
        ids.append(idx)
        outs.append(_peer_expert_ln(xs, idx, gate, u_pack, v_pack, g, b))
    return jnp.concatenate(outs, axis=0), tuple(ids)


def kernel(x, dn_w_in, dn_conv, dn_a_log, dn_dt_bias, dn_norm_w, dn_w_out, shared_w_kv, attn_w_q, attn_w_out,
           peer_w_query, peer_sub_keys, peer_u, peer_v, ln_mix_g, ln_mix_b, ln_ffn_g, ln_ffn_b):
    batch, seq, d = x.shape
    n_a = DEPTH // 2
    tabs = _rope_lane_tables(seq)
    gw = N_GROUPS * WIDTH
    dn_w_pad = [jnp.pad(w, ((0, 0), (0, DN_PROJ_PAD - w.shape[1]))).astype(bf16) for w in dn_w_in]
    dn_w_out_b = dn_w_out.astype(bf16)
    w_k, w_v = shared_w_kv[:, :gw].astype(bf16), shared_w_kv[:, gw:].astype(bf16)
    w_q, w_o = attn_w_q.astype(bf16), attn_w_out.astype(bf16)
    peer_wq = peer_w_query.astype(bf16)
    u_pack = [_pack_table(t) for t in peer_u]
    v_pack = [_pack_table(t) for t in peer_v]

    def trunk(x2):
        k_sh = v_sh = routed = None
        for layer in range(DEPTH):
            if layer < n_a:
                o = _deltanet_mixer(x2, dn_w_pad[layer], dn_conv[layer], dn_a_log[layer], dn_dt_bias[layer],
                                    dn_norm_w[layer], 1, seq)
                x2 = _matmul_res_ln(o, dn_w_out_b[layer], x2, ln_mix_g[layer], ln_mix_b[layer])
            else:
                j = layer - n_a
                if j == 0:
                    k_sh = _matmul_rope(x2, w_k, tabs, seq)
                    v_sh = _matmul(x2, w_v, tn=1024)
                q = _matmul_rope(x2, w_q[j], tabs, seq)
                outs, lses = [], []
                for gi, (window, dilation) in enumerate(ATTN_GROUPS):
                    o, l = _dilated_attention(q, k_sh, v_sh, gi, window, dilation, 1, seq)
                    outs.append(o)
                    lses.append(l)
                x2 = _combine_out_ln(outs, lses, w_o[j], x2, ln_mix_g[layer], ln_mix_b[layer])
            x2, first_idx = _peer_ffn_ln(x2, peer_wq[layer], peer_sub_keys[layer], u_pack[layer], v_pack[layer],
                                         ln_ffn_g[layer], ln_ffn_b[layer])
            routed = first_idx if routed is None else routed
        return x2, routed

    outs, routed = [], None
    for b in range(batch):
        xb = x[b]
        if routed is not None:
            xb, _ = lax.optimization_barrier((xb, routed))
        yb, routed = trunk(xb)
        outs.append(yb)
    return jnp.stack(outs, axis=0)
```

```python
import functools
import math

import jax
import jax.numpy as jnp
from jax import lax
from jax.experimental import pallas as pl
from jax.experimental.pallas import tpu as pltpu
from jax.experimental.pallas import tpu_sc as plsc

f32 = jnp.float32
bf16 = jnp.bfloat16
i32 = jnp.int32

D_MODEL = 1024
DEPTH = 2
ALPHA = (2.0 * DEPTH) ** 0.25
LN_EPS = 1e-5

HEADS = 8
HEAD_DIM = 128
WIDTH = HEADS * HEAD_DIM
CONV_K = 4
DN_CHUNK = 64

ATTN_GROUPS = ((128, 1), (512, 4), (2048, 16))
N_GROUPS = len(ATTN_GROUPS)
ATTN_BLOCK = 128
ROT_DIM = HEAD_DIM // 4
ROPE_THETA = 500000.0

PEER_HEADS = 8
PEER_NKEYS = 128
PEER_TOPK = 16
PEER_QDIM = 256
PEER_SEL = PEER_HEADS * PEER_TOPK

LANES = 128
SUBLANES = 8
VMEM_LIMIT = 48 * 1024 * 1024


def _params(sem):
    return pltpu.CompilerParams(dimension_semantics=sem, vmem_limit_bytes=VMEM_LIMIT)


def _dot(a, b):
    return jnp.dot(a.astype(bf16), b.astype(bf16), preferred_element_type=f32)


def _dot_nt(a, b):
    return lax.dot_general(a.astype(bf16), b.astype(bf16), (((1,), (1,)), ((), ())), preferred_element_type=f32)


def _dot_tn(a, b):
    return lax.dot_general(a.astype(bf16), b.astype(bf16), (((0,), (0,)), ((), ())), preferred_element_type=f32)


def _sigmoid(x):
    return 1.0 / (1.0 + jnp.exp(-x))


def _layer_norm(y, g, b):
    mu = jnp.mean(y, -1, keepdims=True)
    yc = y - mu
    var = jnp.mean(yc * yc, -1, keepdims=True)
    return yc * lax.rsqrt(var + LN_EPS) * g + b


def _mm_kernel(a_ref, w_ref, o_ref):
    o_ref[...] = _dot(a_ref[...], w_ref[...])


def _matmul(a, w, tm=512, tn=None):
    m, k = a.shape
    n = w.shape[1]
    tn = tn or n
    return pl.pallas_call(
        _mm_kernel,
        grid=(m // tm, n // tn),
        in_specs=[pl.BlockSpec((tm, k), lambda i, j: (i, 0)), pl.BlockSpec((k, tn), lambda i, j: (0, j))],
        out_specs=pl.BlockSpec((tm, tn), lambda i, j: (i, j)),
        out_shape=jax.ShapeDtypeStruct((m, n), f32),
        compiler_params=_params(("parallel", "parallel")),
        name="matmul",
    )(a, w)


def _rope(y, c, s_lo, s_hi):
    return y * c + pltpu.roll(y, ROT_DIM // 2, axis=1) * s_hi + pltpu.roll(y, LANES - ROT_DIM // 2, axis=1) * s_lo


def _mm_rope_kernel(a_ref, w_ref, c_ref, slo_ref, shi_ref, o_ref, *, heads):
    y = _dot(a_ref[...], w_ref[...])
    c, s_lo, s_hi = c_ref[...], slo_ref[...], shi_ref[...]
    for h in range(heads):
        sl = slice(h * HEAD_DIM, (h + 1) * HEAD_DIM)
        o_ref[:, sl] = _rope(y[:, sl], c, s_lo, s_hi)


def _matmul_rope(a, w, tabs, seq, tm=512, tn=1024):
    m, k = a.shape
    n = w.shape[1]
    nsb = seq // tm
    tab_spec = pl.BlockSpec((tm, LANES), lambda i, j: (i % nsb, 0))
    return pl.pallas_call(
        functools.partial(_mm_rope_kernel, heads=tn // HEAD_DIM),
        grid=(m // tm, n // tn),
        in_specs=[pl.BlockSpec((tm, k), lambda i, j: (i, 0)), pl.BlockSpec((k, tn), lambda i, j: (0, j)),
                  tab_spec, tab_spec, tab_spec],
        out_specs=pl.BlockSpec((tm, tn), lambda i, j: (i, j)),
        out_shape=jax.ShapeDtypeStruct((m, n), f32),
        compiler_params=_params(("parallel", "parallel")),
        name="matmul_rope",
    )(a, w, *tabs)


def _mm_res_ln_kernel(a_ref, w_ref, x_ref, g_ref, b_ref, o_ref):
    h = _dot(a_ref[...], w_ref[...])
    o_ref[...] = _layer_norm(ALPHA * x_ref[...] + h, g_ref[...], b_ref[...])


def _matmul_res_ln(a, w, x, g, b, tm=512):
    m, k = a.shape
    n = w.shape[1]
    row = pl.BlockSpec((1, n), lambda i: (0, 0))
    return pl.pallas_call(
        _mm_res_ln_kernel,
        grid=(m // tm,),
        in_specs=[pl.BlockSpec((tm, k), lambda i: (i, 0)), pl.BlockSpec((k, n), lambda i: (0, 0)),
                  pl.BlockSpec((tm, n), lambda i: (i, 0)), row, row],
        out_specs=pl.BlockSpec((tm, n), lambda i: (i, 0)),
        out_shape=jax.ShapeDtypeStruct((m, n), f32),
        compiler_params=_params(("parallel",)),
        name="matmul_res_ln",
    )(a, w, x, g.reshape(1, n), b.reshape(1, n))


DN_PROJ_PAD = 4 * WIDTH + LANES


def _dn_prep_kernel(proj_ref, halo_ref, conv_ref, alog_ref, dtb_ref, q_ref, k_ref, v_ref, gate_ref, *, tm, nsb):
    i = pl.program_id(0)
    x = proj_ref[:, : 3 * WIDTH]
    halo = jnp.where(i % nsb == 0, 0.0, halo_ref[...])
    w = conv_ref[...]
    row8 = lax.broadcasted_iota(i32, (SUBLANES, 1), 0)
    acc = x * w[CONV_K - 1:CONV_K]
    for s in range(1, CONV_K):
        xs = pltpu.roll(x, s, axis=0)
        hs = pltpu.roll(halo, s, axis=0)
        first = jnp.where(row8 < s, hs, xs[:SUBLANES])
        xs = jnp.concatenate([first, xs[SUBLANES:]], axis=0)
        acc = acc + xs * w[CONV_K - 1 - s:CONV_K - s]
    qkv = acc * _sigmoid(acc)
    for h in range(HEADS):
        sl = slice(h * HEAD_DIM, (h + 1) * HEAD_DIM)
        qh = qkv[:, sl]
        q_ref[:, sl] = qh * lax.rsqrt(jnp.sum(qh * qh, -1, keepdims=True) + 1e-6) * (HEAD_DIM ** -0.5)
        kh = qkv[:, WIDTH + h * HEAD_DIM: WIDTH + (h + 1) * HEAD_DIM]
        k_ref[:, sl] = kh * lax.rsqrt(jnp.sum(kh * kh, -1, keepdims=True) + 1e-6)
    v_ref[...] = qkv[:, 2 * WIDTH:]
    ab = proj_ref[:, 4 * WIDTH:]
    z = ab + dtb_ref[...]
    softplus = jnp.maximum(z, 0.0) + jnp.log(1.0 + jnp.exp(-jnp.abs(z)))
    g = -jnp.exp(alog_ref[...]) * softplus
    row = lax.broadcasted_iota(i32, (tm, 1), 0) % DN_CHUNK
    s = 1
    while s < DN_CHUNK:
        g = g + jnp.where(row >= s, pltpu.roll(g, s, axis=0), 0.0)
        s *= 2
    lane = lax.broadcasted_iota(i32, (1, LANES), 1)
    gate_ref[...] = jnp.where(lane < HEADS, g, _sigmoid(ab))


def _dn_prep(proj, conv_w, a_log, dt_bias, seq, tm=256):
    m = proj.shape[0]
    nsb = seq // tm
    pad = LANES - HEADS
    alog = jnp.pad(a_log.astype(f32), (0, pad)).reshape(1, LANES)
    dtb = jnp.pad(dt_bias.astype(f32), (0, pad)).reshape(1, LANES)
    hb = tm // SUBLANES
    out_w = pl.BlockSpec((tm, WIDTH), lambda i: (i, 0))
    return pl.pallas_call(
        functools.partial(_dn_prep_kernel, tm=tm, nsb=nsb),
        grid=(m // tm,),
        in_specs=[pl.BlockSpec((tm, DN_PROJ_PAD), lambda i: (i, 0)),
                  pl.BlockSpec((SUBLANES, 3 * WIDTH), lambda i: (jnp.maximum(i * hb - 1, 0), 0)),
                  pl.BlockSpec((CONV_K, 3 * WIDTH), lambda i: (0, 0)),
                  pl.BlockSpec((1, LANES), lambda i: (0, 0)), pl.BlockSpec((1, LANES), lambda i: (0, 0))],
        out_specs=[out_w, out_w, out_w, pl.BlockSpec((tm, LANES), lambda i: (i, 0))],
        out_shape=[jax.ShapeDtypeStruct((m, WIDTH), f32)] * 3 + [jax.ShapeDtypeStruct((m, LANES), f32)],
        compiler_params=_params(("parallel",)),
        name="dn_prep",
    )(proj, proj, conv_w, alog, dtb)


def _dn_delta_kernel(q_ref, k_ref, v_ref, z_ref, gate_ref, gt_ref, nw_ref, o_ref, state_ref, *, chunks):
    @pl.when(pl.program_id(1) == 0)
    def _():
        state_ref[...] = jnp.zeros_like(state_ref)

    c = DN_CHUNK
    ri = lax.broadcasted_iota(i32, (c, c), 0)
    ci = lax.broadcasted_iota(i32, (c, c), 1)
    causal = ri >= ci
    strict = ri > ci
    eye = (ri == ci).astype(f32)
    nw = nw_ref[...]
    inst = [(ch, h) for ch in range(chunks) for h in range(HEADS)]
    pre = {}
    for ch, h in inst:
        rows = slice(ch * c, (ch + 1) * c)
        sl = slice(h * HEAD_DIM, (h + 1) * HEAD_DIM)
        q, k, v = q_ref[rows, sl], k_ref[rows, sl], v_ref[rows, sl]
        gcol = gate_ref[rows, h:h + 1]
        beta = gate_ref[rows, HEADS + h:HEADS + h + 1]
        grow = gt_ref[h:h + 1, rows]
        decay = jnp.where(causal, jnp.exp(jnp.where(causal, gcol - grow, 0.0)), 0.0)
        kb = k * beta
        egc = jnp.exp(gcol)
        glast = gcol[c - 1:c, :]
        qk = _dot_nt(jnp.concatenate([q, kb], axis=0), k)
        m = jnp.where(strict, qk[c:] * decay, 0.0)
        pre[ch, h] = dict(a_qk=qk[:c] * decay, inv=eye - m, p=m,
                          rhs=jnp.concatenate([v * beta, kb * egc], axis=1), qe=q * egc,
                          kd=k * jnp.exp(glast - gcol), eg=jnp.exp(glast))
    step = 2
    while step < c:
        for key in inst:
            pre[key]["p"] = _dot(pre[key]["p"], pre[key]["p"])
        for key in inst:
            pre[key]["inv"] = pre[key]["inv"] + _dot(pre[key]["inv"], pre[key]["p"])
        step *= 2
    for key in inst:
        pre[key]["uw"] = _dot(pre[key]["inv"], pre[key]["rhs"])
    states = [state_ref[h] for h in range(HEADS)]
    for ch in range(chunks):
        rows = slice(ch * c, (ch + 1) * c)
        for h in range(HEADS):
            sl = slice(h * HEAD_DIM, (h + 1) * HEAD_DIM)
            d = pre[ch, h]
            u, w = d["uw"][:, :HEAD_DIM], d["uw"][:, HEAD_DIM:]
            ws = _dot(jnp.concatenate([w, d["qe"]], axis=0), states[h])
            v_new = u - ws[:c]
            o = ws[c:] + _dot(d["a_qk"], v_new)
            states[h] = states[h] * d["eg"] + _dot_tn(d["kd"], v_new)
            o = o * lax.rsqrt(jnp.mean(o * o, -1, keepdims=True) + 1e-6) * nw
            zz = z_ref[rows, sl]
            o_ref[rows, sl] = o * (zz * _sigmoid(zz))
    for h in range(HEADS):
        state_ref[h] = states[h]


def _dn_delta(q, k, v, proj, gates, gates_t, norm_w, batch, seq, cb=128):
    nb = seq // cb
    wide = pl.BlockSpec((cb, WIDTH), lambda b, i: (b * nb + i, 0))
    return pl.pallas_call(
        functools.partial(_dn_delta_kernel, chunks=cb // DN_CHUNK),
        grid=(batch, nb),
        in_specs=[wide, wide, wide,
                  pl.BlockSpec((cb, WIDTH), lambda b, i: (b * nb + i, 3)),
                  pl.BlockSpec((cb, LANES), lambda b, i: (b * nb + i, 0)),
                  pl.BlockSpec((SUBLANES, cb), lambda b, i: (0, b * nb + i)),
                  pl.BlockSpec((1, HEAD_DIM), lambda b, i: (0, 0))],
        out_specs=wide,
        out_shape=jax.ShapeDtypeStruct((batch * seq, WIDTH), f32),
        scratch_shapes=[pltpu.VMEM((HEADS, HEAD_DIM, HEAD_DIM), f32)],
        compiler_params=_params(("parallel", "arbitrary")),
        name="dn_delta",
    )(q, k, v, proj, gates, gates_t, norm_w.reshape(1, HEAD_DIM).astype(f32))


def _deltanet_mixer(x2, w_pad, conv_w, a_log, dt_bias, norm_w, batch, seq):
    proj = _matmul(x2, w_pad, tn=DN_PROJ_PAD // 3)
    q, k, v, gates = _dn_prep(proj, conv_w, a_log, dt_bias, seq)
    gates_t = gates[:, :SUBLANES].T
    return _dn_delta(q, k, v, proj, gates, gates_t, norm_w, batch, seq)


def _attn_kernel(q_ref, kp_ref, kc_ref, vp_ref, vc_ref, o_ref, l_ref, *, steps, dilation, heads):
    nbi = pl.program_id(1)
    blk = ATTN_BLOCK
    qi = lax.broadcasted_iota(i32, (blk, 2 * blk), 0)
    kj = lax.broadcasted_iota(i32, (blk, 2 * blk), 1)
    dist = qi + blk - kj
    mask = (dist >= 0) & (dist <= steps) & ((kj >= blk) | (nbi > 0))
    scale = HEAD_DIM ** -0.5
    for r in range(dilation):
        rows = pl.ds(r, blk, stride=dilation) if dilation > 1 else slice(None)
        for h in range(heads):
            sl = slice(h * HEAD_DIM, (h + 1) * HEAD_DIM)
            q = q_ref[rows, sl]
            kk = jnp.concatenate([kp_ref[rows, sl], kc_ref[rows, sl]], axis=0)
            vv = jnp.concatenate([vp_ref[rows, sl], vc_ref[rows, sl]], axis=0)
            sc = jnp.where(mask, _dot_nt(q, kk) * scale, -jnp.inf)
            mx = jnp.max(sc, -1, keepdims=True)
            p = jnp.exp(sc - mx)
            den = jnp.sum(p, -1, keepdims=True)
            o_ref[rows, sl] = _dot(p, vv) / den
            l_ref[rows, sl] = jnp.broadcast_to(mx + jnp.log(den), (blk, HEAD_DIM))


def _dilated_attention(q, k, v, group, window, dilation, batch, seq):
    rows = ATTN_BLOCK * dilation
    nb = seq // rows
    heads = HEADS if dilation == 1 else 1
    hblocks = HEADS // heads
    blk = (rows, heads * HEAD_DIM)

    def cur(b, i, j):
        return (b * nb + i, group * hblocks + j)

    def prev(b, i, j):
        return (b * nb + jnp.maximum(i - 1, 0), group * hblocks + j)

    out_spec = pl.BlockSpec(blk, lambda b, i, j: (b * nb + i, j))
    return pl.pallas_call(
        functools.partial(_attn_kernel, steps=window // dilation, dilation=dilation, heads=heads),
        grid=(batch, nb, hblocks),
        in_specs=[pl.BlockSpec(blk, cur), pl.BlockSpec(blk, prev), pl.BlockSpec(blk, cur),
                  pl.BlockSpec(blk, prev), pl.BlockSpec(blk, cur)],
        out_specs=[out_spec, out_spec],
        out_shape=[jax.ShapeDtypeStruct((batch * seq, WIDTH), f32)] * 2,
        compiler_params=_params(("parallel", "parallel", "parallel")),
        name=f"dilated_attn_d{dilation}",
    )(q, k, k, v, v)


def _combine_out_ln_kernel(o0, o1, o2, l0, l1, l2, w_ref, x_ref, g_ref, b_ref, out_ref):
    ls = [l0[...], l1[...], l2[...]]
    mx = jnp.maximum(jnp.maximum(ls[0], ls[1]), ls[2])
    es = [jnp.exp(l - mx) for l in ls]
    o = (es[0] * o0[...] + es[1] * o1[...] + es[2] * o2[...]) / (es[0] + es[1] + es[2])
    h = _dot(o, w_ref[...])
    out_ref[...] = _layer_norm(ALPHA * x_ref[...] + h, g_ref[...], b_ref[...])


def _combine_out_ln(outs, lses, w, x, g, b, tm=256):
    m, n = x.shape
    blk = pl.BlockSpec((tm, n), lambda i: (i, 0))
    row = pl.BlockSpec((1, n), lambda i: (0, 0))
    return pl.pallas_call(
        _combine_out_ln_kernel,
        grid=(m // tm,),
        in_specs=[blk] * 6 + [pl.BlockSpec((WIDTH, n), lambda i: (0, 0)), blk, row, row],
        out_specs=blk,
        out_shape=jax.ShapeDtypeStruct((m, n), f32),
        compiler_params=_params(("parallel",)),
        name="attn_combine_out_ln",
    )(*outs, *lses, w, x, g.reshape(1, n), b.reshape(1, n))


def _rope_lane_tables(seq):
    half = ROT_DIM // 2
    inv_freq = ROPE_THETA ** (-jnp.arange(half, dtype=f32) * 2.0 / ROT_DIM)
    ang = jnp.arange(seq, dtype=f32)[:, None] * inv_freq[None, :]
    cos, sin = jnp.cos(ang), jnp.sin(ang)
    ones = jnp.ones((seq, HEAD_DIM - ROT_DIM), f32)
    zeros = jnp.zeros((seq, HEAD_DIM - half), f32)
    c = jnp.concatenate([cos, cos, ones], axis=1)
    s_lo = jnp.concatenate([-sin, zeros], axis=1)
    s_hi = jnp.concatenate([jnp.zeros((seq, half), f32), sin, jnp.zeros((seq, HEAD_DIM - ROT_DIM), f32)], axis=1)
    return c, s_lo, s_hi


def _top_rows(s, order, payload, k):
    big = jnp.int32(2 ** 30)
    vals, pays = [], []
    for _ in range(k):
        m = jnp.max(s, axis=0, keepdims=True)
        pos = jnp.min(jnp.where(s == m, order, big), axis=0, keepdims=True)
        sel = order == pos
        pays.append(pos if payload is None else jnp.max(jnp.where(sel, payload, -1), axis=0, keepdims=True))
        vals.append(m)
        s = jnp.where(sel, -jnp.inf, s)
    return jnp.concatenate(vals, axis=0), jnp.concatenate(pays, axis=0)


def _pair_candidates(t1, t2, combine, fill):
    kt = PEER_TOPK
    row = lax.broadcasted_iota(i32, (SUBLANES, LANES), 0)
    bc = lambda t, r: jnp.broadcast_to(t[r:r + 1], (SUBLANES, LANES))
    lo1, hi1, lo2, hi2 = t1[:SUBLANES], t1[SUBLANES:], t2[:SUBLANES], t2[SUBLANES:]
    tiles = [combine(bc(t1, 0), lo2), combine(bc(t1, 0), hi2), combine(bc(t1, 1), lo2),
             jnp.where(row < kt // 3, combine(bc(t1, 2), lo2), fill),
             jnp.where(row < kt // 4, combine(bc(t1, 3), lo2), fill),
             combine(hi1, bc(t2, 0)),
             jnp.where(row >= 4, combine(lo1, bc(t2, 0)), fill),
             jnp.where(row >= 4, combine(lo1, bc(t2, 1)), jnp.where(row == 0, combine(bc(t1, 4), bc(t2, 2)), fill))]
    return jnp.concatenate(tiles, axis=0)


def _pair_rank():
    kt = PEER_TOPK
    row = lax.broadcasted_iota(i32, (SUBLANES, LANES), 0)
    unused = kt * kt + row
    tiles = [row, row + SUBLANES, kt + row,
             jnp.where(row < kt // 3, 2 * kt + row, unused),
             jnp.where(row < kt // 4, 3 * kt + row, unused + SUBLANES),
             (row + SUBLANES) * kt,
             jnp.where(row >= 4, row * kt, unused + 2 * SUBLANES),
             jnp.where(row >= 4, row * kt + 1, jnp.where(row == 0, 4 * kt + 2, unused + 3 * SUBLANES))]
    return jnp.concatenate(tiles, axis=0)


def _peer_route_kernel(qp_ref, keys_ref, idx_ref, gate_ref):
    kt = PEER_TOPK
    half = PEER_QDIM // 2
    keyid = lax.broadcasted_iota(i32, (PEER_NKEYS, LANES), 0)
    rank = _pair_rank()
    idx_rows, gate_rows = [], []
    for h in range(PEER_HEADS):
        tops = []
        for c in range(2):
            col = (h * 2 + c) * half
            s = _dot_nt(keys_ref[h * 2 + c], qp_ref[:, col:col + half])
            tops.append(_top_rows(s, keyid, None, kt))
        (v1, i1), (v2, i2) = tops
        cand = _pair_candidates(v1, v2, lambda a, b: a + b, -jnp.inf)
        cand_id = _pair_candidates(i1, i2, lambda a, b: a * PEER_NKEYS + b, -1)
        best, ids = _top_rows(cand, rank, cand_id, kt)
        e = jnp.exp(best - best[0:1])
        gate_rows.append(e / jnp.sum(e, axis=0, keepdims=True))
        idx_rows.append(ids)
    idx_ref[...] = jnp.concatenate(idx_rows, axis=0).astype(f32).T.astype(i32)
    gate_ref[...] = jnp.concatenate(gate_rows, axis=0).T


def _peer_route(qp, sub_keys):
    m = qp.shape[0]
    tb = LANES
    keys = sub_keys.reshape(PEER_HEADS * 2, PEER_NKEYS, PEER_QDIM // 2).astype(bf16)
    out = pl.BlockSpec((tb, PEER_SEL), lambda i: (i, 0))
    return pl.pallas_call(
        _peer_route_kernel,
        grid=(m // tb,),
        in_specs=[pl.BlockSpec((tb, qp.shape[1]), lambda i: (i, 0)),
                  pl.BlockSpec(keys.shape, lambda i: (0, 0, 0))],
        out_specs=[out, out],
        out_shape=[jax.ShapeDtypeStruct((m, PEER_SEL), i32), jax.ShapeDtypeStruct((m, PEER_SEL), f32)],
        compiler_params=_params(("parallel",)),
        name="peer_route",
    )(qp, keys)


SC_CORES = 2
SC_SUBCORES = 16
SC_LANES = 16
SC_WORKERS = SC_CORES * SC_SUBCORES
SC_CHUNKS = D_MODEL // SC_LANES
SC_WORDS = D_MODEL // 2
SC_WORD_CHUNKS = SC_WORDS // SC_LANES
SC_TOK_GROUP = 16
SC_ROWS = 64
SC_GATHERS_PER_TOK = PEER_SEL // SC_ROWS
SC_GATHERS = SC_TOK_GROUP * SC_GATHERS_PER_TOK
SC_ROW_GROUP = 8


def _pack_table(tab):
    e = tab.shape[0]
    bits = lax.bitcast_convert_type(tab.astype(bf16), jnp.uint16).astype(jnp.uint32)
    bits = bits.reshape(e, SC_WORD_CHUNKS, 2, SC_LANES)
    word = bits[:, :, 0, :] | (bits[:, :, 1, :] << 16)
    return lax.bitcast_convert_type(word.reshape(e, SC_WORDS), i32)


def _unpack_words(w):
    return lax.bitcast_convert_type(w << 16, f32), lax.bitcast_convert_type(w & (-65536), f32)


def _sc_mesh():
    return plsc.VectorSubcoreMesh(core_axis_name="c", subcore_axis_name="s")


def _sc_worker():
    return lax.axis_index("s") * SC_CORES + lax.axis_index("c")


def _sc_gather_loop(gather, compute):
    gather(0, 0).start()

    @pl.loop(0, SC_GATHERS, step=2)
    def _(q):
        gather(q + 1, 1).start()
        gather(q, 0).wait()
        compute(q, 0)

        @pl.when(q + 2 < SC_GATHERS)
        def _():
            gather(q + 2, 0).start()
        gather(q + 1, 1).wait()
        compute(q + 1, 1)


def _sc_scratch(stage_shape):
    return [pltpu.VMEM((SC_GATHERS, SC_ROWS), i32), pltpu.VMEM(stage_shape[0], f32), pltpu.VMEM(stage_shape[1], f32),
            pltpu.VMEM((SC_ROWS, SC_WORDS), i32), pltpu.VMEM((SC_ROWS, SC_WORDS), i32),
            pltpu.SemaphoreType.DMA, pltpu.SemaphoreType.DMA]


def _peer_udot(x, idx, u_tab):
    t_all = x.shape[0]
    tpw = t_all // SC_WORKERS
    idx2 = idx.reshape(t_all * SC_GATHERS_PER_TOK, SC_ROWS)

    @functools.partial(
        pl.kernel, mesh=_sc_mesh(), compiler_params=pltpu.CompilerParams(needs_layout_passes=False),
        out_type=jax.ShapeDtypeStruct((t_all, PEER_SEL), f32),
        scratch_types=_sc_scratch(((SC_TOK_GROUP, D_MODEL), (SC_TOK_GROUP, PEER_SEL))), name="peer_udot_sc")
    def k(x_hbm, idx_hbm, u_hbm, pre_hbm, idx_v, x_v, pre_v, rows0, rows1, sem0, sem1):
        bufs = ((rows0, sem0), (rows1, sem1))
        base = _sc_worker() * tpw
        lane = lax.iota(i32, SC_LANES)
        zero = jnp.zeros((SC_LANES,), f32)

        def gather(q, b):
            return pltpu.make_async_copy(u_hbm.at[idx_v.at[q]], bufs[b][0], bufs[b][1])

        def compute(q, b):
            rows = bufs[b][0]
            t = q // SC_GATHERS_PER_TOK
            col0 = (q % SC_GATHERS_PER_TOK) * SC_ROWS
            for hh in range(SC_ROWS // SC_LANES):
                vec = zero
                for rg in range(SC_LANES // SC_ROW_GROUP):
                    r0 = hh * SC_LANES + rg * SC_ROW_GROUP

                    def body(c, acc):
                        x_lo = x_v[t, pl.ds(c * 2 * SC_LANES, SC_LANES)]
                        x_hi = x_v[t, pl.ds(c * 2 * SC_LANES + SC_LANES, SC_LANES)]
                        out = []
                        for r in range(SC_ROW_GROUP):
                            lo, hi = _unpack_words(rows[r0 + r, pl.ds(c * SC_LANES, SC_LANES)])
                            out.append(acc[r] + lo * x_lo + hi * x_hi)
                        return tuple(out)

                    acc = lax.fori_loop(0, SC_WORD_CHUNKS, body, (zero,) * SC_ROW_GROUP, unroll=2)
                    for r in range(SC_ROW_GROUP):
                        vec = jnp.where(lane == rg * SC_ROW_GROUP + r, jnp.sum(acc[r]), vec)
                pre_v[t, pl.ds(col0 + hh * SC_LANES, SC_LANES)] = vec

        @pl.loop(0, tpw // SC_TOK_GROUP)
        def _(g):
            tok0 = base + g * SC_TOK_GROUP
            pltpu.sync_copy(idx_hbm.at[pl.ds(tok0 * SC_GATHERS_PER_TOK, SC_GATHERS)], idx_v)
            pltpu.sync_copy(x_hbm.at[pl.ds(tok0, SC_TOK_GROUP)], x_v)
            _sc_gather_loop(gather, compute)
            pltpu.sync_copy(pre_v, pre_hbm.at[pl.ds(tok0, SC_TOK_GROUP)])

    return k(x, idx2, u_tab)


def _peer_vacc(w, idx, v_tab):
    t_all = w.shape[0]
    tpw = t_all // SC_WORKERS
    idx2 = idx.reshape(t_all * SC_GATHERS_PER_TOK, SC_ROWS)
    half_w = SC_WORD_CHUNKS // 2

    @functools.partial(
        pl.kernel, mesh=_sc_mesh(), compiler_params=pltpu.CompilerParams(needs_layout_passes=False),
        out_type=jax.ShapeDtypeStruct((t_all, D_MODEL), f32),
        scratch_types=_sc_scratch(((SC_TOK_GROUP, PEER_SEL), (SC_TOK_GROUP, D_MODEL))), name="peer_vacc_sc")
    def k(w_hbm, idx_hbm, v_hbm, f_hbm, idx_v, w_v, f_v, rows0, rows1, sem0, sem1):
        bufs = ((rows0, sem0), (rows1, sem1))
        base = _sc_worker() * tpw
        zero = jnp.zeros((SC_LANES,), f32)

        def gather(q, b):
            return pltpu.make_async_copy(v_hbm.at[idx_v.at[q]], bufs[b][0], bufs[b][1])

        def compute(q, b):
            rows = bufs[b][0]
            t = q // SC_GATHERS_PER_TOK
            col0 = (q % SC_GATHERS_PER_TOK) * SC_ROWS
            tvec = jnp.full((SC_LANES,), t, i32)
            for half in range(2):
                def body(r, acc):
                    wr = plsc.load_gather(w_v, [tvec, jnp.full((SC_LANES,), col0 + r, i32)])
                    out = []
                    for c in range(half_w):
                        lo, hi = _unpack_words(rows[r, pl.ds((half * half_w + c) * SC_LANES, SC_LANES)])
                        out.append(acc[2 * c] + wr * lo)
                        out.append(acc[2 * c + 1] + wr * hi)
                    return tuple(out)

                acc = lax.fori_loop(0, SC_ROWS, body, (zero,) * (2 * half_w))
                for c in range(2 * half_w):
                    sl = pl.ds((half * 2 * half_w + c) * SC_LANES, SC_LANES)
                    f_v[t, sl] = f_v[t, sl] + acc[c]

        @pl.loop(0, tpw // SC_TOK_GROUP)
        def _(g):
            tok0 = base + g * SC_TOK_GROUP
            pltpu.sync_copy(idx_hbm.at[pl.ds(tok0 * SC_GATHERS_PER_TOK, SC_GATHERS)], idx_v)
            pltpu.sync_copy(w_hbm.at[pl.ds(tok0, SC_TOK_GROUP)], w_v)

            @pl.loop(0, SC_TOK_GROUP)
            def _(t):
                @pl.loop(0, SC_CHUNKS)
                def _(c):
                    f_v[t, pl.ds(c * SC_LANES, SC_LANES)] = zero

            _sc_gather_loop(gather, compute)
            pltpu.sync_copy(f_v, f_hbm.at[pl.ds(tok0, SC_TOK_GROUP)])

    return k(w, idx2, v_tab)


PEER_TC_TOKENS = 512
PEER_TC_BLOCK = 8
PEER_TC_VMEM_LIMIT = 56 * 1024 * 1024


def _split3(a):
    a1 = a.astype(bf16)
    r = a - a1.astype(f32)
    a2 = r.astype(bf16)
    a3 = (r - a2.astype(f32)).astype(bf16)
    return a1, a2, a3


def _rows3(pieces, t):
    return jnp.concatenate([p[t:t + 1] for p in pieces], axis=0)


def _unpack_tile(w):
    lo = lax.bitcast_convert_type(w << 16, f32).astype(bf16)
    hi = lax.bitcast_convert_type(w & (-65536), f32).astype(bf16)
    return lo, hi


def _load_table_once(tab_hbm, tab_v, sem):
    @pl.when(pl.program_id(0) == 0)
    def _():
        cp = pltpu.make_async_copy(tab_hbm, tab_v, sem)
        cp.start()
        cp.wait()


def _copy_rows(idx_ref, t, tab_v, buf):
    def body(r, c):
        buf[pl.ds(r, 1), :] = tab_v[pl.ds(idx_ref[t, r], 1), :]
        return c
    lax.fori_loop(0, PEER_SEL, body, 0, unroll=8)


def _udot_tc_kernel(idx_ref, xlo_ref, xhi_ref, gate_ref, tab_hbm, w_ref, tab_v, buf0, buf1, sem):
    _load_table_once(tab_hbm, tab_v, sem)
    xl, xh = _split3(xlo_ref[...]), _split3(xhi_ref[...])
    rowid = lax.broadcasted_iota(i32, (PEER_TC_BLOCK, 1), 0)
    pre = jnp.zeros((PEER_TC_BLOCK, PEER_SEL), f32)
    nt = (((1,), (1,)), ((), ()))
    for t in range(PEER_TC_BLOCK):
        buf = (buf0, buf1)[t % 2]
        _copy_rows(idx_ref, t, tab_v, buf)
        lo, hi = _unpack_tile(buf[...])
        s = (lax.dot_general(_rows3(xl, t), lo, nt, preferred_element_type=f32)
             + lax.dot_general(_rows3(xh, t), hi, nt, preferred_element_type=f32))
        pre = jnp.where(rowid == t, jnp.sum(s, axis=0, keepdims=True), pre)
    w_ref[...] = gate_ref[...] * (0.5 * pre * (1.0 + lax.erf(pre * (2.0 ** -0.5))))


def _vacc_tc_kernel(idx_ref, w_ref, tab_hbm, flo_ref, fhi_ref, tab_v, buf0, buf1, sem):
    _load_table_once(tab_hbm, tab_v, sem)
    ws = _split3(w_ref[...])
    for t in range(PEER_TC_BLOCK):
        buf = (buf0, buf1)[t % 2]
        _copy_rows(idx_ref, t, tab_v, buf)
        lo, hi = _unpack_tile(buf[...])
        lhs = _rows3(ws, t)
        flo_ref[t:t + 1, :] = jnp.sum(jnp.dot(lhs, lo, preferred_element_type=f32), axis=0, keepdims=True)
        fhi_ref[t:t + 1, :] = jnp.sum(jnp.dot(lhs, hi, preferred_element_type=f32), axis=0, keepdims=True)


def _peer_expert_tc(x, idx, gate, u_pack, v_pack):
    n = x.shape[0]
    tb = PEER_TC_BLOCK
    xp = x.reshape(n, SC_WORD_CHUNKS, 2, SC_LANES)
    x_lo, x_hi = xp[:, :, 0, :].reshape(n, SC_WORDS), xp[:, :, 1, :].reshape(n, SC_WORDS)
    smem = pl.BlockSpec((tb, PEER_SEL), lambda i: (i, 0), memory_space=pltpu.SMEM)
    sel = pl.BlockSpec((tb, PEER_SEL), lambda i: (i, 0))
    half = pl.BlockSpec((tb, SC_WORDS), lambda i: (i, 0))
    whole = pl.BlockSpec(memory_space=pl.ANY)
    scratch = [pltpu.VMEM(u_pack.shape, i32), pltpu.VMEM((PEER_SEL, SC_WORDS), i32),
               pltpu.VMEM((PEER_SEL, SC_WORDS), i32), pltpu.SemaphoreType.DMA]
    params = pltpu.CompilerParams(dimension_semantics=("arbitrary",), vmem_limit_bytes=PEER_TC_VMEM_LIMIT)
    w = pl.pallas_call(
        _udot_tc_kernel, grid=(n // tb,), in_specs=[smem, half, half, sel, whole], out_specs=sel,
        out_shape=jax.ShapeDtypeStruct((n, PEER_SEL), f32), scratch_shapes=scratch, compiler_params=params,
        name="peer_udot_tc",
    )(idx, x_lo, x_hi, gate, u_pack)
    f_lo, f_hi = pl.pallas_call(
        _vacc_tc_kernel, grid=(n // tb,), in_specs=[smem, sel, whole], out_specs=[half, half],
        out_shape=[jax.ShapeDtypeStruct((n, SC_WORDS), f32)] * 2, scratch_shapes=scratch, compiler_params=params,
        name="peer_vacc_tc",
    )(idx, w, v_pack)
    shp = (n, SC_WORD_CHUNKS, SC_LANES)
    return jnp.stack([f_lo.reshape(shp), f_hi.reshape(shp)], axis=2).reshape(n, D_MODEL)


def _peer_act_kernel(pre_ref, gate_ref, w_ref):
    pre = pre_ref[...]
    w_ref[...] = gate_ref[...] * (0.5 * pre * (1.0 + lax.erf(pre * (2.0 ** -0.5))))


def _peer_act(pre, gate, tm=2048):
    m, n = pre.shape
    blk = pl.BlockSpec((tm, n), lambda i: (i, 0))
    return pl.pallas_call(
        _peer_act_kernel, grid=(m // tm,), in_specs=[blk, blk], out_specs=blk,
        out_shape=jax.ShapeDtypeStruct((m, n), f32), compiler_params=_params(("parallel",)), name="peer_act",
    )(pre, gate)


def _res_ln_kernel(x_ref, f_ref, g_ref, b_ref, o_ref):
    o_ref[...] = _layer_norm(ALPHA * x_ref[...] + f_ref[...], g_ref[...], b_ref[...])


def _res_ln(x, f, g, b, tm=512):
    m, n = x.shape
    blk = pl.BlockSpec((tm, n), lambda i: (i, 0))
    row = pl.BlockSpec((1, n), lambda i: (0, 0))
    return pl.pallas_call(
        _res_ln_kernel, grid=(m // tm,), in_specs=[blk, blk, row, row], out_specs=blk,
        out_shape=jax.ShapeDtypeStruct((m, n), f32), compiler_params=_params(("parallel",)), name="res_ln",
    )(x, f, g.reshape(1, n), b.reshape(1, n))


def _peer_expert_ln(x, idx, gate, u_tab, v_tab, g, b):
    n = PEER_TC_TOKENS
    f_tc = _peer_expert_tc(x[:n], idx[:n], gate[:n], u_tab, v_tab)
    pre = _peer_udot(x[n:], idx[n:], u_tab)
    w = _peer_act(pre, gate[n:], tm=512)
    f_sc = _peer_vacc(w, idx[n:], v_tab)
    return _res_ln(x, jnp.concatenate([f_tc, f_sc], axis=0), g, b)


PEER_SPLITS = 2


def _peer_ffn_ln(x2, w_query, sub_keys, u_pack, v_pack, g, b):
    outs = []
    for xs in jnp.split(x2, PEER_SPLITS, axis=0):
        qp = _matmul(xs, w_query)
        idx, gate = _peer_route(qp, sub_keys)
        outs.append(_peer_expert_ln(xs, idx, gate, u_pack, v_pack, g, b))
    return jnp.concatenate(outs, axis=0)


def kernel(x, dn_w_in, dn_conv, dn_a_log, dn_dt_bias, dn_norm_w, dn_w_out, shared_w_kv, attn_w_q, attn_w_out,
           peer_w_query, peer_sub_keys, peer_u, peer_v, ln_mix_g, ln_mix_b, ln_ffn_g, ln_ffn_b):
    batch, seq, d = x.shape
    n_a = DEPTH // 2
    tabs = _rope_lane_tables(seq)
    gw = N_GROUPS * WIDTH
    dn_w_pad = [jnp.pad(w, ((0, 0), (0, DN_PROJ_PAD - w.shape[1]))).astype(bf16) for w in dn_w_in]
    dn_w_out_b = dn_w_out.astype(bf16)
    w_k, w_v = shared_w_kv[:, :gw].astype(bf16), shared_w_kv[:, gw:].astype(bf16)
    w_q, w_o = attn_w_q.astype(bf16), attn_w_out.astype(bf16)
    peer_wq = peer_w_query.astype(bf16)
    u_pack = [_pack_table(t) for t in peer_u]
    v_pack = [_pack_table(t) for t in peer_v]

    def trunk(x2):
        k_sh = v_sh = None
        for layer in range(DEPTH):
            if layer < n_a:
                o = _deltanet_mixer(x2, dn_w_pad[layer], dn_conv[layer], dn_a_log[layer], dn_dt_bias[layer],
                                    dn_norm_w[layer], 1, seq)
                x2 = _matmul_res_ln(o, dn_w_out_b[layer], x2, ln_mix_g[layer], ln_mix_b[layer])
            else:
                j = layer - n_a
                if j == 0:
                    k_sh = _matmul_rope(x2, w_k, tabs, seq)
                    v_sh = _matmul(x2, w_v, tn=1024)
                q = _matmul_rope(x2, w_q[j], tabs, seq)
                outs, lses = [], []
                for gi, (window, dilation) in enumerate(ATTN_GROUPS):
                    o, l = _dilated_attention(q, k_sh, v_sh, gi, window, dilation, 1, seq)
                    outs.append(o)
                    lses.append(l)
                x2 = _combine_out_ln(outs, lses, w_o[j], x2, ln_mix_g[layer], ln_mix_b[layer])
            x2 = _peer_ffn_ln(x2, peer_wq[layer], peer_sub_keys[layer], u_pack[layer], v_pack[layer],
                              ln_ffn_g[layer], ln_ffn_b[layer])
        return x2

    return jnp.stack([trunk(x[b]) for b in range(batch)], axis=0)
```

```python
import functools

import jax
import jax.numpy as jnp
from jax import lax
from jax.experimental import pallas as pl
from jax.experimental.pallas import tpu as pltpu
from jax.experimental.pallas import tpu_sc as plsc

f32 = jnp.float32
bf16 = jnp.bfloat16
i32 = jnp.int32

D_MODEL = 1024
DEPTH = 2
ALPHA = (2.0 * DEPTH) ** 0.25
LN_EPS = 1e-5

HEADS = 8
HEAD_DIM = 128
WIDTH = HEADS * HEAD_DIM
CONV_K = 4
DN_CHUNK = 64

ATTN_GROUPS = ((128, 1), (512, 4), (2048, 16))
N_GROUPS = len(ATTN_GROUPS)
ATTN_BLOCK = 128
ROT_DIM = HEAD_DIM // 4
ROPE_THETA = 500000.0

PEER_HEADS = 8
PEER_NKEYS = 128
PEER_TOPK = 16
PEER_QDIM = 256
PEER_SEL = PEER_HEADS * PEER_TOPK

LANES = 128
SUBLANES = 8
VMEM_LIMIT = 48 * 1024 * 1024

ROWS_MATMUL = 512
ROWS_WIDE = 256
ROWS_DELTA = 2 * DN_CHUNK
ROWS_LANEWISE = 2048
COLS_MATMUL = 1024


def _params(sem):
    return pltpu.CompilerParams(dimension_semantics=sem, vmem_limit_bytes=VMEM_LIMIT)


def _dot(a, b):
    return jnp.dot(a.astype(bf16), b.astype(bf16), preferred_element_type=f32)


def _dot_nt(a, b):
    return lax.dot_general(a.astype(bf16), b.astype(bf16), (((1,), (1,)), ((), ())), preferred_element_type=f32)


def _dot_tn(a, b):
    return lax.dot_general(a.astype(bf16), b.astype(bf16), (((0,), (0,)), ((), ())), preferred_element_type=f32)


def _sigmoid(x):
    return 1.0 / (1.0 + jnp.exp(-x))


def _layer_norm(y, g, b):
    mu = jnp.mean(y, -1, keepdims=True)
    yc = y - mu
    var = jnp.mean(yc * yc, -1, keepdims=True)
    return yc * lax.rsqrt(var + LN_EPS) * g + b


def _mm_kernel(a_ref, w_ref, o_ref):
    o_ref[...] = _dot(a_ref[...], w_ref[...])


def _matmul(a, w, tm=ROWS_MATMUL, tn=None):
    m, k = a.shape
    n = w.shape[1]
    tn = tn or n
    return pl.pallas_call(
        _mm_kernel,
        grid=(m // tm, n // tn),
        in_specs=[pl.BlockSpec((tm, k), lambda i, j: (i, 0)), pl.BlockSpec((k, tn), lambda i, j: (0, j))],
        out_specs=pl.BlockSpec((tm, tn), lambda i, j: (i, j)),
        out_shape=jax.ShapeDtypeStruct((m, n), f32),
        compiler_params=_params(("parallel", "parallel")),
        name="matmul",
    )(a, w)


def _rope(y, c, s_lo, s_hi):
    return y * c + pltpu.roll(y, ROT_DIM // 2, axis=1) * s_hi + pltpu.roll(y, LANES - ROT_DIM // 2, axis=1) * s_lo


def _mm_rope_kernel(a_ref, w_ref, c_ref, slo_ref, shi_ref, o_ref, *, heads):
    y = _dot(a_ref[...], w_ref[...])
    c, s_lo, s_hi = c_ref[...], slo_ref[...], shi_ref[...]
    for h in range(heads):
        sl = slice(h * HEAD_DIM, (h + 1) * HEAD_DIM)
        o_ref[:, sl] = _rope(y[:, sl], c, s_lo, s_hi)


def _matmul_rope(a, w, tabs, seq, tm=ROWS_MATMUL, tn=COLS_MATMUL):
    m, k = a.shape
    n = w.shape[1]
    nsb = seq // tm
    tab_spec = pl.BlockSpec((tm, LANES), lambda i, j: (i % nsb, 0))
    return pl.pallas_call(
        functools.partial(_mm_rope_kernel, heads=tn // HEAD_DIM),
        grid=(m // tm, n // tn),
        in_specs=[pl.BlockSpec((tm, k), lambda i, j: (i, 0)), pl.BlockSpec((k, tn), lambda i, j: (0, j)),
                  tab_spec, tab_spec, tab_spec],
        out_specs=pl.BlockSpec((tm, tn), lambda i, j: (i, j)),
        out_shape=jax.ShapeDtypeStruct((m, n), f32),
        compiler_params=_params(("parallel", "parallel")),
        name="matmul_rope",
    )(a, w, *tabs)


def _mm_res_ln_kernel(a_ref, w_ref, x_ref, g_ref, b_ref, o_ref):
    h = _dot(a_ref[...], w_ref[...])
    o_ref[...] = _layer_norm(ALPHA * x_ref[...] + h, g_ref[...], b_ref[...])


def _matmul_res_ln(a, w, x, g, b, tm=ROWS_MATMUL):
    m, k = a.shape
    n = w.shape[1]
    row = pl.BlockSpec((1, n), lambda i: (0, 0))
    return pl.pallas_call(
        _mm_res_ln_kernel,
        grid=(m // tm,),
        in_specs=[pl.BlockSpec((tm, k), lambda i: (i, 0)), pl.BlockSpec((k, n), lambda i: (0, 0)),
                  pl.BlockSpec((tm, n), lambda i: (i, 0)), row, row],
        out_specs=pl.BlockSpec((tm, n), lambda i: (i, 0)),
        out_shape=jax.ShapeDtypeStruct((m, n), f32),
        compiler_params=_params(("parallel",)),
        name="matmul_res_ln",
    )(a, w, x, g.reshape(1, n), b.reshape(1, n))


DN_PROJ_PAD = 4 * WIDTH + LANES


def _dn_prep_kernel(proj_ref, halo_ref, conv_ref, alog_ref, dtb_ref, q_ref, k_ref, v_ref, gate_ref, *, tm, nsb):
    i = pl.program_id(0)
    x = proj_ref[:, : 3 * WIDTH]
    halo = jnp.where(i % nsb == 0, 0.0, halo_ref[...])
    w = conv_ref[...]
    row8 = lax.broadcasted_iota(i32, (SUBLANES, 1), 0)
    acc = x * w[CONV_K - 1:CONV_K]
    for s in range(1, CONV_K):
        xs = pltpu.roll(x, s, axis=0)
        hs = pltpu.roll(halo, s, axis=0)
        first = jnp.where(row8 < s, hs, xs[:SUBLANES])
        xs = jnp.concatenate([first, xs[SUBLANES:]], axis=0)
        acc = acc + xs * w[CONV_K - 1 - s:CONV_K - s]
    qkv = acc * _sigmoid(acc)
    for h in range(HEADS):
        sl = slice(h * HEAD_DIM, (h + 1) * HEAD_DIM)
        qh = qkv[:, sl]
        q_ref[:, sl] = qh * lax.rsqrt(jnp.sum(qh * qh, -1, keepdims=True) + 1e-6) * (HEAD_DIM ** -0.5)
        kh = qkv[:, WIDTH + h * HEAD_DIM: WIDTH + (h + 1) * HEAD_DIM]
        k_ref[:, sl] = kh * lax.rsqrt(jnp.sum(kh * kh, -1, keepdims=True) + 1e-6)
    v_ref[...] = qkv[:, 2 * WIDTH:]
    ab = proj_ref[:, 4 * WIDTH:]
    z = ab + dtb_ref[...]
    softplus = jnp.maximum(z, 0.0) + jnp.log(1.0 + jnp.exp(-jnp.abs(z)))
    g = -jnp.exp(alog_ref[...]) * softplus
    row = lax.broadcasted_iota(i32, (tm, 1), 0) % DN_CHUNK
    s = 1
    while s < DN_CHUNK:
        g = g + jnp.where(row >= s, pltpu.roll(g, s, axis=0), 0.0)
        s *= 2
    lane = lax.broadcasted_iota(i32, (1, LANES), 1)
    gate_ref[...] = jnp.where(lane < HEADS, g, _sigmoid(ab))


def _dn_prep(proj, conv_w, a_log, dt_bias, seq, tm=ROWS_WIDE):
    m = proj.shape[0]
    nsb = seq // tm
    pad = LANES - HEADS
    alog = jnp.pad(a_log.astype(f32), (0, pad)).reshape(1, LANES)
    dtb = jnp.pad(dt_bias.astype(f32), (0, pad)).reshape(1, LANES)
    hb = tm // SUBLANES
    out_w = pl.BlockSpec((tm, WIDTH), lambda i: (i, 0))
    return pl.pallas_call(
        functools.partial(_dn_prep_kernel, tm=tm, nsb=nsb),
        grid=(m // tm,),
        in_specs=[pl.BlockSpec((tm, DN_PROJ_PAD), lambda i: (i, 0)),
                  pl.BlockSpec((SUBLANES, 3 * WIDTH), lambda i: (jnp.maximum(i * hb - 1, 0), 0)),
                  pl.BlockSpec((CONV_K, 3 * WIDTH), lambda i: (0, 0)),
                  pl.BlockSpec((1, LANES), lambda i: (0, 0)), pl.BlockSpec((1, LANES), lambda i: (0, 0))],
        out_specs=[out_w, out_w, out_w, pl.BlockSpec((tm, LANES), lambda i: (i, 0))],
        out_shape=[jax.ShapeDtypeStruct((m, WIDTH), f32)] * 3 + [jax.ShapeDtypeStruct((m, LANES), f32)],
        compiler_params=_params(("parallel",)),
        name="dn_prep",
    )(proj, proj, conv_w, alog, dtb)


def _dn_delta_kernel(q_ref, k_ref, v_ref, z_ref, gate_ref, gt_ref, nw_ref, o_ref, state_ref, *, chunks):
    @pl.when(pl.program_id(1) == 0)
    def _():
        state_ref[...] = jnp.zeros_like(state_ref)

    c = DN_CHUNK
    ri = lax.broadcasted_iota(i32, (c, c), 0)
    ci = lax.broadcasted_iota(i32, (c, c), 1)
    causal = ri >= ci
    strict = ri > ci
    eye = (ri == ci).astype(f32)
    nw = nw_ref[...]
    inst = [(ch, h) for ch in range(chunks) for h in range(HEADS)]
    pre = {}
    for ch, h in inst:
        rows = slice(ch * c, (ch + 1) * c)
        sl = slice(h * HEAD_DIM, (h + 1) * HEAD_DIM)
        q, k, v = q_ref[rows, sl], k_ref[rows, sl], v_ref[rows, sl]
        gcol = gate_ref[rows, h:h + 1]
        beta = gate_ref[rows, HEADS + h:HEADS + h + 1]
        grow = gt_ref[h:h + 1, rows]
        decay = jnp.where(causal, jnp.exp(jnp.where(causal, gcol - grow, 0.0)), 0.0)
        kb = k * beta
        egc = jnp.exp(gcol)
        glast = gcol[c - 1:c, :]
        qk = _dot_nt(jnp.concatenate([q, kb], axis=0), k)
        m = jnp.where(strict, qk[c:] * decay, 0.0)
        pre[ch, h] = dict(a_qk=qk[:c] * decay, inv=eye - m, p=m,
                          rhs=jnp.concatenate([v * beta, kb * egc], axis=1), qe=q * egc,
                          kd=k * jnp.exp(glast - gcol), eg=jnp.exp(glast))
    step = 2
    while step < c:
        for key in inst:
            pre[key]["p"] = _dot(pre[key]["p"], pre[key]["p"])
        for key in inst:
            pre[key]["inv"] = pre[key]["inv"] + _dot(pre[key]["inv"], pre[key]["p"])
        step *= 2
    for key in inst:
        pre[key]["uw"] = _dot(pre[key]["inv"], pre[key]["rhs"])
    states = [state_ref[h] for h in range(HEADS)]
    for ch in range(chunks):
        rows = slice(ch * c, (ch + 1) * c)
        for h in range(HEADS):
            sl = slice(h * HEAD_DIM, (h + 1) * HEAD_DIM)
            d = pre[ch, h]
            u, w = d["uw"][:, :HEAD_DIM], d["uw"][:, HEAD_DIM:]
            ws = _dot(jnp.concatenate([w, d["qe"]], axis=0), states[h])
            v_new = u - ws[:c]
            o = ws[c:] + _dot(d["a_qk"], v_new)
            states[h] = states[h] * d["eg"] + _dot_tn(d["kd"], v_new)
            o = o * lax.rsqrt(jnp.mean(o * o, -1, keepdims=True) + 1e-6) * nw
            zz = z_ref[rows, sl]
            o_ref[rows, sl] = o * (zz * _sigmoid(zz))
    for h in range(HEADS):
        state_ref[h] = states[h]


def _dn_delta(q, k, v, proj, gates, gates_t, norm_w, batch, seq, cb=ROWS_DELTA):
    nb = seq // cb
    wide = pl.BlockSpec((cb, WIDTH), lambda b, i: (b * nb + i, 0))
    return pl.pallas_call(
        functools.partial(_dn_delta_kernel, chunks=cb // DN_CHUNK),
        grid=(batch, nb),
        in_specs=[wide, wide, wide,
                  pl.BlockSpec((cb, WIDTH), lambda b, i: (b * nb + i, 3)),
                  pl.BlockSpec((cb, LANES), lambda b, i: (b * nb + i, 0)),
                  pl.BlockSpec((SUBLANES, cb), lambda b, i: (0, b * nb + i)),
                  pl.BlockSpec((1, HEAD_DIM), lambda b, i: (0, 0))],
        out_specs=wide,
        out_shape=jax.ShapeDtypeStruct((batch * seq, WIDTH), f32),
        scratch_shapes=[pltpu.VMEM((HEADS, HEAD_DIM, HEAD_DIM), f32)],
        compiler_params=_params(("parallel", "arbitrary")),
        name="dn_delta",
    )(q, k, v, proj, gates, gates_t, norm_w.reshape(1, HEAD_DIM).astype(f32))


def _deltanet_mixer(x2, w_pad, conv_w, a_log, dt_bias, norm_w, batch, seq):
    proj = _matmul(x2, w_pad, tn=DN_PROJ_PAD // 3)
    q, k, v, gates = _dn_prep(proj, conv_w, a_log, dt_bias, seq)
    gates_t = gates[:, :SUBLANES].T
    return _dn_delta(q, k, v, proj, gates, gates_t, norm_w, batch, seq)


def _attn_kernel(q_ref, kp_ref, kc_ref, vp_ref, vc_ref, o_ref, l_ref, *, steps, dilation, heads):
    nbi = pl.program_id(1)
    blk = ATTN_BLOCK
    qi = lax.broadcasted_iota(i32, (blk, 2 * blk), 0)
    kj = lax.broadcasted_iota(i32, (blk, 2 * blk), 1)
    dist = qi + blk - kj
    mask = (dist >= 0) & (dist <= steps) & ((kj >= blk) | (nbi > 0))
    scale = HEAD_DIM ** -0.5
    for r in range(dilation):
        rows = pl.ds(r, blk, stride=dilation) if dilation > 1 else slice(None)
        for h in range(heads):
            sl = slice(h * HEAD_DIM, (h + 1) * HEAD_DIM)
            q = q_ref[rows, sl]
            kk = jnp.concatenate([kp_ref[rows, sl], kc_ref[rows, sl]], axis=0)
            vv = jnp.concatenate([vp_ref[rows, sl], vc_ref[rows, sl]], axis=0)
            sc = jnp.where(mask, _dot_nt(q, kk) * scale, -jnp.inf)
            mx = jnp.max(sc, -1, keepdims=True)
            p = jnp.exp(sc - mx)
            den = jnp.sum(p, -1, keepdims=True)
            o_ref[rows, sl] = _dot(p, vv) / den
            l_ref[rows, sl] = jnp.broadcast_to(mx + jnp.log(den), (blk, HEAD_DIM))


def _dilated_attention(q, k, v, group, window, dilation, batch, seq):
    rows = ATTN_BLOCK * dilation
    nb = seq // rows
    heads = HEADS if dilation == 1 else 1
    hblocks = HEADS // heads
    blk = (rows, heads * HEAD_DIM)

    def cur(b, i, j):
        return (b * nb + i, group * hblocks + j)

    def prev(b, i, j):
        return (b * nb + jnp.maximum(i - 1, 0), group * hblocks + j)

    out_spec = pl.BlockSpec(blk, lambda b, i, j: (b * nb + i, j))
    return pl.pallas_call(
        functools.partial(_attn_kernel, steps=window // dilation, dilation=dilation, heads=heads),
        grid=(batch, nb, hblocks),
        in_specs=[pl.BlockSpec(blk, cur), pl.BlockSpec(blk, prev), pl.BlockSpec(blk, cur),
                  pl.BlockSpec(blk, prev), pl.BlockSpec(blk, cur)],
        out_specs=[out_spec, out_spec],
        out_shape=[jax.ShapeDtypeStruct((batch * seq, WIDTH), f32)] * 2,
        compiler_params=_params(("parallel", "parallel", "parallel")),
        name=f"dilated_attn_d{dilation}",
    )(q, k, k, v, v)


def _combine_out_ln_kernel(o0, o1, o2, l0, l1, l2, w_ref, x_ref, g_ref, b_ref, out_ref):
    ls = [l0[...], l1[...], l2[...]]
    mx = jnp.maximum(jnp.maximum(ls[0], ls[1]), ls[2])
    es = [jnp.exp(l - mx) for l in ls]
    o = (es[0] * o0[...] + es[1] * o1[...] + es[2] * o2[...]) / (es[0] + es[1] + es[2])
    h = _dot(o, w_ref[...])
    out_ref[...] = _layer_norm(ALPHA * x_ref[...] + h, g_ref[...], b_ref[...])


def _combine_out_ln(outs, lses, w, x, g, b, tm=ROWS_WIDE):
    m, n = x.shape
    blk = pl.BlockSpec((tm, n), lambda i: (i, 0))
    row = pl.BlockSpec((1, n), lambda i: (0, 0))
    return pl.pallas_call(
        _combine_out_ln_kernel,
        grid=(m // tm,),
        in_specs=[blk] * 6 + [pl.BlockSpec((WIDTH, n), lambda i: (0, 0)), blk, row, row],
        out_specs=blk,
        out_shape=jax.ShapeDtypeStruct((m, n), f32),
        compiler_params=_params(("parallel",)),
        name="attn_combine_out_ln",
    )(*outs, *lses, w, x, g.reshape(1, n), b.reshape(1, n))


def _rope_lane_tables(seq):
    half = ROT_DIM // 2
    inv_freq = ROPE_THETA ** (-jnp.arange(half, dtype=f32) * 2.0 / ROT_DIM)
    ang = jnp.arange(seq, dtype=f32)[:, None] * inv_freq[None, :]
    cos, sin = jnp.cos(ang), jnp.sin(ang)
    ones = jnp.ones((seq, HEAD_DIM - ROT_DIM), f32)
    zeros = jnp.zeros((seq, HEAD_DIM - half), f32)
    c = jnp.concatenate([cos, cos, ones], axis=1)
    s_lo = jnp.concatenate([-sin, zeros], axis=1)
    s_hi = jnp.concatenate([jnp.zeros((seq, half), f32), sin, jnp.zeros((seq, HEAD_DIM - ROT_DIM), f32)], axis=1)
    return c, s_lo, s_hi


def _top_rows(s, order, payload, k):
    big = jnp.int32(2 ** 30)
    vals, pays = [], []
    for _ in range(k):
        m = jnp.max(s, axis=0, keepdims=True)
        pos = jnp.min(jnp.where(s == m, order, big), axis=0, keepdims=True)
        sel = order == pos
        pays.append(pos if payload is None else jnp.max(jnp.where(sel, payload, -1), axis=0, keepdims=True))
        vals.append(m)
        s = jnp.where(sel, -jnp.inf, s)
    return jnp.concatenate(vals, axis=0), jnp.concatenate(pays, axis=0)


def _pair_candidates(t1, t2, combine, fill):
    kt = PEER_TOPK
    row = lax.broadcasted_iota(i32, (SUBLANES, LANES), 0)
    bc = lambda t, r: jnp.broadcast_to(t[r:r + 1], (SUBLANES, LANES))
    lo1, hi1, lo2, hi2 = t1[:SUBLANES], t1[SUBLANES:], t2[:SUBLANES], t2[SUBLANES:]
    tiles = [combine(bc(t1, 0), lo2), combine(bc(t1, 0), hi2), combine(bc(t1, 1), lo2),
             jnp.where(row < kt // 3, combine(bc(t1, 2), lo2), fill),
             jnp.where(row < kt // 4, combine(bc(t1, 3), lo2), fill),
             combine(hi1, bc(t2, 0)),
             jnp.where(row >= 4, combine(lo1, bc(t2, 0)), fill),
             jnp.where(row >= 4, combine(lo1, bc(t2, 1)), jnp.where(row == 0, combine(bc(t1, 4), bc(t2, 2)), fill))]
    return jnp.concatenate(tiles, axis=0)


def _pair_rank():
    kt = PEER_TOPK
    row = lax.broadcasted_iota(i32, (SUBLANES, LANES), 0)
    unused = kt * kt + row
    tiles = [row, row + SUBLANES, kt + row,
             jnp.where(row < kt // 3, 2 * kt + row, unused),
             jnp.where(row < kt // 4, 3 * kt + row, unused + SUBLANES),
             (row + SUBLANES) * kt,
             jnp.where(row >= 4, row * kt, unused + 2 * SUBLANES),
             jnp.where(row >= 4, row * kt + 1, jnp.where(row == 0, 4 * kt + 2, unused + 3 * SUBLANES))]
    return jnp.concatenate(tiles, axis=0)


def _peer_route_kernel(qp_ref, keys_ref, idx_ref, gate_ref):
    kt = PEER_TOPK
    half = PEER_QDIM // 2
    keyid = lax.broadcasted_iota(i32, (PEER_NKEYS, LANES), 0)
    rank = _pair_rank()
    idx_rows, gate_rows = [], []
    for h in range(PEER_HEADS):
        tops = []
        for c in range(2):
            col = (h * 2 + c) * half
            s = _dot_nt(keys_ref[h * 2 + c], qp_ref[:, col:col + half])
            tops.append(_top_rows(s, keyid, None, kt))
        (v1, i1), (v2, i2) = tops
        cand = _pair_candidates(v1, v2, lambda a, b: a + b, -jnp.inf)
        cand_id = _pair_candidates(i1, i2, lambda a, b: a * PEER_NKEYS + b, -1)
        best, ids = _top_rows(cand, rank, cand_id, kt)
        e = jnp.exp(best - best[0:1])
        gate_rows.append(e / jnp.sum(e, axis=0, keepdims=True))
        idx_rows.append(ids)
    idx_ref[...] = jnp.concatenate(idx_rows, axis=0).astype(f32).T.astype(i32)
    gate_ref[...] = jnp.concatenate(gate_rows, axis=0).T


def _peer_route(qp, sub_keys):
    m = qp.shape[0]
    tb = LANES
    keys = sub_keys.reshape(PEER_HEADS * 2, PEER_NKEYS, PEER_QDIM // 2).astype(bf16)
    out = pl.BlockSpec((tb, PEER_SEL), lambda i: (i, 0))
    return pl.pallas_call(
        _peer_route_kernel,
        grid=(m // tb,),
        in_specs=[pl.BlockSpec((tb, qp.shape[1]), lambda i: (i, 0)),
                  pl.BlockSpec(keys.shape, lambda i: (0, 0, 0))],
        out_specs=[out, out],
        out_shape=[jax.ShapeDtypeStruct((m, PEER_SEL), i32), jax.ShapeDtypeStruct((m, PEER_SEL), f32)],
        compiler_params=_params(("parallel",)),
        name="peer_route",
    )(qp, keys)


SC_CORES = 2
SC_SUBCORES = 16
SC_LANES = 16
SC_WORKERS = SC_CORES * SC_SUBCORES
SC_CHUNKS = D_MODEL // SC_LANES
SC_WORDS = D_MODEL // 2
SC_WORD_CHUNKS = SC_WORDS // SC_LANES
SC_TOK_GROUP = 32
SC_ROWS = 64
SC_GATHERS_PER_TOK = PEER_SEL // SC_ROWS
SC_GATHERS = SC_TOK_GROUP * SC_GATHERS_PER_TOK
SC_ROW_GROUP = 8


def _pack_table(tab):
    e = tab.shape[0]
    bits = lax.bitcast_convert_type(tab.astype(bf16), jnp.uint16).astype(jnp.uint32)
    bits = bits.reshape(e, SC_WORD_CHUNKS, 2, SC_LANES)
    word = bits[:, :, 0, :] | (bits[:, :, 1, :] << 16)
    return lax.bitcast_convert_type(word.reshape(e, SC_WORDS), i32)


def _unpack_words(w):
    return lax.bitcast_convert_type(w << 16, f32), lax.bitcast_convert_type(w & (-65536), f32)


def _sc_mesh():
    return plsc.VectorSubcoreMesh(core_axis_name="c", subcore_axis_name="s")


def _sc_worker():
    return lax.axis_index("s") * SC_CORES + lax.axis_index("c")


def _sc_gather_loop(gather, compute):
    gather(0, 0).start()

    @pl.loop(0, SC_GATHERS, step=2)
    def _(q):
        gather(q + 1, 1).start()
        gather(q, 0).wait()
        compute(q, 0)

        @pl.when(q + 2 < SC_GATHERS)
        def _():
            gather(q + 2, 0).start()
        gather(q + 1, 1).wait()
        compute(q + 1, 1)


def _sc_scratch(stage_shape):
    return [pltpu.VMEM((SC_GATHERS, SC_ROWS), i32), pltpu.VMEM(stage_shape[0], f32), pltpu.VMEM(stage_shape[1], f32),
            pltpu.VMEM((SC_ROWS, SC_WORDS), i32), pltpu.VMEM((SC_ROWS, SC_WORDS), i32),
            pltpu.SemaphoreType.DMA, pltpu.SemaphoreType.DMA]


def _peer_udot(x, idx, u_tab):
    t_all = x.shape[0]
    tpw = t_all // SC_WORKERS
    idx2 = idx.reshape(t_all * SC_GATHERS_PER_TOK, SC_ROWS)

    @functools.partial(
        pl.kernel, mesh=_sc_mesh(), compiler_params=pltpu.CompilerParams(needs_layout_passes=False),
        out_type=jax.ShapeDtypeStruct((t_all, PEER_SEL), f32),
        scratch_types=_sc_scratch(((SC_TOK_GROUP, D_MODEL), (SC_TOK_GROUP, PEER_SEL))), name="peer_udot_sc")
    def k(x_hbm, idx_hbm, u_hbm, pre_hbm, idx_v, x_v, pre_v, rows0, rows1, sem0, sem1):
        bufs = ((rows0, sem0), (rows1, sem1))
        base = _sc_worker() * tpw
        lane = lax.iota(i32, SC_LANES)
        zero = jnp.zeros((SC_LANES,), f32)

        def gather(q, b):
            return pltpu.make_async_copy(u_hbm.at[idx_v.at[q]], bufs[b][0], bufs[b][1])

        def compute(q, b):
            rows = bufs[b][0]
            t = q // SC_GATHERS_PER_TOK
            col0 = (q % SC_GATHERS_PER_TOK) * SC_ROWS
            for hh in range(SC_ROWS // SC_LANES):
                vec = zero
                for rg in range(SC_LANES // SC_ROW_GROUP):
                    r0 = hh * SC_LANES + rg * SC_ROW_GROUP

                    def body(c, acc):
                        x_lo = x_v[t, pl.ds(c * 2 * SC_LANES, SC_LANES)]
                        x_hi = x_v[t, pl.ds(c * 2 * SC_LANES + SC_LANES, SC_LANES)]
                        out = []
                        for r in range(SC_ROW_GROUP):
                            lo, hi = _unpack_words(rows[r0 + r, pl.ds(c * SC_LANES, SC_LANES)])
                            out.append(acc[r] + lo * x_lo + hi * x_hi)
                        return tuple(out)

                    acc = lax.fori_loop(0, SC_WORD_CHUNKS, body, (zero,) * SC_ROW_GROUP, unroll=2)
                    for r in range(SC_ROW_GROUP):
                        vec = jnp.where(lane == rg * SC_ROW_GROUP + r, jnp.sum(acc[r]), vec)
                pre_v[t, pl.ds(col0 + hh * SC_LANES, SC_LANES)] = vec

        @pl.loop(0, tpw // SC_TOK_GROUP)
        def _(g):
            tok0 = base + g * SC_TOK_GROUP
            pltpu.sync_copy(idx_hbm.at[pl.ds(tok0 * SC_GATHERS_PER_TOK, SC_GATHERS)], idx_v)
            pltpu.sync_copy(x_hbm.at[pl.ds(tok0, SC_TOK_GROUP)], x_v)
            _sc_gather_loop(gather, compute)
            pltpu.sync_copy(pre_v, pre_hbm.at[pl.ds(tok0, SC_TOK_GROUP)])

    return k(x, idx2, u_tab)


def _peer_vacc(w, idx, v_tab):
    t_all = w.shape[0]
    tpw = t_all // SC_WORKERS
    idx2 = idx.reshape(t_all * SC_GATHERS_PER_TOK, SC_ROWS)
    half_w = SC_WORD_CHUNKS // 2

    @functools.partial(
        pl.kernel, mesh=_sc_mesh(), compiler_params=pltpu.CompilerParams(needs_layout_passes=False),
        out_type=jax.ShapeDtypeStruct((t_all, D_MODEL), f32),
        scratch_types=_sc_scratch(((SC_TOK_GROUP, PEER_SEL), (SC_TOK_GROUP, D_MODEL))), name="peer_vacc_sc")
    def k(w_hbm, idx_hbm, v_hbm, f_hbm, idx_v, w_v, f_v, rows0, rows1, sem0, sem1):
        bufs = ((rows0, sem0), (rows1, sem1))
        base = _sc_worker() * tpw
        zero = jnp.zeros((SC_LANES,), f32)

        def gather(q, b):
            return pltpu.make_async_copy(v_hbm.at[idx_v.at[q]], bufs[b][0], bufs[b][1])

        def compute(q, b):
            rows = bufs[b][0]
            t = q // SC_GATHERS_PER_TOK
            col0 = (q % SC_GATHERS_PER_TOK) * SC_ROWS
            tvec = jnp.full((SC_LANES,), t, i32)
            for half in range(2):
                def body(r, acc):
                    wr = plsc.load_gather(w_v, [tvec, jnp.full((SC_LANES,), col0 + r, i32)])
                    out = []
                    for c in range(half_w):
                        lo, hi = _unpack_words(rows[r, pl.ds((half * half_w + c) * SC_LANES, SC_LANES)])
                        out.append(acc[2 * c] + wr * lo)
                        out.append(acc[2 * c + 1] + wr * hi)
                    return tuple(out)

                acc = lax.fori_loop(0, SC_ROWS, body, (zero,) * (2 * half_w))
                for c in range(2 * half_w):
                    sl = pl.ds((half * 2 * half_w + c) * SC_LANES, SC_LANES)
                    f_v[t, sl] = f_v[t, sl] + acc[c]

        @pl.loop(0, tpw // SC_TOK_GROUP)
        def _(g):
            tok0 = base + g * SC_TOK_GROUP
            pltpu.sync_copy(idx_hbm.at[pl.ds(tok0 * SC_GATHERS_PER_TOK, SC_GATHERS)], idx_v)
            pltpu.sync_copy(w_hbm.at[pl.ds(tok0, SC_TOK_GROUP)], w_v)

            @pl.loop(0, SC_TOK_GROUP)
            def _(t):
                @pl.loop(0, SC_CHUNKS)
                def _(c):
                    f_v[t, pl.ds(c * SC_LANES, SC_LANES)] = zero

            _sc_gather_loop(gather, compute)
            pltpu.sync_copy(f_v, f_hbm.at[pl.ds(tok0, SC_TOK_GROUP)])

    return k(w, idx2, v_tab)


def _peer_act_kernel(pre_ref, gate_ref, w_ref):
    pre = pre_ref[...]
    w_ref[...] = gate_ref[...] * (0.5 * pre * (1.0 + lax.erf(pre * (2.0 ** -0.5))))


def _peer_act(pre, gate, tm=ROWS_LANEWISE):
    m, n = pre.shape
    blk = pl.BlockSpec((tm, n), lambda i: (i, 0))
    return pl.pallas_call(
        _peer_act_kernel, grid=(m // tm,), in_specs=[blk, blk], out_specs=blk,
        out_shape=jax.ShapeDtypeStruct((m, n), f32), compiler_params=_params(("parallel",)), name="peer_act",
    )(pre, gate)


def _res_ln_kernel(x_ref, f_ref, g_ref, b_ref, o_ref):
    o_ref[...] = _layer_norm(ALPHA * x_ref[...] + f_ref[...], g_ref[...], b_ref[...])


def _res_ln(x, f, g, b, tm=ROWS_MATMUL):
    m, n = x.shape
    blk = pl.BlockSpec((tm, n), lambda i: (i, 0))
    row = pl.BlockSpec((1, n), lambda i: (0, 0))
    return pl.pallas_call(
        _res_ln_kernel, grid=(m // tm,), in_specs=[blk, blk, row, row], out_specs=blk,
        out_shape=jax.ShapeDtypeStruct((m, n), f32), compiler_params=_params(("parallel",)), name="res_ln",
    )(x, f, g.reshape(1, n), b.reshape(1, n))


def _peer_expert_ln(x, idx, gate, u_tab, v_tab, g, b):
    pre = _peer_udot(x, idx, u_tab)
    w = _peer_act(pre, gate)
    f = _peer_vacc(w, idx, v_tab)
    return _res_ln(x, f, g, b)


PEER_SPLITS = 2


def _peer_ffn_ln(x2, w_query, sub_keys, u_pack, v_pack, g, b):
    outs = []
    for xs in jnp.split(x2, PEER_SPLITS, axis=0):
        qp = _matmul(xs, w_query)
        idx, gate = _peer_route(qp, sub_keys)
        outs.append(_peer_expert_ln(xs, idx, gate, u_pack, v_pack, g, b))
    return jnp.concatenate(outs, axis=0)


def kernel(x, dn_w_in, dn_conv, dn_a_log, dn_dt_bias, dn_norm_w, dn_w_out, shared_w_kv, attn_w_q, attn_w_out,
           peer_w_query, peer_sub_keys, peer_u, peer_v, ln_mix_g, ln_mix_b, ln_ffn_g, ln_ffn_b):
    batch, seq, d = x.shape
    n_a = DEPTH // 2
    tabs = _rope_lane_tables(seq)
    gw = N_GROUPS * WIDTH
    dn_w_pad = [jnp.pad(w, ((0, 0), (0, DN_PROJ_PAD - w.shape[1]))).astype(bf16) for w in dn_w_in]
    dn_w_out_b = dn_w_out.astype(bf16)
    w_k, w_v = shared_w_kv[:, :gw].astype(bf16), shared_w_kv[:, gw:].astype(bf16)
    w_q, w_o = attn_w_q.astype(bf16), attn_w_out.astype(bf16)
    peer_wq = peer_w_query.astype(bf16)
    u_pack = [_pack_table(t) for t in peer_u]
    v_pack = [_pack_table(t) for t in peer_v]

    def trunk(x2):
        k_sh = v_sh = None
        for layer in range(DEPTH):
            if layer < n_a:
                o = _deltanet_mixer(x2, dn_w_pad[layer], dn_conv[layer], dn_a_log[layer], dn_dt_bias[layer],
                                    dn_norm_w[layer], 1, seq)
                x2 = _matmul_res_ln(o, dn_w_out_b[layer], x2, ln_mix_g[layer], ln_mix_b[layer])
            else:
                j = layer - n_a
                if j == 0:
                    k_sh = _matmul_rope(x2, w_k, tabs, seq)
                    v_sh = _matmul(x2, w_v, tn=COLS_MATMUL)
                q = _matmul_rope(x2, w_q[j], tabs, seq)
                outs, lses = [], []
                for gi, (window, dilation) in enumerate(ATTN_GROUPS):
                    o, l = _dilated_attention(q, k_sh, v_sh, gi, window, dilation, 1, seq)
                    outs.append(o)
                    lses.append(l)
                x2 = _combine_out_ln(outs, lses, w_o[j], x2, ln_mix_g[layer], ln_mix_b[layer])
            x2 = _peer_ffn_ln(x2, peer_wq[layer], peer_sub_keys[layer], u_pack[layer], v_pack[layer],
                              ln_ffn_g[layer], ln_ffn_b[layer])
        return x2

    return jnp.stack([trunk(x[b]) for b in range(batch)], axis=0)
```

```python
import functools

import jax
import jax.numpy as jnp
from jax import lax
from jax.experimental import pallas as pl
from jax.experimental.pallas import tpu as pltpu
from jax.experimental.pallas import tpu_sc as plsc

f32 = jnp.float32
bf16 = jnp.bfloat16
i32 = jnp.int32

D_MODEL = 1024
DEPTH = 2
ALPHA = (2.0 * DEPTH) ** 0.25
LN_EPS = 1e-5

HEADS = 8
HEAD_DIM = 128
WIDTH = HEADS * HEAD_DIM
CONV_K = 4
DN_CHUNK = 64

ATTN_GROUPS = ((128, 1), (512, 4), (2048, 16))
N_GROUPS = len(ATTN_GROUPS)
ATTN_BLOCK = 128
ROT_DIM = HEAD_DIM // 4
ROPE_THETA = 500000.0

PEER_HEADS = 8
PEER_NKEYS = 128
PEER_TOPK = 16
PEER_QDIM = 256
PEER_SEL = PEER_HEADS * PEER_TOPK

LANES = 128
SUBLANES = 8
VMEM_LIMIT = 48 * 1024 * 1024

ROWS_MATMUL = 512
ROWS_WIDE = 256
ROWS_DELTA = 2 * DN_CHUNK
ROWS_LANEWISE = 2048
COLS_MATMUL = 1024


def _params(sem):
    return pltpu.CompilerParams(dimension_semantics=sem, vmem_limit_bytes=VMEM_LIMIT)


def _dot(a, b):
    return jnp.dot(a.astype(bf16), b.astype(bf16), preferred_element_type=f32)


def _dot_nt(a, b):
    return lax.dot_general(a.astype(bf16), b.astype(bf16), (((1,), (1,)), ((), ())), preferred_element_type=f32)


def _dot_tn(a, b):
    return lax.dot_general(a.astype(bf16), b.astype(bf16), (((0,), (0,)), ((), ())), preferred_element_type=f32)


def _sigmoid(x):
    return 1.0 / (1.0 + jnp.exp(-x))


def _layer_norm(y, g, b):
    mu = jnp.mean(y, -1, keepdims=True)
    yc = y - mu
    var = jnp.mean(yc * yc, -1, keepdims=True)
    return yc * lax.rsqrt(var + LN_EPS) * g + b


def _mm_kernel(a_ref, w_ref, o_ref):
    o_ref[...] = _dot(a_ref[...], w_ref[...])


def _matmul(a, w, tm=ROWS_MATMUL, tn=None):
    m, k = a.shape
    n = w.shape[1]
    tn = tn or n
    return pl.pallas_call(
        _mm_kernel,
        grid=(m // tm, n // tn),
        in_specs=[pl.BlockSpec((tm, k), lambda i, j: (i, 0)), pl.BlockSpec((k, tn), lambda i, j: (0, j))],
        out_specs=pl.BlockSpec((tm, tn), lambda i, j: (i, j)),
        out_shape=jax.ShapeDtypeStruct((m, n), f32),
        compiler_params=_params(("parallel", "parallel")),
        name="matmul",
    )(a, w)


def _rope(y, c, s_lo, s_hi):
    return y * c + pltpu.roll(y, ROT_DIM // 2, axis=1) * s_hi + pltpu.roll(y, LANES - ROT_DIM // 2, axis=1) * s_lo


def _mm_rope_kernel(a_ref, w_ref, c_ref, slo_ref, shi_ref, o_ref, *, heads):
    y = _dot(a_ref[...], w_ref[...])
    c, s_lo, s_hi = c_ref[...], slo_ref[...], shi_ref[...]
    for h in range(heads):
        sl = slice(h * HEAD_DIM, (h + 1) * HEAD_DIM)
        o_ref[:, sl] = _rope(y[:, sl], c, s_lo, s_hi)


def _matmul_rope(a, w, tabs, seq, tm=ROWS_MATMUL, tn=COLS_MATMUL):
    m, k = a.shape
    n = w.shape[1]
    nsb = seq // tm
    tab_spec = pl.BlockSpec((tm, LANES), lambda i, j: (i % nsb, 0))
    return pl.pallas_call(
        functools.partial(_mm_rope_kernel, heads=tn // HEAD_DIM),
        grid=(m // tm, n // tn),
        in_specs=[pl.BlockSpec((tm, k), lambda i, j: (i, 0)), pl.BlockSpec((k, tn), lambda i, j: (0, j)),
                  tab_spec, tab_spec, tab_spec],
        out_specs=pl.BlockSpec((tm, tn), lambda i, j: (i, j)),
        out_shape=jax.ShapeDtypeStruct((m, n), f32),
        compiler_params=_params(("parallel", "parallel")),
        name="matmul_rope",
    )(a, w, *tabs)


def _mm_res_ln_kernel(a_ref, w_ref, x_ref, g_ref, b_ref, o_ref):
    h = _dot(a_ref[...], w_ref[...])
    o_ref[...] = _layer_norm(ALPHA * x_ref[...] + h, g_ref[...], b_ref[...])


def _matmul_res_ln(a, w, x, g, b, tm=ROWS_MATMUL):
    m, k = a.shape
    n = w.shape[1]
    row = pl.BlockSpec((1, n), lambda i: (0, 0))
    return pl.pallas_call(
        _mm_res_ln_kernel,
        grid=(m // tm,),
        in_specs=[pl.BlockSpec((tm, k), lambda i: (i, 0)), pl.BlockSpec((k, n), lambda i: (0, 0)),
                  pl.BlockSpec((tm, n), lambda i: (i, 0)), row, row],
        out_specs=pl.BlockSpec((tm, n), lambda i: (i, 0)),
        out_shape=jax.ShapeDtypeStruct((m, n), f32),
        compiler_params=_params(("parallel",)),
        name="matmul_res_ln",
    )(a, w, x, g.reshape(1, n), b.reshape(1, n))


DN_PROJ_PAD = 4 * WIDTH + LANES


def _dn_prep_kernel(proj_ref, halo_ref, lead_ref, conv_ref, alog_ref, dtb_ref, q_ref, k_ref, v_ref, gate_ref, *, tm):
    i = pl.program_id(0)
    x = proj_ref[:, : 3 * WIDTH]
    halo = jnp.where(i == 0, lead_ref[...], halo_ref[...])
    w = conv_ref[...]
    row8 = lax.broadcasted_iota(i32, (SUBLANES, 1), 0)
    acc = x * w[CONV_K - 1:CONV_K]
    for s in range(1, CONV_K):
        xs = pltpu.roll(x, s, axis=0)
        hs = pltpu.roll(halo, s, axis=0)
        first = jnp.where(row8 < s, hs, xs[:SUBLANES])
        xs = jnp.concatenate([first, xs[SUBLANES:]], axis=0)
        acc = acc + xs * w[CONV_K - 1 - s:CONV_K - s]
    qkv = acc * _sigmoid(acc)
    for h in range(HEADS):
        sl = slice(h * HEAD_DIM, (h + 1) * HEAD_DIM)
        qh = qkv[:, sl]
        q_ref[:, sl] = qh * lax.rsqrt(jnp.sum(qh * qh, -1, keepdims=True) + 1e-6) * (HEAD_DIM ** -0.5)
        kh = qkv[:, WIDTH + h * HEAD_DIM: WIDTH + (h + 1) * HEAD_DIM]
        k_ref[:, sl] = kh * lax.rsqrt(jnp.sum(kh * kh, -1, keepdims=True) + 1e-6)
    v_ref[...] = qkv[:, 2 * WIDTH:]
    ab = proj_ref[:, 4 * WIDTH:]
    z = ab + dtb_ref[...]
    softplus = jnp.maximum(z, 0.0) + jnp.log(1.0 + jnp.exp(-jnp.abs(z)))
    g = -jnp.exp(alog_ref[...]) * softplus
    row = lax.broadcasted_iota(i32, (tm, 1), 0) % DN_CHUNK
    s = 1
    while s < DN_CHUNK:
        g = g + jnp.where(row >= s, pltpu.roll(g, s, axis=0), 0.0)
        s *= 2
    lane = lax.broadcasted_iota(i32, (1, LANES), 1)
    gate_ref[...] = jnp.where(lane < HEADS, g, _sigmoid(ab))


def _dn_prep(proj, lead, conv_w, a_log, dt_bias, tm=ROWS_WIDE):
    m = proj.shape[0]
    pad = LANES - HEADS
    alog = jnp.pad(a_log.astype(f32), (0, pad)).reshape(1, LANES)
    dtb = jnp.pad(dt_bias.astype(f32), (0, pad)).reshape(1, LANES)
    hb = tm // SUBLANES
    out_w = pl.BlockSpec((tm, WIDTH), lambda i: (i, 0))
    return pl.pallas_call(
        functools.partial(_dn_prep_kernel, tm=tm),
        grid=(m // tm,),
        in_specs=[pl.BlockSpec((tm, DN_PROJ_PAD), lambda i: (i, 0)),
                  pl.BlockSpec((SUBLANES, 3 * WIDTH), lambda i: (jnp.maximum(i * hb - 1, 0), 0)),
                  pl.BlockSpec((SUBLANES, 3 * WIDTH), lambda i: (0, 0)),
                  pl.BlockSpec((CONV_K, 3 * WIDTH), lambda i: (0, 0)),
                  pl.BlockSpec((1, LANES), lambda i: (0, 0)), pl.BlockSpec((1, LANES), lambda i: (0, 0))],
        out_specs=[out_w, out_w, out_w, pl.BlockSpec((tm, LANES), lambda i: (i, 0))],
        out_shape=[jax.ShapeDtypeStruct((m, WIDTH), f32)] * 3 + [jax.ShapeDtypeStruct((m, LANES), f32)],
        compiler_params=_params(("parallel",)),
        name="dn_prep",
    )(proj, proj, lead, conv_w, alog, dtb)


def _dn_delta_kernel(q_ref, k_ref, v_ref, z_ref, gate_ref, gt_ref, nw_ref, s0_ref, o_ref, state_ref, *, chunks):
    @pl.when(pl.program_id(0) == 0)
    def _():
        state_ref[...] = s0_ref[...]

    c = DN_CHUNK
    ri = lax.broadcasted_iota(i32, (c, c), 0)
    ci = lax.broadcasted_iota(i32, (c, c), 1)
    causal = ri >= ci
    strict = ri > ci
    eye = (ri == ci).astype(f32)
    nw = nw_ref[...]
    inst = [(ch, h) for ch in range(chunks) for h in range(HEADS)]
    pre = {}
    for ch, h in inst:
        rows = slice(ch * c, (ch + 1) * c)
        sl = slice(h * HEAD_DIM, (h + 1) * HEAD_DIM)
        q, k, v = q_ref[rows, sl], k_ref[rows, sl], v_ref[rows, sl]
        gcol = gate_ref[rows, h:h + 1]
        beta = gate_ref[rows, HEADS + h:HEADS + h + 1]
        grow = gt_ref[h:h + 1, rows]
        decay = jnp.where(causal, jnp.exp(jnp.where(causal, gcol - grow, 0.0)), 0.0)
        kb = k * beta
        egc = jnp.exp(gcol)
        glast = gcol[c - 1:c, :]
        qk = _dot_nt(jnp.concatenate([q, kb], axis=0), k)
        m = jnp.where(strict, qk[c:] * decay, 0.0)
        pre[ch, h] = dict(a_qk=qk[:c] * decay, inv=eye - m, p=m,
                          rhs=jnp.concatenate([v * beta, kb * egc], axis=1), qe=q * egc,
                          kd=k * jnp.exp(glast - gcol), eg=jnp.exp(glast))
    step = 2
    while step < c:
        for key in inst:
            pre[key]["p"] = _dot(pre[key]["p"], pre[key]["p"])
        for key in inst:
            pre[key]["inv"] = pre[key]["inv"] + _dot(pre[key]["inv"], pre[key]["p"])
        step *= 2
    for key in inst:
        pre[key]["uw"] = _dot(pre[key]["inv"], pre[key]["rhs"])
    states = [state_ref[h] for h in range(HEADS)]
    for ch in range(chunks):
        rows = slice(ch * c, (ch + 1) * c)
        for h in range(HEADS):
            sl = slice(h * HEAD_DIM, (h + 1) * HEAD_DIM)
            d = pre[ch, h]
            u, w = d["uw"][:, :HEAD_DIM], d["uw"][:, HEAD_DIM:]
            ws = _dot(jnp.concatenate([w, d["qe"]], axis=0), states[h])
            v_new = u - ws[:c]
            o = ws[c:] + _dot(d["a_qk"], v_new)
            states[h] = states[h] * d["eg"] + _dot_tn(d["kd"], v_new)
            o = o * lax.rsqrt(jnp.mean(o * o, -1, keepdims=True) + 1e-6) * nw
            zz = z_ref[rows, sl]
            o_ref[rows, sl] = o * (zz * _sigmoid(zz))
    for h in range(HEADS):
        state_ref[h] = states[h]


def _dn_delta(q, k, v, proj, gates, gates_t, norm_w, state, cb=ROWS_DELTA):
    m = q.shape[0]
    wide = pl.BlockSpec((cb, WIDTH), lambda i: (i, 0))
    whole_state = pl.BlockSpec(state.shape, lambda i: (0, 0, 0))
    return pl.pallas_call(
        functools.partial(_dn_delta_kernel, chunks=cb // DN_CHUNK),
        grid=(m // cb,),
        in_specs=[wide, wide, wide,
                  pl.BlockSpec((cb, WIDTH), lambda i: (i, 3)),
                  pl.BlockSpec((cb, LANES), lambda i: (i, 0)),
                  pl.BlockSpec((SUBLANES, cb), lambda i: (0, i)),
                  pl.BlockSpec((1, HEAD_DIM), lambda i: (0, 0)),
                  whole_state],
        out_specs=[wide, whole_state],
        out_shape=[jax.ShapeDtypeStruct((m, WIDTH), f32), jax.ShapeDtypeStruct(state.shape, f32)],
        compiler_params=_params(("arbitrary",)),
        name="dn_delta",
    )(q, k, v, proj, gates, gates_t, norm_w.reshape(1, HEAD_DIM).astype(f32), state)


def _deltanet_mixer(x2, w_pad, conv_w, a_log, dt_bias, norm_w, lead, state):
    proj = _matmul(x2, w_pad, tn=DN_PROJ_PAD // 3)
    q, k, v, gates = _dn_prep(proj, lead, conv_w, a_log, dt_bias)
    gates_t = gates[:, :SUBLANES].T
    o, state = _dn_delta(q, k, v, proj, gates, gates_t, norm_w, state)
    return o, proj[-SUBLANES:, :3 * WIDTH], state


def _attn_kernel(q_ref, kp_ref, kc_ref, vp_ref, vc_ref, o_ref, l_ref, *, steps, dilation, heads):
    nbi = pl.program_id(1)
    blk = ATTN_BLOCK
    qi = lax.broadcasted_iota(i32, (blk, 2 * blk), 0)
    kj = lax.broadcasted_iota(i32, (blk, 2 * blk), 1)
    dist = qi + blk - kj
    mask = (dist >= 0) & (dist <= steps) & ((kj >= blk) | (nbi > 0))
    scale = HEAD_DIM ** -0.5
    for r in range(dilation):
        rows = pl.ds(r, blk, stride=dilation) if dilation > 1 else slice(None)
        for h in range(heads):
            sl = slice(h * HEAD_DIM, (h + 1) * HEAD_DIM)
            q = q_ref[rows, sl]
            kk = jnp.concatenate([kp_ref[rows, sl], kc_ref[rows, sl]], axis=0)
            vv = jnp.concatenate([vp_ref[rows, sl], vc_ref[rows, sl]], axis=0)
            sc = jnp.where(mask, _dot_nt(q, kk) * scale, -jnp.inf)
            mx = jnp.max(sc, -1, keepdims=True)
            p = jnp.exp(sc - mx)
            den = jnp.sum(p, -1, keepdims=True)
            o_ref[rows, sl] = _dot(p, vv) / den
            l_ref[rows, sl] = jnp.broadcast_to(mx + jnp.log(den), (blk, HEAD_DIM))


def _dilated_attention(q, k, v, group, window, dilation, batch, seq):
    rows = ATTN_BLOCK * dilation
    nb = seq // rows
    heads = HEADS if dilation == 1 else 1
    hblocks = HEADS // heads
    blk = (rows, heads * HEAD_DIM)

    def cur(b, i, j):
        return (b * nb + i, group * hblocks + j)

    def prev(b, i, j):
        return (b * nb + jnp.maximum(i - 1, 0), group * hblocks + j)

    out_spec = pl.BlockSpec(blk, lambda b, i, j: (b * nb + i, j))
    return pl.pallas_call(
        functools.partial(_attn_kernel, steps=window // dilation, dilation=dilation, heads=heads),
        grid=(batch, nb, hblocks),
        in_specs=[pl.BlockSpec(blk, cur), pl.BlockSpec(blk, prev), pl.BlockSpec(blk, cur),
                  pl.BlockSpec(blk, prev), pl.BlockSpec(blk, cur)],
        out_specs=[out_spec, out_spec],
        out_shape=[jax.ShapeDtypeStruct((batch * seq, WIDTH), f32)] * 2,
        compiler_params=_params(("parallel", "parallel", "parallel")),
        name=f"dilated_attn_d{dilation}",
    )(q, k, k, v, v)


def _combine_out_ln_kernel(o0, o1, o2, l0, l1, l2, w_ref, x_ref, g_ref, b_ref, out_ref):
    ls = [l0[...], l1[...], l2[...]]
    mx = jnp.maximum(jnp.maximum(ls[0], ls[1]), ls[2])
    es = [jnp.exp(l - mx) for l in ls]
    o = (es[0] * o0[...] + es[1] * o1[...] + es[2] * o2[...]) / (es[0] + es[1] + es[2])
    h = _dot(o, w_ref[...])
    out_ref[...] = _layer_norm(ALPHA * x_ref[...] + h, g_ref[...], b_ref[...])


def _combine_out_ln(outs, lses, w, x, g, b, tm=ROWS_WIDE):
    m, n = x.shape
    blk = pl.BlockSpec((tm, n), lambda i: (i, 0))
    row = pl.BlockSpec((1, n), lambda i: (0, 0))
    return pl.pallas_call(
        _combine_out_ln_kernel,
        grid=(m // tm,),
        in_specs=[blk] * 6 + [pl.BlockSpec((WIDTH, n), lambda i: (0, 0)), blk, row, row],
        out_specs=blk,
        out_shape=jax.ShapeDtypeStruct((m, n), f32),
        compiler_params=_params(("parallel",)),
        name="attn_combine_out_ln",
    )(*outs, *lses, w, x, g.reshape(1, n), b.reshape(1, n))


def _rope_lane_tables(seq):
    half = ROT_DIM // 2
    inv_freq = ROPE_THETA ** (-jnp.arange(half, dtype=f32) * 2.0 / ROT_DIM)
    ang = jnp.arange(seq, dtype=f32)[:, None] * inv_freq[None, :]
    cos, sin = jnp.cos(ang), jnp.sin(ang)
    ones = jnp.ones((seq, HEAD_DIM - ROT_DIM), f32)
    zeros = jnp.zeros((seq, HEAD_DIM - half), f32)
    c = jnp.concatenate([cos, cos, ones], axis=1)
    s_lo = jnp.concatenate([-sin, zeros], axis=1)
    s_hi = jnp.concatenate([jnp.zeros((seq, half), f32), sin, jnp.zeros((seq, HEAD_DIM - ROT_DIM), f32)], axis=1)
    return c, s_lo, s_hi


def _top_rows(s, order, payload, k):
    big = jnp.int32(2 ** 30)
    vals, pays = [], []
    for _ in range(k):
        m = jnp.max(s, axis=0, keepdims=True)
        pos = jnp.min(jnp.where(s == m, order, big), axis=0, keepdims=True)
        sel = order == pos
        pays.append(pos if payload is None else jnp.max(jnp.where(sel, payload, -1), axis=0, keepdims=True))
        vals.append(m)
        s = jnp.where(sel, -jnp.inf, s)
    return jnp.concatenate(vals, axis=0), jnp.concatenate(pays, axis=0)


def _pair_candidates(t1, t2, combine, fill):
    kt = PEER_TOPK
    row = lax.broadcasted_iota(i32, (SUBLANES, LANES), 0)
    bc = lambda t, r: jnp.broadcast_to(t[r:r + 1], (SUBLANES, LANES))
    lo1, hi1, lo2, hi2 = t1[:SUBLANES], t1[SUBLANES:], t2[:SUBLANES], t2[SUBLANES:]
    tiles = [combine(bc(t1, 0), lo2), combine(bc(t1, 0), hi2), combine(bc(t1, 1), lo2),
             jnp.where(row < kt // 3, combine(bc(t1, 2), lo2), fill),
             jnp.where(row < kt // 4, combine(bc(t1, 3), lo2), fill),
             combine(hi1, bc(t2, 0)),
             jnp.where(row >= 4, combine(lo1, bc(t2, 0)), fill),
             jnp.where(row >= 4, combine(lo1, bc(t2, 1)), jnp.where(row == 0, combine(bc(t1, 4), bc(t2, 2)), fill))]
    return jnp.concatenate(tiles, axis=0)


def _pair_rank():
    kt = PEER_TOPK
    row = lax.broadcasted_iota(i32, (SUBLANES, LANES), 0)
    unused = kt * kt + row
    tiles = [row, row + SUBLANES, kt + row,
             jnp.where(row < kt // 3, 2 * kt + row, unused),
             jnp.where(row < kt // 4, 3 * kt + row, unused + SUBLANES),
             (row + SUBLANES) * kt,
             jnp.where(row >= 4, row * kt, unused + 2 * SUBLANES),
             jnp.where(row >= 4, row * kt + 1, jnp.where(row == 0, 4 * kt + 2, unused + 3 * SUBLANES))]
    return jnp.concatenate(tiles, axis=0)


def _peer_route_kernel(qp_ref, keys_ref, idx_ref, gate_ref):
    kt = PEER_TOPK
    half = PEER_QDIM // 2
    keyid = lax.broadcasted_iota(i32, (PEER_NKEYS, LANES), 0)
    rank = _pair_rank()
    idx_rows, gate_rows = [], []
    for h in range(PEER_HEADS):
        tops = []
        for c in range(2):
            col = (h * 2 + c) * half
            s = _dot_nt(keys_ref[h * 2 + c], qp_ref[:, col:col + half])
            tops.append(_top_rows(s, keyid, None, kt))
        (v1, i1), (v2, i2) = tops
        cand = _pair_candidates(v1, v2, lambda a, b: a + b, -jnp.inf)
        cand_id = _pair_candidates(i1, i2, lambda a, b: a * PEER_NKEYS + b, -1)
        best, ids = _top_rows(cand, rank, cand_id, kt)
        e = jnp.exp(best - best[0:1])
        gate_rows.append(e / jnp.sum(e, axis=0, keepdims=True))
        idx_rows.append(ids)
    idx_ref[...] = jnp.concatenate(idx_rows, axis=0).astype(f32).T.astype(i32)
    gate_ref[...] = jnp.concatenate(gate_rows, axis=0).T


def _peer_route(qp, sub_keys):
    m = qp.shape[0]
    tb = LANES
    keys = sub_keys.reshape(PEER_HEADS * 2, PEER_NKEYS, PEER_QDIM // 2).astype(bf16)
    out = pl.BlockSpec((tb, PEER_SEL), lambda i: (i, 0))
    return pl.pallas_call(
        _peer_route_kernel,
        grid=(m // tb,),
        in_specs=[pl.BlockSpec((tb, qp.shape[1]), lambda i: (i, 0)),
                  pl.BlockSpec(keys.shape, lambda i: (0, 0, 0))],
        out_specs=[out, out],
        out_shape=[jax.ShapeDtypeStruct((m, PEER_SEL), i32), jax.ShapeDtypeStruct((m, PEER_SEL), f32)],
        compiler_params=_params(("parallel",)),
        name="peer_route",
    )(qp, keys)


SC_CORES = 2
SC_SUBCORES = 16
SC_LANES = 16
SC_WORKERS = SC_CORES * SC_SUBCORES
SC_CHUNKS = D_MODEL // SC_LANES
SC_WORDS = D_MODEL // 2
SC_WORD_CHUNKS = SC_WORDS // SC_LANES
SC_TOK_GROUP = 32
SC_ROWS = 64
SC_GATHERS_PER_TOK = PEER_SEL // SC_ROWS
SC_GATHERS = SC_TOK_GROUP * SC_GATHERS_PER_TOK
SC_ROW_GROUP = 8


def _pack_table(tab):
    e = tab.shape[0]
    bits = lax.bitcast_convert_type(tab.astype(bf16), jnp.uint16).astype(jnp.uint32)
    bits = bits.reshape(e, SC_WORD_CHUNKS, 2, SC_LANES)
    word = bits[:, :, 0, :] | (bits[:, :, 1, :] << 16)
    return lax.bitcast_convert_type(word.reshape(e, SC_WORDS), i32)


def _unpack_words(w):
    return lax.bitcast_convert_type(w << 16, f32), lax.bitcast_convert_type(w & (-65536), f32)


def _sc_mesh():
    return plsc.VectorSubcoreMesh(core_axis_name="c", subcore_axis_name="s")


def _sc_worker():
    return lax.axis_index("s") * SC_CORES + lax.axis_index("c")


def _sc_gather_loop(gather, compute):
    gather(0, 0).start()

    @pl.loop(0, SC_GATHERS, step=2)
    def _(q):
        gather(q + 1, 1).start()
        gather(q, 0).wait()
        compute(q, 0)

        @pl.when(q + 2 < SC_GATHERS)
        def _():
            gather(q + 2, 0).start()
        gather(q + 1, 1).wait()
        compute(q + 1, 1)


def _sc_scratch(stage_shape):
    return [pltpu.VMEM((SC_GATHERS, SC_ROWS), i32), pltpu.VMEM(stage_shape[0], f32), pltpu.VMEM(stage_shape[1], f32),
            pltpu.VMEM((SC_ROWS, SC_WORDS), i32), pltpu.VMEM((SC_ROWS, SC_WORDS), i32),
            pltpu.SemaphoreType.DMA, pltpu.SemaphoreType.DMA]


def _peer_udot(x, idx, u_tab):
    t_all = x.shape[0]
    tpw = t_all // SC_WORKERS
    idx2 = idx.reshape(t_all * SC_GATHERS_PER_TOK, SC_ROWS)

    @functools.partial(
        pl.kernel, mesh=_sc_mesh(), compiler_params=pltpu.CompilerParams(needs_layout_passes=False),
        out_type=jax.ShapeDtypeStruct((t_all, PEER_SEL), f32),
        scratch_types=_sc_scratch(((SC_TOK_GROUP, D_MODEL), (SC_TOK_GROUP, PEER_SEL))), name="peer_udot_sc")
    def k(x_hbm, idx_hbm, u_hbm, pre_hbm, idx_v, x_v, pre_v, rows0, rows1, sem0, sem1):
        bufs = ((rows0, sem0), (rows1, sem1))
        base = _sc_worker() * tpw
        lane = lax.iota(i32, SC_LANES)
        zero = jnp.zeros((SC_LANES,), f32)

        def gather(q, b):
            return pltpu.make_async_copy(u_hbm.at[idx_v.at[q]], bufs[b][0], bufs[b][1])

        def compute(q, b):
            rows = bufs[b][0]
            t = q // SC_GATHERS_PER_TOK
            col0 = (q % SC_GATHERS_PER_TOK) * SC_ROWS
            for hh in range(SC_ROWS // SC_LANES):
                vec = zero
                for rg in range(SC_LANES // SC_ROW_GROUP):
                    r0 = hh * SC_LANES + rg * SC_ROW_GROUP

                    def body(c, acc):
                        x_lo = x_v[t, pl.ds(c * 2 * SC_LANES, SC_LANES)]
                        x_hi = x_v[t, pl.ds(c * 2 * SC_LANES + SC_LANES, SC_LANES)]
                        out = []
                        for r in range(SC_ROW_GROUP):
                            lo, hi = _unpack_words(rows[r0 + r, pl.ds(c * SC_LANES, SC_LANES)])
                            out.append(acc[r] + lo * x_lo + hi * x_hi)
                        return tuple(out)

                    acc = lax.fori_loop(0, SC_WORD_CHUNKS, body, (zero,) * SC_ROW_GROUP, unroll=2)
                    for r in range(SC_ROW_GROUP):
                        vec = jnp.where(lane == rg * SC_ROW_GROUP + r, jnp.sum(acc[r]), vec)
                pre_v[t, pl.ds(col0 + hh * SC_LANES, SC_LANES)] = vec

        @pl.loop(0, tpw // SC_TOK_GROUP)
        def _(g):
            tok0 = base + g * SC_TOK_GROUP
            pltpu.sync_copy(idx_hbm.at[pl.ds(tok0 * SC_GATHERS_PER_TOK, SC_GATHERS)], idx_v)
            pltpu.sync_copy(x_hbm.at[pl.ds(tok0, SC_TOK_GROUP)], x_v)
            _sc_gather_loop(gather, compute)
            pltpu.sync_copy(pre_v, pre_hbm.at[pl.ds(tok0, SC_TOK_GROUP)])

    return k(x, idx2, u_tab)


def _peer_vacc(w, idx, v_tab):
    t_all = w.shape[0]
    tpw = t_all // SC_WORKERS
    idx2 = idx.reshape(t_all * SC_GATHERS_PER_TOK, SC_ROWS)
    half_w = SC_WORD_CHUNKS // 2

    @functools.partial(
        pl.kernel, mesh=_sc_mesh(), compiler_params=pltpu.CompilerParams(needs_layout_passes=False),
        out_type=jax.ShapeDtypeStruct((t_all, D_MODEL), f32),
        scratch_types=_sc_scratch(((SC_TOK_GROUP, PEER_SEL), (SC_TOK_GROUP, D_MODEL))), name="peer_vacc_sc")
    def k(w_hbm, idx_hbm, v_hbm, f_hbm, idx_v, w_v, f_v, rows0, rows1, sem0, sem1):
        bufs = ((rows0, sem0), (rows1, sem1))
        base = _sc_worker() * tpw
        zero = jnp.zeros((SC_LANES,), f32)

        def gather(q, b):
            return pltpu.make_async_copy(v_hbm.at[idx_v.at[q]], bufs[b][0], bufs[b][1])

        def compute(q, b):
            rows = bufs[b][0]
            t = q // SC_GATHERS_PER_TOK
            col0 = (q % SC_GATHERS_PER_TOK) * SC_ROWS
            tvec = jnp.full((SC_LANES,), t, i32)
            for half in range(2):
                def body(r, acc):
                    wr = plsc.load_gather(w_v, [tvec, jnp.full((SC_LANES,), col0 + r, i32)])
                    out = []
                    for c in range(half_w):
                        lo, hi = _unpack_words(rows[r, pl.ds((half * half_w + c) * SC_LANES, SC_LANES)])
                        out.append(acc[2 * c] + wr * lo)
                        out.append(acc[2 * c + 1] + wr * hi)
                    return tuple(out)

                acc = lax.fori_loop(0, SC_ROWS, body, (zero,) * (2 * half_w))
                for c in range(2 * half_w):
                    sl = pl.ds((half * 2 * half_w + c) * SC_LANES, SC_LANES)
                    f_v[t, sl] = f_v[t, sl] + acc[c]

        @pl.loop(0, tpw // SC_TOK_GROUP)
        def _(g):
            tok0 = base + g * SC_TOK_GROUP
            pltpu.sync_copy(idx_hbm.at[pl.ds(tok0 * SC_GATHERS_PER_TOK, SC_GATHERS)], idx_v)
            pltpu.sync_copy(w_hbm.at[pl.ds(tok0, SC_TOK_GROUP)], w_v)

            @pl.loop(0, SC_TOK_GROUP)
            def _(t):
                @pl.loop(0, SC_CHUNKS)
                def _(c):
                    f_v[t, pl.ds(c * SC_LANES, SC_LANES)] = zero

            _sc_gather_loop(gather, compute)
            pltpu.sync_copy(f_v, f_hbm.at[pl.ds(tok0, SC_TOK_GROUP)])

    return k(w, idx2, v_tab)


def _peer_act_kernel(pre_ref, gate_ref, w_ref):
    pre = pre_ref[...]
    w_ref[...] = gate_ref[...] * (0.5 * pre * (1.0 + lax.erf(pre * (2.0 ** -0.5))))


def _peer_act(pre, gate, tm=ROWS_LANEWISE):
    m, n = pre.shape
    blk = pl.BlockSpec((tm, n), lambda i: (i, 0))
    return pl.pallas_call(
        _peer_act_kernel, grid=(m // tm,), in_specs=[blk, blk], out_specs=blk,
        out_shape=jax.ShapeDtypeStruct((m, n), f32), compiler_params=_params(("parallel",)), name="peer_act",
    )(pre, gate)


def _res_ln_kernel(x_ref, f_ref, g_ref, b_ref, o_ref):
    o_ref[...] = _layer_norm(ALPHA * x_ref[...] + f_ref[...], g_ref[...], b_ref[...])


def _res_ln(x, f, g, b, tm=ROWS_MATMUL):
    m, n = x.shape
    blk = pl.BlockSpec((tm, n), lambda i: (i, 0))
    row = pl.BlockSpec((1, n), lambda i: (0, 0))
    return pl.pallas_call(
        _res_ln_kernel, grid=(m // tm,), in_specs=[blk, blk, row, row], out_specs=blk,
        out_shape=jax.ShapeDtypeStruct((m, n), f32), compiler_params=_params(("parallel",)), name="res_ln",
    )(x, f, g.reshape(1, n), b.reshape(1, n))


def _peer_expert_ln(x, idx, gate, u_tab, v_tab, g, b):
    pre = _peer_udot(x, idx, u_tab)
    w = _peer_act(pre, gate)
    f = _peer_vacc(w, idx, v_tab)
    return _res_ln(x, f, g, b)


PEER_SPLITS = 2
DN_SEGMENTS = 2


def _peer_ffn_ln(x2, w_query, sub_keys, u_pack, v_pack, g, b, splits=PEER_SPLITS):
    outs = []
    for xs in jnp.split(x2, splits, axis=0):
        qp = _matmul(xs, w_query)
        idx, gate = _peer_route(qp, sub_keys)
        outs.append(_peer_expert_ln(xs, idx, gate, u_pack, v_pack, g, b))
    return jnp.concatenate(outs, axis=0)


def kernel(x, dn_w_in, dn_conv, dn_a_log, dn_dt_bias, dn_norm_w, dn_w_out, shared_w_kv, attn_w_q, attn_w_out,
           peer_w_query, peer_sub_keys, peer_u, peer_v, ln_mix_g, ln_mix_b, ln_ffn_g, ln_ffn_b):
    batch, seq, d = x.shape
    n_a = DEPTH // 2
    tabs = _rope_lane_tables(seq)
    gw = N_GROUPS * WIDTH
    dn_w_pad = [jnp.pad(w, ((0, 0), (0, DN_PROJ_PAD - w.shape[1]))).astype(bf16) for w in dn_w_in]
    dn_w_out_b = dn_w_out.astype(bf16)
    w_k, w_v = shared_w_kv[:, :gw].astype(bf16), shared_w_kv[:, gw:].astype(bf16)
    w_q, w_o = attn_w_q.astype(bf16), attn_w_out.astype(bf16)
    peer_wq = peer_w_query.astype(bf16)
    u_pack = [_pack_table(t) for t in peer_u]
    v_pack = [_pack_table(t) for t in peer_v]

    def trunk(x2):
        k_sh = v_sh = None
        for layer in range(DEPTH):
            if layer < n_a:
                lead = jnp.zeros((SUBLANES, 3 * WIDTH), f32)
                state = jnp.zeros((HEADS, HEAD_DIM, HEAD_DIM), f32)
                segs = []
                for xs in jnp.split(x2, DN_SEGMENTS, axis=0):
                    o, lead, state = _deltanet_mixer(xs, dn_w_pad[layer], dn_conv[layer], dn_a_log[layer],
                                                     dn_dt_bias[layer], dn_norm_w[layer], lead, state)
                    xs = _matmul_res_ln(o, dn_w_out_b[layer], xs, ln_mix_g[layer], ln_mix_b[layer])
                    segs.append(_peer_ffn_ln(xs, peer_wq[layer], peer_sub_keys[layer], u_pack[layer], v_pack[layer],
                                             ln_ffn_g[layer], ln_ffn_b[layer], splits=1))
                x2 = jnp.concatenate(segs, axis=0)
                continue
            else:
                j = layer - n_a
                if j == 0:
                    k_sh = _matmul_rope(x2, w_k, tabs, seq)
                    v_sh = _matmul(x2, w_v, tn=COLS_MATMUL)
                q = _matmul_rope(x2, w_q[j], tabs, seq)
                outs, lses = [], []
                for gi, (window, dilation) in enumerate(ATTN_GROUPS):
                    o, l = _dilated_attention(q, k_sh, v_sh, gi, window, dilation, 1, seq)
                    outs.append(o)
                    lses.append(l)
                x2 = _combine_out_ln(outs, lses, w_o[j], x2, ln_mix_g[layer], ln_mix_b[layer])
            x2 = _peer_ffn_ln(x2, peer_wq[layer], peer_sub_keys[layer], u_pack[layer], v_pack[layer],
                              ln_ffn_g[layer], ln_ffn_b[layer])
        return x2

    return jnp.stack([trunk(x[b]) for b in range(batch)], axis=0)
```

```python
import functools

import jax
import jax.numpy as jnp
from jax import lax
from jax.experimental import pallas as pl
from jax.experimental.pallas import tpu as pltpu
from jax.experimental.pallas import tpu_sc as plsc

f32 = jnp.float32
bf16 = jnp.bfloat16
i32 = jnp.int32

D_MODEL = 1024
DEPTH = 2
ALPHA = (2.0 * DEPTH) ** 0.25
LN_EPS = 1e-5

HEADS = 8
HEAD_DIM = 128
WIDTH = HEADS * HEAD_DIM
CONV_K = 4
DN_CHUNK = 64

ATTN_GROUPS = ((128, 1), (512, 4), (2048, 16))
N_GROUPS = len(ATTN_GROUPS)
ATTN_BLOCK = 128
ROT_DIM = HEAD_DIM // 4
ROPE_THETA = 500000.0

PEER_HEADS = 8
PEER_NKEYS = 128
PEER_TOPK = 16
PEER_QDIM = 256
PEER_SEL = PEER_HEADS * PEER_TOPK

LANES = 128
SUBLANES = 8
VMEM_LIMIT = 48 * 1024 * 1024

ROWS_MATMUL = 512
ROWS_WIDE = 256
ROWS_DELTA = 2 * DN_CHUNK
ROWS_LANEWISE = 2048
COLS_MATMUL = 1024


def _params(sem):
    return pltpu.CompilerParams(dimension_semantics=sem, vmem_limit_bytes=VMEM_LIMIT)


def _dot(a, b):
    return jnp.dot(a.astype(bf16), b.astype(bf16), preferred_element_type=f32)


def _dot_nt(a, b):
    return lax.dot_general(a.astype(bf16), b.astype(bf16), (((1,), (1,)), ((), ())), preferred_element_type=f32)


def _dot_tn(a, b):
    return lax.dot_general(a.astype(bf16), b.astype(bf16), (((0,), (0,)), ((), ())), preferred_element_type=f32)


def _sigmoid(x):
    return 1.0 / (1.0 + jnp.exp(-x))


def _layer_norm(y, g, b):
    mu = jnp.mean(y, -1, keepdims=True)
    yc = y - mu
    var = jnp.mean(yc * yc, -1, keepdims=True)
    return yc * lax.rsqrt(var + LN_EPS) * g + b


def _mm_kernel(a_ref, w_ref, o_ref):
    o_ref[...] = _dot(a_ref[...], w_ref[...])


def _matmul(a, w, tm=ROWS_MATMUL, tn=None):
    m, k = a.shape
    n = w.shape[1]
    tn = tn or n
    return pl.pallas_call(
        _mm_kernel,
        grid=(m // tm, n // tn),
        in_specs=[pl.BlockSpec((tm, k), lambda i, j: (i, 0)), pl.BlockSpec((k, tn), lambda i, j: (0, j))],
        out_specs=pl.BlockSpec((tm, tn), lambda i, j: (i, j)),
        out_shape=jax.ShapeDtypeStruct((m, n), f32),
        compiler_params=_params(("parallel", "parallel")),
        name="matmul",
    )(a, w)


def _rope(y, c, s_lo, s_hi):
    return y * c + pltpu.roll(y, ROT_DIM // 2, axis=1) * s_hi + pltpu.roll(y, LANES - ROT_DIM // 2, axis=1) * s_lo


def _mm_rope_kernel(a_ref, w_ref, c_ref, slo_ref, shi_ref, o_ref, *, heads):
    y = _dot(a_ref[...], w_ref[...])
    c, s_lo, s_hi = c_ref[...], slo_ref[...], shi_ref[...]
    for h in range(heads):
        sl = slice(h * HEAD_DIM, (h + 1) * HEAD_DIM)
        o_ref[:, sl] = _rope(y[:, sl], c, s_lo, s_hi)


def _matmul_rope(a, w, tabs, seq, tm=ROWS_MATMUL, tn=COLS_MATMUL):
    m, k = a.shape
    n = w.shape[1]
    nsb = seq // tm
    tab_spec = pl.BlockSpec((tm, LANES), lambda i, j: (i % nsb, 0))
    return pl.pallas_call(
        functools.partial(_mm_rope_kernel, heads=tn // HEAD_DIM),
        grid=(m // tm, n // tn),
        in_specs=[pl.BlockSpec((tm, k), lambda i, j: (i, 0)), pl.BlockSpec((k, tn), lambda i, j: (0, j)),
                  tab_spec, tab_spec, tab_spec],
        out_specs=pl.BlockSpec((tm, tn), lambda i, j: (i, j)),
        out_shape=jax.ShapeDtypeStruct((m, n), f32),
        compiler_params=_params(("parallel", "parallel")),
        name="matmul_rope",
    )(a, w, *tabs)


def _mm_res_ln_kernel(a_ref, w_ref, x_ref, g_ref, b_ref, o_ref):
    h = _dot(a_ref[...], w_ref[...])
    o_ref[...] = _layer_norm(ALPHA * x_ref[...] + h, g_ref[...], b_ref[...])


def _matmul_res_ln(a, w, x, g, b, tm=ROWS_MATMUL):
    m, k = a.shape
    n = w.shape[1]
    row = pl.BlockSpec((1, n), lambda i: (0, 0))
    return pl.pallas_call(
        _mm_res_ln_kernel,
        grid=(m // tm,),
        in_specs=[pl.BlockSpec((tm, k), lambda i: (i, 0)), pl.BlockSpec((k, n), lambda i: (0, 0)),
                  pl.BlockSpec((tm, n), lambda i: (i, 0)), row, row],
        out_specs=pl.BlockSpec((tm, n), lambda i: (i, 0)),
        out_shape=jax.ShapeDtypeStruct((m, n), f32),
        compiler_params=_params(("parallel",)),
        name="matmul_res_ln",
    )(a, w, x, g.reshape(1, n), b.reshape(1, n))


DN_PROJ_PAD = 4 * WIDTH + LANES


def _dn_prep_kernel(proj_ref, halo_ref, lead_ref, conv_ref, alog_ref, dtb_ref, q_ref, k_ref, v_ref, gate_ref, *, tm):
    i = pl.program_id(0)
    x = proj_ref[:, : 3 * WIDTH]
    halo = jnp.where(i == 0, lead_ref[...], halo_ref[...])
    w = conv_ref[...]
    row8 = lax.broadcasted_iota(i32, (SUBLANES, 1), 0)
    acc = x * w[CONV_K - 1:CONV_K]
    for s in range(1, CONV_K):
        xs = pltpu.roll(x, s, axis=0)
        hs = pltpu.roll(halo, s, axis=0)
        first = jnp.where(row8 < s, hs, xs[:SUBLANES])
        xs = jnp.concatenate([first, xs[SUBLANES:]], axis=0)
        acc = acc + xs * w[CONV_K - 1 - s:CONV_K - s]
    qkv = acc * _sigmoid(acc)
    for h in range(HEADS):
        sl = slice(h * HEAD_DIM, (h + 1) * HEAD_DIM)
        qh = qkv[:, sl]
        q_ref[:, sl] = qh * lax.rsqrt(jnp.sum(qh * qh, -1, keepdims=True) + 1e-6) * (HEAD_DIM ** -0.5)
        kh = qkv[:, WIDTH + h * HEAD_DIM: WIDTH + (h + 1) * HEAD_DIM]
        k_ref[:, sl] = kh * lax.rsqrt(jnp.sum(kh * kh, -1, keepdims=True) + 1e-6)
    v_ref[...] = qkv[:, 2 * WIDTH:]
    ab = proj_ref[:, 4 * WIDTH:]
    z = ab + dtb_ref[...]
    softplus = jnp.maximum(z, 0.0) + jnp.log(1.0 + jnp.exp(-jnp.abs(z)))
    g = -jnp.exp(alog_ref[...]) * softplus
    row = lax.broadcasted_iota(i32, (tm, 1), 0) % DN_CHUNK
    s = 1
    while s < DN_CHUNK:
        g = g + jnp.where(row >= s, pltpu.roll(g, s, axis=0), 0.0)
        s *= 2
    lane = lax.broadcasted_iota(i32, (1, LANES), 1)
    gate_ref[...] = jnp.where(lane < HEADS, g, _sigmoid(ab))


def _dn_prep(proj, lead, conv_w, a_log, dt_bias, tm=ROWS_WIDE):
    m = proj.shape[0]
    pad = LANES - HEADS
    alog = jnp.pad(a_log.astype(f32), (0, pad)).reshape(1, LANES)
    dtb = jnp.pad(dt_bias.astype(f32), (0, pad)).reshape(1, LANES)
    hb = tm // SUBLANES
    out_w = pl.BlockSpec((tm, WIDTH), lambda i: (i, 0))
    return pl.pallas_call(
        functools.partial(_dn_prep_kernel, tm=tm),
        grid=(m // tm,),
        in_specs=[pl.BlockSpec((tm, DN_PROJ_PAD), lambda i: (i, 0)),
                  pl.BlockSpec((SUBLANES, 3 * WIDTH), lambda i: (jnp.maximum(i * hb - 1, 0), 0)),
                  pl.BlockSpec((SUBLANES, 3 * WIDTH), lambda i: (0, 0)),
                  pl.BlockSpec((CONV_K, 3 * WIDTH), lambda i: (0, 0)),
                  pl.BlockSpec((1, LANES), lambda i: (0, 0)), pl.BlockSpec((1, LANES), lambda i: (0, 0))],
        out_specs=[out_w, out_w, out_w, pl.BlockSpec((tm, LANES), lambda i: (i, 0))],
        out_shape=[jax.ShapeDtypeStruct((m, WIDTH), f32)] * 3 + [jax.ShapeDtypeStruct((m, LANES), f32)],
        compiler_params=_params(("parallel",)),
        name="dn_prep",
    )(proj, proj, lead, conv_w, alog, dtb)


def _dn_delta_kernel(q_ref, k_ref, v_ref, z_ref, gate_ref, gt_ref, nw_ref, s0_ref, o_ref, state_ref, *, chunks):
    @pl.when(pl.program_id(0) == 0)
    def _():
        state_ref[...] = s0_ref[...]

    c = DN_CHUNK
    ri = lax.broadcasted_iota(i32, (c, c), 0)
    ci = lax.broadcasted_iota(i32, (c, c), 1)
    causal = ri >= ci
    strict = ri > ci
    eye = (ri == ci).astype(f32)
    nw = nw_ref[...]
    inst = [(ch, h) for ch in range(chunks) for h in range(HEADS)]
    pre = {}
    for ch, h in inst:
        rows = slice(ch * c, (ch + 1) * c)
        sl = slice(h * HEAD_DIM, (h + 1) * HEAD_DIM)
        q, k, v = q_ref[rows, sl], k_ref[rows, sl], v_ref[rows, sl]
        gcol = gate_ref[rows, h:h + 1]
        beta = gate_ref[rows, HEADS + h:HEADS + h + 1]
        grow = gt_ref[h:h + 1, rows]
        decay = jnp.where(causal, jnp.exp(jnp.where(causal, gcol - grow, 0.0)), 0.0)
        kb = k * beta
        egc = jnp.exp(gcol)
        glast = gcol[c - 1:c, :]
        qk = _dot_nt(jnp.concatenate([q, kb], axis=0), k)
        m = jnp.where(strict, qk[c:] * decay, 0.0)
        pre[ch, h] = dict(a_qk=qk[:c] * decay, inv=eye - m, p=m,
                          rhs=jnp.concatenate([v * beta, kb * egc], axis=1), qe=q * egc,
                          kd=k * jnp.exp(glast - gcol), eg=jnp.exp(glast))
    step = 2
    while step < c:
        for key in inst:
            pre[key]["p"] = _dot(pre[key]["p"], pre[key]["p"])
        for key in inst:
            pre[key]["inv"] = pre[key]["inv"] + _dot(pre[key]["inv"], pre[key]["p"])
        step *= 2
    for key in inst:
        pre[key]["uw"] = _dot(pre[key]["inv"], pre[key]["rhs"])
    states = [state_ref[h] for h in range(HEADS)]
    for ch in range(chunks):
        rows = slice(ch * c, (ch + 1) * c)
        for h in range(HEADS):
            sl = slice(h * HEAD_DIM, (h + 1) * HEAD_DIM)
            d = pre[ch, h]
            u, w = d["uw"][:, :HEAD_DIM], d["uw"][:, HEAD_DIM:]
            ws = _dot(jnp.concatenate([w, d["qe"]], axis=0), states[h])
            v_new = u - ws[:c]
            o = ws[c:] + _dot(d["a_qk"], v_new)
            states[h] = states[h] * d["eg"] + _dot_tn(d["kd"], v_new)
            o = o * lax.rsqrt(jnp.mean(o * o, -1, keepdims=True) + 1e-6) * nw
            zz = z_ref[rows, sl]
            o_ref[rows, sl] = o * (zz * _sigmoid(zz))
    for h in range(HEADS):
        state_ref[h] = states[h]


def _dn_delta(q, k, v, proj, gates, gates_t, norm_w, state, cb=ROWS_DELTA):
    m = q.shape[0]
    wide = pl.BlockSpec((cb, WIDTH), lambda i: (i, 0))
    whole_state = pl.BlockSpec(state.shape, lambda i: (0, 0, 0))
    return pl.pallas_call(
        functools.partial(_dn_delta_kernel, chunks=cb // DN_CHUNK),
        grid=(m // cb,),
        in_specs=[wide, wide, wide,
                  pl.BlockSpec((cb, WIDTH), lambda i: (i, 3)),
                  pl.BlockSpec((cb, LANES), lambda i: (i, 0)),
                  pl.BlockSpec((SUBLANES, cb), lambda i: (0, i)),
                  pl.BlockSpec((1, HEAD_DIM), lambda i: (0, 0)),
                  whole_state],
        out_specs=[wide, whole_state],
        out_shape=[jax.ShapeDtypeStruct((m, WIDTH), f32), jax.ShapeDtypeStruct(state.shape, f32)],
        compiler_params=_params(("arbitrary",)),
        name="dn_delta",
    )(q, k, v, proj, gates, gates_t, norm_w.reshape(1, HEAD_DIM).astype(f32), state)


def _deltanet_mixer(x2, w_pad, conv_w, a_log, dt_bias, norm_w, lead, state):
    proj = _matmul(x2, w_pad, tn=DN_PROJ_PAD // 3)
    q, k, v, gates = _dn_prep(proj, lead, conv_w, a_log, dt_bias)
    gates_t = gates[:, :SUBLANES].T
    o, state = _dn_delta(q, k, v, proj, gates, gates_t, norm_w, state)
    return o, proj[-SUBLANES:, :3 * WIDTH], state


def _attn_kernel(q_ref, kp_ref, kc_ref, vp_ref, vc_ref, o_ref, l_ref, *, steps, dilation, heads):
    nbi = pl.program_id(1)
    blk = ATTN_BLOCK
    qi = lax.broadcasted_iota(i32, (blk, 2 * blk), 0)
    kj = lax.broadcasted_iota(i32, (blk, 2 * blk), 1)
    dist = qi + blk - kj
    mask = (dist >= 0) & (dist <= steps) & ((kj >= blk) | (nbi > 0))
    scale = HEAD_DIM ** -0.5
    for r in range(dilation):
        rows = pl.ds(r, blk, stride=dilation) if dilation > 1 else slice(None)
        for h in range(heads):
            sl = slice(h * HEAD_DIM, (h + 1) * HEAD_DIM)
            q = q_ref[rows, sl]
            kk = jnp.concatenate([kp_ref[rows, sl], kc_ref[rows, sl]], axis=0)
            vv = jnp.concatenate([vp_ref[rows, sl], vc_ref[rows, sl]], axis=0)
            sc = jnp.where(mask, _dot_nt(q, kk) * scale, -jnp.inf)
            mx = jnp.max(sc, -1, keepdims=True)
            p = jnp.exp(sc - mx)
            den = jnp.sum(p, -1, keepdims=True)
            o_ref[rows, sl] = _dot(p, vv) / den
            l_ref[rows, sl] = jnp.broadcast_to(mx + jnp.log(den), (blk, HEAD_DIM))


def _dilated_attention(q, k, v, group, window, dilation, batch, seq):
    rows = ATTN_BLOCK * dilation
    nb = seq // rows
    heads = HEADS if dilation == 1 else 1
    hblocks = HEADS // heads
    blk = (rows, heads * HEAD_DIM)

    def cur(b, i, j):
        return (b * nb + i, group * hblocks + j)

    def prev(b, i, j):
        return (b * nb + jnp.maximum(i - 1, 0), group * hblocks + j)

    out_spec = pl.BlockSpec(blk, lambda b, i, j: (b * nb + i, j))
    return pl.pallas_call(
        functools.partial(_attn_kernel, steps=window // dilation, dilation=dilation, heads=heads),
        grid=(batch, nb, hblocks),
        in_specs=[pl.BlockSpec(blk, cur), pl.BlockSpec(blk, prev), pl.BlockSpec(blk, cur),
                  pl.BlockSpec(blk, prev), pl.BlockSpec(blk, cur)],
        out_specs=[out_spec, out_spec],
        out_shape=[jax.ShapeDtypeStruct((batch * seq, WIDTH), f32)] * 2,
        compiler_params=_params(("parallel", "parallel", "parallel")),
        name=f"dilated_attn_d{dilation}",
    )(q, k, k, v, v)


def _combine_out_ln_kernel(o0, o1, o2, l0, l1, l2, w_ref, x_ref, g_ref, b_ref, out_ref):
    ls = [l0[...], l1[...], l2[...]]
    mx = jnp.maximum(jnp.maximum(ls[0], ls[1]), ls[2])
    es = [jnp.exp(l - mx) for l in ls]
    o = (es[0] * o0[...] + es[1] * o1[...] + es[2] * o2[...]) / (es[0] + es[1] + es[2])
    h = _dot(o, w_ref[...])
    out_ref[...] = _layer_norm(ALPHA * x_ref[...] + h, g_ref[...], b_ref[...])


def _combine_out_ln(outs, lses, w, x, g, b, tm=ROWS_WIDE):
    m, n = x.shape
    blk = pl.BlockSpec((tm, n), lambda i: (i, 0))
    row = pl.BlockSpec((1, n), lambda i: (0, 0))
    return pl.pallas_call(
        _combine_out_ln_kernel,
        grid=(m // tm,),
        in_specs=[blk] * 6 + [pl.BlockSpec((WIDTH, n), lambda i: (0, 0)), blk, row, row],
        out_specs=blk,
        out_shape=jax.ShapeDtypeStruct((m, n), f32),
        compiler_params=_params(("parallel",)),
        name="attn_combine_out_ln",
    )(*outs, *lses, w, x, g.reshape(1, n), b.reshape(1, n))


def _rope_lane_tables(seq):
    half = ROT_DIM // 2
    inv_freq = ROPE_THETA ** (-jnp.arange(half, dtype=f32) * 2.0 / ROT_DIM)
    ang = jnp.arange(seq, dtype=f32)[:, None] * inv_freq[None, :]
    cos, sin = jnp.cos(ang), jnp.sin(ang)
    ones = jnp.ones((seq, HEAD_DIM - ROT_DIM), f32)
    zeros = jnp.zeros((seq, HEAD_DIM - half), f32)
    c = jnp.concatenate([cos, cos, ones], axis=1)
    s_lo = jnp.concatenate([-sin, zeros], axis=1)
    s_hi = jnp.concatenate([jnp.zeros((seq, half), f32), sin, jnp.zeros((seq, HEAD_DIM - ROT_DIM), f32)], axis=1)
    return c, s_lo, s_hi


def _top_rows(s, order, payload, k):
    big = jnp.int32(2 ** 30)
    vals, pays = [], []
    for _ in range(k):
        m = jnp.max(s, axis=0, keepdims=True)
        pos = jnp.min(jnp.where(s == m, order, big), axis=0, keepdims=True)
        sel = order == pos
        pays.append(pos if payload is None else jnp.max(jnp.where(sel, payload, -1), axis=0, keepdims=True))
        vals.append(m)
        s = jnp.where(sel, -jnp.inf, s)
    return jnp.concatenate(vals, axis=0), jnp.concatenate(pays, axis=0)


def _pair_candidates(t1, t2, combine, fill):
    kt = PEER_TOPK
    row = lax.broadcasted_iota(i32, (SUBLANES, LANES), 0)
    bc = lambda t, r: jnp.broadcast_to(t[r:r + 1], (SUBLANES, LANES))
    lo1, hi1, lo2, hi2 = t1[:SUBLANES], t1[SUBLANES:], t2[:SUBLANES], t2[SUBLANES:]
    tiles = [combine(bc(t1, 0), lo2), combine(bc(t1, 0), hi2), combine(bc(t1, 1), lo2),
             jnp.where(row < kt // 3, combine(bc(t1, 2), lo2), fill),
             jnp.where(row < kt // 4, combine(bc(t1, 3), lo2), fill),
             combine(hi1, bc(t2, 0)),
             jnp.where(row >= 4, combine(lo1, bc(t2, 0)), fill),
             jnp.where(row >= 4, combine(lo1, bc(t2, 1)), jnp.where(row == 0, combine(bc(t1, 4), bc(t2, 2)), fill))]
    return jnp.concatenate(tiles, axis=0)


def _pair_rank():
    kt = PEER_TOPK
    row = lax.broadcasted_iota(i32, (SUBLANES, LANES), 0)
    unused = kt * kt + row
    tiles = [row, row + SUBLANES, kt + row,
             jnp.where(row < kt // 3, 2 * kt + row, unused),
             jnp.where(row < kt // 4, 3 * kt + row, unused + SUBLANES),
             (row + SUBLANES) * kt,
             jnp.where(row >= 4, row * kt, unused + 2 * SUBLANES),
             jnp.where(row >= 4, row * kt + 1, jnp.where(row == 0, 4 * kt + 2, unused + 3 * SUBLANES))]
    return jnp.concatenate(tiles, axis=0)


def _peer_route_kernel(qp_ref, keys_ref, idx_ref, gate_ref):
    kt = PEER_TOPK
    half = PEER_QDIM // 2
    keyid = lax.broadcasted_iota(i32, (PEER_NKEYS, LANES), 0)
    rank = _pair_rank()
    idx_rows, gate_rows = [], []
    for h in range(PEER_HEADS):
        tops = []
        for c in range(2):
            col = (h * 2 + c) * half
            s = _dot_nt(keys_ref[h * 2 + c], qp_ref[:, col:col + half])
            tops.append(_top_rows(s, keyid, None, kt))
        (v1, i1), (v2, i2) = tops
        cand = _pair_candidates(v1, v2, lambda a, b: a + b, -jnp.inf)
        cand_id = _pair_candidates(i1, i2, lambda a, b: a * PEER_NKEYS + b, -1)
        best, ids = _top_rows(cand, rank, cand_id, kt)
        e = jnp.exp(best - best[0:1])
        gate_rows.append(e / jnp.sum(e, axis=0, keepdims=True))
        idx_rows.append(ids)
    idx_ref[...] = jnp.concatenate(idx_rows, axis=0).astype(f32).T.astype(i32)
    gate_ref[...] = jnp.concatenate(gate_rows, axis=0).T


def _peer_route(qp, sub_keys):
    m = qp.shape[0]
    tb = LANES
    keys = sub_keys.reshape(PEER_HEADS * 2, PEER_NKEYS, PEER_QDIM // 2).astype(bf16)
    out = pl.BlockSpec((tb, PEER_SEL), lambda i: (i, 0))
    return pl.pallas_call(
        _peer_route_kernel,
        grid=(m // tb,),
        in_specs=[pl.BlockSpec((tb, qp.shape[1]), lambda i: (i, 0)),
                  pl.BlockSpec(keys.shape, lambda i: (0, 0, 0))],
        out_specs=[out, out],
        out_shape=[jax.ShapeDtypeStruct((m, PEER_SEL), i32), jax.ShapeDtypeStruct((m, PEER_SEL), f32)],
        compiler_params=_params(("parallel",)),
        name="peer_route",
    )(qp, keys)


SC_CORES = 2
SC_SUBCORES = 16
SC_LANES = 16
SC_WORKERS = SC_CORES * SC_SUBCORES
SC_CHUNKS = D_MODEL // SC_LANES
SC_WORDS = D_MODEL // 2
SC_WORD_CHUNKS = SC_WORDS // SC_LANES
SC_TOK_GROUP = 32
SC_ROWS = 64
SC_GATHERS_PER_TOK = PEER_SEL // SC_ROWS
SC_GATHERS = SC_TOK_GROUP * SC_GATHERS_PER_TOK
SC_ROW_GROUP = 8


def _pack_table(tab):
    e = tab.shape[0]
    bits = lax.bitcast_convert_type(tab.astype(bf16), jnp.uint16).astype(jnp.uint32)
    bits = bits.reshape(e, SC_WORD_CHUNKS, 2, SC_LANES)
    word = bits[:, :, 0, :] | (bits[:, :, 1, :] << 16)
    return lax.bitcast_convert_type(word.reshape(e, SC_WORDS), i32)


def _unpack_words(w):
    return lax.bitcast_convert_type(w << 16, f32), lax.bitcast_convert_type(w & (-65536), f32)


def _sc_mesh():
    return plsc.VectorSubcoreMesh(core_axis_name="c", subcore_axis_name="s")


def _sc_worker():
    return lax.axis_index("s") * SC_CORES + lax.axis_index("c")


def _sc_gather_loop(gather, compute):
    gather(0, 0).start()

    @pl.loop(0, SC_GATHERS, step=2)
    def _(q):
        gather(q + 1, 1).start()
        gather(q, 0).wait()
        compute(q, 0)

        @pl.when(q + 2 < SC_GATHERS)
        def _():
            gather(q + 2, 0).start()
        gather(q + 1, 1).wait()
        compute(q + 1, 1)


def _sc_scratch(stage_shape):
    return [pltpu.VMEM((SC_GATHERS, SC_ROWS), i32), pltpu.VMEM(stage_shape[0], f32), pltpu.VMEM(stage_shape[1], f32),
            pltpu.VMEM((SC_ROWS, SC_WORDS), i32), pltpu.VMEM((SC_ROWS, SC_WORDS), i32),
            pltpu.SemaphoreType.DMA, pltpu.SemaphoreType.DMA]


def _peer_udot(x, idx, u_tab):
    t_all = x.shape[0]
    tpw = t_all // SC_WORKERS
    idx2 = idx.reshape(t_all * SC_GATHERS_PER_TOK, SC_ROWS)

    @functools.partial(
        pl.kernel, mesh=_sc_mesh(), compiler_params=pltpu.CompilerParams(needs_layout_passes=False),
        out_type=jax.ShapeDtypeStruct((t_all, PEER_SEL), f32),
        scratch_types=_sc_scratch(((SC_TOK_GROUP, D_MODEL), (SC_TOK_GROUP, PEER_SEL))), name="peer_udot_sc")
    def k(x_hbm, idx_hbm, u_hbm, pre_hbm, idx_v, x_v, pre_v, rows0, rows1, sem0, sem1):
        bufs = ((rows0, sem0), (rows1, sem1))
        base = _sc_worker() * tpw
        lane = lax.iota(i32, SC_LANES)
        zero = jnp.zeros((SC_LANES,), f32)

        def gather(q, b):
            return pltpu.make_async_copy(u_hbm.at[idx_v.at[q]], bufs[b][0], bufs[b][1])

        def compute(q, b):
            rows = bufs[b][0]
            t = q // SC_GATHERS_PER_TOK
            col0 = (q % SC_GATHERS_PER_TOK) * SC_ROWS
            for hh in range(SC_ROWS // SC_LANES):
                vec = zero
                for rg in range(SC_LANES // SC_ROW_GROUP):
                    r0 = hh * SC_LANES + rg * SC_ROW_GROUP

                    def body(c, acc):
                        x_lo = x_v[t, pl.ds(c * 2 * SC_LANES, SC_LANES)]
                        x_hi = x_v[t, pl.ds(c * 2 * SC_LANES + SC_LANES, SC_LANES)]
                        out = []
                        for r in range(SC_ROW_GROUP):
                            lo, hi = _unpack_words(rows[r0 + r, pl.ds(c * SC_LANES, SC_LANES)])
                            out.append(acc[r] + lo * x_lo + hi * x_hi)
                        return tuple(out)

                    acc = lax.fori_loop(0, SC_WORD_CHUNKS, body, (zero,) * SC_ROW_GROUP, unroll=2)
                    for r in range(SC_ROW_GROUP):
                        vec = jnp.where(lane == rg * SC_ROW_GROUP + r, jnp.sum(acc[r]), vec)
                pre_v[t, pl.ds(col0 + hh * SC_LANES, SC_LANES)] = vec

        @pl.loop(0, tpw // SC_TOK_GROUP)
        def _(g):
            tok0 = base + g * SC_TOK_GROUP
            pltpu.sync_copy(idx_hbm.at[pl.ds(tok0 * SC_GATHERS_PER_TOK, SC_GATHERS)], idx_v)
            pltpu.sync_copy(x_hbm.at[pl.ds(tok0, SC_TOK_GROUP)], x_v)
            _sc_gather_loop(gather, compute)
            pltpu.sync_copy(pre_v, pre_hbm.at[pl.ds(tok0, SC_TOK_GROUP)])

    return k(x, idx2, u_tab)


def _peer_vacc(w, idx, v_tab):
    t_all = w.shape[0]
    tpw = t_all // SC_WORKERS
    idx2 = idx.reshape(t_all * SC_GATHERS_PER_TOK, SC_ROWS)
    half_w = SC_WORD_CHUNKS // 2

    @functools.partial(
        pl.kernel, mesh=_sc_mesh(), compiler_params=pltpu.CompilerParams(needs_layout_passes=False),
        out_type=jax.ShapeDtypeStruct((t_all, D_MODEL), f32),
        scratch_types=_sc_scratch(((SC_TOK_GROUP, PEER_SEL), (SC_TOK_GROUP, D_MODEL))), name="peer_vacc_sc")
    def k(w_hbm, idx_hbm, v_hbm, f_hbm, idx_v, w_v, f_v, rows0, rows1, sem0, sem1):
        bufs = ((rows0, sem0), (rows1, sem1))
        base = _sc_worker() * tpw
        zero = jnp.zeros((SC_LANES,), f32)

        def gather(q, b):
            return pltpu.make_async_copy(v_hbm.at[idx_v.at[q]], bufs[b][0], bufs[b][1])

        def compute(q, b):
            rows = bufs[b][0]
            t = q // SC_GATHERS_PER_TOK
            col0 = (q % SC_GATHERS_PER_TOK) * SC_ROWS
            tvec = jnp.full((SC_LANES,), t, i32)
            for half in range(2):
                def body(r, acc):
                    wr = plsc.load_gather(w_v, [tvec, jnp.full((SC_LANES,), col0 + r, i32)])
                    out = []
                    for c in range(half_w):
                        lo, hi = _unpack_words(rows[r, pl.ds((half * half_w + c) * SC_LANES, SC_LANES)])
                        out.append(acc[2 * c] + wr * lo)
                        out.append(acc[2 * c + 1] + wr * hi)
                    return tuple(out)

                acc = lax.fori_loop(0, SC_ROWS, body, (zero,) * (2 * half_w))
                for c in range(2 * half_w):
                    sl = pl.ds((half * 2 * half_w + c) * SC_LANES, SC_LANES)
                    f_v[t, sl] = f_v[t, sl] + acc[c]

        @pl.loop(0, tpw // SC_TOK_GROUP)
        def _(g):
            tok0 = base + g * SC_TOK_GROUP
            pltpu.sync_copy(idx_hbm.at[pl.ds(tok0 * SC_GATHERS_PER_TOK, SC_GATHERS)], idx_v)
            pltpu.sync_copy(w_hbm.at[pl.ds(tok0, SC_TOK_GROUP)], w_v)

            @pl.loop(0, SC_TOK_GROUP)
            def _(t):
                @pl.loop(0, SC_CHUNKS)
                def _(c):
                    f_v[t, pl.ds(c * SC_LANES, SC_LANES)] = zero

            _sc_gather_loop(gather, compute)
            pltpu.sync_copy(f_v, f_hbm.at[pl.ds(tok0, SC_TOK_GROUP)])

    return k(w, idx2, v_tab)


def _peer_act_kernel(pre_ref, gate_ref, w_ref):
    pre = pre_ref[...]
    w_ref[...] = gate_ref[...] * (0.5 * pre * (1.0 + lax.erf(pre * (2.0 ** -0.5))))


def _peer_act(pre, gate, tm=ROWS_LANEWISE):
    m, n = pre.shape
    blk = pl.BlockSpec((tm, n), lambda i: (i, 0))
    return pl.pallas_call(
        _peer_act_kernel, grid=(m // tm,), in_specs=[blk, blk], out_specs=blk,
        out_shape=jax.ShapeDtypeStruct((m, n), f32), compiler_params=_params(("parallel",)), name="peer_act",
    )(pre, gate)


def _res_ln_kernel(x_ref, f_ref, g_ref, b_ref, o_ref):
    o_ref[...] = _layer_norm(ALPHA * x_ref[...] + f_ref[...], g_ref[...], b_ref[...])


def _res_ln(x, f, g, b, tm=ROWS_MATMUL):
    m, n = x.shape
    blk = pl.BlockSpec((tm, n), lambda i: (i, 0))
    row = pl.BlockSpec((1, n), lambda i: (0, 0))
    return pl.pallas_call(
        _res_ln_kernel, grid=(m // tm,), in_specs=[blk, blk, row, row], out_specs=blk,
        out_shape=jax.ShapeDtypeStruct((m, n), f32), compiler_params=_params(("parallel",)), name="res_ln",
    )(x, f, g.reshape(1, n), b.reshape(1, n))


def _peer_expert_ln(x, idx, gate, u_tab, v_tab, g, b):
    pre = _peer_udot(x, idx, u_tab)
    w = _peer_act(pre, gate)
    f = _peer_vacc(w, idx, v_tab)
    return _res_ln(x, f, g, b)


PEER_SPLITS = 2
DN_SEGMENTS = 4


def _peer_ffn_ln(x2, w_query, sub_keys, u_pack, v_pack, g, b, splits=PEER_SPLITS):
    outs = []
    for xs in jnp.split(x2, splits, axis=0):
        qp = _matmul(xs, w_query)
        idx, gate = _peer_route(qp, sub_keys)
        outs.append(_peer_expert_ln(xs, idx, gate, u_pack, v_pack, g, b))
    return jnp.concatenate(outs, axis=0)


def kernel(x, dn_w_in, dn_conv, dn_a_log, dn_dt_bias, dn_norm_w, dn_w_out, shared_w_kv, attn_w_q, attn_w_out,
           peer_w_query, peer_sub_keys, peer_u, peer_v, ln_mix_g, ln_mix_b, ln_ffn_g, ln_ffn_b):
    batch, seq, d = x.shape
    n_a = DEPTH // 2
    tabs = _rope_lane_tables(seq)
    gw = N_GROUPS * WIDTH
    dn_w_pad = [jnp.pad(w, ((0, 0), (0, DN_PROJ_PAD - w.shape[1]))).astype(bf16) for w in dn_w_in]
    dn_w_out_b = dn_w_out.astype(bf16)
    w_k, w_v = shared_w_kv[:, :gw].astype(bf16), shared_w_kv[:, gw:].astype(bf16)
    w_q, w_o = attn_w_q.astype(bf16), attn_w_out.astype(bf16)
    peer_wq = peer_w_query.astype(bf16)
    u_pack = [_pack_table(t) for t in peer_u]
    v_pack = [_pack_table(t) for t in peer_v]

    def trunk(x2):
        k_sh = v_sh = None
        for layer in range(DEPTH):
            if layer < n_a:
                lead = jnp.zeros((SUBLANES, 3 * WIDTH), f32)
                state = jnp.zeros((HEADS, HEAD_DIM, HEAD_DIM), f32)
                segs = []
                for xs in jnp.split(x2, DN_SEGMENTS, axis=0):
                    o, lead, state = _deltanet_mixer(xs, dn_w_pad[layer], dn_conv[layer], dn_a_log[layer],
                                                     dn_dt_bias[layer], dn_norm_w[layer], lead, state)
                    xs = _matmul_res_ln(o, dn_w_out_b[layer], xs, ln_mix_g[layer], ln_mix_b[layer])
                    segs.append(_peer_ffn_ln(xs, peer_wq[layer], peer_sub_keys[layer], u_pack[layer], v_pack[layer],
                                             ln_ffn_g[layer], ln_ffn_b[layer], splits=1))
                x2 = jnp.concatenate(segs, axis=0)
                continue
            else:
                j = layer - n_a
                if j == 0:
                    k_sh = _matmul_rope(x2, w_k, tabs, seq)
                    v_sh = _matmul(x2, w_v, tn=COLS_MATMUL)
                q = _matmul_rope(x2, w_q[j], tabs, seq)
                outs, lses = [], []
                for gi, (window, dilation) in enumerate(ATTN_GROUPS):
                    o, l = _dilated_attention(q, k_sh, v_sh, gi, window, dilation, 1, seq)
                    outs.append(o)
                    lses.append(l)
                x2 = _combine_out_ln(outs, lses, w_o[j], x2, ln_mix_g[layer], ln_mix_b[layer])
            x2 = _peer_ffn_ln(x2, peer_wq[layer], peer_sub_keys[layer], u_pack[layer], v_pack[layer],
                              ln_ffn_g[layer], ln_ffn_b[layer])
        return x2

    return jnp.stack([trunk(x[b]) for b in range(batch)], axis=0)
```

```python
import functools

import jax
import jax.numpy as jnp
from jax import lax
from jax.experimental import pallas as pl
from jax.experimental.pallas import tpu as pltpu
from jax.experimental.pallas import tpu_sc as plsc

f32 = jnp.float32
bf16 = jnp.bfloat16
i32 = jnp.int32

D_MODEL = 1024
DEPTH = 2
ALPHA = (2.0 * DEPTH) ** 0.25
LN_EPS = 1e-5

HEADS = 8
HEAD_DIM = 128
WIDTH = HEADS * HEAD_DIM
CONV_K = 4
DN_CHUNK = 64

ATTN_GROUPS = ((128, 1), (512, 4), (2048, 16))
N_GROUPS = len(ATTN_GROUPS)
ATTN_BLOCK = 128
ROT_DIM = HEAD_DIM // 4
ROPE_THETA = 500000.0

PEER_HEADS = 8
PEER_NKEYS = 128
PEER_TOPK = 16
PEER_QDIM = 256
PEER_SEL = PEER_HEADS * PEER_TOPK

LANES = 128
SUBLANES = 8
VMEM_LIMIT = 48 * 1024 * 1024

ROWS_MATMUL = 512
ROWS_WIDE = 256
ROWS_DELTA = 2 * DN_CHUNK
ROWS_LANEWISE = 2048
COLS_MATMUL = 1024


def _params(sem):
    return pltpu.CompilerParams(dimension_semantics=sem, vmem_limit_bytes=VMEM_LIMIT)


def _dot(a, b):
    return jnp.dot(a.astype(bf16), b.astype(bf16), preferred_element_type=f32)


def _dot_nt(a, b):
    return lax.dot_general(a.astype(bf16), b.astype(bf16), (((1,), (1,)), ((), ())), preferred_element_type=f32)


def _dot_tn(a, b):
    return lax.dot_general(a.astype(bf16), b.astype(bf16), (((0,), (0,)), ((), ())), preferred_element_type=f32)


def _sigmoid(x):
    return 1.0 / (1.0 + jnp.exp(-x))


def _layer_norm(y, g, b):
    mu = jnp.mean(y, -1, keepdims=True)
    yc = y - mu
    var = jnp.mean(yc * yc, -1, keepdims=True)
    return yc * lax.rsqrt(var + LN_EPS) * g + b


def _mm_kernel(a_ref, w_ref, o_ref):
    o_ref[...] = _dot(a_ref[...], w_ref[...])


def _matmul(a, w, tm=ROWS_MATMUL, tn=None):
    m, k = a.shape
    n = w.shape[1]
    tn = tn or n
    return pl.pallas_call(
        _mm_kernel,
        grid=(m // tm, n // tn),
        in_specs=[pl.BlockSpec((tm, k), lambda i, j: (i, 0)), pl.BlockSpec((k, tn), lambda i, j: (0, j))],
        out_specs=pl.BlockSpec((tm, tn), lambda i, j: (i, j)),
        out_shape=jax.ShapeDtypeStruct((m, n), f32),
        compiler_params=_params(("parallel", "parallel")),
        name="matmul",
    )(a, w)


def _rope(y, c, s_lo, s_hi):
    return y * c + pltpu.roll(y, ROT_DIM // 2, axis=1) * s_hi + pltpu.roll(y, LANES - ROT_DIM // 2, axis=1) * s_lo


def _mm_rope_kernel(a_ref, w_ref, c_ref, slo_ref, shi_ref, o_ref, *, heads):
    y = _dot(a_ref[...], w_ref[...])
    c, s_lo, s_hi = c_ref[...], slo_ref[...], shi_ref[...]
    for h in range(heads):
        sl = slice(h * HEAD_DIM, (h + 1) * HEAD_DIM)
        o_ref[:, sl] = _rope(y[:, sl], c, s_lo, s_hi)


def _matmul_rope(a, w, tabs, seq, tm=ROWS_MATMUL, tn=COLS_MATMUL):
    m, k = a.shape
    n = w.shape[1]
    nsb = seq // tm
    tab_spec = pl.BlockSpec((tm, LANES), lambda i, j: (i % nsb, 0))
    return pl.pallas_call(
        functools.partial(_mm_rope_kernel, heads=tn // HEAD_DIM),
        grid=(m // tm, n // tn),
        in_specs=[pl.BlockSpec((tm, k), lambda i, j: (i, 0)), pl.BlockSpec((k, tn), lambda i, j: (0, j)),
                  tab_spec, tab_spec, tab_spec],
        out_specs=pl.BlockSpec((tm, tn), lambda i, j: (i, j)),
        out_shape=jax.ShapeDtypeStruct((m, n), f32),
        compiler_params=_params(("parallel", "parallel")),
        name="matmul_rope",
    )(a, w, *tabs)


def _mm_res_ln_kernel(a_ref, w_ref, x_ref, g_ref, b_ref, o_ref):
    h = _dot(a_ref[...], w_ref[...])
    o_ref[...] = _layer_norm(ALPHA * x_ref[...] + h, g_ref[...], b_ref[...])


def _matmul_res_ln(a, w, x, g, b, tm=ROWS_MATMUL):
    m, k = a.shape
    n = w.shape[1]
    row = pl.BlockSpec((1, n), lambda i: (0, 0))
    return pl.pallas_call(
        _mm_res_ln_kernel,
        grid=(m // tm,),
        in_specs=[pl.BlockSpec((tm, k), lambda i: (i, 0)), pl.BlockSpec((k, n), lambda i: (0, 0)),
                  pl.BlockSpec((tm, n), lambda i: (i, 0)), row, row],
        out_specs=pl.BlockSpec((tm, n), lambda i: (i, 0)),
        out_shape=jax.ShapeDtypeStruct((m, n), f32),
        compiler_params=_params(("parallel",)),
        name="matmul_res_ln",
    )(a, w, x, g.reshape(1, n), b.reshape(1, n))


DN_PROJ_PAD = 4 * WIDTH + LANES


def _dn_prep_kernel(proj_ref, halo_ref, lead_ref, conv_ref, alog_ref, dtb_ref, q_ref, k_ref, v_ref, gate_ref, *, tm):
    i = pl.program_id(0)
    x = proj_ref[:, : 3 * WIDTH]
    halo = jnp.where(i == 0, lead_ref[...], halo_ref[...])
    w = conv_ref[...]
    row8 = lax.broadcasted_iota(i32, (SUBLANES, 1), 0)
    acc = x * w[CONV_K - 1:CONV_K]
    for s in range(1, CONV_K):
        xs = pltpu.roll(x, s, axis=0)
        hs = pltpu.roll(halo, s, axis=0)
        first = jnp.where(row8 < s, hs, xs[:SUBLANES])
        xs = jnp.concatenate([first, xs[SUBLANES:]], axis=0)
        acc = acc + xs * w[CONV_K - 1 - s:CONV_K - s]
    qkv = acc * _sigmoid(acc)
    for h in range(HEADS):
        sl = slice(h * HEAD_DIM, (h + 1) * HEAD_DIM)
        qh = qkv[:, sl]
        q_ref[:, sl] = qh * lax.rsqrt(jnp.sum(qh * qh, -1, keepdims=True) + 1e-6) * (HEAD_DIM ** -0.5)
        kh = qkv[:, WIDTH + h * HEAD_DIM: WIDTH + (h + 1) * HEAD_DIM]
        k_ref[:, sl] = kh * lax.rsqrt(jnp.sum(kh * kh, -1, keepdims=True) + 1e-6)
    v_ref[...] = qkv[:, 2 * WIDTH:]
    ab = proj_ref[:, 4 * WIDTH:]
    z = ab + dtb_ref[...]
    softplus = jnp.maximum(z, 0.0) + jnp.log(1.0 + jnp.exp(-jnp.abs(z)))
    g = -jnp.exp(alog_ref[...]) * softplus
    row = lax.broadcasted_iota(i32, (tm, 1), 0) % DN_CHUNK
    s = 1
    while s < DN_CHUNK:
        g = g + jnp.where(row >= s, pltpu.roll(g, s, axis=0), 0.0)
        s *= 2
    lane = lax.broadcasted_iota(i32, (1, LANES), 1)
    gate_ref[...] = jnp.where(lane < HEADS, g, _sigmoid(ab))


def _dn_prep(proj, lead, conv_w, a_log, dt_bias, tm=ROWS_WIDE):
    m = proj.shape[0]
    pad = LANES - HEADS
    alog = jnp.pad(a_log.astype(f32), (0, pad)).reshape(1, LANES)
    dtb = jnp.pad(dt_bias.astype(f32), (0, pad)).reshape(1, LANES)
    hb = tm // SUBLANES
    out_w = pl.BlockSpec((tm, WIDTH), lambda i: (i, 0))
    return pl.pallas_call(
        functools.partial(_dn_prep_kernel, tm=tm),
        grid=(m // tm,),
        in_specs=[pl.BlockSpec((tm, DN_PROJ_PAD), lambda i: (i, 0)),
                  pl.BlockSpec((SUBLANES, 3 * WIDTH), lambda i: (jnp.maximum(i * hb - 1, 0), 0)),
                  pl.BlockSpec((SUBLANES, 3 * WIDTH), lambda i: (0, 0)),
                  pl.BlockSpec((CONV_K, 3 * WIDTH), lambda i: (0, 0)),
                  pl.BlockSpec((1, LANES), lambda i: (0, 0)), pl.BlockSpec((1, LANES), lambda i: (0, 0))],
        out_specs=[out_w, out_w, out_w, pl.BlockSpec((tm, LANES), lambda i: (i, 0))],
        out_shape=[jax.ShapeDtypeStruct((m, WIDTH), f32)] * 3 + [jax.ShapeDtypeStruct((m, LANES), f32)],
        compiler_params=_params(("parallel",)),
        name="dn_prep",
    )(proj, proj, lead, conv_w, alog, dtb)


def _dn_delta_kernel(q_ref, k_ref, v_ref, z_ref, gate_ref, gt_ref, nw_ref, s0_ref, o_ref, state_ref, *, chunks):
    @pl.when(pl.program_id(0) == 0)
    def _():
        state_ref[...] = s0_ref[...]

    c = DN_CHUNK
    ri = lax.broadcasted_iota(i32, (c, c), 0)
    ci = lax.broadcasted_iota(i32, (c, c), 1)
    causal = ri >= ci
    strict = ri > ci
    eye = (ri == ci).astype(f32)
    nw = nw_ref[...]
    inst = [(ch, h) for ch in range(chunks) for h in range(HEADS)]
    pre = {}
    for ch, h in inst:
        rows = slice(ch * c, (ch + 1) * c)
        sl = slice(h * HEAD_DIM, (h + 1) * HEAD_DIM)
        q, k, v = q_ref[rows, sl], k_ref[rows, sl], v_ref[rows, sl]
        gcol = gate_ref[rows, h:h + 1]
        beta = gate_ref[rows, HEADS + h:HEADS + h + 1]
        grow = gt_ref[h:h + 1, rows]
        decay = jnp.where(causal, jnp.exp(jnp.where(causal, gcol - grow, 0.0)), 0.0)
        kb = k * beta
        egc = jnp.exp(gcol)
        glast = gcol[c - 1:c, :]
        qk = _dot_nt(jnp.concatenate([q, kb], axis=0), k)
        m = jnp.where(strict, qk[c:] * decay, 0.0)
        pre[ch, h] = dict(a_qk=qk[:c] * decay, inv=eye - m, p=m,
                          rhs=jnp.concatenate([v * beta, kb * egc], axis=1), qe=q * egc,
                          kd=k * jnp.exp(glast - gcol), eg=jnp.exp(glast))
    step = 2
    while step < c:
        for key in inst:
            pre[key]["p"] = _dot(pre[key]["p"], pre[key]["p"])
        for key in inst:
            pre[key]["inv"] = pre[key]["inv"] + _dot(pre[key]["inv"], pre[key]["p"])
        step *= 2
    for key in inst:
        pre[key]["uw"] = _dot(pre[key]["inv"], pre[key]["rhs"])
    states = [state_ref[h] for h in range(HEADS)]
    for ch in range(chunks):
        rows = slice(ch * c, (ch + 1) * c)
        for h in range(HEADS):
            sl = slice(h * HEAD_DIM, (h + 1) * HEAD_DIM)
            d = pre[ch, h]
            u, w = d["uw"][:, :HEAD_DIM], d["uw"][:, HEAD_DIM:]
            ws = _dot(jnp.concatenate([w, d["qe"]], axis=0), states[h])
            v_new = u - ws[:c]
            o = ws[c:] + _dot(d["a_qk"], v_new)
            states[h] = states[h] * d["eg"] + _dot_tn(d["kd"], v_new)
            o = o * lax.rsqrt(jnp.mean(o * o, -1, keepdims=True) + 1e-6) * nw
            zz = z_ref[rows, sl]
            o_ref[rows, sl] = o * (zz * _sigmoid(zz))
    for h in range(HEADS):
        state_ref[h] = states[h]


def _dn_delta(q, k, v, proj, gates, gates_t, norm_w, state, cb=ROWS_DELTA):
    m = q.shape[0]
    wide = pl.BlockSpec((cb, WIDTH), lambda i: (i, 0))
    whole_state = pl.BlockSpec(state.shape, lambda i: (0, 0, 0))
    return pl.pallas_call(
        functools.partial(_dn_delta_kernel, chunks=cb // DN_CHUNK),
        grid=(m // cb,),
        in_specs=[wide, wide, wide,
                  pl.BlockSpec((cb, WIDTH), lambda i: (i, 3)),
                  pl.BlockSpec((cb, LANES), lambda i: (i, 0)),
                  pl.BlockSpec((SUBLANES, cb), lambda i: (0, i)),
                  pl.BlockSpec((1, HEAD_DIM), lambda i: (0, 0)),
                  whole_state],
        out_specs=[wide, whole_state],
        out_shape=[jax.ShapeDtypeStruct((m, WIDTH), f32), jax.ShapeDtypeStruct(state.shape, f32)],
        compiler_params=_params(("arbitrary",)),
        name="dn_delta",
    )(q, k, v, proj, gates, gates_t, norm_w.reshape(1, HEAD_DIM).astype(f32), state)


def _deltanet_mixer(x2, w_pad, conv_w, a_log, dt_bias, norm_w, lead, state):
    proj = _matmul(x2, w_pad, tn=DN_PROJ_PAD // 3)
    q, k, v, gates = _dn_prep(proj, lead, conv_w, a_log, dt_bias)
    gates_t = gates[:, :SUBLANES].T
    o, state = _dn_delta(q, k, v, proj, gates, gates_t, norm_w, state)
    return o, proj[-SUBLANES:, :3 * WIDTH], state


def _attn_kernel(q_ref, kp_ref, kc_ref, vp_ref, vc_ref, o_ref, l_ref, *, steps, dilation, heads):
    nbi = pl.program_id(1)
    blk = ATTN_BLOCK
    qi = lax.broadcasted_iota(i32, (blk, 2 * blk), 0)
    kj = lax.broadcasted_iota(i32, (blk, 2 * blk), 1)
    dist = qi + blk - kj
    mask = (dist >= 0) & (dist <= steps) & ((kj >= blk) | (nbi > 0))
    scale = HEAD_DIM ** -0.5
    for r in range(dilation):
        rows = pl.ds(r, blk, stride=dilation) if dilation > 1 else slice(None)
        for h in range(heads):
            sl = slice(h * HEAD_DIM, (h + 1) * HEAD_DIM)
            q = q_ref[rows, sl]
            kk = jnp.concatenate([kp_ref[rows, sl], kc_ref[rows, sl]], axis=0)
            vv = jnp.concatenate([vp_ref[rows, sl], vc_ref[rows, sl]], axis=0)
            sc = jnp.where(mask, _dot_nt(q, kk) * scale, -jnp.inf)
            mx = jnp.max(sc, -1, keepdims=True)
            p = jnp.exp(sc - mx)
            den = jnp.sum(p, -1, keepdims=True)
            o_ref[rows, sl] = _dot(p, vv) / den
            l_ref[rows, sl] = jnp.broadcast_to(mx + jnp.log(den), (blk, HEAD_DIM))


def _dilated_attention(q, k, v, group, window, dilation, batch, seq):
    rows = ATTN_BLOCK * dilation
    nb = seq // rows
    heads = HEADS if dilation == 1 else 1
    hblocks = HEADS // heads
    blk = (rows, heads * HEAD_DIM)

    def cur(b, i, j):
        return (b * nb + i, group * hblocks + j)

    def prev(b, i, j):
        return (b * nb + jnp.maximum(i - 1, 0), group * hblocks + j)

    out_spec = pl.BlockSpec(blk, lambda b, i, j: (b * nb + i, j))
    return pl.pallas_call(
        functools.partial(_attn_kernel, steps=window // dilation, dilation=dilation, heads=heads),
        grid=(batch, nb, hblocks),
        in_specs=[pl.BlockSpec(blk, cur), pl.BlockSpec(blk, prev), pl.BlockSpec(blk, cur),
                  pl.BlockSpec(blk, prev), pl.BlockSpec(blk, cur)],
        out_specs=[out_spec, out_spec],
        out_shape=[jax.ShapeDtypeStruct((batch * seq, WIDTH), f32)] * 2,
        compiler_params=_params(("parallel", "parallel", "parallel")),
        name=f"dilated_attn_d{dilation}",
    )(q, k, k, v, v)


def _combine_out_ln_kernel(o0, o1, o2, l0, l1, l2, w_ref, x_ref, g_ref, b_ref, out_ref):
    ls = [l0[...], l1[...], l2[...]]
    mx = jnp.maximum(jnp.maximum(ls[0], ls[1]), ls[2])
    es = [jnp.exp(l - mx) for l in ls]
    o = (es[0] * o0[...] + es[1] * o1[...] + es[2] * o2[...]) / (es[0] + es[1] + es[2])
    h = _dot(o, w_ref[...])
    out_ref[...] = _layer_norm(ALPHA * x_ref[...] + h, g_ref[...], b_ref[...])


def _combine_out_ln(outs, lses, w, x, g, b, tm=ROWS_WIDE):
    m, n = x.shape
    blk = pl.BlockSpec((tm, n), lambda i: (i, 0))
    row = pl.BlockSpec((1, n), lambda i: (0, 0))
    return pl.pallas_call(
        _combine_out_ln_kernel,
        grid=(m // tm,),
        in_specs=[blk] * 6 + [pl.BlockSpec((WIDTH, n), lambda i: (0, 0)), blk, row, row],
        out_specs=blk,
        out_shape=jax.ShapeDtypeStruct((m, n), f32),
        compiler_params=_params(("parallel",)),
        name="attn_combine_out_ln",
    )(*outs, *lses, w, x, g.reshape(1, n), b.reshape(1, n))


def _rope_lane_tables(seq):
    half = ROT_DIM // 2
    inv_freq = ROPE_THETA ** (-jnp.arange(half, dtype=f32) * 2.0 / ROT_DIM)
    ang = jnp.arange(seq, dtype=f32)[:, None] * inv_freq[None, :]
    cos, sin = jnp.cos(ang), jnp.sin(ang)
    ones = jnp.ones((seq, HEAD_DIM - ROT_DIM), f32)
    zeros = jnp.zeros((seq, HEAD_DIM - half), f32)
    c = jnp.concatenate([cos, cos, ones], axis=1)
    s_lo = jnp.concatenate([-sin, zeros], axis=1)
    s_hi = jnp.concatenate([jnp.zeros((seq, half), f32), sin, jnp.zeros((seq, HEAD_DIM - ROT_DIM), f32)], axis=1)
    return c, s_lo, s_hi


def _top_rows(s, order, payload, k):
    big = jnp.int32(2 ** 30)
    vals, pays = [], []
    for _ in range(k):
        m = jnp.max(s, axis=0, keepdims=True)
        pos = jnp.min(jnp.where(s == m, order, big), axis=0, keepdims=True)
        sel = order == pos
        pays.append(pos if payload is None else jnp.max(jnp.where(sel, payload, -1), axis=0, keepdims=True))
        vals.append(m)
        s = jnp.where(sel, -jnp.inf, s)
    return jnp.concatenate(vals, axis=0), jnp.concatenate(pays, axis=0)


def _pair_candidates(t1, t2, combine, fill):
    kt = PEER_TOPK
    row = lax.broadcasted_iota(i32, (SUBLANES, LANES), 0)
    bc = lambda t, r: jnp.broadcast_to(t[r:r + 1], (SUBLANES, LANES))
    lo1, hi1, lo2, hi2 = t1[:SUBLANES], t1[SUBLANES:], t2[:SUBLANES], t2[SUBLANES:]
    tiles = [combine(bc(t1, 0), lo2), combine(bc(t1, 0), hi2), combine(bc(t1, 1), lo2),
             jnp.where(row < kt // 3, combine(bc(t1, 2), lo2), fill),
             jnp.where(row < kt // 4, combine(bc(t1, 3), lo2), fill),
             combine(hi1, bc(t2, 0)),
             jnp.where(row >= 4, combine(lo1, bc(t2, 0)), fill),
             jnp.where(row >= 4, combine(lo1, bc(t2, 1)), jnp.where(row == 0, combine(bc(t1, 4), bc(t2, 2)), fill))]
    return jnp.concatenate(tiles, axis=0)


def _pair_rank():
    kt = PEER_TOPK
    row = lax.broadcasted_iota(i32, (SUBLANES, LANES), 0)
    unused = kt * kt + row
    tiles = [row, row + SUBLANES, kt + row,
             jnp.where(row < kt // 3, 2 * kt + row, unused),
             jnp.where(row < kt // 4, 3 * kt + row, unused + SUBLANES),
             (row + SUBLANES) * kt,
             jnp.where(row >= 4, row * kt, unused + 2 * SUBLANES),
             jnp.where(row >= 4, row * kt + 1, jnp.where(row == 0, 4 * kt + 2, unused + 3 * SUBLANES))]
    return jnp.concatenate(tiles, axis=0)


def _peer_route_kernel(qp_ref, keys_ref, idx_ref, gate_ref):
    kt = PEER_TOPK
    half = PEER_QDIM // 2
    keyid = lax.broadcasted_iota(i32, (PEER_NKEYS, LANES), 0)
    rank = _pair_rank()
    idx_rows, gate_rows = [], []
    for h in range(PEER_HEADS):
        tops = []
        for c in range(2):
            col = (h * 2 + c) * half
            s = _dot_nt(keys_ref[h * 2 + c], qp_ref[:, col:col + half])
            tops.append(_top_rows(s, keyid, None, kt))
        (v1, i1), (v2, i2) = tops
        cand = _pair_candidates(v1, v2, lambda a, b: a + b, -jnp.inf)
        cand_id = _pair_candidates(i1, i2, lambda a, b: a * PEER_NKEYS + b, -1)
        best, ids = _top_rows(cand, rank, cand_id, kt)
        e = jnp.exp(best - best[0:1])
        gate_rows.append(e / jnp.sum(e, axis=0, keepdims=True))
        idx_rows.append(ids)
    idx_ref[...] = jnp.concatenate(idx_rows, axis=0).astype(f32).T.astype(i32)
    gate_ref[...] = jnp.concatenate(gate_rows, axis=0).T


def _peer_route(qp, sub_keys):
    m = qp.shape[0]
    tb = LANES
    keys = sub_keys.reshape(PEER_HEADS * 2, PEER_NKEYS, PEER_QDIM // 2).astype(bf16)
    out = pl.BlockSpec((tb, PEER_SEL), lambda i: (i, 0))
    return pl.pallas_call(
        _peer_route_kernel,
        grid=(m // tb,),
        in_specs=[pl.BlockSpec((tb, qp.shape[1]), lambda i: (i, 0)),
                  pl.BlockSpec(keys.shape, lambda i: (0, 0, 0))],
        out_specs=[out, out],
        out_shape=[jax.ShapeDtypeStruct((m, PEER_SEL), i32), jax.ShapeDtypeStruct((m, PEER_SEL), f32)],
        compiler_params=_params(("parallel",)),
        name="peer_route",
    )(qp, keys)


SC_CORES = 2
SC_SUBCORES = 16
SC_LANES = 16
SC_WORKERS = SC_CORES * SC_SUBCORES
SC_CHUNKS = D_MODEL // SC_LANES
SC_WORDS = D_MODEL // 2
SC_WORD_CHUNKS = SC_WORDS // SC_LANES
SC_TOK_GROUP = 32
SC_ROWS = 64
SC_GATHERS_PER_TOK = PEER_SEL // SC_ROWS
SC_GATHERS = SC_TOK_GROUP * SC_GATHERS_PER_TOK
SC_ROW_GROUP = 8


def _pack_table(tab):
    e = tab.shape[0]
    bits = lax.bitcast_convert_type(tab.astype(bf16), jnp.uint16).astype(jnp.uint32)
    bits = bits.reshape(e, SC_WORD_CHUNKS, 2, SC_LANES)
    word = bits[:, :, 0, :] | (bits[:, :, 1, :] << 16)
    return lax.bitcast_convert_type(word.reshape(e, SC_WORDS), i32)


def _unpack_words(w):
    return lax.bitcast_convert_type(w << 16, f32), lax.bitcast_convert_type(w & (-65536), f32)


def _sc_mesh():
    return plsc.VectorSubcoreMesh(core_axis_name="c", subcore_axis_name="s")


def _sc_worker():
    return lax.axis_index("s") * SC_CORES + lax.axis_index("c")


def _sc_gather_loop(gather, compute):
    gather(0, 0).start()

    @pl.loop(0, SC_GATHERS, step=2)
    def _(q):
        gather(q + 1, 1).start()
        gather(q, 0).wait()
        compute(q, 0)

        @pl.when(q + 2 < SC_GATHERS)
        def _():
            gather(q + 2, 0).start()
        gather(q + 1, 1).wait()
        compute(q + 1, 1)


def _sc_scratch(stage_shape):
    return [pltpu.VMEM((SC_GATHERS, SC_ROWS), i32), pltpu.VMEM(stage_shape[0], f32), pltpu.VMEM(stage_shape[1], f32),
            pltpu.VMEM((SC_ROWS, SC_WORDS), i32), pltpu.VMEM((SC_ROWS, SC_WORDS), i32),
            pltpu.SemaphoreType.DMA, pltpu.SemaphoreType.DMA]


def _peer_udot(x, idx, u_tab):
    t_all = x.shape[0]
    tpw = t_all // SC_WORKERS
    idx2 = idx.reshape(t_all * SC_GATHERS_PER_TOK, SC_ROWS)

    @functools.partial(
        pl.kernel, mesh=_sc_mesh(), compiler_params=pltpu.CompilerParams(needs_layout_passes=False),
        out_type=jax.ShapeDtypeStruct((t_all, PEER_SEL), f32),
        scratch_types=_sc_scratch(((SC_TOK_GROUP, D_MODEL), (SC_TOK_GROUP, PEER_SEL))), name="peer_udot_sc")
    def k(x_hbm, idx_hbm, u_hbm, pre_hbm, idx_v, x_v, pre_v, rows0, rows1, sem0, sem1):
        bufs = ((rows0, sem0), (rows1, sem1))
        base = _sc_worker() * tpw
        lane = lax.iota(i32, SC_LANES)
        zero = jnp.zeros((SC_LANES,), f32)

        def gather(q, b):
            return pltpu.make_async_copy(u_hbm.at[idx_v.at[q]], bufs[b][0], bufs[b][1])

        def compute(q, b):
            rows = bufs[b][0]
            t = q // SC_GATHERS_PER_TOK
            col0 = (q % SC_GATHERS_PER_TOK) * SC_ROWS
            for hh in range(SC_ROWS // SC_LANES):
                vec = zero
                for rg in range(SC_LANES // SC_ROW_GROUP):
                    r0 = hh * SC_LANES + rg * SC_ROW_GROUP

                    def body(c, acc):
                        x_lo = x_v[t, pl.ds(c * 2 * SC_LANES, SC_LANES)]
                        x_hi = x_v[t, pl.ds(c * 2 * SC_LANES + SC_LANES, SC_LANES)]
                        out = []
                        for r in range(SC_ROW_GROUP):
                            lo, hi = _unpack_words(rows[r0 + r, pl.ds(c * SC_LANES, SC_LANES)])
                            out.append(acc[r] + lo * x_lo + hi * x_hi)
                        return tuple(out)

                    acc = lax.fori_loop(0, SC_WORD_CHUNKS, body, (zero,) * SC_ROW_GROUP, unroll=2)
                    for r in range(SC_ROW_GROUP):
                        vec = jnp.where(lane == rg * SC_ROW_GROUP + r, jnp.sum(acc[r]), vec)
                pre_v[t, pl.ds(col0 + hh * SC_LANES, SC_LANES)] = vec

        @pl.loop(0, tpw // SC_TOK_GROUP)
        def _(g):
            tok0 = base + g * SC_TOK_GROUP
            pltpu.sync_copy(idx_hbm.at[pl.ds(tok0 * SC_GATHERS_PER_TOK, SC_GATHERS)], idx_v)
            pltpu.sync_copy(x_hbm.at[pl.ds(tok0, SC_TOK_GROUP)], x_v)
            _sc_gather_loop(gather, compute)
            pltpu.sync_copy(pre_v, pre_hbm.at[pl.ds(tok0, SC_TOK_GROUP)])

    return k(x, idx2, u_tab)


def _peer_vacc(w, idx, v_tab):
    t_all = w.shape[0]
    tpw = t_all // SC_WORKERS
    idx2 = idx.reshape(t_all * SC_GATHERS_PER_TOK, SC_ROWS)
    half_w = SC_WORD_CHUNKS // 2

    @functools.partial(
        pl.kernel, mesh=_sc_mesh(), compiler_params=pltpu.CompilerParams(needs_layout_passes=False),
        out_type=jax.ShapeDtypeStruct((t_all, D_MODEL), f32),
        scratch_types=_sc_scratch(((SC_TOK_GROUP, PEER_SEL), (SC_TOK_GROUP, D_MODEL))), name="peer_vacc_sc")
    def k(w_hbm, idx_hbm, v_hbm, f_hbm, idx_v, w_v, f_v, rows0, rows1, sem0, sem1):
        bufs = ((rows0, sem0), (rows1, sem1))
        base = _sc_worker() * tpw
        zero = jnp.zeros((SC_LANES,), f32)

        def gather(q, b):
            return pltpu.make_async_copy(v_hbm.at[idx_v.at[q]], bufs[b][0], bufs[b][1])

        def compute(q, b):
            rows = bufs[b][0]
            t = q // SC_GATHERS_PER_TOK
            col0 = (q % SC_GATHERS_PER_TOK) * SC_ROWS
            tvec = jnp.full((SC_LANES,), t, i32)
            for half in range(2):
                def body(r, acc):
                    wr = plsc.load_gather(w_v, [tvec, jnp.full((SC_LANES,), col0 + r, i32)])
                    out = []
                    for c in range(half_w):
                        lo, hi = _unpack_words(rows[r, pl.ds((half * half_w + c) * SC_LANES, SC_LANES)])
                        out.append(acc[2 * c] + wr * lo)
                        out.append(acc[2 * c + 1] + wr * hi)
                    return tuple(out)

                acc = lax.fori_loop(0, SC_ROWS, body, (zero,) * (2 * half_w))
                for c in range(2 * half_w):
                    sl = pl.ds((half * 2 * half_w + c) * SC_LANES, SC_LANES)
                    f_v[t, sl] = f_v[t, sl] + acc[c]

        @pl.loop(0, tpw // SC_TOK_GROUP)
        def _(g):
            tok0 = base + g * SC_TOK_GROUP
            pltpu.sync_copy(idx_hbm.at[pl.ds(tok0 * SC_GATHERS_PER_TOK, SC_GATHERS)], idx_v)
            pltpu.sync_copy(w_hbm.at[pl.ds(tok0, SC_TOK_GROUP)], w_v)

            @pl.loop(0, SC_TOK_GROUP)
            def _(t):
                @pl.loop(0, SC_CHUNKS)
                def _(c):
                    f_v[t, pl.ds(c * SC_LANES, SC_LANES)] = zero

            _sc_gather_loop(gather, compute)
            pltpu.sync_copy(f_v, f_hbm.at[pl.ds(tok0, SC_TOK_GROUP)])

    return k(w, idx2, v_tab)


def _peer_act_kernel(pre_ref, gate_ref, w_ref):
    pre = pre_ref[...]
    w_ref[...] = gate_ref[...] * (0.5 * pre * (1.0 + lax.erf(pre * (2.0 ** -0.5))))


def _peer_act(pre, gate, tm=ROWS_LANEWISE):
    m, n = pre.shape
    blk = pl.BlockSpec((tm, n), lambda i: (i, 0))
    return pl.pallas_call(
        _peer_act_kernel, grid=(m // tm,), in_specs=[blk, blk], out_specs=blk,
        out_shape=jax.ShapeDtypeStruct((m, n), f32), compiler_params=_params(("parallel",)), name="peer_act",
    )(pre, gate)


def _res_ln_kernel(x_ref, f_ref, g_ref, b_ref, o_ref):
    o_ref[...] = _layer_norm(ALPHA * x_ref[...] + f_ref[...], g_ref[...], b_ref[...])


def _res_ln(x, f, g, b, tm=ROWS_MATMUL):
    m, n = x.shape
    blk = pl.BlockSpec((tm, n), lambda i: (i, 0))
    row = pl.BlockSpec((1, n), lambda i: (0, 0))
    return pl.pallas_call(
        _res_ln_kernel, grid=(m // tm,), in_specs=[blk, blk, row, row], out_specs=blk,
        out_shape=jax.ShapeDtypeStruct((m, n), f32), compiler_params=_params(("parallel",)), name="res_ln",
    )(x, f, g.reshape(1, n), b.reshape(1, n))


def _peer_expert_ln(x, idx, gate, u_tab, v_tab, g, b):
    pre = _peer_udot(x, idx, u_tab)
    w = _peer_act(pre, gate)
    f = _peer_vacc(w, idx, v_tab)
    return _res_ln(x, f, g, b)


PEER_SPLITS = 2
DN_SEGMENT_EIGHTHS = (1, 1, 2, 4)


def _peer_ffn_ln(x2, w_query, sub_keys, u_pack, v_pack, g, b, splits=PEER_SPLITS):
    outs = []
    for xs in jnp.split(x2, splits, axis=0):
        qp = _matmul(xs, w_query)
        idx, gate = _peer_route(qp, sub_keys)
        outs.append(_peer_expert_ln(xs, idx, gate, u_pack, v_pack, g, b))
    return jnp.concatenate(outs, axis=0)


def kernel(x, dn_w_in, dn_conv, dn_a_log, dn_dt_bias, dn_norm_w, dn_w_out, shared_w_kv, attn_w_q, attn_w_out,
           peer_w_query, peer_sub_keys, peer_u, peer_v, ln_mix_g, ln_mix_b, ln_ffn_g, ln_ffn_b):
    batch, seq, d = x.shape
    n_a = DEPTH // 2
    tabs = _rope_lane_tables(seq)
    gw = N_GROUPS * WIDTH
    dn_w_pad = [jnp.pad(w, ((0, 0), (0, DN_PROJ_PAD - w.shape[1]))).astype(bf16) for w in dn_w_in]
    dn_w_out_b = dn_w_out.astype(bf16)
    w_k, w_v = shared_w_kv[:, :gw].astype(bf16), shared_w_kv[:, gw:].astype(bf16)
    w_q, w_o = attn_w_q.astype(bf16), attn_w_out.astype(bf16)
    peer_wq = peer_w_query.astype(bf16)
    u_pack = [_pack_table(t) for t in peer_u]
    v_pack = [_pack_table(t) for t in peer_v]

    def trunk(x2):
        k_sh = v_sh = None
        for layer in range(DEPTH):
            if layer < n_a:
                lead = jnp.zeros((SUBLANES, 3 * WIDTH), f32)
                state = jnp.zeros((HEADS, HEAD_DIM, HEAD_DIM), f32)
                segs = []
                cuts = [sum(DN_SEGMENT_EIGHTHS[:i]) * seq // 8 for i in range(1, len(DN_SEGMENT_EIGHTHS))]
                for xs in jnp.split(x2, cuts, axis=0):
                    o, lead, state = _deltanet_mixer(xs, dn_w_pad[layer], dn_conv[layer], dn_a_log[layer],
                                                     dn_dt_bias[layer], dn_norm_w[layer], lead, state)
                    xs = _matmul_res_ln(o, dn_w_out_b[layer], xs, ln_mix_g[layer], ln_mix_b[layer])
                    segs.append(_peer_ffn_ln(xs, peer_wq[layer], peer_sub_keys[layer], u_pack[layer], v_pack[layer],
                                             ln_ffn_g[layer], ln_ffn_b[layer], splits=1))
                x2 = jnp.concatenate(segs, axis=0)
                continue
            else:
                j = layer - n_a
                if j == 0:
                    k_sh = _matmul_rope(x2, w_k, tabs, seq)
                    v_sh = _matmul(x2, w_v, tn=COLS_MATMUL)
                q = _matmul_rope(x2, w_q[j], tabs, seq)
                outs, lses = [], []
                for gi, (window, dilation) in enumerate(ATTN_GROUPS):
                    o, l = _dilated_attention(q, k_sh, v_sh, gi, window, dilation, 1, seq)
                    outs.append(o)
                    lses.append(l)
                x2 = _combine_out_ln(outs, lses, w_o[j], x2, ln_mix_g[layer], ln_mix_b[layer])
            x2 = _peer_ffn_ln(x2, peer_wq[layer], peer_sub_keys[layer], u_pack[layer], v_pack[layer],
                              ln_ffn_g[layer], ln_ffn_b[layer])
        return x2

    return jnp.stack([trunk(x[b]) for b in range(batch)], axis=0)
```

```python
import functools

import jax
import jax.numpy as jnp
from jax import lax
from jax.experimental import pallas as pl
from jax.experimental.pallas import tpu as pltpu
from jax.experimental.pallas import tpu_sc as plsc

f32 = jnp.float32
bf16 = jnp.bfloat16
i32 = jnp.int32

D_MODEL = 1024
DEPTH = 2
ALPHA = (2.0 * DEPTH) ** 0.25
LN_EPS = 1e-5

HEADS = 8
HEAD_DIM = 128
WIDTH = HEADS * HEAD_DIM
CONV_K = 4
DN_CHUNK = 64

ATTN_GROUPS = ((128, 1), (512, 4), (2048, 16))
N_GROUPS = len(ATTN_GROUPS)
ATTN_BLOCK = 128
ROT_DIM = HEAD_DIM // 4
ROPE_THETA = 500000.0

PEER_HEADS = 8
PEER_NKEYS = 128
PEER_TOPK = 16
PEER_QDIM = 256
PEER_SEL = PEER_HEADS * PEER_TOPK

LANES = 128
SUBLANES = 8
VMEM_LIMIT = 48 * 1024 * 1024

ROWS_MATMUL = 512
ROWS_WIDE = 256
ROWS_DELTA = 2 * DN_CHUNK
ROWS_LANEWISE = 2048
COLS_MATMUL = 1024


def _params(sem):
    return pltpu.CompilerParams(dimension_semantics=sem, vmem_limit_bytes=VMEM_LIMIT)


def _dot(a, b):
    return jnp.dot(a.astype(bf16), b.astype(bf16), preferred_element_type=f32)


def _dot_nt(a, b):
    return lax.dot_general(a.astype(bf16), b.astype(bf16), (((1,), (1,)), ((), ())), preferred_element_type=f32)


def _dot_tn(a, b):
    return lax.dot_general(a.astype(bf16), b.astype(bf16), (((0,), (0,)), ((), ())), preferred_element_type=f32)


def _sigmoid(x):
    return 1.0 / (1.0 + jnp.exp(-x))


def _layer_norm(y, g, b):
    mu = jnp.mean(y, -1, keepdims=True)
    yc = y - mu
    var = jnp.mean(yc * yc, -1, keepdims=True)
    return yc * lax.rsqrt(var + LN_EPS) * g + b


def _mm_kernel(a_ref, w_ref, o_ref):
    o_ref[...] = _dot(a_ref[...], w_ref[...])


def _matmul(a, w, tm=ROWS_MATMUL, tn=None):
    m, k = a.shape
    n = w.shape[1]
    tn = tn or n
    return pl.pallas_call(
        _mm_kernel,
        grid=(m // tm, n // tn),
        in_specs=[pl.BlockSpec((tm, k), lambda i, j: (i, 0)), pl.BlockSpec((k, tn), lambda i, j: (0, j))],
        out_specs=pl.BlockSpec((tm, tn), lambda i, j: (i, j)),
        out_shape=jax.ShapeDtypeStruct((m, n), f32),
        compiler_params=_params(("parallel", "parallel")),
        name="matmul",
    )(a, w)


def _rope(y, c, s_lo, s_hi):
    return y * c + pltpu.roll(y, ROT_DIM // 2, axis=1) * s_hi + pltpu.roll(y, LANES - ROT_DIM // 2, axis=1) * s_lo


def _mm_rope_kernel(a_ref, w_ref, c_ref, slo_ref, shi_ref, o_ref, *, heads):
    y = _dot(a_ref[...], w_ref[...])
    c, s_lo, s_hi = c_ref[...], slo_ref[...], shi_ref[...]
    for h in range(heads):
        sl = slice(h * HEAD_DIM, (h + 1) * HEAD_DIM)
        o_ref[:, sl] = _rope(y[:, sl], c, s_lo, s_hi)


def _matmul_rope(a, w, tabs, seq, tm=ROWS_MATMUL, tn=COLS_MATMUL):
    m, k = a.shape
    n = w.shape[1]
    nsb = seq // tm
    tab_spec = pl.BlockSpec((tm, LANES), lambda i, j: (i % nsb, 0))
    return pl.pallas_call(
        functools.partial(_mm_rope_kernel, heads=tn // HEAD_DIM),
        grid=(m // tm, n // tn),
        in_specs=[pl.BlockSpec((tm, k), lambda i, j: (i, 0)), pl.BlockSpec((k, tn), lambda i, j: (0, j)),
                  tab_spec, tab_spec, tab_spec],
        out_specs=pl.BlockSpec((tm, tn), lambda i, j: (i, j)),
        out_shape=jax.ShapeDtypeStruct((m, n), f32),
        compiler_params=_params(("parallel", "parallel")),
        name="matmul_rope",
    )(a, w, *tabs)


def _mm_res_ln_kernel(a_ref, w_ref, x_ref, g_ref, b_ref, o_ref):
    h = _dot(a_ref[...], w_ref[...])
    o_ref[...] = _layer_norm(ALPHA * x_ref[...] + h, g_ref[...], b_ref[...])


def _matmul_res_ln(a, w, x, g, b, tm=ROWS_MATMUL):
    m, k = a.shape
    n = w.shape[1]
    row = pl.BlockSpec((1, n), lambda i: (0, 0))
    return pl.pallas_call(
        _mm_res_ln_kernel,
        grid=(m // tm,),
        in_specs=[pl.BlockSpec((tm, k), lambda i: (i, 0)), pl.BlockSpec((k, n), lambda i: (0, 0)),
                  pl.BlockSpec((tm, n), lambda i: (i, 0)), row, row],
        out_specs=pl.BlockSpec((tm, n), lambda i: (i, 0)),
        out_shape=jax.ShapeDtypeStruct((m, n), f32),
        compiler_params=_params(("parallel",)),
        name="matmul_res_ln",
    )(a, w, x, g.reshape(1, n), b.reshape(1, n))


DN_PROJ_PAD = 4 * WIDTH + LANES


def _dn_prep_kernel(proj_ref, halo_ref, lead_ref, conv_ref, alog_ref, dtb_ref, q_ref, k_ref, v_ref, gate_ref, *, tm):
    i = pl.program_id(0)
    x = proj_ref[:, : 3 * WIDTH]
    halo = jnp.where(i == 0, lead_ref[...], halo_ref[...])
    w = conv_ref[...]
    row8 = lax.broadcasted_iota(i32, (SUBLANES, 1), 0)
    acc = x * w[CONV_K - 1:CONV_K]
    for s in range(1, CONV_K):
        xs = pltpu.roll(x, s, axis=0)
        hs = pltpu.roll(halo, s, axis=0)
        first = jnp.where(row8 < s, hs, xs[:SUBLANES])
        xs = jnp.concatenate([first, xs[SUBLANES:]], axis=0)
        acc = acc + xs * w[CONV_K - 1 - s:CONV_K - s]
    qkv = acc * _sigmoid(acc)
    for h in range(HEADS):
        sl = slice(h * HEAD_DIM, (h + 1) * HEAD_DIM)
        qh = qkv[:, sl]
        q_ref[:, sl] = qh * lax.rsqrt(jnp.sum(qh * qh, -1, keepdims=True) + 1e-6) * (HEAD_DIM ** -0.5)
        kh = qkv[:, WIDTH + h * HEAD_DIM: WIDTH + (h + 1) * HEAD_DIM]
        k_ref[:, sl] = kh * lax.rsqrt(jnp.sum(kh * kh, -1, keepdims=True) + 1e-6)
    v_ref[...] = qkv[:, 2 * WIDTH:]
    ab = proj_ref[:, 4 * WIDTH:]
    z = ab + dtb_ref[...]
    softplus = jnp.maximum(z, 0.0) + jnp.log(1.0 + jnp.exp(-jnp.abs(z)))
    g = -jnp.exp(alog_ref[...]) * softplus
    row = lax.broadcasted_iota(i32, (tm, 1), 0) % DN_CHUNK
    s = 1
    while s < DN_CHUNK:
        g = g + jnp.where(row >= s, pltpu.roll(g, s, axis=0), 0.0)
        s *= 2
    lane = lax.broadcasted_iota(i32, (1, LANES), 1)
    gate_ref[...] = jnp.where(lane < HEADS, g, _sigmoid(ab))


def _dn_prep(proj, lead, conv_w, a_log, dt_bias, tm=ROWS_WIDE):
    m = proj.shape[0]
    pad = LANES - HEADS
    alog = jnp.pad(a_log.astype(f32), (0, pad)).reshape(1, LANES)
    dtb = jnp.pad(dt_bias.astype(f32), (0, pad)).reshape(1, LANES)
    hb = tm // SUBLANES
    out_w = pl.BlockSpec((tm, WIDTH), lambda i: (i, 0))
    return pl.pallas_call(
        functools.partial(_dn_prep_kernel, tm=tm),
        grid=(m // tm,),
        in_specs=[pl.BlockSpec((tm, DN_PROJ_PAD), lambda i: (i, 0)),
                  pl.BlockSpec((SUBLANES, 3 * WIDTH), lambda i: (jnp.maximum(i * hb - 1, 0), 0)),
                  pl.BlockSpec((SUBLANES, 3 * WIDTH), lambda i: (0, 0)),
                  pl.BlockSpec((CONV_K, 3 * WIDTH), lambda i: (0, 0)),
                  pl.BlockSpec((1, LANES), lambda i: (0, 0)), pl.BlockSpec((1, LANES), lambda i: (0, 0))],
        out_specs=[out_w, out_w, out_w, pl.BlockSpec((tm, LANES), lambda i: (i, 0))],
        out_shape=[jax.ShapeDtypeStruct((m, WIDTH), f32)] * 3 + [jax.ShapeDtypeStruct((m, LANES), f32)],
        compiler_params=_params(("parallel",)),
        name="dn_prep",
    )(proj, proj, lead, conv_w, alog, dtb)


def _dn_delta_kernel(q_ref, k_ref, v_ref, z_ref, gate_ref, gt_ref, nw_ref, s0_ref, o_ref, state_ref, *, chunks):
    @pl.when(pl.program_id(0) == 0)
    def _():
        state_ref[...] = s0_ref[...]

    c = DN_CHUNK
    ri = lax.broadcasted_iota(i32, (c, c), 0)
    ci = lax.broadcasted_iota(i32, (c, c), 1)
    causal = ri >= ci
    strict = ri > ci
    eye = (ri == ci).astype(f32)
    nw = nw_ref[...]
    inst = [(ch, h) for ch in range(chunks) for h in range(HEADS)]
    pre = {}
    for ch, h in inst:
        rows = slice(ch * c, (ch + 1) * c)
        sl = slice(h * HEAD_DIM, (h + 1) * HEAD_DIM)
        q, k, v = q_ref[rows, sl], k_ref[rows, sl], v_ref[rows, sl]
        gcol = gate_ref[rows, h:h + 1]
        beta = gate_ref[rows, HEADS + h:HEADS + h + 1]
        grow = gt_ref[h:h + 1, rows]
        decay = jnp.where(causal, jnp.exp(jnp.where(causal, gcol - grow, 0.0)), 0.0)
        kb = k * beta
        egc = jnp.exp(gcol)
        glast = gcol[c - 1:c, :]
        qk = _dot_nt(jnp.concatenate([q, kb], axis=0), k)
        m = jnp.where(strict, qk[c:] * decay, 0.0)
        pre[ch, h] = dict(a_qk=qk[:c] * decay, inv=eye - m, p=m,
                          rhs=jnp.concatenate([v * beta, kb * egc], axis=1), qe=q * egc,
                          kd=k * jnp.exp(glast - gcol), eg=jnp.exp(glast))
    step = 2
    while step < c:
        for key in inst:
            pre[key]["p"] = _dot(pre[key]["p"], pre[key]["p"])
        for key in inst:
            pre[key]["inv"] = pre[key]["inv"] + _dot(pre[key]["inv"], pre[key]["p"])
        step *= 2
    for key in inst:
        pre[key]["uw"] = _dot(pre[key]["inv"], pre[key]["rhs"])
    states = [state_ref[h] for h in range(HEADS)]
    for ch in range(chunks):
        rows = slice(ch * c, (ch + 1) * c)
        for h in range(HEADS):
            sl = slice(h * HEAD_DIM, (h + 1) * HEAD_DIM)
            d = pre[ch, h]
            u, w = d["uw"][:, :HEAD_DIM], d["uw"][:, HEAD_DIM:]
            ws = _dot(jnp.concatenate([w, d["qe"]], axis=0), states[h])
            v_new = u - ws[:c]
            o = ws[c:] + _dot(d["a_qk"], v_new)
            states[h] = states[h] * d["eg"] + _dot_tn(d["kd"], v_new)
            o = o * lax.rsqrt(jnp.mean(o * o, -1, keepdims=True) + 1e-6) * nw
            zz = z_ref[rows, sl]
            o_ref[rows, sl] = o * (zz * _sigmoid(zz))
    for h in range(HEADS):
        state_ref[h] = states[h]


def _dn_delta(q, k, v, proj, gates, gates_t, norm_w, state, cb=ROWS_DELTA):
    m = q.shape[0]
    wide = pl.BlockSpec((cb, WIDTH), lambda i: (i, 0))
    whole_state = pl.BlockSpec(state.shape, lambda i: (0, 0, 0))
    return pl.pallas_call(
        functools.partial(_dn_delta_kernel, chunks=cb // DN_CHUNK),
        grid=(m // cb,),
        in_specs=[wide, wide, wide,
                  pl.BlockSpec((cb, WIDTH), lambda i: (i, 3)),
                  pl.BlockSpec((cb, LANES), lambda i: (i, 0)),
                  pl.BlockSpec((SUBLANES, cb), lambda i: (0, i)),
                  pl.BlockSpec((1, HEAD_DIM), lambda i: (0, 0)),
                  whole_state],
        out_specs=[wide, whole_state],
        out_shape=[jax.ShapeDtypeStruct((m, WIDTH), f32), jax.ShapeDtypeStruct(state.shape, f32)],
        compiler_params=_params(("arbitrary",)),
        name="dn_delta",
    )(q, k, v, proj, gates, gates_t, norm_w.reshape(1, HEAD_DIM).astype(f32), state)


def _deltanet_mixer(x2, w_pad, conv_w, a_log, dt_bias, norm_w, lead, state):
    proj = _matmul(x2, w_pad, tn=DN_PROJ_PAD // 3)
    q, k, v, gates = _dn_prep(proj, lead, conv_w, a_log, dt_bias)
    gates_t = gates[:, :SUBLANES].T
    o, state = _dn_delta(q, k, v, proj, gates, gates_t, norm_w, state)
    return o, proj[-SUBLANES:, :3 * WIDTH], state


def _attn_kernel(q_ref, kp_ref, kc_ref, vp_ref, vc_ref, o_ref, l_ref, *, steps, dilation, heads):
    nbi = pl.program_id(1)
    blk = ATTN_BLOCK
    qi = lax.broadcasted_iota(i32, (blk, 2 * blk), 0)
    kj = lax.broadcasted_iota(i32, (blk, 2 * blk), 1)
    dist = qi + blk - kj
    mask = (dist >= 0) & (dist <= steps) & ((kj >= blk) | (nbi > 0))
    scale = HEAD_DIM ** -0.5
    for r in range(dilation):
        rows = pl.ds(r, blk, stride=dilation) if dilation > 1 else slice(None)
        for h in range(heads):
            sl = slice(h * HEAD_DIM, (h + 1) * HEAD_DIM)
            q = q_ref[rows, sl]
            kk = jnp.concatenate([kp_ref[rows, sl], kc_ref[rows, sl]], axis=0)
            vv = jnp.concatenate([vp_ref[rows, sl], vc_ref[rows, sl]], axis=0)
            sc = jnp.where(mask, _dot_nt(q, kk) * scale, -jnp.inf)
            mx = jnp.max(sc, -1, keepdims=True)
            p = jnp.exp(sc - mx)
            den = jnp.sum(p, -1, keepdims=True)
            o_ref[rows, sl] = _dot(p, vv) / den
            l_ref[rows, sl] = jnp.broadcast_to(mx + jnp.log(den), (blk, HEAD_DIM))


def _dilated_attention(q, k, v, group, window, dilation, batch, seq):
    rows = ATTN_BLOCK * dilation
    nb = seq // rows
    heads = HEADS if dilation == 1 else 1
    hblocks = HEADS // heads
    blk = (rows, heads * HEAD_DIM)

    def cur(b, i, j):
        return (b * nb + i, group * hblocks + j)

    def prev(b, i, j):
        return (b * nb + jnp.maximum(i - 1, 0), group * hblocks + j)

    out_spec = pl.BlockSpec(blk, lambda b, i, j: (b * nb + i, j))
    return pl.pallas_call(
        functools.partial(_attn_kernel, steps=window // dilation, dilation=dilation, heads=heads),
        grid=(batch, nb, hblocks),
        in_specs=[pl.BlockSpec(blk, cur), pl.BlockSpec(blk, prev), pl.BlockSpec(blk, cur),
                  pl.BlockSpec(blk, prev), pl.BlockSpec(blk, cur)],
        out_specs=[out_spec, out_spec],
        out_shape=[jax.ShapeDtypeStruct((batch * seq, WIDTH), f32)] * 2,
        compiler_params=_params(("parallel", "parallel", "parallel")),
        name=f"dilated_attn_d{dilation}",
    )(q, k, k, v, v)


def _combine_out_ln_kernel(o0, o1, o2, l0, l1, l2, w_ref, x_ref, g_ref, b_ref, out_ref):
    ls = [l0[...], l1[...], l2[...]]
    mx = jnp.maximum(jnp.maximum(ls[0], ls[1]), ls[2])
    es = [jnp.exp(l - mx) for l in ls]
    o = (es[0] * o0[...] + es[1] * o1[...] + es[2] * o2[...]) / (es[0] + es[1] + es[2])
    h = _dot(o, w_ref[...])
    out_ref[...] = _layer_norm(ALPHA * x_ref[...] + h, g_ref[...], b_ref[...])


def _combine_out_ln(outs, lses, w, x, g, b, tm=ROWS_WIDE):
    m, n = x.shape
    blk = pl.BlockSpec((tm, n), lambda i: (i, 0))
    row = pl.BlockSpec((1, n), lambda i: (0, 0))
    return pl.pallas_call(
        _combine_out_ln_kernel,
        grid=(m // tm,),
        in_specs=[blk] * 6 + [pl.BlockSpec((WIDTH, n), lambda i: (0, 0)), blk, row, row],
        out_specs=blk,
        out_shape=jax.ShapeDtypeStruct((m, n), f32),
        compiler_params=_params(("parallel",)),
        name="attn_combine_out_ln",
    )(*outs, *lses, w, x, g.reshape(1, n), b.reshape(1, n))


def _rope_lane_tables(seq):
    half = ROT_DIM // 2
    inv_freq = ROPE_THETA ** (-jnp.arange(half, dtype=f32) * 2.0 / ROT_DIM)
    ang = jnp.arange(seq, dtype=f32)[:, None] * inv_freq[None, :]
    cos, sin = jnp.cos(ang), jnp.sin(ang)
    ones = jnp.ones((seq, HEAD_DIM - ROT_DIM), f32)
    zeros = jnp.zeros((seq, HEAD_DIM - half), f32)
    c = jnp.concatenate([cos, cos, ones], axis=1)
    s_lo = jnp.concatenate([-sin, zeros], axis=1)
    s_hi = jnp.concatenate([jnp.zeros((seq, half), f32), sin, jnp.zeros((seq, HEAD_DIM - ROT_DIM), f32)], axis=1)
    return c, s_lo, s_hi


def _top_rows(s, order, payload, k):
    big = jnp.int32(2 ** 30)
    vals, pays = [], []
    for _ in range(k):
        m = jnp.max(s, axis=0, keepdims=True)
        pos = jnp.min(jnp.where(s == m, order, big), axis=0, keepdims=True)
        sel = order == pos
        pays.append(pos if payload is None else jnp.max(jnp.where(sel, payload, -1), axis=0, keepdims=True))
        vals.append(m)
        s = jnp.where(sel, -jnp.inf, s)
    return jnp.concatenate(vals, axis=0), jnp.concatenate(pays, axis=0)


def _pair_candidates(t1, t2, combine, fill):
    kt = PEER_TOPK
    row = lax.broadcasted_iota(i32, (SUBLANES, LANES), 0)
    bc = lambda t, r: jnp.broadcast_to(t[r:r + 1], (SUBLANES, LANES))
    lo1, hi1, lo2, hi2 = t1[:SUBLANES], t1[SUBLANES:], t2[:SUBLANES], t2[SUBLANES:]
    tiles = [combine(bc(t1, 0), lo2), combine(bc(t1, 0), hi2), combine(bc(t1, 1), lo2),
             jnp.where(row < kt // 3, combine(bc(t1, 2), lo2), fill),
             jnp.where(row < kt // 4, combine(bc(t1, 3), lo2), fill),
             combine(hi1, bc(t2, 0)),
             jnp.where(row >= 4, combine(lo1, bc(t2, 0)), fill),
             jnp.where(row >= 4, combine(lo1, bc(t2, 1)), jnp.where(row == 0, combine(bc(t1, 4), bc(t2, 2)), fill))]
    return jnp.concatenate(tiles, axis=0)


def _pair_rank():
    kt = PEER_TOPK
    row = lax.broadcasted_iota(i32, (SUBLANES, LANES), 0)
    unused = kt * kt + row
    tiles = [row, row + SUBLANES, kt + row,
             jnp.where(row < kt // 3, 2 * kt + row, unused),
             jnp.where(row < kt // 4, 3 * kt + row, unused + SUBLANES),
             (row + SUBLANES) * kt,
             jnp.where(row >= 4, row * kt, unused + 2 * SUBLANES),
             jnp.where(row >= 4, row * kt + 1, jnp.where(row == 0, 4 * kt + 2, unused + 3 * SUBLANES))]
    return jnp.concatenate(tiles, axis=0)


def _peer_route_kernel(qp_ref, keys_ref, idx_ref, gate_ref):
    kt = PEER_TOPK
    half = PEER_QDIM // 2
    keyid = lax.broadcasted_iota(i32, (PEER_NKEYS, LANES), 0)
    rank = _pair_rank()
    idx_rows, gate_rows = [], []
    for h in range(PEER_HEADS):
        tops = []
        for c in range(2):
            col = (h * 2 + c) * half
            s = _dot_nt(keys_ref[h * 2 + c], qp_ref[:, col:col + half])
            tops.append(_top_rows(s, keyid, None, kt))
        (v1, i1), (v2, i2) = tops
        cand = _pair_candidates(v1, v2, lambda a, b: a + b, -jnp.inf)
        cand_id = _pair_candidates(i1, i2, lambda a, b: a * PEER_NKEYS + b, -1)
        best, ids = _top_rows(cand, rank, cand_id, kt)
        e = jnp.exp(best - best[0:1])
        gate_rows.append(e / jnp.sum(e, axis=0, keepdims=True))
        idx_rows.append(ids)
    idx_ref[...] = jnp.concatenate(idx_rows, axis=0).astype(f32).T.astype(i32)
    gate_ref[...] = jnp.concatenate(gate_rows, axis=0).T


def _peer_route(qp, sub_keys):
    m = qp.shape[0]
    tb = LANES
    keys = sub_keys.reshape(PEER_HEADS * 2, PEER_NKEYS, PEER_QDIM // 2).astype(bf16)
    out = pl.BlockSpec((tb, PEER_SEL), lambda i: (i, 0))
    return pl.pallas_call(
        _peer_route_kernel,
        grid=(m // tb,),
        in_specs=[pl.BlockSpec((tb, qp.shape[1]), lambda i: (i, 0)),
                  pl.BlockSpec(keys.shape, lambda i: (0, 0, 0))],
        out_specs=[out, out],
        out_shape=[jax.ShapeDtypeStruct((m, PEER_SEL), i32), jax.ShapeDtypeStruct((m, PEER_SEL), f32)],
        compiler_params=_params(("parallel",)),
        name="peer_route",
    )(qp, keys)


SC_CORES = 2
SC_SUBCORES = 16
SC_LANES = 16
SC_WORKERS = SC_CORES * SC_SUBCORES
SC_CHUNKS = D_MODEL // SC_LANES
SC_WORDS = D_MODEL // 2
SC_WORD_CHUNKS = SC_WORDS // SC_LANES
SC_TOK_GROUP = 32
SC_ROWS = 64
SC_GATHERS_PER_TOK = PEER_SEL // SC_ROWS
SC_GATHERS = SC_TOK_GROUP * SC_GATHERS_PER_TOK
SC_ROW_GROUP = 8


def _pack_table(tab):
    e = tab.shape[0]
    bits = lax.bitcast_convert_type(tab.astype(bf16), jnp.uint16).astype(jnp.uint32)
    bits = bits.reshape(e, SC_WORD_CHUNKS, 2, SC_LANES)
    word = bits[:, :, 0, :] | (bits[:, :, 1, :] << 16)
    return lax.bitcast_convert_type(word.reshape(e, SC_WORDS), i32)


def _unpack_words(w):
    return lax.bitcast_convert_type(w << 16, f32), lax.bitcast_convert_type(w & (-65536), f32)


def _sc_mesh():
    return plsc.VectorSubcoreMesh(core_axis_name="c", subcore_axis_name="s")


def _sc_worker():
    return lax.axis_index("s") * SC_CORES + lax.axis_index("c")


def _sc_gather_loop(gather, compute):
    gather(0, 0).start()

    @pl.loop(0, SC_GATHERS, step=2)
    def _(q):
        gather(q + 1, 1).start()
        gather(q, 0).wait()
        compute(q, 0)

        @pl.when(q + 2 < SC_GATHERS)
        def _():
            gather(q + 2, 0).start()
        gather(q + 1, 1).wait()
        compute(q + 1, 1)


def _sc_scratch(stage_shape):
    return [pltpu.VMEM((SC_GATHERS, SC_ROWS), i32), pltpu.VMEM(stage_shape[0], f32), pltpu.VMEM(stage_shape[1], f32),
            pltpu.VMEM((SC_ROWS, SC_WORDS), i32), pltpu.VMEM((SC_ROWS, SC_WORDS), i32),
            pltpu.SemaphoreType.DMA, pltpu.SemaphoreType.DMA]


def _peer_udot(x, idx, u_tab):
    t_all = x.shape[0]
    tpw = t_all // SC_WORKERS
    idx2 = idx.reshape(t_all * SC_GATHERS_PER_TOK, SC_ROWS)

    @functools.partial(
        pl.kernel, mesh=_sc_mesh(), compiler_params=pltpu.CompilerParams(needs_layout_passes=False),
        out_type=jax.ShapeDtypeStruct((t_all, PEER_SEL), f32),
        scratch_types=_sc_scratch(((SC_TOK_GROUP, D_MODEL), (SC_TOK_GROUP, PEER_SEL))), name="peer_udot_sc")
    def k(x_hbm, idx_hbm, u_hbm, pre_hbm, idx_v, x_v, pre_v, rows0, rows1, sem0, sem1):
        bufs = ((rows0, sem0), (rows1, sem1))
        base = _sc_worker() * tpw
        lane = lax.iota(i32, SC_LANES)
        zero = jnp.zeros((SC_LANES,), f32)

        def gather(q, b):
            return pltpu.make_async_copy(u_hbm.at[idx_v.at[q]], bufs[b][0], bufs[b][1])

        def compute(q, b):
            rows = bufs[b][0]
            t = q // SC_GATHERS_PER_TOK
            col0 = (q % SC_GATHERS_PER_TOK) * SC_ROWS
            for hh in range(SC_ROWS // SC_LANES):
                vec = zero
                for rg in range(SC_LANES // SC_ROW_GROUP):
                    r0 = hh * SC_LANES + rg * SC_ROW_GROUP

                    def body(c, acc):
                        x_lo = x_v[t, pl.ds(c * 2 * SC_LANES, SC_LANES)]
                        x_hi = x_v[t, pl.ds(c * 2 * SC_LANES + SC_LANES, SC_LANES)]
                        out = []
                        for r in range(SC_ROW_GROUP):
                            lo, hi = _unpack_words(rows[r0 + r, pl.ds(c * SC_LANES, SC_LANES)])
                            out.append(acc[r] + lo * x_lo + hi * x_hi)
                        return tuple(out)

                    acc = lax.fori_loop(0, SC_WORD_CHUNKS, body, (zero,) * SC_ROW_GROUP, unroll=2)
                    for r in range(SC_ROW_GROUP):
                        vec = jnp.where(lane == rg * SC_ROW_GROUP + r, jnp.sum(acc[r]), vec)
                pre_v[t, pl.ds(col0 + hh * SC_LANES, SC_LANES)] = vec

        @pl.loop(0, tpw // SC_TOK_GROUP)
        def _(g):
            tok0 = base + g * SC_TOK_GROUP
            pltpu.sync_copy(idx_hbm.at[pl.ds(tok0 * SC_GATHERS_PER_TOK, SC_GATHERS)], idx_v)
            pltpu.sync_copy(x_hbm.at[pl.ds(tok0, SC_TOK_GROUP)], x_v)
            _sc_gather_loop(gather, compute)
            pltpu.sync_copy(pre_v, pre_hbm.at[pl.ds(tok0, SC_TOK_GROUP)])

    return k(x, idx2, u_tab)


def _peer_vacc(w, idx, v_tab):
    t_all = w.shape[0]
    tpw = t_all // SC_WORKERS
    idx2 = idx.reshape(t_all * SC_GATHERS_PER_TOK, SC_ROWS)
    half_w = SC_WORD_CHUNKS // 2

    @functools.partial(
        pl.kernel, mesh=_sc_mesh(), compiler_params=pltpu.CompilerParams(needs_layout_passes=False),
        out_type=jax.ShapeDtypeStruct((t_all, D_MODEL), f32),
        scratch_types=_sc_scratch(((SC_TOK_GROUP, PEER_SEL), (SC_TOK_GROUP, D_MODEL))), name="peer_vacc_sc")
    def k(w_hbm, idx_hbm, v_hbm, f_hbm, idx_v, w_v, f_v, rows0, rows1, sem0, sem1):
        bufs = ((rows0, sem0), (rows1, sem1))
        base = _sc_worker() * tpw
        zero = jnp.zeros((SC_LANES,), f32)

        def gather(q, b):
            return pltpu.make_async_copy(v_hbm.at[idx_v.at[q]], bufs[b][0], bufs[b][1])

        def compute(q, b):
            rows = bufs[b][0]
            t = q // SC_GATHERS_PER_TOK
            col0 = (q % SC_GATHERS_PER_TOK) * SC_ROWS
            tvec = jnp.full((SC_LANES,), t, i32)
            for half in range(2):
                def body(r, acc):
                    wr = plsc.load_gather(w_v, [tvec, jnp.full((SC_LANES,), col0 + r, i32)])
                    out = []
                    for c in range(half_w):
                        lo, hi = _unpack_words(rows[r, pl.ds((half * half_w + c) * SC_LANES, SC_LANES)])
                        out.append(acc[2 * c] + wr * lo)
                        out.append(acc[2 * c + 1] + wr * hi)
                    return tuple(out)

                acc = lax.fori_loop(0, SC_ROWS, body, (zero,) * (2 * half_w))
                for c in range(2 * half_w):
                    sl = pl.ds((half * 2 * half_w + c) * SC_LANES, SC_LANES)
                    f_v[t, sl] = f_v[t, sl] + acc[c]

        @pl.loop(0, tpw // SC_TOK_GROUP)
        def _(g):
            tok0 = base + g * SC_TOK_GROUP
            pltpu.sync_copy(idx_hbm.at[pl.ds(tok0 * SC_GATHERS_PER_TOK, SC_GATHERS)], idx_v)
            pltpu.sync_copy(w_hbm.at[pl.ds(tok0, SC_TOK_GROUP)], w_v)

            @pl.loop(0, SC_TOK_GROUP)
            def _(t):
                @pl.loop(0, SC_CHUNKS)
                def _(c):
                    f_v[t, pl.ds(c * SC_LANES, SC_LANES)] = zero

            _sc_gather_loop(gather, compute)
            pltpu.sync_copy(f_v, f_hbm.at[pl.ds(tok0, SC_TOK_GROUP)])

    return k(w, idx2, v_tab)


def _peer_act_kernel(pre_ref, gate_ref, w_ref):
    pre = pre_ref[...]
    w_ref[...] = gate_ref[...] * (0.5 * pre * (1.0 + lax.erf(pre * (2.0 ** -0.5))))


def _peer_act(pre, gate, tm=ROWS_LANEWISE):
    m, n = pre.shape
    blk = pl.BlockSpec((tm, n), lambda i: (i, 0))
    return pl.pallas_call(
        _peer_act_kernel, grid=(m // tm,), in_specs=[blk, blk], out_specs=blk,
        out_shape=jax.ShapeDtypeStruct((m, n), f32), compiler_params=_params(("parallel",)), name="peer_act",
    )(pre, gate)


def _res_ln_kernel(x_ref, f_ref, g_ref, b_ref, o_ref):
    o_ref[...] = _layer_norm(ALPHA * x_ref[...] + f_ref[...], g_ref[...], b_ref[...])


def _res_ln(x, f, g, b, tm=ROWS_MATMUL):
    m, n = x.shape
    blk = pl.BlockSpec((tm, n), lambda i: (i, 0))
    row = pl.BlockSpec((1, n), lambda i: (0, 0))
    return pl.pallas_call(
        _res_ln_kernel, grid=(m // tm,), in_specs=[blk, blk, row, row], out_specs=blk,
        out_shape=jax.ShapeDtypeStruct((m, n), f32), compiler_params=_params(("parallel",)), name="res_ln",
    )(x, f, g.reshape(1, n), b.reshape(1, n))


def _peer_expert_ln(x, idx, gate, u_tab, v_tab, g, b):
    pre = _peer_udot(x, idx, u_tab)
    w = _peer_act(pre, gate)
    f = _peer_vacc(w, idx, v_tab)
    return _res_ln(x, f, g, b)


PEER_SPLITS = 1
DN_SEGMENTS = 4


def _peer_ffn_ln(x2, w_query, sub_keys, u_pack, v_pack, g, b, splits=PEER_SPLITS):
    outs = []
    for xs in jnp.split(x2, splits, axis=0):
        qp = _matmul(xs, w_query)
        idx, gate = _peer_route(qp, sub_keys)
        outs.append(_peer_expert_ln(xs, idx, gate, u_pack, v_pack, g, b))
    return jnp.concatenate(outs, axis=0)


def kernel(x, dn_w_in, dn_conv, dn_a_log, dn_dt_bias, dn_norm_w, dn_w_out, shared_w_kv, attn_w_q, attn_w_out,
           peer_w_query, peer_sub_keys, peer_u, peer_v, ln_mix_g, ln_mix_b, ln_ffn_g, ln_ffn_b):
    batch, seq, d = x.shape
    n_a = DEPTH // 2
    tabs = _rope_lane_tables(seq)
    gw = N_GROUPS * WIDTH
    dn_w_pad = [jnp.pad(w, ((0, 0), (0, DN_PROJ_PAD - w.shape[1]))).astype(bf16) for w in dn_w_in]
    dn_w_out_b = dn_w_out.astype(bf16)
    w_k, w_v = shared_w_kv[:, :gw].astype(bf16), shared_w_kv[:, gw:].astype(bf16)
    w_q, w_o = attn_w_q.astype(bf16), attn_w_out.astype(bf16)
    peer_wq = peer_w_query.astype(bf16)
    u_pack = [_pack_table(t) for t in peer_u]
    v_pack = [_pack_table(t) for t in peer_v]

    def trunk(x2):
        k_sh = v_sh = None
        for layer in range(DEPTH):
            if layer < n_a:
                lead = jnp.zeros((SUBLANES, 3 * WIDTH), f32)
                state = jnp.zeros((HEADS, HEAD_DIM, HEAD_DIM), f32)
                segs = []
                for xs in jnp.split(x2, DN_SEGMENTS, axis=0):
                    o, lead, state = _deltanet_mixer(xs, dn_w_pad[layer], dn_conv[layer], dn_a_log[layer],
                                                     dn_dt_bias[layer], dn_norm_w[layer], lead, state)
                    xs = _matmul_res_ln(o, dn_w_out_b[layer], xs, ln_mix_g[layer], ln_mix_b[layer])
                    segs.append(_peer_ffn_ln(xs, peer_wq[layer], peer_sub_keys[layer], u_pack[layer], v_pack[layer],
                                             ln_ffn_g[layer], ln_ffn_b[layer], splits=1))
                x2 = jnp.concatenate(segs, axis=0)
                continue
            else:
                j = layer - n_a
                if j == 0:
                    k_sh = _matmul_rope(x2, w_k, tabs, seq)
                    v_sh = _matmul(x2, w_v, tn=COLS_MATMUL)
                q = _matmul_rope(x2, w_q[j], tabs, seq)
                outs, lses = [], []
                for gi, (window, dilation) in enumerate(ATTN_GROUPS):
                    o, l = _dilated_attention(q, k_sh, v_sh, gi, window, dilation, 1, seq)
                    outs.append(o)
                    lses.append(l)
                x2 = _combine_out_ln(outs, lses, w_o[j], x2, ln_mix_g[layer], ln_mix_b[layer])
            x2 = _peer_ffn_ln(x2, peer_wq[layer], peer_sub_keys[layer], u_pack[layer], v_pack[layer],
                              ln_ffn_g[layer], ln_ffn_b[layer])
        return x2

    return jnp.stack([trunk(x[b]) for b in range(batch)], axis=0)
```

```python
import functools

import jax
import jax.numpy as jnp
from jax import lax
from jax.experimental import pallas as pl
from jax.experimental.pallas import tpu as pltpu
from jax.experimental.pallas import tpu_sc as plsc

f32 = jnp.float32
bf16 = jnp.bfloat16
i32 = jnp.int32

D_MODEL = 1024
DEPTH = 2
ALPHA = (2.0 * DEPTH) ** 0.25
LN_EPS = 1e-5

HEADS = 8
HEAD_DIM = 128
WIDTH = HEADS * HEAD_DIM
CONV_K = 4
DN_CHUNK = 64

ATTN_GROUPS = ((128, 1), (512, 4), (2048, 16))
N_GROUPS = len(ATTN_GROUPS)
ATTN_BLOCK = 128
ROT_DIM = HEAD_DIM // 4
ROPE_THETA = 500000.0

PEER_HEADS = 8
PEER_NKEYS = 128
PEER_TOPK = 16
PEER_QDIM = 256
PEER_SEL = PEER_HEADS * PEER_TOPK

LANES = 128
SUBLANES = 8
VMEM_LIMIT = 48 * 1024 * 1024

ROWS_MATMUL = 512
ROWS_WIDE = 256
ROWS_DELTA = 2 * DN_CHUNK
ROWS_LANEWISE = 2048
COLS_MATMUL = 1024


def _params(sem):
    return pltpu.CompilerParams(dimension_semantics=sem, vmem_limit_bytes=VMEM_LIMIT)


def _dot(a, b):
    return jnp.dot(a.astype(bf16), b.astype(bf16), preferred_element_type=f32)


def _dot_nt(a, b):
    return lax.dot_general(a.astype(bf16), b.astype(bf16), (((1,), (1,)), ((), ())), preferred_element_type=f32)


def _dot_tn(a, b):
    return lax.dot_general(a.astype(bf16), b.astype(bf16), (((0,), (0,)), ((), ())), preferred_element_type=f32)


def _sigmoid(x):
    return 1.0 / (1.0 + jnp.exp(-x))


def _layer_norm(y, g, b):
    mu = jnp.mean(y, -1, keepdims=True)
    yc = y - mu
    var = jnp.mean(yc * yc, -1, keepdims=True)
    return yc * lax.rsqrt(var + LN_EPS) * g + b


def _mm_kernel(a_ref, w_ref, o_ref):
    o_ref[...] = _dot(a_ref[...], w_ref[...])


def _matmul(a, w, tm=ROWS_MATMUL, tn=None):
    m, k = a.shape
    n = w.shape[1]
    tn = tn or n
    return pl.pallas_call(
        _mm_kernel,
        grid=(m // tm, n // tn),
        in_specs=[pl.BlockSpec((tm, k), lambda i, j: (i, 0)), pl.BlockSpec((k, tn), lambda i, j: (0, j))],
        out_specs=pl.BlockSpec((tm, tn), lambda i, j: (i, j)),
        out_shape=jax.ShapeDtypeStruct((m, n), f32),
        compiler_params=_params(("parallel", "parallel")),
        name="matmul",
    )(a, w)


def _rope(y, c, s_lo, s_hi):
    return y * c + pltpu.roll(y, ROT_DIM // 2, axis=1) * s_hi + pltpu.roll(y, LANES - ROT_DIM // 2, axis=1) * s_lo


def _mm_rope_kernel(a_ref, w_ref, c_ref, slo_ref, shi_ref, o_ref, *, heads):
    y = _dot(a_ref[...], w_ref[...])
    c, s_lo, s_hi = c_ref[...], slo_ref[...], shi_ref[...]
    for h in range(heads):
        sl = slice(h * HEAD_DIM, (h + 1) * HEAD_DIM)
        o_ref[:, sl] = _rope(y[:, sl], c, s_lo, s_hi)


def _matmul_rope(a, w, tabs, seq, tm=ROWS_MATMUL, tn=COLS_MATMUL):
    m, k = a.shape
    n = w.shape[1]
    nsb = seq // tm
    tab_spec = pl.BlockSpec((tm, LANES), lambda i, j: (i % nsb, 0))
    return pl.pallas_call(
        functools.partial(_mm_rope_kernel, heads=tn // HEAD_DIM),
        grid=(m // tm, n // tn),
        in_specs=[pl.BlockSpec((tm, k), lambda i, j: (i, 0)), pl.BlockSpec((k, tn), lambda i, j: (0, j)),
                  tab_spec, tab_spec, tab_spec],
        out_specs=pl.BlockSpec((tm, tn), lambda i, j: (i, j)),
        out_shape=jax.ShapeDtypeStruct((m, n), f32),
        compiler_params=_params(("parallel", "parallel")),
        name="matmul_rope",
    )(a, w, *tabs)


def _mm_res_ln_kernel(a_ref, w_ref, x_ref, g_ref, b_ref, o_ref):
    h = _dot(a_ref[...], w_ref[...])
    o_ref[...] = _layer_norm(ALPHA * x_ref[...] + h, g_ref[...], b_ref[...])


def _matmul_res_ln(a, w, x, g, b, tm=ROWS_MATMUL):
    m, k = a.shape
    n = w.shape[1]
    row = pl.BlockSpec((1, n), lambda i: (0, 0))
    return pl.pallas_call(
        _mm_res_ln_kernel,
        grid=(m // tm,),
        in_specs=[pl.BlockSpec((tm, k), lambda i: (i, 0)), pl.BlockSpec((k, n), lambda i: (0, 0)),
                  pl.BlockSpec((tm, n), lambda i: (i, 0)), row, row],
        out_specs=pl.BlockSpec((tm, n), lambda i: (i, 0)),
        out_shape=jax.ShapeDtypeStruct((m, n), f32),
        compiler_params=_params(("parallel",)),
        name="matmul_res_ln",
    )(a, w, x, g.reshape(1, n), b.reshape(1, n))


DN_PROJ_PAD = 4 * WIDTH + LANES


def _dn_prep_kernel(proj_ref, halo_ref, lead_ref, conv_ref, alog_ref, dtb_ref, q_ref, k_ref, v_ref, gate_ref, *, tm):
    i = pl.program_id(0)
    x = proj_ref[:, : 3 * WIDTH]
    halo = jnp.where(i == 0, lead_ref[...], halo_ref[...])
    w = conv_ref[...]
    row8 = lax.broadcasted_iota(i32, (SUBLANES, 1), 0)
    acc = x * w[CONV_K - 1:CONV_K]
    for s in range(1, CONV_K):
        xs = pltpu.roll(x, s, axis=0)
        hs = pltpu.roll(halo, s, axis=0)
        first = jnp.where(row8 < s, hs, xs[:SUBLANES])
        xs = jnp.concatenate([first, xs[SUBLANES:]], axis=0)
        acc = acc + xs * w[CONV_K - 1 - s:CONV_K - s]
    qkv = acc * _sigmoid(acc)
    for h in range(HEADS):
        sl = slice(h * HEAD_DIM, (h + 1) * HEAD_DIM)
        qh = qkv[:, sl]
        q_ref[:, sl] = qh * lax.rsqrt(jnp.sum(qh * qh, -1, keepdims=True) + 1e-6) * (HEAD_DIM ** -0.5)
        kh = qkv[:, WIDTH + h * HEAD_DIM: WIDTH + (h + 1) * HEAD_DIM]
        k_ref[:, sl] = kh * lax.rsqrt(jnp.sum(kh * kh, -1, keepdims=True) + 1e-6)
    v_ref[...] = qkv[:, 2 * WIDTH:]
    ab = proj_ref[:, 4 * WIDTH:]
    z = ab + dtb_ref[...]
    softplus = jnp.maximum(z, 0.0) + jnp.log(1.0 + jnp.exp(-jnp.abs(z)))
    g = -jnp.exp(alog_ref[...]) * softplus
    row = lax.broadcasted_iota(i32, (tm, 1), 0) % DN_CHUNK
    s = 1
    while s < DN_CHUNK:
        g = g + jnp.where(row >= s, pltpu.roll(g, s, axis=0), 0.0)
        s *= 2
    lane = lax.broadcasted_iota(i32, (1, LANES), 1)
    gate_ref[...] = jnp.where(lane < HEADS, g, _sigmoid(ab))


def _dn_prep(proj, lead, conv_w, a_log, dt_bias, tm=ROWS_WIDE):
    m = proj.shape[0]
    pad = LANES - HEADS
    alog = jnp.pad(a_log.astype(f32), (0, pad)).reshape(1, LANES)
    dtb = jnp.pad(dt_bias.astype(f32), (0, pad)).reshape(1, LANES)
    hb = tm // SUBLANES
    out_w = pl.BlockSpec((tm, WIDTH), lambda i: (i, 0))
    return pl.pallas_call(
        functools.partial(_dn_prep_kernel, tm=tm),
        grid=(m // tm,),
        in_specs=[pl.BlockSpec((tm, DN_PROJ_PAD), lambda i: (i, 0)),
                  pl.BlockSpec((SUBLANES, 3 * WIDTH), lambda i: (jnp.maximum(i * hb - 1, 0), 0)),
                  pl.BlockSpec((SUBLANES, 3 * WIDTH), lambda i: (0, 0)),
                  pl.BlockSpec((CONV_K, 3 * WIDTH), lambda i: (0, 0)),
                  pl.BlockSpec((1, LANES), lambda i: (0, 0)), pl.BlockSpec((1, LANES), lambda i: (0, 0))],
        out_specs=[out_w, out_w, out_w, pl.BlockSpec((tm, LANES), lambda i: (i, 0))],
        out_shape=[jax.ShapeDtypeStruct((m, WIDTH), f32)] * 3 + [jax.ShapeDtypeStruct((m, LANES), f32)],
        compiler_params=_params(("parallel",)),
        name="dn_prep",
    )(proj, proj, lead, conv_w, alog, dtb)


def _dn_delta_kernel(q_ref, k_ref, v_ref, z_ref, gate_ref, gt_ref, nw_ref, s0_ref, o_ref, state_ref, *, chunks):
    @pl.when(pl.program_id(0) == 0)
    def _():
        state_ref[...] = s0_ref[...]

    c = DN_CHUNK
    ri = lax.broadcasted_iota(i32, (c, c), 0)
    ci = lax.broadcasted_iota(i32, (c, c), 1)
    causal = ri >= ci
    strict = ri > ci
    eye = (ri == ci).astype(f32)
    nw = nw_ref[...]
    inst = [(ch, h) for ch in range(chunks) for h in range(HEADS)]
    pre = {}
    for ch, h in inst:
        rows = slice(ch * c, (ch + 1) * c)
        sl = slice(h * HEAD_DIM, (h + 1) * HEAD_DIM)
        q, k, v = q_ref[rows, sl], k_ref[rows, sl], v_ref[rows, sl]
        gcol = gate_ref[rows, h:h + 1]
        beta = gate_ref[rows, HEADS + h:HEADS + h + 1]
        grow = gt_ref[h:h + 1, rows]
        decay = jnp.where(causal, jnp.exp(jnp.where(causal, gcol - grow, 0.0)), 0.0)
        kb = k * beta
        egc = jnp.exp(gcol)
        glast = gcol[c - 1:c, :]
        qk = _dot_nt(jnp.concatenate([q, kb], axis=0), k)
        m = jnp.where(strict, qk[c:] * decay, 0.0)
        pre[ch, h] = dict(a_qk=qk[:c] * decay, inv=eye - m, p=m,
                          rhs=jnp.concatenate([v * beta, kb * egc], axis=1), qe=q * egc,
                          kd=k * jnp.exp(glast - gcol), eg=jnp.exp(glast))
    step = 2
    while step < c:
        for key in inst:
            pre[key]["p"] = _dot(pre[key]["p"], pre[key]["p"])
        for key in inst:
            pre[key]["inv"] = pre[key]["inv"] + _dot(pre[key]["inv"], pre[key]["p"])
        step *= 2
    for key in inst:
        pre[key]["uw"] = _dot(pre[key]["inv"], pre[key]["rhs"])
    states = [state_ref[h] for h in range(HEADS)]
    for ch in range(chunks):
        rows = slice(ch * c, (ch + 1) * c)
        for h in range(HEADS):
            sl = slice(h * HEAD_DIM, (h + 1) * HEAD_DIM)
            d = pre[ch, h]
            u, w = d["uw"][:, :HEAD_DIM], d["uw"][:, HEAD_DIM:]
            ws = _dot(jnp.concatenate([w, d["qe"]], axis=0), states[h])
            v_new = u - ws[:c]
            o = ws[c:] + _dot(d["a_qk"], v_new)
            states[h] = states[h] * d["eg"] + _dot_tn(d["kd"], v_new)
            o = o * lax.rsqrt(jnp.mean(o * o, -1, keepdims=True) + 1e-6) * nw
            zz = z_ref[rows, sl]
            o_ref[rows, sl] = o * (zz * _sigmoid(zz))
    for h in range(HEADS):
        state_ref[h] = states[h]


def _dn_delta(q, k, v, proj, gates, gates_t, norm_w, state, cb=ROWS_DELTA):
    m = q.shape[0]
    wide = pl.BlockSpec((cb, WIDTH), lambda i: (i, 0))
    whole_state = pl.BlockSpec(state.shape, lambda i: (0, 0, 0))
    return pl.pallas_call(
        functools.partial(_dn_delta_kernel, chunks=cb // DN_CHUNK),
        grid=(m // cb,),
        in_specs=[wide, wide, wide,
                  pl.BlockSpec((cb, WIDTH), lambda i: (i, 3)),
                  pl.BlockSpec((cb, LANES), lambda i: (i, 0)),
                  pl.BlockSpec((SUBLANES, cb), lambda i: (0, i)),
                  pl.BlockSpec((1, HEAD_DIM), lambda i: (0, 0)),
                  whole_state],
        out_specs=[wide, whole_state],
        out_shape=[jax.ShapeDtypeStruct((m, WIDTH), f32), jax.ShapeDtypeStruct(state.shape, f32)],
        compiler_params=_params(("arbitrary",)),
        name="dn_delta",
    )(q, k, v, proj, gates, gates_t, norm_w.reshape(1, HEAD_DIM).astype(f32), state)


def _deltanet_mixer(x2, w_pad, conv_w, a_log, dt_bias, norm_w, lead, state):
    proj = _matmul(x2, w_pad, tn=DN_PROJ_PAD // 3)
    q, k, v, gates = _dn_prep(proj, lead, conv_w, a_log, dt_bias)
    gates_t = gates[:, :SUBLANES].T
    o, state = _dn_delta(q, k, v, proj, gates, gates_t, norm_w, state)
    return o, proj[-SUBLANES:, :3 * WIDTH], state


def _attn_kernel(q_ref, kp_ref, kc_ref, vp_ref, vc_ref, o_ref, l_ref, *, steps, dilation, heads):
    nbi = pl.program_id(1)
    blk = ATTN_BLOCK
    qi = lax.broadcasted_iota(i32, (blk, 2 * blk), 0)
    kj = lax.broadcasted_iota(i32, (blk, 2 * blk), 1)
    dist = qi + blk - kj
    mask = (dist >= 0) & (dist <= steps) & ((kj >= blk) | (nbi > 0))
    scale = HEAD_DIM ** -0.5
    for r in range(dilation):
        rows = pl.ds(r, blk, stride=dilation) if dilation > 1 else slice(None)
        for h in range(heads):
            sl = slice(h * HEAD_DIM, (h + 1) * HEAD_DIM)
            q = q_ref[rows, sl]
            kk = jnp.concatenate([kp_ref[rows, sl], kc_ref[rows, sl]], axis=0)
            vv = jnp.concatenate([vp_ref[rows, sl], vc_ref[rows, sl]], axis=0)
            sc = jnp.where(mask, _dot_nt(q, kk) * scale, -jnp.inf)
            mx = jnp.max(sc, -1, keepdims=True)
            p = jnp.exp(sc - mx)
            den = jnp.sum(p, -1, keepdims=True)
            o_ref[rows, sl] = _dot(p, vv) / den
            l_ref[rows, sl] = jnp.broadcast_to(mx + jnp.log(den), (blk, HEAD_DIM))


def _dilated_attention(q, k, v, group, window, dilation, batch, seq):
    rows = ATTN_BLOCK * dilation
    nb = seq // rows
    heads = HEADS if dilation == 1 else 1
    hblocks = HEADS // heads
    blk = (rows, heads * HEAD_DIM)

    def cur(b, i, j):
        return (b * nb + i, group * hblocks + j)

    def prev(b, i, j):
        return (b * nb + jnp.maximum(i - 1, 0), group * hblocks + j)

    out_spec = pl.BlockSpec(blk, lambda b, i, j: (b * nb + i, j))
    return pl.pallas_call(
        functools.partial(_attn_kernel, steps=window // dilation, dilation=dilation, heads=heads),
        grid=(batch, nb, hblocks),
        in_specs=[pl.BlockSpec(blk, cur), pl.BlockSpec(blk, prev), pl.BlockSpec(blk, cur),
                  pl.BlockSpec(blk, prev), pl.BlockSpec(blk, cur)],
        out_specs=[out_spec, out_spec],
        out_shape=[jax.ShapeDtypeStruct((batch * seq, WIDTH), f32)] * 2,
        compiler_params=_params(("parallel", "parallel", "parallel")),
        name=f"dilated_attn_d{dilation}",
    )(q, k, k, v, v)


def _combine_out_ln_kernel(o0, o1, o2, l0, l1, l2, w_ref, x_ref, g_ref, b_ref, out_ref):
    ls = [l0[...], l1[...], l2[...]]
    mx = jnp.maximum(jnp.maximum(ls[0], ls[1]), ls[2])
    es = [jnp.exp(l - mx) for l in ls]
    o = (es[0] * o0[...] + es[1] * o1[...] + es[2] * o2[...]) / (es[0] + es[1] + es[2])
    h = _dot(o, w_ref[...])
    out_ref[...] = _layer_norm(ALPHA * x_ref[...] + h, g_ref[...], b_ref[...])


def _combine_out_ln(outs, lses, w, x, g, b, tm=ROWS_WIDE):
    m, n = x.shape
    blk = pl.BlockSpec((tm, n), lambda i: (i, 0))
    row = pl.BlockSpec((1, n), lambda i: (0, 0))
    return pl.pallas_call(
        _combine_out_ln_kernel,
        grid=(m // tm,),
        in_specs=[blk] * 6 + [pl.BlockSpec((WIDTH, n), lambda i: (0, 0)), blk, row, row],
        out_specs=blk,
        out_shape=jax.ShapeDtypeStruct((m, n), f32),
        compiler_params=_params(("parallel",)),
        name="attn_combine_out_ln",
    )(*outs, *lses, w, x, g.reshape(1, n), b.reshape(1, n))


def _rope_lane_tables(seq):
    half = ROT_DIM // 2
    inv_freq = ROPE_THETA ** (-jnp.arange(half, dtype=f32) * 2.0 / ROT_DIM)
    ang = jnp.arange(seq, dtype=f32)[:, None] * inv_freq[None, :]
    cos, sin = jnp.cos(ang), jnp.sin(ang)
    ones = jnp.ones((seq, HEAD_DIM - ROT_DIM), f32)
    zeros = jnp.zeros((seq, HEAD_DIM - half), f32)
    c = jnp.concatenate([cos, cos, ones], axis=1)
    s_lo = jnp.concatenate([-sin, zeros], axis=1)
    s_hi = jnp.concatenate([jnp.zeros((seq, half), f32), sin, jnp.zeros((seq, HEAD_DIM - ROT_DIM), f32)], axis=1)
    return c, s_lo, s_hi


def _top_rows(s, order, payload, k):
    big = jnp.int32(2 ** 30)
    vals, pays = [], []
    for _ in range(k):
        m = jnp.max(s, axis=0, keepdims=True)
        pos = jnp.min(jnp.where(s == m, order, big), axis=0, keepdims=True)
        sel = order == pos
        pays.append(pos if payload is None else jnp.max(jnp.where(sel, payload, -1), axis=0, keepdims=True))
        vals.append(m)
        s = jnp.where(sel, -jnp.inf, s)
    return jnp.concatenate(vals, axis=0), jnp.concatenate(pays, axis=0)


def _pair_candidates(t1, t2, combine, fill):
    kt = PEER_TOPK
    row = lax.broadcasted_iota(i32, (SUBLANES, LANES), 0)
    bc = lambda t, r: jnp.broadcast_to(t[r:r + 1], (SUBLANES, LANES))
    lo1, hi1, lo2, hi2 = t1[:SUBLANES], t1[SUBLANES:], t2[:SUBLANES], t2[SUBLANES:]
    tiles = [combine(bc(t1, 0), lo2), combine(bc(t1, 0), hi2), combine(bc(t1, 1), lo2),
             jnp.where(row < kt // 3, combine(bc(t1, 2), lo2), fill),
             jnp.where(row < kt // 4, combine(bc(t1, 3), lo2), fill),
             combine(hi1, bc(t2, 0)),
             jnp.where(row >= 4, combine(lo1, bc(t2, 0)), fill),
             jnp.where(row >= 4, combine(lo1, bc(t2, 1)), jnp.where(row == 0, combine(bc(t1, 4), bc(t2, 2)), fill))]
    return jnp.concatenate(tiles, axis=0)


def _pair_rank():
    kt = PEER_TOPK
    row = lax.broadcasted_iota(i32, (SUBLANES, LANES), 0)
    unused = kt * kt + row
    tiles = [row, row + SUBLANES, kt + row,
             jnp.where(row < kt // 3, 2 * kt + row, unused),
             jnp.where(row < kt // 4, 3 * kt + row, unused + SUBLANES),
             (row + SUBLANES) * kt,
             jnp.where(row >= 4, row * kt, unused + 2 * SUBLANES),
             jnp.where(row >= 4, row * kt + 1, jnp.where(row == 0, 4 * kt + 2, unused + 3 * SUBLANES))]
    return jnp.concatenate(tiles, axis=0)


def _peer_route_kernel(qp_ref, keys_ref, idx_ref, gate_ref):
    kt = PEER_TOPK
    half = PEER_QDIM // 2
    keyid = lax.broadcasted_iota(i32, (PEER_NKEYS, LANES), 0)
    rank = _pair_rank()
    idx_rows, gate_rows = [], []
    for h in range(PEER_HEADS):
        tops = []
        for c in range(2):
            col = (h * 2 + c) * half
            s = _dot_nt(keys_ref[h * 2 + c], qp_ref[:, col:col + half])
            tops.append(_top_rows(s, keyid, None, kt))
        (v1, i1), (v2, i2) = tops
        cand = _pair_candidates(v1, v2, lambda a, b: a + b, -jnp.inf)
        cand_id = _pair_candidates(i1, i2, lambda a, b: a * PEER_NKEYS + b, -1)
        best, ids = _top_rows(cand, rank, cand_id, kt)
        e = jnp.exp(best - best[0:1])
        gate_rows.append(e / jnp.sum(e, axis=0, keepdims=True))
        idx_rows.append(ids)
    idx_ref[...] = jnp.concatenate(idx_rows, axis=0).astype(f32).T.astype(i32)
    gate_ref[...] = jnp.concatenate(gate_rows, axis=0).T


def _peer_route(qp, sub_keys):
    m = qp.shape[0]
    tb = LANES
    keys = sub_keys.reshape(PEER_HEADS * 2, PEER_NKEYS, PEER_QDIM // 2).astype(bf16)
    out = pl.BlockSpec((tb, PEER_SEL), lambda i: (i, 0))
    return pl.pallas_call(
        _peer_route_kernel,
        grid=(m // tb,),
        in_specs=[pl.BlockSpec((tb, qp.shape[1]), lambda i: (i, 0)),
                  pl.BlockSpec(keys.shape, lambda i: (0, 0, 0))],
        out_specs=[out, out],
        out_shape=[jax.ShapeDtypeStruct((m, PEER_SEL), i32), jax.ShapeDtypeStruct((m, PEER_SEL), f32)],
        compiler_params=_params(("parallel",)),
        name="peer_route",
    )(qp, keys)


SC_CORES = 2
SC_SUBCORES = 16
SC_LANES = 16
SC_WORKERS = SC_CORES * SC_SUBCORES
SC_CHUNKS = D_MODEL // SC_LANES
SC_WORDS = D_MODEL // 2
SC_WORD_CHUNKS = SC_WORDS // SC_LANES
SC_TOK_GROUP = 32
SC_ROWS = 64
SC_GATHERS_PER_TOK = PEER_SEL // SC_ROWS
SC_GATHERS = SC_TOK_GROUP * SC_GATHERS_PER_TOK
SC_ROW_GROUP = 8


def _pack_table(tab):
    e = tab.shape[0]
    bits = lax.bitcast_convert_type(tab.astype(bf16), jnp.uint16).astype(jnp.uint32)
    bits = bits.reshape(e, SC_WORD_CHUNKS, 2, SC_LANES)
    word = bits[:, :, 0, :] | (bits[:, :, 1, :] << 16)
    return lax.bitcast_convert_type(word.reshape(e, SC_WORDS), i32)


def _unpack_words(w):
    return lax.bitcast_convert_type(w << 16, f32), lax.bitcast_convert_type(w & (-65536), f32)


def _sc_mesh():
    return plsc.VectorSubcoreMesh(core_axis_name="c", subcore_axis_name="s")


def _sc_worker():
    return lax.axis_index("s") * SC_CORES + lax.axis_index("c")


def _sc_gather_loop(gather, compute):
    gather(0, 0).start()

    @pl.loop(0, SC_GATHERS, step=2)
    def _(q):
        gather(q + 1, 1).start()
        gather(q, 0).wait()
        compute(q, 0)

        @pl.when(q + 2 < SC_GATHERS)
        def _():
            gather(q + 2, 0).start()
        gather(q + 1, 1).wait()
        compute(q + 1, 1)


def _sc_scratch(stage_shape):
    return [pltpu.VMEM((SC_GATHERS, SC_ROWS), i32), pltpu.VMEM(stage_shape[0], f32), pltpu.VMEM(stage_shape[1], f32),
            pltpu.VMEM((SC_ROWS, SC_WORDS), i32), pltpu.VMEM((SC_ROWS, SC_WORDS), i32),
            pltpu.SemaphoreType.DMA, pltpu.SemaphoreType.DMA]


def _peer_udot(x, idx, u_tab):
    t_all = x.shape[0]
    tpw = t_all // SC_WORKERS
    idx2 = idx.reshape(t_all * SC_GATHERS_PER_TOK, SC_ROWS)

    @functools.partial(
        pl.kernel, mesh=_sc_mesh(), compiler_params=pltpu.CompilerParams(needs_layout_passes=False),
        out_type=jax.ShapeDtypeStruct((t_all, PEER_SEL), f32),
        scratch_types=_sc_scratch(((SC_TOK_GROUP, D_MODEL), (SC_TOK_GROUP, PEER_SEL))), name="peer_udot_sc")
    def k(x_hbm, idx_hbm, u_hbm, pre_hbm, idx_v, x_v, pre_v, rows0, rows1, sem0, sem1):
        bufs = ((rows0, sem0), (rows1, sem1))
        base = _sc_worker() * tpw
        lane = lax.iota(i32, SC_LANES)
        zero = jnp.zeros((SC_LANES,), f32)

        def gather(q, b):
            return pltpu.make_async_copy(u_hbm.at[idx_v.at[q]], bufs[b][0], bufs[b][1])

        def compute(q, b):
            rows = bufs[b][0]
            t = q // SC_GATHERS_PER_TOK
            col0 = (q % SC_GATHERS_PER_TOK) * SC_ROWS
            for hh in range(SC_ROWS // SC_LANES):
                vec = zero
                for rg in range(SC_LANES // SC_ROW_GROUP):
                    r0 = hh * SC_LANES + rg * SC_ROW_GROUP

                    def body(c, acc):
                        x_lo = x_v[t, pl.ds(c * 2 * SC_LANES, SC_LANES)]
                        x_hi = x_v[t, pl.ds(c * 2 * SC_LANES + SC_LANES, SC_LANES)]
                        out = []
                        for r in range(SC_ROW_GROUP):
                            lo, hi = _unpack_words(rows[r0 + r, pl.ds(c * SC_LANES, SC_LANES)])
                            out.append(acc[r] + lo * x_lo + hi * x_hi)
                        return tuple(out)

                    acc = lax.fori_loop(0, SC_WORD_CHUNKS, body, (zero,) * SC_ROW_GROUP, unroll=2)
                    for r in range(SC_ROW_GROUP):
                        vec = jnp.where(lane == rg * SC_ROW_GROUP + r, jnp.sum(acc[r]), vec)
                pre_v[t, pl.ds(col0 + hh * SC_LANES, SC_LANES)] = vec

        @pl.loop(0, tpw // SC_TOK_GROUP)
        def _(g):
            tok0 = base + g * SC_TOK_GROUP
            pltpu.sync_copy(idx_hbm.at[pl.ds(tok0 * SC_GATHERS_PER_TOK, SC_GATHERS)], idx_v)
            pltpu.sync_copy(x_hbm.at[pl.ds(tok0, SC_TOK_GROUP)], x_v)
            _sc_gather_loop(gather, compute)
            pltpu.sync_copy(pre_v, pre_hbm.at[pl.ds(tok0, SC_TOK_GROUP)])

    return k(x, idx2, u_tab)


def _peer_vacc(w, idx, v_tab):
    t_all = w.shape[0]
    tpw = t_all // SC_WORKERS
    idx2 = idx.reshape(t_all * SC_GATHERS_PER_TOK, SC_ROWS)
    half_w = SC_WORD_CHUNKS // 2

    @functools.partial(
        pl.kernel, mesh=_sc_mesh(), compiler_params=pltpu.CompilerParams(needs_layout_passes=False),
        out_type=jax.ShapeDtypeStruct((t_all, D_MODEL), f32),
        scratch_types=_sc_scratch(((SC_TOK_GROUP, PEER_SEL), (SC_TOK_GROUP, D_MODEL))), name="peer_vacc_sc")
    def k(w_hbm, idx_hbm, v_hbm, f_hbm, idx_v, w_v, f_v, rows0, rows1, sem0, sem1):
        bufs = ((rows0, sem0), (rows1, sem1))
        base = _sc_worker() * tpw
        zero = jnp.zeros((SC_LANES,), f32)

        def gather(q, b):
            return pltpu.make_async_copy(v_hbm.at[idx_v.at[q]], bufs[b][0], bufs[b][1])

        def compute(q, b):
            rows = bufs[b][0]
            t = q // SC_GATHERS_PER_TOK
            col0 = (q % SC_GATHERS_PER_TOK) * SC_ROWS
            tvec = jnp.full((SC_LANES,), t, i32)
            for half in range(2):
                def body(r, acc):
                    wr = plsc.load_gather(w_v, [tvec, jnp.full((SC_LANES,), col0 + r, i32)])
                    out = []
                    for c in range(half_w):
                        lo, hi = _unpack_words(rows[r, pl.ds((half * half_w + c) * SC_LANES, SC_LANES)])
                        out.append(acc[2 * c] + wr * lo)
                        out.append(acc[2 * c + 1] + wr * hi)
                    return tuple(out)

                acc = lax.fori_loop(0, SC_ROWS, body, (zero,) * (2 * half_w))
                for c in range(2 * half_w):
                    sl = pl.ds((half * 2 * half_w + c) * SC_LANES, SC_LANES)
                    f_v[t, sl] = f_v[t, sl] + acc[c]

        @pl.loop(0, tpw // SC_TOK_GROUP)
        def _(g):
            tok0 = base + g * SC_TOK_GROUP
            pltpu.sync_copy(idx_hbm.at[pl.ds(tok0 * SC_GATHERS_PER_TOK, SC_GATHERS)], idx_v)
            pltpu.sync_copy(w_hbm.at[pl.ds(tok0, SC_TOK_GROUP)], w_v)

            @pl.loop(0, SC_TOK_GROUP)
            def _(t):
                @pl.loop(0, SC_CHUNKS)
                def _(c):
                    f_v[t, pl.ds(c * SC_LANES, SC_LANES)] = zero

            _sc_gather_loop(gather, compute)
            pltpu.sync_copy(f_v, f_hbm.at[pl.ds(tok0, SC_TOK_GROUP)])

    return k(w, idx2, v_tab)


def _peer_act_kernel(pre_ref, gate_ref, w_ref):
    pre = pre_ref[...]
    w_ref[...] = gate_ref[...] * (0.5 * pre * (1.0 + lax.erf(pre * (2.0 ** -0.5))))


def _peer_act(pre, gate, tm=ROWS_LANEWISE):
    m, n = pre.shape
    blk = pl.BlockSpec((tm, n), lambda i: (i, 0))
    return pl.pallas_call(
        _peer_act_kernel, grid=(m // tm,), in_specs=[blk, blk], out_specs=blk,
        out_shape=jax.ShapeDtypeStruct((m, n), f32), compiler_params=_params(("parallel",)), name="peer_act",
    )(pre, gate)


def _res_ln_kernel(x_ref, f_ref, g_ref, b_ref, o_ref):
    o_ref[...] = _layer_norm(ALPHA * x_ref[...] + f_ref[...], g_ref[...], b_ref[...])


def _res_ln(x, f, g, b, tm=ROWS_MATMUL):
    m, n = x.shape
    blk = pl.BlockSpec((tm, n), lambda i: (i, 0))
    row = pl.BlockSpec((1, n), lambda i: (0, 0))
    return pl.pallas_call(
        _res_ln_kernel, grid=(m // tm,), in_specs=[blk, blk, row, row], out_specs=blk,
        out_shape=jax.ShapeDtypeStruct((m, n), f32), compiler_params=_params(("parallel",)), name="res_ln",
    )(x, f, g.reshape(1, n), b.reshape(1, n))


def _peer_expert_ln(x, idx, gate, u_tab, v_tab, g, b):
    pre = _peer_udot(x, idx, u_tab)
    w = _peer_act(pre, gate)
    f = _peer_vacc(w, idx, v_tab)
    return _res_ln(x, f, g, b)


PEER_SPLITS = 2
DN_SEGMENTS = 8


def _peer_ffn_ln(x2, w_query, sub_keys, u_pack, v_pack, g, b, splits=PEER_SPLITS):
    outs = []
    for xs in jnp.split(x2, splits, axis=0):
        qp = _matmul(xs, w_query)
        idx, gate = _peer_route(qp, sub_keys)
        outs.append(_peer_expert_ln(xs, idx, gate, u_pack, v_pack, g, b))
    return jnp.concatenate(outs, axis=0)


def kernel(x, dn_w_in, dn_conv, dn_a_log, dn_dt_bias, dn_norm_w, dn_w_out, shared_w_kv, attn_w_q, attn_w_out,
           peer_w_query, peer_sub_keys, peer_u, peer_v, ln_mix_g, ln_mix_b, ln_ffn_g, ln_ffn_b):
    batch, seq, d = x.shape
    n_a = DEPTH // 2
    tabs = _rope_lane_tables(seq)
    gw = N_GROUPS * WIDTH
    dn_w_pad = [jnp.pad(w, ((0, 0), (0, DN_PROJ_PAD - w.shape[1]))).astype(bf16) for w in dn_w_in]
    dn_w_out_b = dn_w_out.astype(bf16)
    w_k, w_v = shared_w_kv[:, :gw].astype(bf16), shared_w_kv[:, gw:].astype(bf16)
    w_q, w_o = attn_w_q.astype(bf16), attn_w_out.astype(bf16)
    peer_wq = peer_w_query.astype(bf16)
    u_pack = [_pack_table(t) for t in peer_u]
    v_pack = [_pack_table(t) for t in peer_v]

    def trunk(x2):
        k_sh = v_sh = None
        for layer in range(DEPTH):
            if layer < n_a:
                lead = jnp.zeros((SUBLANES, 3 * WIDTH), f32)
                state = jnp.zeros((HEADS, HEAD_DIM, HEAD_DIM), f32)
                segs = []
                for xs in jnp.split(x2, DN_SEGMENTS, axis=0):
                    o, lead, state = _deltanet_mixer(xs, dn_w_pad[layer], dn_conv[layer], dn_a_log[layer],
                                                     dn_dt_bias[layer], dn_norm_w[layer], lead, state)
                    xs = _matmul_res_ln(o, dn_w_out_b[layer], xs, ln_mix_g[layer], ln_mix_b[layer])
                    segs.append(_peer_ffn_ln(xs, peer_wq[layer], peer_sub_keys[layer], u_pack[layer], v_pack[layer],
                                             ln_ffn_g[layer], ln_ffn_b[layer], splits=1))
                x2 = jnp.concatenate(segs, axis=0)
                continue
            else:
                j = layer - n_a
                if j == 0:
                    k_sh = _matmul_rope(x2, w_k, tabs, seq)
                    v_sh = _matmul(x2, w_v, tn=COLS_MATMUL)
                q = _matmul_rope(x2, w_q[j], tabs, seq)
                outs, lses = [], []
                for gi, (window, dilation) in enumerate(ATTN_GROUPS):
                    o, l = _dilated_attention(q, k_sh, v_sh, gi, window, dilation, 1, seq)
                    outs.append(o)
                    lses.append(l)
                x2 = _combine_out_ln(outs, lses, w_o[j], x2, ln_mix_g[layer], ln_mix_b[layer])
            x2 = _peer_ffn_ln(x2, peer_wq[layer], peer_sub_keys[layer], u_pack[layer], v_pack[layer],
                              ln_ffn_g[layer], ln_ffn_b[layer])
        return x2

    return jnp.stack([trunk(x[b]) for b in range(batch)], axis=0)
```

```python
import functools

import jax
import jax.numpy as jnp
from jax import lax
from jax.experimental import pallas as pl
from jax.experimental.pallas import tpu as pltpu
from jax.experimental.pallas import tpu_sc as plsc

f32 = jnp.float32
bf16 = jnp.bfloat16
i32 = jnp.int32

D_MODEL = 1024
DEPTH = 2
ALPHA = (2.0 * DEPTH) ** 0.25
LN_EPS = 1e-5

HEADS = 8
HEAD_DIM = 128
WIDTH = HEADS * HEAD_DIM
CONV_K = 4
DN_CHUNK = 64

ATTN_GROUPS = ((128, 1), (512, 4), (2048, 16))
N_GROUPS = len(ATTN_GROUPS)
ATTN_BLOCK = 128
ROT_DIM = HEAD_DIM // 4
ROPE_THETA = 500000.0

PEER_HEADS = 8
PEER_NKEYS = 128
PEER_TOPK = 16
PEER_QDIM = 256
PEER_SEL = PEER_HEADS * PEER_TOPK

LANES = 128
SUBLANES = 8
VMEM_LIMIT = 48 * 1024 * 1024

ROWS_MATMUL = 512
ROWS_WIDE = 256
ROWS_DELTA = 2 * DN_CHUNK
ROWS_LANEWISE = 2048
COLS_MATMUL = 1024


def _params(sem):
    return pltpu.CompilerParams(dimension_semantics=sem, vmem_limit_bytes=VMEM_LIMIT)


def _dot(a, b):
    return jnp.dot(a.astype(bf16), b.astype(bf16), preferred_element_type=f32)


def _dot_nt(a, b):
    return lax.dot_general(a.astype(bf16), b.astype(bf16), (((1,), (1,)), ((), ())), preferred_element_type=f32)


def _dot_tn(a, b):
    return lax.dot_general(a.astype(bf16), b.astype(bf16), (((0,), (0,)), ((), ())), preferred_element_type=f32)


def _sigmoid(x):
    return 1.0 / (1.0 + jnp.exp(-x))


def _layer_norm(y, g, b):
    mu = jnp.mean(y, -1, keepdims=True)
    yc = y - mu
    var = jnp.mean(yc * yc, -1, keepdims=True)
    return yc * lax.rsqrt(var + LN_EPS) * g + b


def _mm_kernel(a_ref, w_ref, o_ref):
    o_ref[...] = _dot(a_ref[...], w_ref[...])


def _matmul(a, w, tm=ROWS_MATMUL, tn=None):
    m, k = a.shape
    n = w.shape[1]
    tn = tn or n
    return pl.pallas_call(
        _mm_kernel,
        grid=(m // tm, n // tn),
        in_specs=[pl.BlockSpec((tm, k), lambda i, j: (i, 0)), pl.BlockSpec((k, tn), lambda i, j: (0, j))],
        out_specs=pl.BlockSpec((tm, tn), lambda i, j: (i, j)),
        out_shape=jax.ShapeDtypeStruct((m, n), f32),
        compiler_params=_params(("parallel", "parallel")),
        name="matmul",
    )(a, w)


def _rope(y, c, s_lo, s_hi):
    return y * c + pltpu.roll(y, ROT_DIM // 2, axis=1) * s_hi + pltpu.roll(y, LANES - ROT_DIM // 2, axis=1) * s_lo


def _mm_rope_kernel(a_ref, w_ref, c_ref, slo_ref, shi_ref, o_ref, *, heads):
    y = _dot(a_ref[...], w_ref[...])
    c, s_lo, s_hi = c_ref[...], slo_ref[...], shi_ref[...]
    for h in range(heads):
        sl = slice(h * HEAD_DIM, (h + 1) * HEAD_DIM)
        o_ref[:, sl] = _rope(y[:, sl], c, s_lo, s_hi)


def _matmul_rope(a, w, tabs, seq, tm=ROWS_MATMUL, tn=COLS_MATMUL):
    m, k = a.shape
    n = w.shape[1]
    nsb = seq // tm
    tab_spec = pl.BlockSpec((tm, LANES), lambda i, j: (i % nsb, 0))
    return pl.pallas_call(
        functools.partial(_mm_rope_kernel, heads=tn // HEAD_DIM),
        grid=(m // tm, n // tn),
        in_specs=[pl.BlockSpec((tm, k), lambda i, j: (i, 0)), pl.BlockSpec((k, tn), lambda i, j: (0, j)),
                  tab_spec, tab_spec, tab_spec],
        out_specs=pl.BlockSpec((tm, tn), lambda i, j: (i, j)),
        out_shape=jax.ShapeDtypeStruct((m, n), f32),
        compiler_params=_params(("parallel", "parallel")),
        name="matmul_rope",
    )(a, w, *tabs)


def _mm_res_ln_kernel(a_ref, w_ref, x_ref, g_ref, b_ref, o_ref):
    h = _dot(a_ref[...], w_ref[...])
    o_ref[...] = _layer_norm(ALPHA * x_ref[...] + h, g_ref[...], b_ref[...])


def _matmul_res_ln(a, w, x, g, b, tm=ROWS_MATMUL):
    m, k = a.shape
    n = w.shape[1]
    row = pl.BlockSpec((1, n), lambda i: (0, 0))
    return pl.pallas_call(
        _mm_res_ln_kernel,
        grid=(m // tm,),
        in_specs=[pl.BlockSpec((tm, k), lambda i: (i, 0)), pl.BlockSpec((k, n), lambda i: (0, 0)),
                  pl.BlockSpec((tm, n), lambda i: (i, 0)), row, row],
        out_specs=pl.BlockSpec((tm, n), lambda i: (i, 0)),
        out_shape=jax.ShapeDtypeStruct((m, n), f32),
        compiler_params=_params(("parallel",)),
        name="matmul_res_ln",
    )(a, w, x, g.reshape(1, n), b.reshape(1, n))


DN_PROJ_PAD = 4 * WIDTH + LANES


def _dn_prep_kernel(proj_ref, halo_ref, lead_ref, conv_ref, alog_ref, dtb_ref, q_ref, k_ref, v_ref, gate_ref, *, tm):
    i = pl.program_id(0)
    x = proj_ref[:, : 3 * WIDTH]
    halo = jnp.where(i == 0, lead_ref[...], halo_ref[...])
    w = conv_ref[...]
    row8 = lax.broadcasted_iota(i32, (SUBLANES, 1), 0)
    acc = x * w[CONV_K - 1:CONV_K]
    for s in range(1, CONV_K):
        xs = pltpu.roll(x, s, axis=0)
        hs = pltpu.roll(halo, s, axis=0)
        first = jnp.where(row8 < s, hs, xs[:SUBLANES])
        xs = jnp.concatenate([first, xs[SUBLANES:]], axis=0)
        acc = acc + xs * w[CONV_K - 1 - s:CONV_K - s]
    qkv = acc * _sigmoid(acc)
    for h in range(HEADS):
        sl = slice(h * HEAD_DIM, (h + 1) * HEAD_DIM)
        qh = qkv[:, sl]
        q_ref[:, sl] = qh * lax.rsqrt(jnp.sum(qh * qh, -1, keepdims=True) + 1e-6) * (HEAD_DIM ** -0.5)
        kh = qkv[:, WIDTH + h * HEAD_DIM: WIDTH + (h + 1) * HEAD_DIM]
        k_ref[:, sl] = kh * lax.rsqrt(jnp.sum(kh * kh, -1, keepdims=True) + 1e-6)
    v_ref[...] = qkv[:, 2 * WIDTH:]
    ab = proj_ref[:, 4 * WIDTH:]
    z = ab + dtb_ref[...]
    softplus = jnp.maximum(z, 0.0) + jnp.log(1.0 + jnp.exp(-jnp.abs(z)))
    g = -jnp.exp(alog_ref[...]) * softplus
    row = lax.broadcasted_iota(i32, (tm, 1), 0) % DN_CHUNK
    s = 1
    while s < DN_CHUNK:
        g = g + jnp.where(row >= s, pltpu.roll(g, s, axis=0), 0.0)
        s *= 2
    lane = lax.broadcasted_iota(i32, (1, LANES), 1)
    gate_ref[...] = jnp.where(lane < HEADS, g, _sigmoid(ab))


def _dn_prep(proj, lead, conv_w, a_log, dt_bias, tm=ROWS_WIDE):
    m = proj.shape[0]
    pad = LANES - HEADS
    alog = jnp.pad(a_log.astype(f32), (0, pad)).reshape(1, LANES)
    dtb = jnp.pad(dt_bias.astype(f32), (0, pad)).reshape(1, LANES)
    hb = tm // SUBLANES
    out_w = pl.BlockSpec((tm, WIDTH), lambda i: (i, 0))
    return pl.pallas_call(
        functools.partial(_dn_prep_kernel, tm=tm),
        grid=(m // tm,),
        in_specs=[pl.BlockSpec((tm, DN_PROJ_PAD), lambda i: (i, 0)),
                  pl.BlockSpec((SUBLANES, 3 * WIDTH), lambda i: (jnp.maximum(i * hb - 1, 0), 0)),
                  pl.BlockSpec((SUBLANES, 3 * WIDTH), lambda i: (0, 0)),
                  pl.BlockSpec((CONV_K, 3 * WIDTH), lambda i: (0, 0)),
                  pl.BlockSpec((1, LANES), lambda i: (0, 0)), pl.BlockSpec((1, LANES), lambda i: (0, 0))],
        out_specs=[out_w, out_w, out_w, pl.BlockSpec((tm, LANES), lambda i: (i, 0))],
        out_shape=[jax.ShapeDtypeStruct((m, WIDTH), f32)] * 3 + [jax.ShapeDtypeStruct((m, LANES), f32)],
        compiler_params=_params(("parallel",)),
        name="dn_prep",
    )(proj, proj, lead, conv_w, alog, dtb)


def _dn_delta_kernel(q_ref, k_ref, v_ref, z_ref, gate_ref, gt_ref, nw_ref, s0_ref, o_ref, state_ref, *, chunks):
    @pl.when(pl.program_id(0) == 0)
    def _():
        state_ref[...] = s0_ref[...]

    c = DN_CHUNK
    ri = lax.broadcasted_iota(i32, (c, c), 0)
    ci = lax.broadcasted_iota(i32, (c, c), 1)
    causal = ri >= ci
    strict = ri > ci
    eye = (ri == ci).astype(f32)
    nw = nw_ref[...]
    inst = [(ch, h) for ch in range(chunks) for h in range(HEADS)]
    pre = {}
    for ch, h in inst:
        rows = slice(ch * c, (ch + 1) * c)
        sl = slice(h * HEAD_DIM, (h + 1) * HEAD_DIM)
        q, k, v = q_ref[rows, sl], k_ref[rows, sl], v_ref[rows, sl]
        gcol = gate_ref[rows, h:h + 1]
        beta = gate_ref[rows, HEADS + h:HEADS + h + 1]
        grow = gt_ref[h:h + 1, rows]
        decay = jnp.where(causal, jnp.exp(jnp.where(causal, gcol - grow, 0.0)), 0.0)
        kb = k * beta
        egc = jnp.exp(gcol)
        glast = gcol[c - 1:c, :]
        qk = _dot_nt(jnp.concatenate([q, kb], axis=0), k)
        m = jnp.where(strict, qk[c:] * decay, 0.0)
        pre[ch, h] = dict(a_qk=qk[:c] * decay, inv=eye - m, p=m,
                          rhs=jnp.concatenate([v * beta, kb * egc], axis=1), qe=q * egc,
                          kd=k * jnp.exp(glast - gcol), eg=jnp.exp(glast))
    step = 2
    while step < c:
        for key in inst:
            pre[key]["p"] = _dot(pre[key]["p"], pre[key]["p"])
        for key in inst:
            pre[key]["inv"] = pre[key]["inv"] + _dot(pre[key]["inv"], pre[key]["p"])
        step *= 2
    for key in inst:
        pre[key]["uw"] = _dot(pre[key]["inv"], pre[key]["rhs"])
    states = [state_ref[h] for h in range(HEADS)]
    for ch in range(chunks):
        rows = slice(ch * c, (ch + 1) * c)
        for h in range(HEADS):
            sl = slice(h * HEAD_DIM, (h + 1) * HEAD_DIM)
            d = pre[ch, h]
            u, w = d["uw"][:, :HEAD_DIM], d["uw"][:, HEAD_DIM:]
            ws = _dot(jnp.concatenate([w, d["qe"]], axis=0), states[h])
            v_new = u - ws[:c]
            o = ws[c:] + _dot(d["a_qk"], v_new)
            states[h] = states[h] * d["eg"] + _dot_tn(d["kd"], v_new)
            o = o * lax.rsqrt(jnp.mean(o * o, -1, keepdims=True) + 1e-6) * nw
            zz = z_ref[rows, sl]
            o_ref[rows, sl] = o * (zz * _sigmoid(zz))
    for h in range(HEADS):
        state_ref[h] = states[h]


def _dn_delta(q, k, v, proj, gates, gates_t, norm_w, state, cb=ROWS_DELTA):
    m = q.shape[0]
    wide = pl.BlockSpec((cb, WIDTH), lambda i: (i, 0))
    whole_state = pl.BlockSpec(state.shape, lambda i: (0, 0, 0))
    return pl.pallas_call(
        functools.partial(_dn_delta_kernel, chunks=cb // DN_CHUNK),
        grid=(m // cb,),
        in_specs=[wide, wide, wide,
                  pl.BlockSpec((cb, WIDTH), lambda i: (i, 3)),
                  pl.BlockSpec((cb, LANES), lambda i: (i, 0)),
                  pl.BlockSpec((SUBLANES, cb), lambda i: (0, i)),
                  pl.BlockSpec((1, HEAD_DIM), lambda i: (0, 0)),
                  whole_state],
        out_specs=[wide, whole_state],
        out_shape=[jax.ShapeDtypeStruct((m, WIDTH), f32), jax.ShapeDtypeStruct(state.shape, f32)],
        compiler_params=_params(("arbitrary",)),
        name="dn_delta",
    )(q, k, v, proj, gates, gates_t, norm_w.reshape(1, HEAD_DIM).astype(f32), state)


def _deltanet_mixer(x2, w_pad, conv_w, a_log, dt_bias, norm_w, lead, state):
    proj = _matmul(x2, w_pad, tn=DN_PROJ_PAD // 3)
    q, k, v, gates = _dn_prep(proj, lead, conv_w, a_log, dt_bias)
    gates_t = gates[:, :SUBLANES].T
    o, state = _dn_delta(q, k, v, proj, gates, gates_t, norm_w, state)
    return o, proj[-SUBLANES:, :3 * WIDTH], state


def _attn_kernel(q_ref, kp_ref, kc_ref, vp_ref, vc_ref, o_ref, l_ref, *, steps, dilation, heads):
    nbi = pl.program_id(1)
    blk = ATTN_BLOCK
    qi = lax.broadcasted_iota(i32, (blk, 2 * blk), 0)
    kj = lax.broadcasted_iota(i32, (blk, 2 * blk), 1)
    dist = qi + blk - kj
    mask = (dist >= 0) & (dist <= steps) & ((kj >= blk) | (nbi > 0))
    scale = HEAD_DIM ** -0.5
    for r in range(dilation):
        rows = pl.ds(r, blk, stride=dilation) if dilation > 1 else slice(None)
        for h in range(heads):
            sl = slice(h * HEAD_DIM, (h + 1) * HEAD_DIM)
            q = q_ref[rows, sl]
            kk = jnp.concatenate([kp_ref[rows, sl], kc_ref[rows, sl]], axis=0)
            vv = jnp.concatenate([vp_ref[rows, sl], vc_ref[rows, sl]], axis=0)
            sc = jnp.where(mask, _dot_nt(q, kk) * scale, -jnp.inf)
            mx = jnp.max(sc, -1, keepdims=True)
            p = jnp.exp(sc - mx)
            den = jnp.sum(p, -1, keepdims=True)
            o_ref[rows, sl] = _dot(p, vv) / den
            l_ref[rows, sl] = jnp.broadcast_to(mx + jnp.log(den), (blk, HEAD_DIM))


def _dilated_attention(q, k, v, group, window, dilation, batch, seq):
    rows = ATTN_BLOCK * dilation
    nb = seq // rows
    heads = HEADS if dilation == 1 else 1
    hblocks = HEADS // heads
    blk = (rows, heads * HEAD_DIM)

    def cur(b, i, j):
        return (b * nb + i, group * hblocks + j)

    def prev(b, i, j):
        return (b * nb + jnp.maximum(i - 1, 0), group * hblocks + j)

    out_spec = pl.BlockSpec(blk, lambda b, i, j: (b * nb + i, j))
    return pl.pallas_call(
        functools.partial(_attn_kernel, steps=window // dilation, dilation=dilation, heads=heads),
        grid=(batch, nb, hblocks),
        in_specs=[pl.BlockSpec(blk, cur), pl.BlockSpec(blk, prev), pl.BlockSpec(blk, cur),
                  pl.BlockSpec(blk, prev), pl.BlockSpec(blk, cur)],
        out_specs=[out_spec, out_spec],
        out_shape=[jax.ShapeDtypeStruct((batch * seq, WIDTH), f32)] * 2,
        compiler_params=_params(("parallel", "parallel", "parallel")),
        name=f"dilated_attn_d{dilation}",
    )(q, k, k, v, v)


def _combine_out_ln_kernel(o0, o1, o2, l0, l1, l2, w_ref, x_ref, g_ref, b_ref, out_ref):
    ls = [l0[...], l1[...], l2[...]]
    mx = jnp.maximum(jnp.maximum(ls[0], ls[1]), ls[2])
    es = [jnp.exp(l - mx) for l in ls]
    o = (es[0] * o0[...] + es[1] * o1[...] + es[2] * o2[...]) / (es[0] + es[1] + es[2])
    h = _dot(o, w_ref[...])
    out_ref[...] = _layer_norm(ALPHA * x_ref[...] + h, g_ref[...], b_ref[...])


def _combine_out_ln(outs, lses, w, x, g, b, tm=ROWS_WIDE):
    m, n = x.shape
    blk = pl.BlockSpec((tm, n), lambda i: (i, 0))
    row = pl.BlockSpec((1, n), lambda i: (0, 0))
    return pl.pallas_call(
        _combine_out_ln_kernel,
        grid=(m // tm,),
        in_specs=[blk] * 6 + [pl.BlockSpec((WIDTH, n), lambda i: (0, 0)), blk, row, row],
        out_specs=blk,
        out_shape=jax.ShapeDtypeStruct((m, n), f32),
        compiler_params=_params(("parallel",)),
        name="attn_combine_out_ln",
    )(*outs, *lses, w, x, g.reshape(1, n), b.reshape(1, n))


def _rope_lane_tables(seq):
    half = ROT_DIM // 2
    inv_freq = ROPE_THETA ** (-jnp.arange(half, dtype=f32) * 2.0 / ROT_DIM)
    ang = jnp.arange(seq, dtype=f32)[:, None] * inv_freq[None, :]
    cos, sin = jnp.cos(ang), jnp.sin(ang)
    ones = jnp.ones((seq, HEAD_DIM - ROT_DIM), f32)
    zeros = jnp.zeros((seq, HEAD_DIM - half), f32)
    c = jnp.concatenate([cos, cos, ones], axis=1)
    s_lo = jnp.concatenate([-sin, zeros], axis=1)
    s_hi = jnp.concatenate([jnp.zeros((seq, half), f32), sin, jnp.zeros((seq, HEAD_DIM - ROT_DIM), f32)], axis=1)
    return c, s_lo, s_hi


def _top_rows(s, order, payload, k):
    big = jnp.int32(2 ** 30)
    vals, pays = [], []
    for _ in range(k):
        m = jnp.max(s, axis=0, keepdims=True)
        pos = jnp.min(jnp.where(s == m, order, big), axis=0, keepdims=True)
        sel = order == pos
        pays.append(pos if payload is None else jnp.max(jnp.where(sel, payload, -1), axis=0, keepdims=True))
        vals.append(m)
        s = jnp.where(sel, -jnp.inf, s)
    return jnp.concatenate(vals, axis=0), jnp.concatenate(pays, axis=0)


def _pair_candidates(t1, t2, combine, fill):
    kt = PEER_TOPK
    row = lax.broadcasted_iota(i32, (SUBLANES, LANES), 0)
    bc = lambda t, r: jnp.broadcast_to(t[r:r + 1], (SUBLANES, LANES))
    lo1, hi1, lo2, hi2 = t1[:SUBLANES], t1[SUBLANES:], t2[:SUBLANES], t2[SUBLANES:]
    tiles = [combine(bc(t1, 0), lo2), combine(bc(t1, 0), hi2), combine(bc(t1, 1), lo2),
             jnp.where(row < kt // 3, combine(bc(t1, 2), lo2), fill),
             jnp.where(row < kt // 4, combine(bc(t1, 3), lo2), fill),
             combine(hi1, bc(t2, 0)),
             jnp.where(row >= 4, combine(lo1, bc(t2, 0)), fill),
             jnp.where(row >= 4, combine(lo1, bc(t2, 1)), jnp.where(row == 0, combine(bc(t1, 4), bc(t2, 2)), fill))]
    return jnp.concatenate(tiles, axis=0)


def _pair_rank():
    kt = PEER_TOPK
    row = lax.broadcasted_iota(i32, (SUBLANES, LANES), 0)
    unused = kt * kt + row
    tiles = [row, row + SUBLANES, kt + row,
             jnp.where(row < kt // 3, 2 * kt + row, unused),
             jnp.where(row < kt // 4, 3 * kt + row, unused + SUBLANES),
             (row + SUBLANES) * kt,
             jnp.where(row >= 4, row * kt, unused + 2 * SUBLANES),
             jnp.where(row >= 4, row * kt + 1, jnp.where(row == 0, 4 * kt + 2, unused + 3 * SUBLANES))]
    return jnp.concatenate(tiles, axis=0)


def _peer_route_kernel(qp_ref, keys_ref, idx_ref, gate_ref):
    kt = PEER_TOPK
    half = PEER_QDIM // 2
    keyid = lax.broadcasted_iota(i32, (PEER_NKEYS, LANES), 0)
    rank = _pair_rank()
    idx_rows, gate_rows = [], []
    for h in range(PEER_HEADS):
        tops = []
        for c in range(2):
            col = (h * 2 + c) * half
            s = _dot_nt(keys_ref[h * 2 + c], qp_ref[:, col:col + half])
            tops.append(_top_rows(s, keyid, None, kt))
        (v1, i1), (v2, i2) = tops
        cand = _pair_candidates(v1, v2, lambda a, b: a + b, -jnp.inf)
        cand_id = _pair_candidates(i1, i2, lambda a, b: a * PEER_NKEYS + b, -1)
        best, ids = _top_rows(cand, rank, cand_id, kt)
        e = jnp.exp(best - best[0:1])
        gate_rows.append(e / jnp.sum(e, axis=0, keepdims=True))
        idx_rows.append(ids)
    idx_ref[...] = jnp.concatenate(idx_rows, axis=0).astype(f32).T.astype(i32)
    gate_ref[...] = jnp.concatenate(gate_rows, axis=0).T


def _peer_route(qp, sub_keys):
    m = qp.shape[0]
    tb = LANES
    keys = sub_keys.reshape(PEER_HEADS * 2, PEER_NKEYS, PEER_QDIM // 2).astype(bf16)
    out = pl.BlockSpec((tb, PEER_SEL), lambda i: (i, 0))
    return pl.pallas_call(
        _peer_route_kernel,
        grid=(m // tb,),
        in_specs=[pl.BlockSpec((tb, qp.shape[1]), lambda i: (i, 0)),
                  pl.BlockSpec(keys.shape, lambda i: (0, 0, 0))],
        out_specs=[out, out],
        out_shape=[jax.ShapeDtypeStruct((m, PEER_SEL), i32), jax.ShapeDtypeStruct((m, PEER_SEL), f32)],
        compiler_params=_params(("parallel",)),
        name="peer_route",
    )(qp, keys)


SC_CORES = 2
SC_SUBCORES = 16
SC_LANES = 16
SC_WORKERS = SC_CORES * SC_SUBCORES
SC_CHUNKS = D_MODEL // SC_LANES
SC_WORDS = D_MODEL // 2
SC_WORD_CHUNKS = SC_WORDS // SC_LANES
SC_TOK_GROUP = 32
SC_ROWS = 64
SC_GATHERS_PER_TOK = PEER_SEL // SC_ROWS
SC_GATHERS = SC_TOK_GROUP * SC_GATHERS_PER_TOK
SC_ROW_GROUP = 8


def _pack_table(tab):
    e = tab.shape[0]
    bits = lax.bitcast_convert_type(tab.astype(bf16), jnp.uint16).astype(jnp.uint32)
    bits = bits.reshape(e, SC_WORD_CHUNKS, 2, SC_LANES)
    word = bits[:, :, 0, :] | (bits[:, :, 1, :] << 16)
    return lax.bitcast_convert_type(word.reshape(e, SC_WORDS), i32)


def _unpack_words(w):
    return lax.bitcast_convert_type(w << 16, f32), lax.bitcast_convert_type(w & (-65536), f32)


def _sc_mesh():
    return plsc.VectorSubcoreMesh(core_axis_name="c", subcore_axis_name="s")


def _sc_worker():
    return lax.axis_index("s") * SC_CORES + lax.axis_index("c")


def _sc_gather_loop(gather, compute):
    gather(0, 0).start()

    @pl.loop(0, SC_GATHERS, step=2)
    def _(q):
        gather(q + 1, 1).start()
        gather(q, 0).wait()
        compute(q, 0)

        @pl.when(q + 2 < SC_GATHERS)
        def _():
            gather(q + 2, 0).start()
        gather(q + 1, 1).wait()
        compute(q + 1, 1)


def _sc_scratch(stage_shape):
    return [pltpu.VMEM((SC_GATHERS, SC_ROWS), i32), pltpu.VMEM(stage_shape[0], f32), pltpu.VMEM(stage_shape[1], f32),
            pltpu.VMEM((SC_ROWS, SC_WORDS), i32), pltpu.VMEM((SC_ROWS, SC_WORDS), i32),
            pltpu.SemaphoreType.DMA, pltpu.SemaphoreType.DMA]


def _peer_udot(x, idx, u_tab):
    t_all = x.shape[0]
    tpw = t_all // SC_WORKERS
    idx2 = idx.reshape(t_all * SC_GATHERS_PER_TOK, SC_ROWS)

    @functools.partial(
        pl.kernel, mesh=_sc_mesh(), compiler_params=pltpu.CompilerParams(needs_layout_passes=False),
        out_type=jax.ShapeDtypeStruct((t_all, PEER_SEL), f32),
        scratch_types=_sc_scratch(((SC_TOK_GROUP, D_MODEL), (SC_TOK_GROUP, PEER_SEL))), name="peer_udot_sc")
    def k(x_hbm, idx_hbm, u_hbm, pre_hbm, idx_v, x_v, pre_v, rows0, rows1, sem0, sem1):
        bufs = ((rows0, sem0), (rows1, sem1))
        base = _sc_worker() * tpw
        lane = lax.iota(i32, SC_LANES)
        zero = jnp.zeros((SC_LANES,), f32)

        def gather(q, b):
            return pltpu.make_async_copy(u_hbm.at[idx_v.at[q]], bufs[b][0], bufs[b][1])

        def compute(q, b):
            rows = bufs[b][0]
            t = q // SC_GATHERS_PER_TOK
            col0 = (q % SC_GATHERS_PER_TOK) * SC_ROWS
            for hh in range(SC_ROWS // SC_LANES):
                vec = zero
                for rg in range(SC_LANES // SC_ROW_GROUP):
                    r0 = hh * SC_LANES + rg * SC_ROW_GROUP

                    def body(c, acc):
                        x_lo = x_v[t, pl.ds(c * 2 * SC_LANES, SC_LANES)]
                        x_hi = x_v[t, pl.ds(c * 2 * SC_LANES + SC_LANES, SC_LANES)]
                        out = []
                        for r in range(SC_ROW_GROUP):
                            lo, hi = _unpack_words(rows[r0 + r, pl.ds(c * SC_LANES, SC_LANES)])
                            out.append(acc[r] + lo * x_lo + hi * x_hi)
                        return tuple(out)

                    acc = lax.fori_loop(0, SC_WORD_CHUNKS, body, (zero,) * SC_ROW_GROUP, unroll=2)
                    for r in range(SC_ROW_GROUP):
                        vec = jnp.where(lane == rg * SC_ROW_GROUP + r, jnp.sum(acc[r]), vec)
                pre_v[t, pl.ds(col0 + hh * SC_LANES, SC_LANES)] = vec

        @pl.loop(0, tpw // SC_TOK_GROUP)
        def _(g):
            tok0 = base + g * SC_TOK_GROUP
            pltpu.sync_copy(idx_hbm.at[pl.ds(tok0 * SC_GATHERS_PER_TOK, SC_GATHERS)], idx_v)
            pltpu.sync_copy(x_hbm.at[pl.ds(tok0, SC_TOK_GROUP)], x_v)
            _sc_gather_loop(gather, compute)
            pltpu.sync_copy(pre_v, pre_hbm.at[pl.ds(tok0, SC_TOK_GROUP)])

    return k(x, idx2, u_tab)


def _peer_vacc(w, idx, v_tab):
    t_all = w.shape[0]
    tpw = t_all // SC_WORKERS
    idx2 = idx.reshape(t_all * SC_GATHERS_PER_TOK, SC_ROWS)
    half_w = SC_WORD_CHUNKS // 2

    @functools.partial(
        pl.kernel, mesh=_sc_mesh(), compiler_params=pltpu.CompilerParams(needs_layout_passes=False),
        out_type=jax.ShapeDtypeStruct((t_all, D_MODEL), f32),
        scratch_types=_sc_scratch(((SC_TOK_GROUP, PEER_SEL), (SC_TOK_GROUP, D_MODEL))), name="peer_vacc_sc")
    def k(w_hbm, idx_hbm, v_hbm, f_hbm, idx_v, w_v, f_v, rows0, rows1, sem0, sem1):
        bufs = ((rows0, sem0), (rows1, sem1))
        base = _sc_worker() * tpw
        zero = jnp.zeros((SC_LANES,), f32)

        def gather(q, b):
            return pltpu.make_async_copy(v_hbm.at[idx_v.at[q]], bufs[b][0], bufs[b][1])

        def compute(q, b):
            rows = bufs[b][0]
            t = q // SC_GATHERS_PER_TOK
            col0 = (q % SC_GATHERS_PER_TOK) * SC_ROWS
            tvec = jnp.full((SC_LANES,), t, i32)
            for half in range(2):
                def body(r, acc):
                    wr = plsc.load_gather(w_v, [tvec, jnp.full((SC_LANES,), col0 + r, i32)])
                    out = []
                    for c in range(half_w):
                        lo, hi = _unpack_words(rows[r, pl.ds((half * half_w + c) * SC_LANES, SC_LANES)])
                        out.append(acc[2 * c] + wr * lo)
                        out.append(acc[2 * c + 1] + wr * hi)
                    return tuple(out)

                acc = lax.fori_loop(0, SC_ROWS, body, (zero,) * (2 * half_w))
                for c in range(2 * half_w):
                    sl = pl.ds((half * 2 * half_w + c) * SC_LANES, SC_LANES)
                    f_v[t, sl] = f_v[t, sl] + acc[c]

        @pl.loop(0, tpw // SC_TOK_GROUP)
        def _(g):
            tok0 = base + g * SC_TOK_GROUP
            pltpu.sync_copy(idx_hbm.at[pl.ds(tok0 * SC_GATHERS_PER_TOK, SC_GATHERS)], idx_v)
            pltpu.sync_copy(w_hbm.at[pl.ds(tok0, SC_TOK_GROUP)], w_v)

            @pl.loop(0, SC_TOK_GROUP)
            def _(t):
                @pl.loop(0, SC_CHUNKS)
                def _(c):
                    f_v[t, pl.ds(c * SC_LANES, SC_LANES)] = zero

            _sc_gather_loop(gather, compute)
            pltpu.sync_copy(f_v, f_hbm.at[pl.ds(tok0, SC_TOK_GROUP)])

    return k(w, idx2, v_tab)


def _peer_act_kernel(pre_ref, gate_ref, w_ref):
    pre = pre_ref[...]
    w_ref[...] = gate_ref[...] * (0.5 * pre * (1.0 + lax.erf(pre * (2.0 ** -0.5))))


def _peer_act(pre, gate, tm=ROWS_LANEWISE):
    m, n = pre.shape
    blk = pl.BlockSpec((tm, n), lambda i: (i, 0))
    return pl.pallas_call(
        _peer_act_kernel, grid=(m // tm,), in_specs=[blk, blk], out_specs=blk,
        out_shape=jax.ShapeDtypeStruct((m, n), f32), compiler_params=_params(("parallel",)), name="peer_act",
    )(pre, gate)


def _res_ln_kernel(x_ref, f_ref, g_ref, b_ref, o_ref):
    o_ref[...] = _layer_norm(ALPHA * x_ref[...] + f_ref[...], g_ref[...], b_ref[...])


def _res_ln(x, f, g, b, tm=ROWS_MATMUL):
    m, n = x.shape
    blk = pl.BlockSpec((tm, n), lambda i: (i, 0))
    row = pl.BlockSpec((1, n), lambda i: (0, 0))
    return pl.pallas_call(
        _res_ln_kernel, grid=(m // tm,), in_specs=[blk, blk, row, row], out_specs=blk,
        out_shape=jax.ShapeDtypeStruct((m, n), f32), compiler_params=_params(("parallel",)), name="res_ln",
    )(x, f, g.reshape(1, n), b.reshape(1, n))


def _peer_expert_ln(x, idx, gate, u_tab, v_tab, g, b):
    pre = _peer_udot(x, idx, u_tab)
    w = _peer_act(pre, gate)
    f = _peer_vacc(w, idx, v_tab)
    return _res_ln(x, f, g, b)


PEER_SPLITS = 2
DN_SEGMENTS = 4
DN_MERGE_TAIL = 2


def _peer_ffn_ln(x2, w_query, sub_keys, u_pack, v_pack, g, b, splits=PEER_SPLITS):
    outs = []
    for xs in jnp.split(x2, splits, axis=0):
        qp = _matmul(xs, w_query)
        idx, gate = _peer_route(qp, sub_keys)
        outs.append(_peer_expert_ln(xs, idx, gate, u_pack, v_pack, g, b))
    return jnp.concatenate(outs, axis=0)


def kernel(x, dn_w_in, dn_conv, dn_a_log, dn_dt_bias, dn_norm_w, dn_w_out, shared_w_kv, attn_w_q, attn_w_out,
           peer_w_query, peer_sub_keys, peer_u, peer_v, ln_mix_g, ln_mix_b, ln_ffn_g, ln_ffn_b):
    batch, seq, d = x.shape
    n_a = DEPTH // 2
    tabs = _rope_lane_tables(seq)
    gw = N_GROUPS * WIDTH
    dn_w_pad = [jnp.pad(w, ((0, 0), (0, DN_PROJ_PAD - w.shape[1]))).astype(bf16) for w in dn_w_in]
    dn_w_out_b = dn_w_out.astype(bf16)
    w_k, w_v = shared_w_kv[:, :gw].astype(bf16), shared_w_kv[:, gw:].astype(bf16)
    w_q, w_o = attn_w_q.astype(bf16), attn_w_out.astype(bf16)
    peer_wq = peer_w_query.astype(bf16)
    u_pack = [_pack_table(t) for t in peer_u]
    v_pack = [_pack_table(t) for t in peer_v]

    def trunk(x2):
        k_sh = v_sh = None
        for layer in range(DEPTH):
            if layer < n_a:
                lead = jnp.zeros((SUBLANES, 3 * WIDTH), f32)
                state = jnp.zeros((HEADS, HEAD_DIM, HEAD_DIM), f32)
                segs, pending = [], []
                for si, xs in enumerate(jnp.split(x2, DN_SEGMENTS, axis=0)):
                    o, lead, state = _deltanet_mixer(xs, dn_w_pad[layer], dn_conv[layer], dn_a_log[layer],
                                                     dn_dt_bias[layer], dn_norm_w[layer], lead, state)
                    pending.append(_matmul_res_ln(o, dn_w_out_b[layer], xs, ln_mix_g[layer], ln_mix_b[layer]))
                    if si < DN_SEGMENTS - DN_MERGE_TAIL or si == DN_SEGMENTS - 1:
                        xs = jnp.concatenate(pending, axis=0) if len(pending) > 1 else pending[0]
                        pending = []
                        segs.append(_peer_ffn_ln(xs, peer_wq[layer], peer_sub_keys[layer], u_pack[layer],
                                                 v_pack[layer], ln_ffn_g[layer], ln_ffn_b[layer], splits=1))
                x2 = jnp.concatenate(segs, axis=0)
                continue
            else:
                j = layer - n_a
                if j == 0:
                    k_sh = _matmul_rope(x2, w_k, tabs, seq)
                    v_sh = _matmul(x2, w_v, tn=COLS_MATMUL)
                q = _matmul_rope(x2, w_q[j], tabs, seq)
                outs, lses = [], []
                for gi, (window, dilation) in enumerate(ATTN_GROUPS):
                    o, l = _dilated_attention(q, k_sh, v_sh, gi, window, dilation, 1, seq)
                    outs.append(o)
                    lses.append(l)
                x2 = _combine_out_ln(outs, lses, w_o[j], x2, ln_mix_g[layer], ln_mix_b[layer])
            x2 = _peer_ffn_ln(x2, peer_wq[layer], peer_sub_keys[layer], u_pack[layer], v_pack[layer],
                              ln_ffn_g[layer], ln_ffn_b[layer])
        return x2

    return jnp.stack([trunk(x[b]) for b in range(batch)], axis=0)
```

```python
import functools

import jax
import jax.numpy as jnp
from jax import lax
from jax.experimental import pallas as pl
from jax.experimental.pallas import tpu as pltpu
from jax.experimental.pallas import tpu_sc as plsc

f32 = jnp.float32
bf16 = jnp.bfloat16
i32 = jnp.int32

D_MODEL = 1024
DEPTH = 2
ALPHA = (2.0 * DEPTH) ** 0.25
LN_EPS = 1e-5

HEADS = 8
HEAD_DIM = 128
WIDTH = HEADS * HEAD_DIM
CONV_K = 4
DN_CHUNK = 64

ATTN_GROUPS = ((128, 1), (512, 4), (2048, 16))
N_GROUPS = len(ATTN_GROUPS)
ATTN_BLOCK = 128
ROT_DIM = HEAD_DIM // 4
ROPE_THETA = 500000.0

PEER_HEADS = 8
PEER_NKEYS = 128
PEER_TOPK = 16
PEER_QDIM = 256
PEER_SEL = PEER_HEADS * PEER_TOPK

LANES = 128
SUBLANES = 8
VMEM_LIMIT = 48 * 1024 * 1024

ROWS_MATMUL = 512
ROWS_WIDE = 256
ROWS_DELTA = 2 * DN_CHUNK
ROWS_LANEWISE = 2048
COLS_MATMUL = 1024


def _params(sem):
    return pltpu.CompilerParams(dimension_semantics=sem, vmem_limit_bytes=VMEM_LIMIT)


def _dot(a, b):
    return jnp.dot(a.astype(bf16), b.astype(bf16), preferred_element_type=f32)


def _dot_nt(a, b):
    return lax.dot_general(a.astype(bf16), b.astype(bf16), (((1,), (1,)), ((), ())), preferred_element_type=f32)


def _dot_tn(a, b):
    return lax.dot_general(a.astype(bf16), b.astype(bf16), (((0,), (0,)), ((), ())), preferred_element_type=f32)


def _sigmoid(x):
    return 1.0 / (1.0 + jnp.exp(-x))


def _layer_norm(y, g, b):
    mu = jnp.mean(y, -1, keepdims=True)
    yc = y - mu
    var = jnp.mean(yc * yc, -1, keepdims=True)
    return yc * lax.rsqrt(var + LN_EPS) * g + b


def _mm_kernel(a_ref, w_ref, o_ref):
    o_ref[...] = _dot(a_ref[...], w_ref[...])


def _matmul(a, w, tm=ROWS_MATMUL, tn=None):
    m, k = a.shape
    n = w.shape[1]
    tn = tn or n
    return pl.pallas_call(
        _mm_kernel,
        grid=(m // tm, n // tn),
        in_specs=[pl.BlockSpec((tm, k), lambda i, j: (i, 0)), pl.BlockSpec((k, tn), lambda i, j: (0, j))],
        out_specs=pl.BlockSpec((tm, tn), lambda i, j: (i, j)),
        out_shape=jax.ShapeDtypeStruct((m, n), f32),
        compiler_params=_params(("parallel", "parallel")),
        name="matmul",
    )(a, w)


def _rope(y, c, s_lo, s_hi):
    return y * c + pltpu.roll(y, ROT_DIM // 2, axis=1) * s_hi + pltpu.roll(y, LANES - ROT_DIM // 2, axis=1) * s_lo


def _mm_rope_kernel(a_ref, w_ref, c_ref, slo_ref, shi_ref, o_ref, *, heads):
    y = _dot(a_ref[...], w_ref[...])
    c, s_lo, s_hi = c_ref[...], slo_ref[...], shi_ref[...]
    for h in range(heads):
        sl = slice(h * HEAD_DIM, (h + 1) * HEAD_DIM)
        o_ref[:, sl] = _rope(y[:, sl], c, s_lo, s_hi)


def _matmul_rope(a, w, tabs, seq, tm=ROWS_MATMUL, tn=COLS_MATMUL):
    m, k = a.shape
    n = w.shape[1]
    nsb = seq // tm
    tab_spec = pl.BlockSpec((tm, LANES), lambda i, j: (i % nsb, 0))
    return pl.pallas_call(
        functools.partial(_mm_rope_kernel, heads=tn // HEAD_DIM),
        grid=(m // tm, n // tn),
        in_specs=[pl.BlockSpec((tm, k), lambda i, j: (i, 0)), pl.BlockSpec((k, tn), lambda i, j: (0, j)),
                  tab_spec, tab_spec, tab_spec],
        out_specs=pl.BlockSpec((tm, tn), lambda i, j: (i, j)),
        out_shape=jax.ShapeDtypeStruct((m, n), f32),
        compiler_params=_params(("parallel", "parallel")),
        name="matmul_rope",
    )(a, w, *tabs)


def _mm_res_ln_kernel(a_ref, w_ref, x_ref, g_ref, b_ref, o_ref):
    h = _dot(a_ref[...], w_ref[...])
    o_ref[...] = _layer_norm(ALPHA * x_ref[...] + h, g_ref[...], b_ref[...])


def _matmul_res_ln(a, w, x, g, b, tm=ROWS_MATMUL):
    m, k = a.shape
    n = w.shape[1]
    row = pl.BlockSpec((1, n), lambda i: (0, 0))
    return pl.pallas_call(
        _mm_res_ln_kernel,
        grid=(m // tm,),
        in_specs=[pl.BlockSpec((tm, k), lambda i: (i, 0)), pl.BlockSpec((k, n), lambda i: (0, 0)),
                  pl.BlockSpec((tm, n), lambda i: (i, 0)), row, row],
        out_specs=pl.BlockSpec((tm, n), lambda i: (i, 0)),
        out_shape=jax.ShapeDtypeStruct((m, n), f32),
        compiler_params=_params(("parallel",)),
        name="matmul_res_ln",
    )(a, w, x, g.reshape(1, n), b.reshape(1, n))


DN_PROJ_PAD = 4 * WIDTH + LANES


def _dn_prep_kernel(proj_ref, halo_ref, lead_ref, conv_ref, alog_ref, dtb_ref, q_ref, k_ref, v_ref, gate_ref, *, tm):
    i = pl.program_id(0)
    x = proj_ref[:, : 3 * WIDTH]
    halo = jnp.where(i == 0, lead_ref[...], halo_ref[...])
    w = conv_ref[...]
    row8 = lax.broadcasted_iota(i32, (SUBLANES, 1), 0)
    acc = x * w[CONV_K - 1:CONV_K]
    for s in range(1, CONV_K):
        xs = pltpu.roll(x, s, axis=0)
        hs = pltpu.roll(halo, s, axis=0)
        first = jnp.where(row8 < s, hs, xs[:SUBLANES])
        xs = jnp.concatenate([first, xs[SUBLANES:]], axis=0)
        acc = acc + xs * w[CONV_K - 1 - s:CONV_K - s]
    qkv = acc * _sigmoid(acc)
    for h in range(HEADS):
        sl = slice(h * HEAD_DIM, (h + 1) * HEAD_DIM)
        qh = qkv[:, sl]
        q_ref[:, sl] = qh * lax.rsqrt(jnp.sum(qh * qh, -1, keepdims=True) + 1e-6) * (HEAD_DIM ** -0.5)
        kh = qkv[:, WIDTH + h * HEAD_DIM: WIDTH + (h + 1) * HEAD_DIM]
        k_ref[:, sl] = kh * lax.rsqrt(jnp.sum(kh * kh, -1, keepdims=True) + 1e-6)
    v_ref[...] = qkv[:, 2 * WIDTH:]
    ab = proj_ref[:, 4 * WIDTH:]
    z = ab + dtb_ref[...]
    softplus = jnp.maximum(z, 0.0) + jnp.log(1.0 + jnp.exp(-jnp.abs(z)))
    g = -jnp.exp(alog_ref[...]) * softplus
    row = lax.broadcasted_iota(i32, (tm, 1), 0) % DN_CHUNK
    s = 1
    while s < DN_CHUNK:
        g = g + jnp.where(row >= s, pltpu.roll(g, s, axis=0), 0.0)
        s *= 2
    lane = lax.broadcasted_iota(i32, (1, LANES), 1)
    gate_ref[...] = jnp.where(lane < HEADS, g, _sigmoid(ab))


def _dn_prep(proj, lead, conv_w, a_log, dt_bias, tm=ROWS_WIDE):
    m = proj.shape[0]
    pad = LANES - HEADS
    alog = jnp.pad(a_log.astype(f32), (0, pad)).reshape(1, LANES)
    dtb = jnp.pad(dt_bias.astype(f32), (0, pad)).reshape(1, LANES)
    hb = tm // SUBLANES
    out_w = pl.BlockSpec((tm, WIDTH), lambda i: (i, 0))
    return pl.pallas_call(
        functools.partial(_dn_prep_kernel, tm=tm),
        grid=(m // tm,),
        in_specs=[pl.BlockSpec((tm, DN_PROJ_PAD), lambda i: (i, 0)),
                  pl.BlockSpec((SUBLANES, 3 * WIDTH), lambda i: (jnp.maximum(i * hb - 1, 0), 0)),
                  pl.BlockSpec((SUBLANES, 3 * WIDTH), lambda i: (0, 0)),
                  pl.BlockSpec((CONV_K, 3 * WIDTH), lambda i: (0, 0)),
                  pl.BlockSpec((1, LANES), lambda i: (0, 0)), pl.BlockSpec((1, LANES), lambda i: (0, 0))],
        out_specs=[out_w, out_w, out_w, pl.BlockSpec((tm, LANES), lambda i: (i, 0))],
        out_shape=[jax.ShapeDtypeStruct((m, WIDTH), f32)] * 3 + [jax.ShapeDtypeStruct((m, LANES), f32)],
        compiler_params=_params(("parallel",)),
        name="dn_prep",
    )(proj, proj, lead, conv_w, alog, dtb)


def _dn_delta_kernel(q_ref, k_ref, v_ref, z_ref, gate_ref, gt_ref, nw_ref, s0_ref, o_ref, state_ref, *, chunks):
    @pl.when(pl.program_id(0) == 0)
    def _():
        state_ref[...] = s0_ref[...]

    c = DN_CHUNK
    ri = lax.broadcasted_iota(i32, (c, c), 0)
    ci = lax.broadcasted_iota(i32, (c, c), 1)
    causal = ri >= ci
    strict = ri > ci
    eye = (ri == ci).astype(f32)
    nw = nw_ref[...]
    inst = [(ch, h) for ch in range(chunks) for h in range(HEADS)]
    pre = {}
    for ch, h in inst:
        rows = slice(ch * c, (ch + 1) * c)
        sl = slice(h * HEAD_DIM, (h + 1) * HEAD_DIM)
        q, k, v = q_ref[rows, sl], k_ref[rows, sl], v_ref[rows, sl]
        gcol = gate_ref[rows, h:h + 1]
        beta = gate_ref[rows, HEADS + h:HEADS + h + 1]
        grow = gt_ref[h:h + 1, rows]
        decay = jnp.where(causal, jnp.exp(jnp.where(causal, gcol - grow, 0.0)), 0.0)
        kb = k * beta
        egc = jnp.exp(gcol)
        glast = gcol[c - 1:c, :]
        qk = _dot_nt(jnp.concatenate([q, kb], axis=0), k)
        m = jnp.where(strict, qk[c:] * decay, 0.0)
        pre[ch, h] = dict(a_qk=qk[:c] * decay, inv=eye - m, p=m,
                          rhs=jnp.concatenate([v * beta, kb * egc], axis=1), qe=q * egc,
                          kd=k * jnp.exp(glast - gcol), eg=jnp.exp(glast))
    step = 2
    while step < c:
        for key in inst:
            pre[key]["p"] = _dot(pre[key]["p"], pre[key]["p"])
        for key in inst:
            pre[key]["inv"] = pre[key]["inv"] + _dot(pre[key]["inv"], pre[key]["p"])
        step *= 2
    for key in inst:
        pre[key]["uw"] = _dot(pre[key]["inv"], pre[key]["rhs"])
    states = [state_ref[h] for h in range(HEADS)]
    for ch in range(chunks):
        rows = slice(ch * c, (ch + 1) * c)
        for h in range(HEADS):
            sl = slice(h * HEAD_DIM, (h + 1) * HEAD_DIM)
            d = pre[ch, h]
            u, w = d["uw"][:, :HEAD_DIM], d["uw"][:, HEAD_DIM:]
            ws = _dot(jnp.concatenate([w, d["qe"]], axis=0), states[h])
            v_new = u - ws[:c]
            o = ws[c:] + _dot(d["a_qk"], v_new)
            states[h] = states[h] * d["eg"] + _dot_tn(d["kd"], v_new)
            o = o * lax.rsqrt(jnp.mean(o * o, -1, keepdims=True) + 1e-6) * nw
            zz = z_ref[rows, sl]
            o_ref[rows, sl] = o * (zz * _sigmoid(zz))
    for h in range(HEADS):
        state_ref[h] = states[h]


def _dn_delta(q, k, v, proj, gates, gates_t, norm_w, state, cb=ROWS_DELTA):
    m = q.shape[0]
    wide = pl.BlockSpec((cb, WIDTH), lambda i: (i, 0))
    whole_state = pl.BlockSpec(state.shape, lambda i: (0, 0, 0))
    return pl.pallas_call(
        functools.partial(_dn_delta_kernel, chunks=cb // DN_CHUNK),
        grid=(m // cb,),
        in_specs=[wide, wide, wide,
                  pl.BlockSpec((cb, WIDTH), lambda i: (i, 3)),
                  pl.BlockSpec((cb, LANES), lambda i: (i, 0)),
                  pl.BlockSpec((SUBLANES, cb), lambda i: (0, i)),
                  pl.BlockSpec((1, HEAD_DIM), lambda i: (0, 0)),
                  whole_state],
        out_specs=[wide, whole_state],
        out_shape=[jax.ShapeDtypeStruct((m, WIDTH), f32), jax.ShapeDtypeStruct(state.shape, f32)],
        compiler_params=_params(("arbitrary",)),
        name="dn_delta",
    )(q, k, v, proj, gates, gates_t, norm_w.reshape(1, HEAD_DIM).astype(f32), state)


def _deltanet_mixer(x2, w_pad, conv_w, a_log, dt_bias, norm_w, lead, state):
    proj = _matmul(x2, w_pad, tn=DN_PROJ_PAD // 3)
    q, k, v, gates = _dn_prep(proj, lead, conv_w, a_log, dt_bias)
    gates_t = gates[:, :SUBLANES].T
    o, state = _dn_delta(q, k, v, proj, gates, gates_t, norm_w, state)
    return o, proj[-SUBLANES:, :3 * WIDTH], state


def _attn_kernel(q_ref, kp_ref, kc_ref, vp_ref, vc_ref, o_ref, l_ref, *, steps, dilation, heads):
    nbi = pl.program_id(1)
    blk = ATTN_BLOCK
    qi = lax.broadcasted_iota(i32, (blk, 2 * blk), 0)
    kj = lax.broadcasted_iota(i32, (blk, 2 * blk), 1)
    dist = qi + blk - kj
    mask = (dist >= 0) & (dist <= steps) & ((kj >= blk) | (nbi > 0))
    scale = HEAD_DIM ** -0.5
    for r in range(dilation):
        rows = pl.ds(r, blk, stride=dilation) if dilation > 1 else slice(None)
        for h in range(heads):
            sl = slice(h * HEAD_DIM, (h + 1) * HEAD_DIM)
            q = q_ref[rows, sl]
            kk = jnp.concatenate([kp_ref[rows, sl], kc_ref[rows, sl]], axis=0)
            vv = jnp.concatenate([vp_ref[rows, sl], vc_ref[rows, sl]], axis=0)
            sc = jnp.where(mask, _dot_nt(q, kk) * scale, -jnp.inf)
            mx = jnp.max(sc, -1, keepdims=True)
            p = jnp.exp(sc - mx)
            den = jnp.sum(p, -1, keepdims=True)
            o_ref[rows, sl] = _dot(p, vv) / den
            l_ref[rows, sl] = jnp.broadcast_to(mx + jnp.log(den), (blk, HEAD_DIM))


def _dilated_attention(q, k, v, group, window, dilation, batch, seq):
    rows = ATTN_BLOCK * dilation
    nb = seq // rows
    heads = HEADS if dilation == 1 else 1
    hblocks = HEADS // heads
    blk = (rows, heads * HEAD_DIM)

    def cur(b, i, j):
        return (b * nb + i, group * hblocks + j)

    def prev(b, i, j):
        return (b * nb + jnp.maximum(i - 1, 0), group * hblocks + j)

    out_spec = pl.BlockSpec(blk, lambda b, i, j: (b * nb + i, j))
    return pl.pallas_call(
        functools.partial(_attn_kernel, steps=window // dilation, dilation=dilation, heads=heads),
        grid=(batch, nb, hblocks),
        in_specs=[pl.BlockSpec(blk, cur), pl.BlockSpec(blk, prev), pl.BlockSpec(blk, cur),
                  pl.BlockSpec(blk, prev), pl.BlockSpec(blk, cur)],
        out_specs=[out_spec, out_spec],
        out_shape=[jax.ShapeDtypeStruct((batch * seq, WIDTH), f32)] * 2,
        compiler_params=_params(("parallel", "parallel", "parallel")),
        name=f"dilated_attn_d{dilation}",
    )(q, k, k, v, v)


def _combine_out_ln_kernel(o0, o1, o2, l0, l1, l2, w_ref, x_ref, g_ref, b_ref, out_ref):
    ls = [l0[...], l1[...], l2[...]]
    mx = jnp.maximum(jnp.maximum(ls[0], ls[1]), ls[2])
    es = [jnp.exp(l - mx) for l in ls]
    o = (es[0] * o0[...] + es[1] * o1[...] + es[2] * o2[...]) / (es[0] + es[1] + es[2])
    h = _dot(o, w_ref[...])
    out_ref[...] = _layer_norm(ALPHA * x_ref[...] + h, g_ref[...], b_ref[...])


def _combine_out_ln(outs, lses, w, x, g, b, tm=ROWS_WIDE):
    m, n = x.shape
    blk = pl.BlockSpec((tm, n), lambda i: (i, 0))
    row = pl.BlockSpec((1, n), lambda i: (0, 0))
    return pl.pallas_call(
        _combine_out_ln_kernel,
        grid=(m // tm,),
        in_specs=[blk] * 6 + [pl.BlockSpec((WIDTH, n), lambda i: (0, 0)), blk, row, row],
        out_specs=blk,
        out_shape=jax.ShapeDtypeStruct((m, n), f32),
        compiler_params=_params(("parallel",)),
        name="attn_combine_out_ln",
    )(*outs, *lses, w, x, g.reshape(1, n), b.reshape(1, n))


def _rope_lane_tables(seq):
    half = ROT_DIM // 2
    inv_freq = ROPE_THETA ** (-jnp.arange(half, dtype=f32) * 2.0 / ROT_DIM)
    ang = jnp.arange(seq, dtype=f32)[:, None] * inv_freq[None, :]
    cos, sin = jnp.cos(ang), jnp.sin(ang)
    ones = jnp.ones((seq, HEAD_DIM - ROT_DIM), f32)
    zeros = jnp.zeros((seq, HEAD_DIM - half), f32)
    c = jnp.concatenate([cos, cos, ones], axis=1)
    s_lo = jnp.concatenate([-sin, zeros], axis=1)
    s_hi = jnp.concatenate([jnp.zeros((seq, half), f32), sin, jnp.zeros((seq, HEAD_DIM - ROT_DIM), f32)], axis=1)
    return c, s_lo, s_hi


def _top_rows(s, order, payload, k):
    big = jnp.int32(2 ** 30)
    vals, pays = [], []
    for _ in range(k):
        m = jnp.max(s, axis=0, keepdims=True)
        pos = jnp.min(jnp.where(s == m, order, big), axis=0, keepdims=True)
        sel = order == pos
        pays.append(pos if payload is None else jnp.max(jnp.where(sel, payload, -1), axis=0, keepdims=True))
        vals.append(m)
        s = jnp.where(sel, -jnp.inf, s)
    return jnp.concatenate(vals, axis=0), jnp.concatenate(pays, axis=0)


def _pair_candidates(t1, t2, combine, fill):
    kt = PEER_TOPK
    row = lax.broadcasted_iota(i32, (SUBLANES, LANES), 0)
    bc = lambda t, r: jnp.broadcast_to(t[r:r + 1], (SUBLANES, LANES))
    lo1, hi1, lo2, hi2 = t1[:SUBLANES], t1[SUBLANES:], t2[:SUBLANES], t2[SUBLANES:]
    tiles = [combine(bc(t1, 0), lo2), combine(bc(t1, 0), hi2), combine(bc(t1, 1), lo2),
             jnp.where(row < kt // 3, combine(bc(t1, 2), lo2), fill),
             jnp.where(row < kt // 4, combine(bc(t1, 3), lo2), fill),
             combine(hi1, bc(t2, 0)),
             jnp.where(row >= 4, combine(lo1, bc(t2, 0)), fill),
             jnp.where(row >= 4, combine(lo1, bc(t2, 1)), jnp.where(row == 0, combine(bc(t1, 4), bc(t2, 2)), fill))]
    return jnp.concatenate(tiles, axis=0)


def _pair_rank():
    kt = PEER_TOPK
    row = lax.broadcasted_iota(i32, (SUBLANES, LANES), 0)
    unused = kt * kt + row
    tiles = [row, row + SUBLANES, kt + row,
             jnp.where(row < kt // 3, 2 * kt + row, unused),
             jnp.where(row < kt // 4, 3 * kt + row, unused + SUBLANES),
             (row + SUBLANES) * kt,
             jnp.where(row >= 4, row * kt, unused + 2 * SUBLANES),
             jnp.where(row >= 4, row * kt + 1, jnp.where(row == 0, 4 * kt + 2, unused + 3 * SUBLANES))]
    return jnp.concatenate(tiles, axis=0)


def _peer_route_kernel(qp_ref, keys_ref, idx_ref, gate_ref):
    kt = PEER_TOPK
    half = PEER_QDIM // 2
    keyid = lax.broadcasted_iota(i32, (PEER_NKEYS, LANES), 0)
    rank = _pair_rank()
    idx_rows, gate_rows = [], []
    for h in range(PEER_HEADS):
        tops = []
        for c in range(2):
            col = (h * 2 + c) * half
            s = _dot_nt(keys_ref[h * 2 + c], qp_ref[:, col:col + half])
            tops.append(_top_rows(s, keyid, None, kt))
        (v1, i1), (v2, i2) = tops
        cand = _pair_candidates(v1, v2, lambda a, b: a + b, -jnp.inf)
        cand_id = _pair_candidates(i1, i2, lambda a, b: a * PEER_NKEYS + b, -1)
        best, ids = _top_rows(cand, rank, cand_id, kt)
        e = jnp.exp(best - best[0:1])
        gate_rows.append(e / jnp.sum(e, axis=0, keepdims=True))
        idx_rows.append(ids)
    idx_ref[...] = jnp.concatenate(idx_rows, axis=0).astype(f32).T.astype(i32)
    gate_ref[...] = jnp.concatenate(gate_rows, axis=0).T


def _peer_route(qp, sub_keys):
    m = qp.shape[0]
    tb = LANES
    keys = sub_keys.reshape(PEER_HEADS * 2, PEER_NKEYS, PEER_QDIM // 2).astype(bf16)
    out = pl.BlockSpec((tb, PEER_SEL), lambda i: (i, 0))
    return pl.pallas_call(
        _peer_route_kernel,
        grid=(m // tb,),
        in_specs=[pl.BlockSpec((tb, qp.shape[1]), lambda i: (i, 0)),
                  pl.BlockSpec(keys.shape, lambda i: (0, 0, 0))],
        out_specs=[out, out],
        out_shape=[jax.ShapeDtypeStruct((m, PEER_SEL), i32), jax.ShapeDtypeStruct((m, PEER_SEL), f32)],
        compiler_params=_params(("parallel",)),
        name="peer_route",
    )(qp, keys)


SC_CORES = 2
SC_SUBCORES = 16
SC_LANES = 16
SC_WORKERS = SC_CORES * SC_SUBCORES
SC_CHUNKS = D_MODEL // SC_LANES
SC_WORDS = D_MODEL // 2
SC_WORD_CHUNKS = SC_WORDS // SC_LANES
SC_TOK_GROUP = 32
SC_ROWS = 64
SC_GATHERS_PER_TOK = PEER_SEL // SC_ROWS
SC_GATHERS = SC_TOK_GROUP * SC_GATHERS_PER_TOK
SC_ROW_GROUP = 8


def _pack_table(tab):
    e = tab.shape[0]
    bits = lax.bitcast_convert_type(tab.astype(bf16), jnp.uint16).astype(jnp.uint32)
    bits = bits.reshape(e, SC_WORD_CHUNKS, 2, SC_LANES)
    word = bits[:, :, 0, :] | (bits[:, :, 1, :] << 16)
    return lax.bitcast_convert_type(word.reshape(e, SC_WORDS), i32)


def _unpack_words(w):
    return lax.bitcast_convert_type(w << 16, f32), lax.bitcast_convert_type(w & (-65536), f32)


def _sc_mesh():
    return plsc.VectorSubcoreMesh(core_axis_name="c", subcore_axis_name="s")


def _sc_worker():
    return lax.axis_index("s") * SC_CORES + lax.axis_index("c")


def _sc_gather_loop(gather, compute):
    gather(0, 0).start()

    @pl.loop(0, SC_GATHERS, step=2)
    def _(q):
        gather(q + 1, 1).start()
        gather(q, 0).wait()
        compute(q, 0)

        @pl.when(q + 2 < SC_GATHERS)
        def _():
            gather(q + 2, 0).start()
        gather(q + 1, 1).wait()
        compute(q + 1, 1)


def _sc_scratch(stage_shape):
    return [pltpu.VMEM((SC_GATHERS, SC_ROWS), i32), pltpu.VMEM(stage_shape[0], f32), pltpu.VMEM(stage_shape[1], f32),
            pltpu.VMEM((SC_ROWS, SC_WORDS), i32), pltpu.VMEM((SC_ROWS, SC_WORDS), i32),
            pltpu.SemaphoreType.DMA, pltpu.SemaphoreType.DMA]


def _peer_udot(x, idx, u_tab):
    t_all = x.shape[0]
    tpw = t_all // SC_WORKERS
    idx2 = idx.reshape(t_all * SC_GATHERS_PER_TOK, SC_ROWS)

    @functools.partial(
        pl.kernel, mesh=_sc_mesh(), compiler_params=pltpu.CompilerParams(needs_layout_passes=False),
        out_type=jax.ShapeDtypeStruct((t_all, PEER_SEL), f32),
        scratch_types=_sc_scratch(((SC_TOK_GROUP, D_MODEL), (SC_TOK_GROUP, PEER_SEL))), name="peer_udot_sc")
    def k(x_hbm, idx_hbm, u_hbm, pre_hbm, idx_v, x_v, pre_v, rows0, rows1, sem0, sem1):
        bufs = ((rows0, sem0), (rows1, sem1))
        base = _sc_worker() * tpw
        lane = lax.iota(i32, SC_LANES)
        zero = jnp.zeros((SC_LANES,), f32)

        def gather(q, b):
            return pltpu.make_async_copy(u_hbm.at[idx_v.at[q]], bufs[b][0], bufs[b][1])

        def compute(q, b):
            rows = bufs[b][0]
            t = q // SC_GATHERS_PER_TOK
            col0 = (q % SC_GATHERS_PER_TOK) * SC_ROWS
            for hh in range(SC_ROWS // SC_LANES):
                vec = zero
                for rg in range(SC_LANES // SC_ROW_GROUP):
                    r0 = hh * SC_LANES + rg * SC_ROW_GROUP

                    def body(c, acc):
                        x_lo = x_v[t, pl.ds(c * 2 * SC_LANES, SC_LANES)]
                        x_hi = x_v[t, pl.ds(c * 2 * SC_LANES + SC_LANES, SC_LANES)]
                        out = []
                        for r in range(SC_ROW_GROUP):
                            lo, hi = _unpack_words(rows[r0 + r, pl.ds(c * SC_LANES, SC_LANES)])
                            out.append(acc[r] + lo * x_lo + hi * x_hi)
                        return tuple(out)

                    acc = lax.fori_loop(0, SC_WORD_CHUNKS, body, (zero,) * SC_ROW_GROUP, unroll=2)
                    for r in range(SC_ROW_GROUP):
                        vec = jnp.where(lane == rg * SC_ROW_GROUP + r, jnp.sum(acc[r]), vec)
                pre_v[t, pl.ds(col0 + hh * SC_LANES, SC_LANES)] = vec

        @pl.loop(0, tpw // SC_TOK_GROUP)
        def _(g):
            tok0 = base + g * SC_TOK_GROUP
            pltpu.sync_copy(idx_hbm.at[pl.ds(tok0 * SC_GATHERS_PER_TOK, SC_GATHERS)], idx_v)
            pltpu.sync_copy(x_hbm.at[pl.ds(tok0, SC_TOK_GROUP)], x_v)
            _sc_gather_loop(gather, compute)
            pltpu.sync_copy(pre_v, pre_hbm.at[pl.ds(tok0, SC_TOK_GROUP)])

    return k(x, idx2, u_tab)


def _peer_vacc(w, idx, v_tab):
    t_all = w.shape[0]
    tpw = t_all // SC_WORKERS
    idx2 = idx.reshape(t_all * SC_GATHERS_PER_TOK, SC_ROWS)
    half_w = SC_WORD_CHUNKS // 2

    @functools.partial(
        pl.kernel, mesh=_sc_mesh(), compiler_params=pltpu.CompilerParams(needs_layout_passes=False),
        out_type=jax.ShapeDtypeStruct((t_all, D_MODEL), f32),
        scratch_types=_sc_scratch(((SC_TOK_GROUP, PEER_SEL), (SC_TOK_GROUP, D_MODEL))), name="peer_vacc_sc")
    def k(w_hbm, idx_hbm, v_hbm, f_hbm, idx_v, w_v, f_v, rows0, rows1, sem0, sem1):
        bufs = ((rows0, sem0), (rows1, sem1))
        base = _sc_worker() * tpw
        zero = jnp.zeros((SC_LANES,), f32)

        def gather(q, b):
            return pltpu.make_async_copy(v_hbm.at[idx_v.at[q]], bufs[b][0], bufs[b][1])

        def compute(q, b):
            rows = bufs[b][0]
            t = q // SC_GATHERS_PER_TOK
            col0 = (q % SC_GATHERS_PER_TOK) * SC_ROWS
            tvec = jnp.full((SC_LANES,), t, i32)
            for half in range(2):
                def body(r, acc):
                    wr = plsc.load_gather(w_v, [tvec, jnp.full((SC_LANES,), col0 + r, i32)])
                    out = []
                    for c in range(half_w):
                        lo, hi = _unpack_words(rows[r, pl.ds((half * half_w + c) * SC_LANES, SC_LANES)])
                        out.append(acc[2 * c] + wr * lo)
                        out.append(acc[2 * c + 1] + wr * hi)
                    return tuple(out)

                acc = lax.fori_loop(0, SC_ROWS, body, (zero,) * (2 * half_w))
                for c in range(2 * half_w):
                    sl = pl.ds((half * 2 * half_w + c) * SC_LANES, SC_LANES)
                    f_v[t, sl] = f_v[t, sl] + acc[c]

        @pl.loop(0, tpw // SC_TOK_GROUP)
        def _(g):
            tok0 = base + g * SC_TOK_GROUP
            pltpu.sync_copy(idx_hbm.at[pl.ds(tok0 * SC_GATHERS_PER_TOK, SC_GATHERS)], idx_v)
            pltpu.sync_copy(w_hbm.at[pl.ds(tok0, SC_TOK_GROUP)], w_v)

            @pl.loop(0, SC_TOK_GROUP)
            def _(t):
                @pl.loop(0, SC_CHUNKS)
                def _(c):
                    f_v[t, pl.ds(c * SC_LANES, SC_LANES)] = zero

            _sc_gather_loop(gather, compute)
            pltpu.sync_copy(f_v, f_hbm.at[pl.ds(tok0, SC_TOK_GROUP)])

    return k(w, idx2, v_tab)


def _peer_act_kernel(pre_ref, gate_ref, w_ref):
    pre = pre_ref[...]
    w_ref[...] = gate_ref[...] * (0.5 * pre * (1.0 + lax.erf(pre * (2.0 ** -0.5))))


def _peer_act(pre, gate, tm=ROWS_LANEWISE):
    m, n = pre.shape
    blk = pl.BlockSpec((tm, n), lambda i: (i, 0))
    return pl.pallas_call(
        _peer_act_kernel, grid=(m // tm,), in_specs=[blk, blk], out_specs=blk,
        out_shape=jax.ShapeDtypeStruct((m, n), f32), compiler_params=_params(("parallel",)), name="peer_act",
    )(pre, gate)


def _res_ln_kernel(x_ref, f_ref, g_ref, b_ref, o_ref):
    o_ref[...] = _layer_norm(ALPHA * x_ref[...] + f_ref[...], g_ref[...], b_ref[...])


def _res_ln(x, f, g, b, tm=ROWS_MATMUL):
    m, n = x.shape
    blk = pl.BlockSpec((tm, n), lambda i: (i, 0))
    row = pl.BlockSpec((1, n), lambda i: (0, 0))
    return pl.pallas_call(
        _res_ln_kernel, grid=(m // tm,), in_specs=[blk, blk, row, row], out_specs=blk,
        out_shape=jax.ShapeDtypeStruct((m, n), f32), compiler_params=_params(("parallel",)), name="res_ln",
    )(x, f, g.reshape(1, n), b.reshape(1, n))


def _peer_expert_ln(x, idx, gate, u_tab, v_tab, g, b):
    pre = _peer_udot(x, idx, u_tab)
    w = _peer_act(pre, gate)
    f = _peer_vacc(w, idx, v_tab)
    return _res_ln(x, f, g, b)


PEER_SPLITS = 2
DN_SEGMENTS = 4


def _peer_ffn_ln(x2, w_query, sub_keys, u_pack, v_pack, g, b, splits=PEER_SPLITS):
    outs = []
    for xs in jnp.split(x2, splits, axis=0):
        qp = _matmul(xs, w_query)
        idx, gate = _peer_route(qp, sub_keys)
        outs.append(_peer_expert_ln(xs, idx, gate, u_pack, v_pack, g, b))
    return jnp.concatenate(outs, axis=0)


def kernel(x, dn_w_in, dn_conv, dn_a_log, dn_dt_bias, dn_norm_w, dn_w_out, shared_w_kv, attn_w_q, attn_w_out,
           peer_w_query, peer_sub_keys, peer_u, peer_v, ln_mix_g, ln_mix_b, ln_ffn_g, ln_ffn_b):
    batch, seq, d = x.shape
    n_a = DEPTH // 2
    tabs = _rope_lane_tables(seq)
    gw = N_GROUPS * WIDTH
    dn_w_pad = [jnp.pad(w, ((0, 0), (0, DN_PROJ_PAD - w.shape[1]))).astype(bf16) for w in dn_w_in]
    dn_w_out_b = dn_w_out.astype(bf16)
    w_k, w_v = shared_w_kv[:, :gw].astype(bf16), shared_w_kv[:, gw:].astype(bf16)
    w_q, w_o = attn_w_q.astype(bf16), attn_w_out.astype(bf16)
    peer_wq = peer_w_query.astype(bf16)
    u_pack = [_pack_table(t) for t in peer_u]
    v_pack = [_pack_table(t) for t in peer_v]

    def trunk(x2):
        k_sh = v_sh = None
        for layer in range(DEPTH):
            if layer < n_a:
                lead = jnp.zeros((SUBLANES, 3 * WIDTH), f32)
                state = jnp.zeros((HEADS, HEAD_DIM, HEAD_DIM), f32)
                segs = []
                for si, xs in enumerate(jnp.split(x2, DN_SEGMENTS, axis=0)):
                    o, lead, state = _deltanet_mixer(xs, dn_w_pad[layer], dn_conv[layer], dn_a_log[layer],
                                                     dn_dt_bias[layer], dn_norm_w[layer], lead, state)
                    xs = _matmul_res_ln(o, dn_w_out_b[layer], xs, ln_mix_g[layer], ln_mix_b[layer])
                    segs.append(_peer_ffn_ln(xs, peer_wq[layer], peer_sub_keys[layer], u_pack[layer], v_pack[layer],
                                             ln_ffn_g[layer], ln_ffn_b[layer], splits=2 if si == 0 else 1))
                x2 = jnp.concatenate(segs, axis=0)
                continue
            else:
                j = layer - n_a
                if j == 0:
                    k_sh = _matmul_rope(x2, w_k, tabs, seq)
                    v_sh = _matmul(x2, w_v, tn=COLS_MATMUL)
                q = _matmul_rope(x2, w_q[j], tabs, seq)
                outs, lses = [], []
                for gi, (window, dilation) in enumerate(ATTN_GROUPS):
                    o, l = _dilated_attention(q, k_sh, v_sh, gi, window, dilation, 1, seq)
                    outs.append(o)
                    lses.append(l)
                x2 = _combine_out_ln(outs, lses, w_o[j], x2, ln_mix_g[layer], ln_mix_b[layer])
            x2 = _peer_ffn_ln(x2, peer_wq[layer], peer_sub_keys[layer], u_pack[layer], v_pack[layer],
                              ln_ffn_g[layer], ln_ffn_b[layer])
        return x2

    return jnp.stack([trunk(x[b]) for b in range(batch)], axis=0)
```
